```python
import jax, jax.numpy as jnp
from jax import lax
import numpy as np

D_MODEL = 1024
BATCH = 8
SEQ = 16384
DEPTH = 4

N_MIXERS = 3
EXPAND = 2
D_INNER = EXPAND * D_MODEL
CHUNK = 64

GDN_HEAD_DIM = 128
GDN_V_HEADS = D_INNER // GDN_HEAD_DIM
GDN_QK_HEADS = GDN_V_HEADS // 2
GDN_QK_DIM = GDN_QK_HEADS * GDN_HEAD_DIM
GDN_V_DIM = GDN_V_HEADS * GDN_HEAD_DIM
GDN_CONV = 4
GDN_CONV_DIM = 2 * GDN_QK_DIM + GDN_V_DIM
GDN_IN = GDN_CONV_DIM + GDN_V_DIM + 2 * GDN_V_HEADS

SC_WIDTH = D_INNER
SC_CONV = 3
SC_IN = 4 * SC_WIDTH

SSD_HEAD_DIM = 64
SSD_HEADS = D_INNER // SSD_HEAD_DIM
SSD_GROUPS = 4
SSD_STATE = 128
SSD_CONV = 4
SSD_CONV_DIM = D_INNER + 2 * SSD_GROUPS * SSD_STATE
SSD_IN = D_INNER + SSD_CONV_DIM + SSD_HEADS

DEEPNORM_ALPHA = (2 * DEPTH) ** 0.25
DEEPNORM_BETA = (8 * DEPTH) ** -0.25
RMS_EPS = 1e-6
LN_EPS = 1e-5
L2_EPS = 1e-6

kernel_name = 'hybrid_gdn_shortconv_ssd_deepnorm'


def causal_depthwise_conv(u, w):
    width, ch = w.shape
    return lax.conv_general_dilated(u, w[:, None, :], window_strides=(1,), padding=[(width - 1, 0)],
                                    dimension_numbers=('NWC', 'WIO', 'NWC'), feature_group_count=ch)


def rms_normalize(u):
    uf = u.astype(jnp.float32)
    return uf * lax.rsqrt(jnp.mean(uf * uf, axis=-1, keepdims=True) + RMS_EPS)


def l2_normalize(u):
    return u * lax.rsqrt(jnp.sum(u * u, axis=-1, keepdims=True) + L2_EPS)


def layer_norm(u, g, b):
    uf = u.astype(jnp.float32)
    mu = jnp.mean(uf, axis=-1, keepdims=True)
    var = jnp.mean(jnp.square(uf - mu), axis=-1, keepdims=True)
    return ((uf - mu) * lax.rsqrt(var + LN_EPS)).astype(u.dtype) * g + b


def chunked_gated_delta_rule(q, k, v, g, beta):
    b, l, h, dk = q.shape
    dv = v.shape[-1]
    n = l // CHUNK
    to_blocks = lambda t: t.reshape(b, n, CHUNK, h, t.shape[-1]).transpose(1, 0, 3, 2, 4)
    q, k, v = to_blocks(q), to_blocks(k), to_blocks(v)
    g = g.reshape(b, n, CHUNK, h).transpose(1, 0, 3, 2)
    beta = beta.reshape(b, n, CHUNK, h).transpose(1, 0, 3, 2)
    g_cum = jnp.cumsum(g, axis=-1)
    causal = jnp.tril(jnp.ones((CHUNK, CHUNK), dtype=bool))
    strict = jnp.tril(jnp.ones((CHUNK, CHUNK), dtype=bool), -1)
    decay = jnp.exp(jnp.where(causal, g_cum[..., :, None] - g_cum[..., None, :], -jnp.inf))
    k_beta = k * beta[..., None]
    a_strict = jnp.where(strict, jnp.einsum('nbhcd,nbhmd->nbhcm', k_beta, k) * decay, 0.0)
    rhs = jnp.concatenate([v * beta[..., None], k_beta * jnp.exp(g_cum)[..., None]], axis=-1)
    solved = lax.linalg.triangular_solve(a_strict, rhs, left_side=True, lower=True, unit_diagonal=True)
    u_vals, w_keys = solved[..., :dv], solved[..., dv:]
    attn_intra = jnp.einsum('nbhcd,nbhmd->nbhcm', q, k) * decay
    q_decay = q * jnp.exp(g_cum)[..., None]
    k_tail = k * jnp.exp(g_cum[..., -1:] - g_cum)[..., None]
    chunk_decay = jnp.exp(g_cum[..., -1])

    def step(state, inp):
        qd, wk, uv, att, kt, cd = inp
        v_new = uv - jnp.einsum('bhcd,bhde->bhce', wk, state)
        out = jnp.einsum('bhcd,bhde->bhce', qd, state) + jnp.einsum('bhcm,bhme->bhce', att, v_new)
        state = state * cd[..., None, None] + jnp.einsum('bhcd,bhce->bhde', kt, v_new)
        return state, out

    s0 = jnp.zeros((b, h, dk, dv), dtype=q.dtype)
    _, o = lax.scan(step, s0, (q_decay, w_keys, u_vals, attn_intra, k_tail, chunk_decay))
    return o.transpose(1, 0, 3, 2, 4).reshape(b, l, h, dv)


def gated_deltanet_mixer(x, w_in, conv_w, a_log, dt_bias, norm_w, w_out):
    b, l, _ = x.shape
    qkv, z, beta_raw, a_raw = jnp.split(
        x @ w_in, [GDN_CONV_DIM, GDN_CONV_DIM + GDN_V_DIM, GDN_CONV_DIM + GDN_V_DIM + GDN_V_HEADS], axis=-1)
    qkv = jax.nn.silu(causal_depthwise_conv(qkv, conv_w)).astype(jnp.float32)
    q, k, v = jnp.split(qkv, [GDN_QK_DIM, 2 * GDN_QK_DIM], axis=-1)
    q = l2_normalize(q.reshape(b, l, GDN_QK_HEADS, GDN_HEAD_DIM)) * (GDN_HEAD_DIM ** -0.5)
    k = l2_normalize(k.reshape(b, l, GDN_QK_HEADS, GDN_HEAD_DIM))
    q = jnp.repeat(q, GDN_V_HEADS // GDN_QK_HEADS, axis=2)
    k = jnp.repeat(k, GDN_V_HEADS // GDN_QK_HEADS, axis=2)
    v = v.reshape(b, l, GDN_V_HEADS, GDN_HEAD_DIM)
    beta = jax.nn.sigmoid(beta_raw.astype(jnp.float32))
    g = -jnp.exp(a_log.astype(jnp.float32)) * jax.nn.softplus(a_raw.astype(jnp.float32) + dt_bias.astype(jnp.float32))
    o = chunked_gated_delta_rule(q, k, v, g, beta)
    z = z.reshape(b, l, GDN_V_HEADS, GDN_HEAD_DIM)
    o = rms_normalize(o).astype(x.dtype) * norm_w * jax.nn.silu(z)
    return o.reshape(b, l, GDN_V_DIM) @ w_out


def short_conv_mixer(x, w_in, conv_w, w_out):
    h, b_gate, c_gate, z = jnp.split(x @ w_in, 4, axis=-1)
    y = b_gate * causal_depthwise_conv(c_gate * h, conv_w)
    return (y * jax.nn.silu(z)) @ w_out


def ssd_chunked(x, dt, a, b_mat, c_mat):
    bsz, l, h, p = x.shape
    g, s = b_mat.shape[2], b_mat.shape[3]
    e = h // g
    n = l // CHUNK
    xdt = (x * dt[..., None]).reshape(bsz, n, CHUNK, g, e, p).transpose(1, 0, 3, 4, 2, 5)
    a_cum = jnp.cumsum((dt * a).reshape(bsz, n, CHUNK, g, e).transpose(1, 0, 3, 4, 2), axis=-1)
    bm = b_mat.reshape(bsz, n, CHUNK, g, s).transpose(1, 0, 3, 2, 4)
    cm = c_mat.reshape(bsz, n, CHUNK, g, s).transpose(1, 0, 3, 2, 4)
    causal = jnp.tril(jnp.ones((CHUNK, CHUNK), dtype=bool))
    seg = jnp.exp(jnp.where(causal, a_cum[..., :, None] - a_cum[..., None, :], -jnp.inf))
    cb = jnp.einsum('nbgcs,nbgms->nbgcm', cm, bm)
    y_diag = jnp.einsum('nbgecm,nbgemp->nbgecp', seg * cb[:, :, :, None], xdt)
    states = jnp.einsum('nbgcs,nbgecp->nbgeps', bm, xdt * jnp.exp(a_cum[..., -1:] - a_cum)[..., None])
    c_decay = jnp.exp(a_cum)
    chunk_decay = jnp.exp(a_cum[..., -1])

    def step(state, inp):
        st, c_n, cdec, tot = inp
        y_off = jnp.einsum('bgcs,bgeps,bgec->bgecp', c_n, state, cdec)
        state = state * tot[..., None, None] + st
        return state, y_off

    s0 = jnp.zeros((bsz, g, e, p, s), dtype=x.dtype)
    _, y_off = lax.scan(step, s0, (states, cm, c_decay, chunk_decay))
    y = y_diag + y_off
    return y.transpose(1, 0, 4, 2, 3, 5).reshape(bsz, l, h, p)


def mamba2_mixer(x, w_in, conv_w, conv_b, a_log, dt_bias, d_skip, norm_w, w_out):
    b, l, _ = x.shape
    z, xbc, dt = jnp.split(x @ w_in, [D_INNER, D_INNER + SSD_CONV_DIM], axis=-1)
    xbc = jax.nn.silu(causal_depthwise_conv(xbc, conv_w) + conv_b).astype(jnp.float32)
    xs, bm, cm = jnp.split(xbc, [D_INNER, D_INNER + SSD_GROUPS * SSD_STATE], axis=-1)
    xs = xs.reshape(b, l, SSD_HEADS, SSD_HEAD_DIM)
    dt = jax.nn.softplus(dt.astype(jnp.float32) + dt_bias.astype(jnp.float32))
    a = -jnp.exp(a_log.astype(jnp.float32))
    y = ssd_chunked(xs, dt, a, bm.reshape(b, l, SSD_GROUPS, SSD_STATE), cm.reshape(b, l, SSD_GROUPS, SSD_STATE))
    y = y + d_skip.astype(jnp.float32)[:, None] * xs
    y = y.reshape(b, l, D_INNER) * jax.nn.silu(z.astype(jnp.float32))
    y = rms_normalize(y.reshape(b, l, SSD_GROUPS, D_INNER // SSD_GROUPS)).reshape(b, l, D_INNER)
    return (y.astype(x.dtype) * norm_w) @ w_out


def _fwd_setup_inputs(seed: int = 0) -> dict:
    key = jax.random.key(seed)
    ks = jax.random.split(key, 24)
    na = (DEPTH + 2) // N_MIXERS
    nb = (DEPTH + 1) // N_MIXERS
    nc = DEPTH // N_MIXERS
    nrm = lambda k, shape, scale: scale * jax.random.normal(k, shape, dtype=jnp.float32)

    def dt_bias_init(k, shape):
        dt = jnp.exp(jax.random.uniform(k, shape, minval=np.log(1e-3), maxval=np.log(1e-1)))
        return dt + jnp.log(-jnp.expm1(-dt))

    def a_log_init(k, shape):
        return jnp.log(jax.random.uniform(k, shape, minval=1.0, maxval=16.0))

    out_scale = DEEPNORM_BETA * D_INNER ** -0.5
    return {
        'x': nrm(ks[0], (BATCH, SEQ, D_MODEL), 1.0),
        'gdn_w_in': nrm(ks[1], (na, D_MODEL, GDN_IN), D_MODEL ** -0.5),
        'gdn_conv_w': nrm(ks[2], (na, GDN_CONV, GDN_CONV_DIM), GDN_CONV ** -0.5),
        'gdn_a_log': a_log_init(ks[3], (na, GDN_V_HEADS)),
        'gdn_dt_bias': dt_bias_init(ks[4], (na, GDN_V_HEADS)),
        'gdn_norm_w': 1.0 + nrm(ks[5], (na, GDN_HEAD_DIM), 0.02),
        'gdn_w_out': nrm(ks[6], (na, GDN_V_DIM, D_MODEL), out_scale),
        'sc_w_in': nrm(ks[7], (nb, D_MODEL, SC_IN), D_MODEL ** -0.5),
        'sc_conv_w': nrm(ks[8], (nb, SC_CONV, SC_WIDTH), SC_CONV ** -0.5),
        'sc_w_out': nrm(ks[9], (nb, SC_WIDTH, D_MODEL), out_scale),
        'ssd_w_in': nrm(ks[10], (nc, D_MODEL, SSD_IN), D_MODEL ** -0.5),
        'ssd_conv_w': nrm(ks[11], (nc, SSD_CONV, SSD_CONV_DIM), SSD_CONV ** -0.5),
        'ssd_conv_b': nrm(ks[12], (nc, SSD_CONV_DIM), 0.02),
        'ssd_a_log': a_log_init(ks[13], (nc, SSD_HEADS)),
        'ssd_dt_bias': dt_bias_init(ks[14], (nc, SSD_HEADS)),
        'ssd_d_skip': 1.0 + nrm(ks[15], (nc, SSD_HEADS), 0.1),
        'ssd_norm_w': 1.0 + nrm(ks[16], (nc, D_INNER), 0.02),
        'ssd_w_out': nrm(ks[17], (nc, D_INNER, D_MODEL), out_scale),
        'ln_g': 1.0 + nrm(ks[18], (DEPTH, D_MODEL), 0.02),
        'ln_b': nrm(ks[19], (DEPTH, D_MODEL), 0.02),
    }


def _fwd_reference(x, gdn_w_in, gdn_conv_w, gdn_a_log, gdn_dt_bias, gdn_norm_w, gdn_w_out,
              sc_w_in, sc_conv_w, sc_w_out,
              ssd_w_in, ssd_conv_w, ssd_conv_b, ssd_a_log, ssd_dt_bias, ssd_d_skip, ssd_norm_w, ssd_w_out,
              ln_g, ln_b):
    for i in range(DEPTH):
        j = i // N_MIXERS
        kind = i % N_MIXERS
        if kind == 0:
            y = gated_deltanet_mixer(x, gdn_w_in[j], gdn_conv_w[j], gdn_a_log[j], gdn_dt_bias[j],
                                     gdn_norm_w[j], gdn_w_out[j])
        elif kind == 1:
            y = short_conv_mixer(x, sc_w_in[j], sc_conv_w[j], sc_w_out[j])
        else:
            y = mamba2_mixer(x, ssd_w_in[j], ssd_conv_w[j], ssd_conv_b[j], ssd_a_log[j], ssd_dt_bias[j],
                             ssd_d_skip[j], ssd_norm_w[j], ssd_w_out[j])
        x = layer_norm(DEEPNORM_ALPHA * x + y, ln_g[i], ln_b[i])
    return x


import jax as _jax
import jax.numpy as _jnp

TWIN_FORMAT = 'train_step'
FWD_PARAMS = ['x', 'gdn_w_in', 'gdn_conv_w', 'gdn_a_log', 'gdn_dt_bias', 'gdn_norm_w', 'gdn_w_out', 'sc_w_in', 'sc_conv_w', 'sc_w_out', 'ssd_w_in', 'ssd_conv_w', 'ssd_conv_b', 'ssd_a_log', 'ssd_dt_bias', 'ssd_d_skip', 'ssd_norm_w', 'ssd_w_out', 'ln_g', 'ln_b']
TWIN_WEIGHTS = ['gdn_w_in', 'gdn_conv_w', 'gdn_a_log', 'gdn_dt_bias', 'gdn_norm_w', 'gdn_w_out', 'sc_w_in', 'sc_conv_w', 'sc_w_out', 'ssd_w_in', 'ssd_conv_w', 'ssd_conv_b', 'ssd_a_log', 'ssd_dt_bias', 'ssd_d_skip', 'ssd_norm_w', 'ssd_w_out', 'ln_g', 'ln_b']
TWIN_DIFF_INPUT = 'x'
TWIN_INPUTS = ['x', 'gdn_w_in', 'gdn_conv_w', 'gdn_a_log', 'gdn_dt_bias', 'gdn_norm_w', 'gdn_w_out', 'sc_w_in', 'sc_conv_w', 'sc_w_out', 'ssd_w_in', 'ssd_conv_w', 'ssd_conv_b', 'ssd_a_log', 'ssd_dt_bias', 'ssd_d_skip', 'ssd_norm_w', 'ssd_w_out', 'ln_g', 'ln_b', 'loss_target', 'm_gdn_w_in', 'm_gdn_conv_w', 'm_gdn_a_log', 'm_gdn_dt_bias', 'm_gdn_norm_w', 'm_gdn_w_out', 'm_sc_w_in', 'm_sc_conv_w', 'm_sc_w_out', 'm_ssd_w_in', 'm_ssd_conv_w', 'm_ssd_conv_b', 'm_ssd_a_log', 'm_ssd_dt_bias', 'm_ssd_d_skip', 'm_ssd_norm_w', 'm_ssd_w_out', 'm_ln_g', 'm_ln_b', 'v_gdn_w_in', 'v_gdn_conv_w', 'v_gdn_a_log', 'v_gdn_dt_bias', 'v_gdn_norm_w', 'v_gdn_w_out', 'v_sc_w_in', 'v_sc_conv_w', 'v_sc_w_out', 'v_ssd_w_in', 'v_ssd_conv_w', 'v_ssd_conv_b', 'v_ssd_a_log', 'v_ssd_dt_bias', 'v_ssd_d_skip', 'v_ssd_norm_w', 'v_ssd_w_out', 'v_ln_g', 'v_ln_b']
TWIN_OUTPUTS = ['loss', 'grad_x', 'grad_gdn_w_in', 'grad_gdn_conv_w', 'grad_gdn_a_log', 'grad_gdn_dt_bias', 'grad_gdn_norm_w', 'grad_gdn_w_out', 'grad_sc_w_in', 'grad_sc_conv_w', 'grad_sc_w_out', 'grad_ssd_w_in', 'grad_ssd_conv_w', 'grad_ssd_conv_b', 'grad_ssd_a_log', 'grad_ssd_dt_bias', 'grad_ssd_d_skip', 'grad_ssd_norm_w', 'grad_ssd_w_out', 'grad_ln_g', 'grad_ln_b', 'delta_gdn_w_in', 'delta_gdn_conv_w', 'delta_gdn_a_log', 'delta_gdn_dt_bias', 'delta_gdn_norm_w', 'delta_gdn_w_out', 'delta_sc_w_in', 'delta_sc_conv_w', 'delta_sc_w_out', 'delta_ssd_w_in', 'delta_ssd_conv_w', 'delta_ssd_conv_b', 'delta_ssd_a_log', 'delta_ssd_dt_bias', 'delta_ssd_d_skip', 'delta_ssd_norm_w', 'delta_ssd_w_out', 'delta_ln_g', 'delta_ln_b', 'new_m_gdn_w_in', 'new_m_gdn_conv_w', 'new_m_gdn_a_log', 'new_m_gdn_dt_bias', 'new_m_gdn_norm_w', 'new_m_gdn_w_out', 'new_m_sc_w_in', 'new_m_sc_conv_w', 'new_m_sc_w_out', 'new_m_ssd_w_in', 'new_m_ssd_conv_w', 'new_m_ssd_conv_b', 'new_m_ssd_a_log', 'new_m_ssd_dt_bias', 'new_m_ssd_d_skip', 'new_m_ssd_norm_w', 'new_m_ssd_w_out', 'new_m_ln_g', 'new_m_ln_b', 'new_v_gdn_w_in', 'new_v_gdn_conv_w', 'new_v_gdn_a_log', 'new_v_gdn_dt_bias', 'new_v_gdn_norm_w', 'new_v_gdn_w_out', 'new_v_sc_w_in', 'new_v_sc_conv_w', 'new_v_sc_w_out', 'new_v_ssd_w_in', 'new_v_ssd_conv_w', 'new_v_ssd_conv_b', 'new_v_ssd_a_log', 'new_v_ssd_dt_bias', 'new_v_ssd_d_skip', 'new_v_ssd_norm_w', 'new_v_ssd_w_out', 'new_v_ln_g', 'new_v_ln_b']
TWIN_LEAF_KINDS = {'loss': 'loss', 'grad_x': 'grad_x', 'grad_gdn_w_in': 'grad_w', 'grad_gdn_conv_w': 'grad_w', 'grad_gdn_a_log': 'grad_w', 'grad_gdn_dt_bias': 'grad_w', 'grad_gdn_norm_w': 'grad_w', 'grad_gdn_w_out': 'grad_w', 'grad_sc_w_in': 'grad_w', 'grad_sc_conv_w': 'grad_w', 'grad_sc_w_out': 'grad_w', 'grad_ssd_w_in': 'grad_w', 'grad_ssd_conv_w': 'grad_w', 'grad_ssd_conv_b': 'grad_w', 'grad_ssd_a_log': 'grad_w', 'grad_ssd_dt_bias': 'grad_w', 'grad_ssd_d_skip': 'grad_w', 'grad_ssd_norm_w': 'grad_w', 'grad_ssd_w_out': 'grad_w', 'grad_ln_g': 'grad_w', 'grad_ln_b': 'grad_w', 'delta_gdn_w_in': 'delta_w', 'delta_gdn_conv_w': 'delta_w', 'delta_gdn_a_log': 'delta_w', 'delta_gdn_dt_bias': 'delta_w', 'delta_gdn_norm_w': 'delta_w', 'delta_gdn_w_out': 'delta_w', 'delta_sc_w_in': 'delta_w', 'delta_sc_conv_w': 'delta_w', 'delta_sc_w_out': 'delta_w', 'delta_ssd_w_in': 'delta_w', 'delta_ssd_conv_w': 'delta_w', 'delta_ssd_conv_b': 'delta_w', 'delta_ssd_a_log': 'delta_w', 'delta_ssd_dt_bias': 'delta_w', 'delta_ssd_d_skip': 'delta_w', 'delta_ssd_norm_w': 'delta_w', 'delta_ssd_w_out': 'delta_w', 'delta_ln_g': 'delta_w', 'delta_ln_b': 'delta_w', 'new_m_gdn_w_in': 'new_m', 'new_m_gdn_conv_w': 'new_m', 'new_m_gdn_a_log': 'new_m', 'new_m_gdn_dt_bias': 'new_m', 'new_m_gdn_norm_w': 'new_m', 'new_m_gdn_w_out': 'new_m', 'new_m_sc_w_in': 'new_m', 'new_m_sc_conv_w': 'new_m', 'new_m_sc_w_out': 'new_m', 'new_m_ssd_w_in': 'new_m', 'new_m_ssd_conv_w': 'new_m', 'new_m_ssd_conv_b': 'new_m', 'new_m_ssd_a_log': 'new_m', 'new_m_ssd_dt_bias': 'new_m', 'new_m_ssd_d_skip': 'new_m', 'new_m_ssd_norm_w': 'new_m', 'new_m_ssd_w_out': 'new_m', 'new_m_ln_g': 'new_m', 'new_m_ln_b': 'new_m', 'new_v_gdn_w_in': 'new_v', 'new_v_gdn_conv_w': 'new_v', 'new_v_gdn_a_log': 'new_v', 'new_v_gdn_dt_bias': 'new_v', 'new_v_gdn_norm_w': 'new_v', 'new_v_gdn_w_out': 'new_v', 'new_v_sc_w_in': 'new_v', 'new_v_sc_conv_w': 'new_v', 'new_v_sc_w_out': 'new_v', 'new_v_ssd_w_in': 'new_v', 'new_v_ssd_conv_w': 'new_v', 'new_v_ssd_conv_b': 'new_v', 'new_v_ssd_a_log': 'new_v', 'new_v_ssd_dt_bias': 'new_v', 'new_v_ssd_d_skip': 'new_v', 'new_v_ssd_norm_w': 'new_v', 'new_v_ssd_w_out': 'new_v', 'new_v_ln_g': 'new_v', 'new_v_ln_b': 'new_v'}


def _forward(args):
    return _fwd_reference(*[args[k] for k in FWD_PARAMS])


def _output_shape():
    def fwd():
        inp = _fwd_setup_inputs(0)
        return _fwd_reference(*[inp[k] for k in FWD_PARAMS])
    out = _jax.eval_shape(fwd)
    return out.shape, out.dtype

N_MICROBATCH = 1
ADAM_LR = 0.001
ADAM_B1 = 0.9
ADAM_B2 = 0.999
ADAM_EPS = 1e-08
ADAM_WD = 0.01
ADAM_STEP = 10
PER_EXAMPLE_BATCH_AXIS = {'x': 0, 'loss_target': 0}
SHARED_INPUTS = []
_WEIGHT_DTYPES = {'gdn_w_in': _jnp.float32, 'gdn_conv_w': _jnp.float32, 'gdn_a_log': _jnp.float32, 'gdn_dt_bias': _jnp.float32, 'gdn_norm_w': _jnp.float32, 'gdn_w_out': _jnp.float32, 'sc_w_in': _jnp.float32, 'sc_conv_w': _jnp.float32, 'sc_w_out': _jnp.float32, 'ssd_w_in': _jnp.float32, 'ssd_conv_w': _jnp.float32, 'ssd_conv_b': _jnp.float32, 'ssd_a_log': _jnp.float32, 'ssd_dt_bias': _jnp.float32, 'ssd_d_skip': _jnp.float32, 'ssd_norm_w': _jnp.float32, 'ssd_w_out': _jnp.float32, 'ln_g': _jnp.float32, 'ln_b': _jnp.float32}
MOMENT_SCALE = {'gdn_w_in': 3.695659e-02, 'gdn_conv_w': 3.815714e-02, 'gdn_a_log': 1.415225e-01, 'gdn_dt_bias': 1.374010e-01, 'gdn_norm_w': 1.940316e-01, 'gdn_w_out': 1.533584e-01, 'sc_w_in': 3.902286e-02, 'sc_conv_w': 4.024461e-02, 'sc_w_out': 1.303510e-01, 'ssd_w_in': 5.873745e-02, 'ssd_conv_w': 5.562881e-02, 'ssd_conv_b': 9.924918e-02, 'ssd_a_log': 1.746797e-01, 'ssd_dt_bias': 2.425570e-01, 'ssd_d_skip': 3.860005e-01, 'ssd_norm_w': 6.971619e-02, 'ssd_w_out': 2.427102e-01, 'ln_g': 6.414341e+01, 'ln_b': 4.447718e+00}


def _to_microbatches(a, axis):
    t = _jnp.moveaxis(a, axis, 0)
    t = t.reshape((N_MICROBATCH, t.shape[0] // N_MICROBATCH) + t.shape[1:])
    return _jnp.moveaxis(t, 1, axis + 1)


def setup_inputs(seed: int = 0) -> dict:
    inp = _fwd_setup_inputs(seed)
    key = _jax.random.fold_in(_jax.random.key(seed), 7919)
    shape, _ = _output_shape()
    out = dict(inp)
    out["loss_target"] = _jax.random.normal(_jax.random.fold_in(key, 0), shape, _jnp.float32)
    for i, name in enumerate(TWIN_WEIGHTS):
        w = inp[name].astype(_jnp.float32)
        if MOMENT_SCALE is None:
            s = _jnp.sqrt(_jnp.mean(_jnp.square(w)) + 1e-30)
        else:
            s = MOMENT_SCALE[name]
        km, kv = _jax.random.split(_jax.random.fold_in(key, i + 1))
        out[name] = w
        out["m_" + name] = s * _jax.random.normal(km, w.shape, _jnp.float32)
        out["v_" + name] = (s * s) * _jax.random.uniform(kv, w.shape, _jnp.float32, 0.5, 1.5)
    if N_MICROBATCH > 1:
        for name, axis in PER_EXAMPLE_BATCH_AXIS.items():
            out[name] = _to_microbatches(out[name], axis)
    return {'x': out['x'], 'gdn_w_in': out['gdn_w_in'], 'gdn_conv_w': out['gdn_conv_w'], 'gdn_a_log': out['gdn_a_log'], 'gdn_dt_bias': out['gdn_dt_bias'], 'gdn_norm_w': out['gdn_norm_w'], 'gdn_w_out': out['gdn_w_out'], 'sc_w_in': out['sc_w_in'], 'sc_conv_w': out['sc_conv_w'], 'sc_w_out': out['sc_w_out'], 'ssd_w_in': out['ssd_w_in'], 'ssd_conv_w': out['ssd_conv_w'], 'ssd_conv_b': out['ssd_conv_b'], 'ssd_a_log': out['ssd_a_log'], 'ssd_dt_bias': out['ssd_dt_bias'], 'ssd_d_skip': out['ssd_d_skip'], 'ssd_norm_w': out['ssd_norm_w'], 'ssd_w_out': out['ssd_w_out'], 'ln_g': out['ln_g'], 'ln_b': out['ln_b'], 'loss_target': out['loss_target'], 'm_gdn_w_in': out['m_gdn_w_in'], 'm_gdn_conv_w': out['m_gdn_conv_w'], 'm_gdn_a_log': out['m_gdn_a_log'], 'm_gdn_dt_bias': out['m_gdn_dt_bias'], 'm_gdn_norm_w': out['m_gdn_norm_w'], 'm_gdn_w_out': out['m_gdn_w_out'], 'm_sc_w_in': out['m_sc_w_in'], 'm_sc_conv_w': out['m_sc_conv_w'], 'm_sc_w_out': out['m_sc_w_out'], 'm_ssd_w_in': out['m_ssd_w_in'], 'm_ssd_conv_w': out['m_ssd_conv_w'], 'm_ssd_conv_b': out['m_ssd_conv_b'], 'm_ssd_a_log': out['m_ssd_a_log'], 'm_ssd_dt_bias': out['m_ssd_dt_bias'], 'm_ssd_d_skip': out['m_ssd_d_skip'], 'm_ssd_norm_w': out['m_ssd_norm_w'], 'm_ssd_w_out': out['m_ssd_w_out'], 'm_ln_g': out['m_ln_g'], 'm_ln_b': out['m_ln_b'], 'v_gdn_w_in': out['v_gdn_w_in'], 'v_gdn_conv_w': out['v_gdn_conv_w'], 'v_gdn_a_log': out['v_gdn_a_log'], 'v_gdn_dt_bias': out['v_gdn_dt_bias'], 'v_gdn_norm_w': out['v_gdn_norm_w'], 'v_gdn_w_out': out['v_gdn_w_out'], 'v_sc_w_in': out['v_sc_w_in'], 'v_sc_conv_w': out['v_sc_conv_w'], 'v_sc_w_out': out['v_sc_w_out'], 'v_ssd_w_in': out['v_ssd_w_in'], 'v_ssd_conv_w': out['v_ssd_conv_w'], 'v_ssd_conv_b': out['v_ssd_conv_b'], 'v_ssd_a_log': out['v_ssd_a_log'], 'v_ssd_dt_bias': out['v_ssd_dt_bias'], 'v_ssd_d_skip': out['v_ssd_d_skip'], 'v_ssd_norm_w': out['v_ssd_norm_w'], 'v_ssd_w_out': out['v_ssd_w_out'], 'v_ln_g': out['v_ln_g'], 'v_ln_b': out['v_ln_b']}


def _loss(weights, diff, rest, loss_target):
    with _jax.named_scope("forward"):
        args = {**rest, TWIN_DIFF_INPUT: diff, **{k: w.astype(_WEIGHT_DTYPES[k]) for k, w in weights.items()}}
        y = _forward(args)
    with _jax.named_scope("loss_head"):
        err = _jnp.square(y.astype(_jnp.float32) - loss_target)
        return 0.5 * _jnp.sum(_jnp.mean(err, axis=-1)) if err.ndim else 0.5 * err


def _adamw(w, g, m, v):
    m = ADAM_B1 * m + (1.0 - ADAM_B1) * g
    v = ADAM_B2 * v + (1.0 - ADAM_B2) * _jnp.square(g)
    m_hat = m / (1.0 - ADAM_B1 ** ADAM_STEP)
    v_hat = v / (1.0 - ADAM_B2 ** ADAM_STEP)
    delta = -ADAM_LR * (m_hat / (_jnp.sqrt(v_hat) + ADAM_EPS) + ADAM_WD * w)
    return delta, m, v


def reference(x, gdn_w_in, gdn_conv_w, gdn_a_log, gdn_dt_bias, gdn_norm_w, gdn_w_out, sc_w_in, sc_conv_w, sc_w_out, ssd_w_in, ssd_conv_w, ssd_conv_b, ssd_a_log, ssd_dt_bias, ssd_d_skip, ssd_norm_w, ssd_w_out, ln_g, ln_b, loss_target, m_gdn_w_in, m_gdn_conv_w, m_gdn_a_log, m_gdn_dt_bias, m_gdn_norm_w, m_gdn_w_out, m_sc_w_in, m_sc_conv_w, m_sc_w_out, m_ssd_w_in, m_ssd_conv_w, m_ssd_conv_b, m_ssd_a_log, m_ssd_dt_bias, m_ssd_d_skip, m_ssd_norm_w, m_ssd_w_out, m_ln_g, m_ln_b, v_gdn_w_in, v_gdn_conv_w, v_gdn_a_log, v_gdn_dt_bias, v_gdn_norm_w, v_gdn_w_out, v_sc_w_in, v_sc_conv_w, v_sc_w_out, v_ssd_w_in, v_ssd_conv_w, v_ssd_conv_b, v_ssd_a_log, v_ssd_dt_bias, v_ssd_d_skip, v_ssd_norm_w, v_ssd_w_out, v_ln_g, v_ln_b):
    given = dict(x=x, gdn_w_in=gdn_w_in, gdn_conv_w=gdn_conv_w, gdn_a_log=gdn_a_log, gdn_dt_bias=gdn_dt_bias, gdn_norm_w=gdn_norm_w, gdn_w_out=gdn_w_out, sc_w_in=sc_w_in, sc_conv_w=sc_conv_w, sc_w_out=sc_w_out, ssd_w_in=ssd_w_in, ssd_conv_w=ssd_conv_w, ssd_conv_b=ssd_conv_b, ssd_a_log=ssd_a_log, ssd_dt_bias=ssd_dt_bias, ssd_d_skip=ssd_d_skip, ssd_norm_w=ssd_norm_w, ssd_w_out=ssd_w_out, ln_g=ln_g, ln_b=ln_b, loss_target=loss_target, m_gdn_w_in=m_gdn_w_in, m_gdn_conv_w=m_gdn_conv_w, m_gdn_a_log=m_gdn_a_log, m_gdn_dt_bias=m_gdn_dt_bias, m_gdn_norm_w=m_gdn_norm_w, m_gdn_w_out=m_gdn_w_out, m_sc_w_in=m_sc_w_in, m_sc_conv_w=m_sc_conv_w, m_sc_w_out=m_sc_w_out, m_ssd_w_in=m_ssd_w_in, m_ssd_conv_w=m_ssd_conv_w, m_ssd_conv_b=m_ssd_conv_b, m_ssd_a_log=m_ssd_a_log, m_ssd_dt_bias=m_ssd_dt_bias, m_ssd_d_skip=m_ssd_d_skip, m_ssd_norm_w=m_ssd_norm_w, m_ssd_w_out=m_ssd_w_out, m_ln_g=m_ln_g, m_ln_b=m_ln_b, v_gdn_w_in=v_gdn_w_in, v_gdn_conv_w=v_gdn_conv_w, v_gdn_a_log=v_gdn_a_log, v_gdn_dt_bias=v_gdn_dt_bias, v_gdn_norm_w=v_gdn_norm_w, v_gdn_w_out=v_gdn_w_out, v_sc_w_in=v_sc_w_in, v_sc_conv_w=v_sc_conv_w, v_sc_w_out=v_sc_w_out, v_ssd_w_in=v_ssd_w_in, v_ssd_conv_w=v_ssd_conv_w, v_ssd_conv_b=v_ssd_conv_b, v_ssd_a_log=v_ssd_a_log, v_ssd_dt_bias=v_ssd_dt_bias, v_ssd_d_skip=v_ssd_d_skip, v_ssd_norm_w=v_ssd_norm_w, v_ssd_w_out=v_ssd_w_out, v_ln_g=v_ln_g, v_ln_b=v_ln_b)
    weights = {n: given[n] for n in TWIN_WEIGHTS}
    shared = {n: given[n] for n in SHARED_INPUTS}
    per_example = {n: given[n] for n in ['x']}
    grad_fn = _jax.value_and_grad(_loss, argnums=(0, 1))

    def one_microbatch(ex, loss_target):
        ex = dict(ex)
        diff = ex.pop(TWIN_DIFF_INPUT)
        return grad_fn(weights, diff, {**shared, **ex}, loss_target)

    if N_MICROBATCH == 1:
        loss, (grad_w, grad_x) = one_microbatch(per_example, given["loss_target"])
    else:
        def body(carry, xs):
            loss_sum, grad_sum = carry
            l_k, (gw_k, gx_k) = one_microbatch(xs[0], xs[1])
            with _jax.named_scope("update"):
                return (loss_sum + l_k, _jax.tree.map(_jnp.add, grad_sum, gw_k)), gx_k

        init = (_jnp.zeros((), _jnp.float32), _jax.tree.map(_jnp.zeros_like, weights))
        (loss, grad_w), grad_x = _jax.lax.scan(body, init, (per_example, given["loss_target"]))
    with _jax.named_scope("update"):
        delta_w, new_m, new_v = {}, {}, {}
        for n in TWIN_WEIGHTS:
            delta_w[n], new_m[n], new_v[n] = _adamw(weights[n], grad_w[n], given["m_" + n], given["v_" + n])
    return (loss, grad_x, *[grad_w[n] for n in TWIN_WEIGHTS], *[delta_w[n] for n in TWIN_WEIGHTS],
            *[new_m[n] for n in TWIN_WEIGHTS], *[new_v[n] for n in TWIN_WEIGHTS])
```

```python
import functools
import math

import jax
import jax.numpy as jnp
from jax import lax
from jax.experimental import pallas as pl
from jax.experimental.pallas import tpu as pltpu

F32 = jnp.float32
MM_DTYPE = jnp.bfloat16

N_DEV = 8
D_MODEL = 1024
D_INNER = 2048
CHUNK = 64
HEAD = 128
GDN_V_HEADS = 16
GDN_QK_HEADS = 8
GDN_QK_DIM = 1024
GDN_CONV_DIM = 4096
SSD_PAIRS = 16
SSD_GROUPS = 4
SSD_STATE = 128
SSD_CONV_DIM = 3072
DEPTH = 4
ALPHA = (2 * DEPTH) ** 0.25
RMS_EPS = 1e-6
LN_EPS = 1e-5
L2_EPS = 1e-6
ADAM_LR, ADAM_B1, ADAM_B2, ADAM_EPS, ADAM_WD, ADAM_STEP = 0.001, 0.9, 0.999, 1e-08, 0.01, 10

VMEM_LIMIT_BYTES = 48 * 1024 * 1024
NEG_BIG = -1e30

_NN = (((1,), (0,)), ((), ()))
_NT = (((1,), (1,)), ((), ()))
_TN = (((0,), (0,)), ((), ()))


def _mm(a, b, dims):
    return lax.dot_general(a.astype(MM_DTYPE), b.astype(MM_DTYPE), dims, preferred_element_type=F32)


def _mmx(a, b, dims):
    return lax.dot_general(a, b, dims, precision=lax.Precision.HIGHEST, preferred_element_type=F32)


def _iota(shape, dim):
    return lax.broadcasted_iota(jnp.int32, shape, dim)


def _eye(n):
    return (_iota((n, n), 0) == _iota((n, n), 1)).astype(F32)


def _sig(x):
    return jax.nn.sigmoid(x)


def _silu(x):
    return x * _sig(x)


def _dsilu(x):
    s = _sig(x)
    return s * (1.0 + x * (1.0 - s))


def _softplus(x):
    return jnp.maximum(x, 0.0) + jnp.log(1.0 + jnp.exp(-jnp.abs(x)))


def _col(x, h):
    return jnp.sum(jnp.where(_iota(x.shape, 1) == h, x, 0.0), axis=1, keepdims=True)


def _row(x, h):
    return jnp.sum(jnp.where(_iota(x.shape, 0) == h, x, 0.0), axis=0, keepdims=True)


def _put_col(acc, col, h):
    return jnp.where(_iota(acc.shape, 1) == h, col, acc)


def _put_row(acc, row, h):
    return jnp.where(_iota(acc.shape, 0) == h, row, acc)


def _put_sub(acc, row, j):
    return acc + jnp.where(_iota(acc.shape, 0) == j, row, 0.0)


def _lanes(h):
    return pl.ds(h * HEAD, HEAD) if isinstance(h, int) else pl.ds(pl.multiple_of(h * HEAD, HEAD), HEAD)


def _total(x):
    return jnp.sum(jnp.sum(x, axis=1, keepdims=True), axis=0, keepdims=True)


def _call(body, name, grid, in_specs, out_specs, out_shape, scratch_shapes=(), semantics=None):
    return pl.pallas_call(
        body, name=name, grid=grid, in_specs=in_specs, out_specs=out_specs, out_shape=out_shape,
        scratch_shapes=list(scratch_shapes),
        compiler_params=pltpu.CompilerParams(dimension_semantics=semantics, vmem_limit_bytes=VMEM_LIMIT_BYTES))


def _tile(n, pref):
    if n <= pref:
        return n
    t = pref
    while n % t:
        t -= 128
    return t


def _exchange(src, name, slabs):
    shape = src.shape[1:] if slabs else src.shape

    def body(src_ref, out_ref, send_sems, recv_sems, local_sem):
        x, y, c = lax.axis_index("x"), lax.axis_index("y"), lax.axis_index("c")
        me = 4 * x + 2 * y + c
        mine = src_ref.at[me] if slabs else src_ref
        local = pltpu.make_async_copy(mine, out_ref.at[me], local_sem)
        local.start()
        sends = []
        for r in range(1, N_DEV):
            px = 1 - x if r & 4 else x
            py = 1 - y if r & 2 else y
            pc = 1 - c if r & 1 else c
            peer = 4 * px + 2 * py + pc
            cp = pltpu.make_async_remote_copy(
                src_ref=src_ref.at[peer] if slabs else src_ref, dst_ref=out_ref.at[me],
                send_sem=send_sems.at[r - 1], recv_sem=recv_sems.at[r - 1],
                device_id=(px, py, pc), device_id_type=pl.DeviceIdType.MESH)
            cp.start()
            sends.append((cp, peer, (px, py, pc)))
        for r, (cp, peer, pid) in enumerate(sends):
            pltpu.make_async_remote_copy(
                src_ref=mine, dst_ref=out_ref.at[peer], send_sem=send_sems.at[r], recv_sem=recv_sems.at[r],
                device_id=pid, device_id_type=pl.DeviceIdType.MESH).wait_recv()
        for cp, _, _ in sends:
            cp.wait_send()
        local.wait()

    return pl.pallas_call(
        body, name=name,
        in_specs=[pl.BlockSpec(memory_space=pl.ANY)], out_specs=pl.BlockSpec(memory_space=pl.ANY),
        out_shape=jax.ShapeDtypeStruct((N_DEV,) + tuple(shape), src.dtype),
        scratch_shapes=[pltpu.SemaphoreType.DMA((N_DEV - 1,)), pltpu.SemaphoreType.DMA((N_DEV - 1,)),
                        pltpu.SemaphoreType.DMA(())],
    )(src)


def _matmul(a, b, mode, name, add=None, add_scale=1.0):
    if mode == "nn":
        (m, k), (_, n) = a.shape, b.shape
    elif mode == "nt":
        (m, k), (n, _) = a.shape, b.shape
    else:
        (k, m), (_, n) = a.shape, b.shape
    tm, tn, tk = _tile(m, 512), _tile(n, 1024), _tile(k, 1024)
    nk = k // tk
    dims = {"nn": _NN, "nt": _NT, "tn": _TN}[mode]

    def body(*refs):
        if add is None:
            a_ref, b_ref, o_ref, acc = refs
        else:
            a_ref, b_ref, r_ref, o_ref, acc = refs
        kk = pl.program_id(2)

        @pl.when(kk == 0)
        def _():
            acc[...] = jnp.zeros_like(acc)

        acc[...] += _mm(a_ref[...], b_ref[...], dims)

        @pl.when(kk == nk - 1)
        def _():
            if add is None:
                o_ref[...] = acc[...]
            else:
                o_ref[...] = acc[...] + add_scale * r_ref[...]

    if mode == "nn":
        specs = [pl.BlockSpec((tm, tk), lambda i, j, q: (i, q)), pl.BlockSpec((tk, tn), lambda i, j, q: (q, j))]
    elif mode == "nt":
        specs = [pl.BlockSpec((tm, tk), lambda i, j, q: (i, q)), pl.BlockSpec((tn, tk), lambda i, j, q: (j, q))]
    else:
        specs = [pl.BlockSpec((tk, tm), lambda i, j, q: (q, i)), pl.BlockSpec((tk, tn), lambda i, j, q: (q, j))]
    out_spec = pl.BlockSpec((tm, tn), lambda i, j, q: (i, j))
    args = [a, b]
    if add is not None:
        specs.append(out_spec)
        args.append(add)
    return _call(body, name, (m // tm, n // tn, nk), specs, out_spec, jax.ShapeDtypeStruct((m, n), F32),
                 [pltpu.VMEM((tm, tn), F32)], ("parallel", "parallel", "arbitrary"))(*args)


CONV_TB = 512
CONV_CB = 512
HALO = 8


def _conv_specs(t, cb_n):
    tb = min(CONV_TB, t)
    nb = tb // HALO
    blk = pl.BlockSpec((tb, cb_n), lambda c, i: (i, c))
    prev = pl.BlockSpec((HALO, cb_n), lambda c, i: (jnp.maximum(i * nb - 1, 0), c))
    nxt = pl.BlockSpec((HALO, cb_n), lambda c, i: (jnp.minimum((i + 1) * nb, t // HALO - 1), c))
    w = pl.BlockSpec((8, cb_n), lambda c, i: (0, c))
    return blk, prev, nxt, w


def _shift_down(ext, s, tb):
    return (pltpu.roll(ext, s, 0) if s else ext)[HALO:HALO + tb]


def _shift_up(ext, s, tb):
    n = ext.shape[0]
    return (pltpu.roll(ext, n - s, 0) if s else ext)[0:tb]


def _conv_fwd(u, w8, ktaps, name, u2=None, bias=False):
    t, ch = u.shape
    cb_n = min(CONV_CB, ch)
    tb = min(CONV_TB, t)
    two = u2 is not None

    def body(*refs):
        if two:
            u_ref, up_ref, v_ref, vp_ref, w_ref, o_ref = refs
        else:
            u_ref, up_ref, w_ref, o_ref = refs
        first = pl.program_id(1) == 0
        blk, halo = u_ref[...], up_ref[...]
        if two:
            blk, halo = blk * v_ref[...], halo * vp_ref[...]
        ext = jnp.concatenate([jnp.where(first, 0.0, halo), blk], axis=0)
        acc = jnp.zeros((tb, cb_n), F32)
        for j in range(ktaps):
            acc = acc + w_ref[j:j + 1, :] * _shift_down(ext, ktaps - 1 - j, tb)
        if bias:
            acc = acc + w_ref[ktaps:ktaps + 1, :]
        o_ref[...] = acc

    blk, prev, _, wspec = _conv_specs(t, cb_n)
    specs, args = [blk, prev], [u, u]
    if two:
        specs += [blk, prev]
        args += [u2, u2]
    specs.append(wspec)
    args.append(w8)
    return _call(body, name, (ch // cb_n, t // tb), specs, blk, jax.ShapeDtypeStruct((t, ch), F32),
                 semantics=("parallel", "parallel"))(*args)


def _conv_bwd(dc, u, w8, ktaps, name, u2=None):
    t, ch = u.shape
    cb_n = min(CONV_CB, ch)
    tb = min(CONV_TB, t)
    two = u2 is not None

    def body(*refs):
        if two:
            dc_ref, dn_ref, u_ref, up_ref, v_ref, vp_ref, w_ref, du_ref, dv_ref, dw_ref = refs
        else:
            dc_ref, dn_ref, u_ref, up_ref, w_ref, du_ref, dw_ref = refs
        i = pl.program_id(1)
        first, last = i == 0, i == t // tb - 1
        d = dc_ref[...]
        dext = jnp.concatenate([d, jnp.where(last, 0.0, dn_ref[...])], axis=0)
        blk, halo = u_ref[...], up_ref[...]
        if two:
            blk, halo = blk * v_ref[...], halo * vp_ref[...]
        uext = jnp.concatenate([jnp.where(first, 0.0, halo), blk], axis=0)
        du = jnp.zeros((tb, cb_n), F32)
        dw = jnp.zeros((8, cb_n), F32)
        for j in range(ktaps):
            s = ktaps - 1 - j
            du = du + w_ref[j:j + 1, :] * _shift_up(dext, s, tb)
            dw = _put_sub(dw, jnp.sum(d * _shift_down(uext, s, tb), axis=0, keepdims=True), j)
        dw = _put_sub(dw, jnp.sum(d, axis=0, keepdims=True), ktaps)
        if two:
            du_ref[...] = du * v_ref[...]
            dv_ref[...] = du * u_ref[...]
        else:
            du_ref[...] = du

        @pl.when(first)
        def _():
            dw_ref[...] = jnp.zeros_like(dw_ref)

        dw_ref[...] += dw

    blk, prev, nxt, wspec = _conv_specs(t, cb_n)
    specs, args = [blk, nxt, blk, prev], [dc, dc, u, u]
    if two:
        specs += [blk, prev]
        args += [u2, u2]
    specs.append(wspec)
    args.append(w8)
    act = jax.ShapeDtypeStruct((t, ch), F32)
    outs = ([blk, blk, wspec], [act, act, jax.ShapeDtypeStruct((8, ch), F32)]) if two else \
        ([blk, wspec], [act, jax.ShapeDtypeStruct((8, ch), F32)])
    return _call(body, name, (ch // cb_n, t // tb), specs, outs[0], outs[1],
                 semantics=("parallel", "arbitrary"))(*args)


EW_TB = 256


def _chunk_mask(n, upper):
    i, j = _iota((n, n), 0), _iota((n, n), 1)
    same = jnp.right_shift(i, 6) == jnp.right_shift(j, 6)
    return (same & ((j >= i) if upper else (i >= j))).astype(F32)


def _rows(width, tb=EW_TB):
    return pl.BlockSpec((tb, width), lambda i: (i, 0))


def _const(rows, width):
    return pl.BlockSpec((rows, width), lambda i: (0, 0))


def _gdn_ew_fwd(c, ba, a_log, dt_bias, name):
    t = c.shape[0]
    tb = EW_TB

    def body(c_ref, ba_ref, al_ref, db_ref, q_ref, k_ref, v_ref, beta_ref, gc_ref):
        for h in range(GDN_QK_HEADS):
            for base, ref, scale in ((0, q_ref, HEAD ** -0.5), (GDN_QK_DIM, k_ref, 1.0)):
                s = _silu(c_ref[:, base + h * HEAD: base + (h + 1) * HEAD])
                r = lax.rsqrt(jnp.sum(s * s, axis=1, keepdims=True) + L2_EPS)
                ref[:, h * HEAD:(h + 1) * HEAD] = s * (r * scale)
        v_ref[...] = _silu(c_ref[:, 2 * GDN_QK_DIM:])
        beta_ref[...] = _sig(ba_ref[:, :HEAD])
        g = -jnp.exp(al_ref[...]) * _softplus(ba_ref[:, HEAD:] + db_ref[...])
        gc_ref[...] = _mmx(_chunk_mask(tb, False), g, _NN)

    act = lambda w: jax.ShapeDtypeStruct((t, w), F32)
    return _call(body, name, (t // tb,),
                 [_rows(GDN_CONV_DIM), _rows(2 * HEAD), _const(1, HEAD), _const(1, HEAD)],
                 [_rows(GDN_QK_DIM), _rows(GDN_QK_DIM), _rows(D_INNER), _rows(HEAD), _rows(HEAD)],
                 [act(GDN_QK_DIM), act(GDN_QK_DIM), act(D_INNER), act(HEAD), act(HEAD)],
                 semantics=("parallel",))(c, ba, a_log, dt_bias)


def _gdn_ew_bwd(c, ba, a_log, dt_bias, dqh, dkh, dv, dbeta, dgc, name):
    t = c.shape[0]
    tb = EW_TB

    def body(c_ref, ba_ref, al_ref, db_ref, dq_ref, dk_ref, dv_ref, dbeta_ref, dgc_ref, dc_ref, dba_ref, acc_ref):
        for h in range(GDN_QK_HEADS):
            for base, ref, scale in ((0, dq_ref, HEAD ** -0.5), (GDN_QK_DIM, dk_ref, 1.0)):
                cq = c_ref[:, base + h * HEAD: base + (h + 1) * HEAD]
                s = _silu(cq)
                r = lax.rsqrt(jnp.sum(s * s, axis=1, keepdims=True) + L2_EPS)
                dn = (ref[:, 2 * h * HEAD:(2 * h + 1) * HEAD] + ref[:, (2 * h + 1) * HEAD:(2 * h + 2) * HEAD]) * scale
                ds = r * dn - s * (r * r * r) * jnp.sum(dn * s, axis=1, keepdims=True)
                dc_ref[:, base + h * HEAD: base + (h + 1) * HEAD] = ds * _dsilu(cq)
        dc_ref[:, 2 * GDN_QK_DIM:] = dv_ref[...] * _dsilu(c_ref[:, 2 * GDN_QK_DIM:])
        beta = _sig(ba_ref[:, :HEAD])
        dba_ref[:, :HEAD] = dbeta_ref[...] * beta * (1.0 - beta)
        pre = ba_ref[:, HEAD:] + db_ref[...]
        ea = jnp.exp(al_ref[...])
        g = -ea * _softplus(pre)
        dg = _mmx(_chunk_mask(tb, True), dgc_ref[...], _NN)
        da_raw = dg * (-ea) * _sig(pre)
        dba_ref[:, HEAD:] = da_raw
        acc = jnp.zeros((8, HEAD), F32)
        acc = _put_sub(acc, jnp.sum(dg * g, axis=0, keepdims=True), 0)
        acc = _put_sub(acc, jnp.sum(da_raw, axis=0, keepdims=True), 1)

        @pl.when(pl.program_id(0) == 0)
        def _():
            acc_ref[...] = jnp.zeros_like(acc_ref)

        acc_ref[...] += acc

    act = lambda w: jax.ShapeDtypeStruct((t, w), F32)
    return _call(body, name, (t // tb,),
                 [_rows(GDN_CONV_DIM), _rows(2 * HEAD), _const(1, HEAD), _const(1, HEAD),
                  _rows(D_INNER), _rows(D_INNER), _rows(D_INNER), _rows(HEAD), _rows(HEAD)],
                 [_rows(GDN_CONV_DIM), _rows(2 * HEAD), _const(8, HEAD)],
                 [act(GDN_CONV_DIM), act(2 * HEAD), jax.ShapeDtypeStruct((8, HEAD), F32)],
                 semantics=("arbitrary",))(c, ba, a_log, dt_bias, dqh, dkh, dv, dbeta, dgc)


def _unit_lower_inverse(a):
    inv = _eye(CHUNK) - a
    p = a
    for _ in range(5):
        p = _mm(p, p, _NN)
        inv = inv + _mm(inv, p, _NN)
    return inv


def _gdn_head_fwd(q, k, v, bcol, gcol, grow, glast, s_in):
    ii, jj = _iota((CHUNK, CHUNK), 0), _iota((CHUNK, CHUNK), 1)
    eg = jnp.exp(gcol)
    decay = jnp.exp(jnp.where(ii >= jj, gcol - grow, NEG_BIG))
    kb = k * bcol
    p = _mm(kb, k, _NT)
    a = jnp.where(ii > jj, p * decay, 0.0)
    inv = _unit_lower_inverse(a)
    rv, rk = v * bcol, kb * eg
    u, w = _mm(inv, rv, _NN), _mm(inv, rk, _NN)
    vn = u - _mm(w, s_in, _NN)
    qk = _mm(q, k, _NT)
    att = qk * decay
    qd = q * eg
    out = _mm(qd, s_in, _NN) + _mm(att, vn, _NN)
    ekt = jnp.exp(glast - gcol)
    kt = k * ekt
    el = jnp.exp(glast)
    s_out = s_in * el + _mm(kt, vn, _TN)
    return dict(eg=eg, decay=decay, kb=kb, p=p, inv=inv, rv=rv, rk=rk, u=u, w=w, vn=vn, qk=qk, att=att, qd=qd,
                out=out, ekt=ekt, kt=kt, el=el, s_out=s_out)


def _gdn_chunk_fwd(qn, kn, v, beta, gc, z, norm_w, name):
    t = qn.shape[0]
    nc = t // CHUNK

    def body(q_ref, k_ref, v_ref, beta_ref, gc_ref, z_ref, nw_ref, o_ref, h_ref, st_ref, state):
        @pl.when(pl.program_id(0) == 0)
        def _():
            state[...] = jnp.zeros_like(state)

        st_ref[0] = state[...]
        gc_all, beta_all = gc_ref[...], beta_ref[...]
        gct = _mmx(_eye(HEAD), gc_all, _NT)
        glast_all = gc_ref[CHUNK - 1:CHUNK, :]
        nw = nw_ref[...]

        def head(h, carry):
            hq = jnp.right_shift(h, 1)
            f = _gdn_head_fwd(q_ref[:, _lanes(hq)], k_ref[:, _lanes(hq)], v_ref[:, _lanes(h)],
                              _col(beta_all, h), _col(gc_all, h), _row(gct, h), _col(glast_all, h), state[h])
            state[h] = f["s_out"]
            o = f["out"]
            o_ref[:, _lanes(h)] = o
            rstd = lax.rsqrt(jnp.mean(o * o, axis=1, keepdims=True) + RMS_EPS)
            h_ref[:, _lanes(h)] = o * rstd * nw * _silu(z_ref[:, _lanes(h)])
            return carry

        lax.fori_loop(0, GDN_V_HEADS, head, 0)

    rows = lambda w: pl.BlockSpec((CHUNK, w), lambda i: (i, 0))
    act = lambda w: jax.ShapeDtypeStruct((t, w), F32)
    return _call(body, name, (nc,),
                 [rows(GDN_QK_DIM), rows(GDN_QK_DIM), rows(D_INNER), rows(HEAD), rows(HEAD), rows(D_INNER),
                  _const(1, HEAD)],
                 [rows(D_INNER), rows(D_INNER), pl.BlockSpec((1, GDN_V_HEADS, HEAD, HEAD), lambda i: (i, 0, 0, 0))],
                 [act(D_INNER), act(D_INNER), jax.ShapeDtypeStruct((nc, GDN_V_HEADS, HEAD, HEAD), F32)],
                 [pltpu.VMEM((GDN_V_HEADS, HEAD, HEAD), F32)], ("arbitrary",))(qn, kn, v, beta, gc, z, norm_w)


def _gdn_chunk_bwd(qn, kn, v, beta, gc, z, norm_w, o, states, dh, name):
    t = qn.shape[0]
    nc = t // CHUNK

    def body(q_ref, k_ref, v_ref, beta_ref, gc_ref, z_ref, nw_ref, o_ref, st_ref, dh_ref,
             dq_ref, dk_ref, dv_ref, dz_ref, dbeta_ref, dgc_ref, acc_ref, dstate):
        @pl.when(pl.program_id(0) == 0)
        def _():
            dstate[...] = jnp.zeros_like(dstate)
            acc_ref[...] = jnp.zeros_like(acc_ref)

        gc_all, beta_all = gc_ref[...], beta_ref[...]
        gct = _mmx(_eye(HEAD), gc_all, _NT)
        glast_all = gc_ref[CHUNK - 1:CHUNK, :]
        nw = nw_ref[...]
        ii, jj = _iota((CHUNK, CHUNK), 0), _iota((CHUNK, CHUNK), 1)
        last_row = _iota((CHUNK, 1), 0) == CHUNK - 1

        def head(h, carry):
            dbeta_acc, dgc_acc, dgrow_acc, dnw_acc = carry
            hq = jnp.right_shift(h, 1)
            q, k, vv = q_ref[:, _lanes(hq)], k_ref[:, _lanes(hq)], v_ref[:, _lanes(h)]
            bcol, gcol = _col(beta_all, h), _col(gc_all, h)
            s_in = st_ref[0, h]
            f = _gdn_head_fwd(q, k, vv, bcol, gcol, _row(gct, h), _col(glast_all, h), s_in)
            oo, zz, dhh = o_ref[:, _lanes(h)], z_ref[:, _lanes(h)], dh_ref[:, _lanes(h)]
            rstd = lax.rsqrt(jnp.mean(oo * oo, axis=1, keepdims=True) + RMS_EPS)
            on, sz = oo * rstd, _silu(zz)
            dnw_acc = dnw_acc + jnp.sum(dhh * on * sz, axis=0, keepdims=True)
            dz_ref[:, _lanes(h)] = dhh * on * nw * _dsilu(zz)
            don = dhh * nw * sz
            do = rstd * (don - on * jnp.mean(don * on, axis=1, keepdims=True))
            dsn = dstate[h]
            decay, eg, inv = f["decay"], f["eg"], f["inv"]
            ds_prev = dsn * f["el"]
            d_glast = _total(dsn * s_in) * f["el"]
            dkt = _mm(f["vn"], dsn, _NT)
            dvn = _mm(f["kt"], dsn, _NN)
            dqd = _mm(do, s_in, _NT)
            ds_prev = ds_prev + _mm(f["qd"], do, _TN)
            datt = _mm(do, f["vn"], _NT)
            dvn = dvn + _mm(f["att"], do, _TN)
            dqk = datt * decay
            dq = dqd * eg + _mm(dqk, k, _NN)
            dk = _mm(dqk, q, _TN)
            ddecay = datt * f["qk"]
            dgcol = jnp.sum(dqd * f["qd"], axis=1, keepdims=True)
            dw = -_mm(dvn, s_in, _NT)
            ds_prev = ds_prev - _mm(f["w"], dvn, _TN)
            drv, drk = _mm(inv, dvn, _TN), _mm(inv, dw, _TN)
            da = jnp.where(ii > jj, -(_mm(drv, f["u"], _NT) + _mm(drk, f["w"], _NT)), 0.0)
            dp = da * decay
            ddecay = ddecay + da * f["p"]
            dkb = _mm(dp, k, _NN) + drk * eg
            dk = dk + _mm(dp, f["kb"], _TN)
            dv_ref[:, _lanes(h)] = drv * bcol
            dbeta = jnp.sum(drv * vv, axis=1, keepdims=True) + jnp.sum(dkb * k, axis=1, keepdims=True)
            dgcol = dgcol + jnp.sum(drk * f["rk"], axis=1, keepdims=True)
            dk = dk + dkb * bcol + dkt * f["ekt"]
            tail = dkt * f["kt"]
            d_glast = d_glast + _total(tail)
            dgcol = dgcol - jnp.sum(tail, axis=1, keepdims=True)
            e = ddecay * decay
            dgcol = dgcol + jnp.sum(e, axis=1, keepdims=True) + jnp.where(last_row, d_glast, 0.0)
            dgrow = -jnp.sum(e, axis=0, keepdims=True)
            dstate[h] = ds_prev
            dq_ref[:, _lanes(h)] = dq
            dk_ref[:, _lanes(h)] = dk
            return (_put_col(dbeta_acc, dbeta, h), _put_col(dgc_acc, dgcol, h), _put_row(dgrow_acc, dgrow, h), dnw_acc)

        zero = jnp.zeros((CHUNK, HEAD), F32)
        dbeta_acc, dgc_acc, dgrow_acc, dnw_acc = lax.fori_loop(
            0, GDN_V_HEADS, head, (zero, zero, jnp.zeros((HEAD, CHUNK), F32), jnp.zeros((1, HEAD), F32)))
        dbeta_ref[...] = dbeta_acc
        dgc_ref[...] = dgc_acc + _mmx(_eye(CHUNK), dgrow_acc, _NT)
        acc_ref[...] += _put_sub(jnp.zeros((8, HEAD), F32), dnw_acc, 0)

    rows = lambda w: pl.BlockSpec((CHUNK, w), lambda i: (nc - 1 - i, 0))
    act = lambda w: jax.ShapeDtypeStruct((t, w), F32)
    return _call(body, name, (nc,),
                 [rows(GDN_QK_DIM), rows(GDN_QK_DIM), rows(D_INNER), rows(HEAD), rows(HEAD), rows(D_INNER),
                  _const(1, HEAD), rows(D_INNER),
                  pl.BlockSpec((1, GDN_V_HEADS, HEAD, HEAD), lambda i: (nc - 1 - i, 0, 0, 0)), rows(D_INNER)],
                 [rows(D_INNER), rows(D_INNER), rows(D_INNER), rows(D_INNER), rows(HEAD), rows(HEAD), _const(8, HEAD)],
                 [act(D_INNER), act(D_INNER), act(D_INNER), act(D_INNER), act(HEAD), act(HEAD),
                  jax.ShapeDtypeStruct((8, HEAD), F32)],
                 [pltpu.VMEM((GDN_V_HEADS, HEAD, HEAD), F32)], ("arbitrary",)
                 )(qn, kn, v, beta, gc, z, norm_w, o, states, dh)


def _sc_gate_fwd(bg, cv, z, name):
    t, w = bg.shape

    def body(b_ref, c_ref, z_ref, o_ref):
        o_ref[...] = b_ref[...] * c_ref[...] * _silu(z_ref[...])

    return _call(body, name, (t // EW_TB,), [_rows(w)] * 3, _rows(w), jax.ShapeDtypeStruct((t, w), F32),
                 semantics=("parallel",))(bg, cv, z)


def _sc_gate_bwd(dh, bg, cv, z, name):
    t, w = bg.shape

    def body(d_ref, b_ref, c_ref, z_ref, db_ref, dc_ref, dz_ref):
        d, b, c, zz = d_ref[...], b_ref[...], c_ref[...], z_ref[...]
        sz = _silu(zz)
        db_ref[...] = d * c * sz
        dc_ref[...] = d * b * sz
        dz_ref[...] = d * b * c * _dsilu(zz)

    act = jax.ShapeDtypeStruct((t, w), F32)
    return _call(body, name, (t // EW_TB,), [_rows(w)] * 4, [_rows(w)] * 3, [act] * 3,
                 semantics=("parallel",))(dh, bg, cv, z)


XBC_B = D_INNER
XBC_C = D_INNER + SSD_GROUPS * SSD_STATE


def _ssd_scalars(dtp, dt_bias, a_log):
    dt = _softplus(dtp + dt_bias)
    a = -jnp.exp(a_log)
    da = dt * a
    ac = _mmx(_chunk_mask(CHUNK, False), da, _NN)
    act = _mmx(_eye(HEAD), ac, _NT)
    aclast = jnp.sum(jnp.where(_iota(ac.shape, 0) == CHUNK - 1, ac, 0.0), axis=0, keepdims=True)
    return dt, a, da, ac, act, aclast


def _ssd_pair_fwd(x2, bg, cg, cb, dt, ac, act, aclast, p, s2):
    ii, jj = _iota((CHUNK, CHUNK), 0), _iota((CHUNK, CHUNK), 1)
    half = _iota((CHUNK, HEAD), 1) < 64
    ha, hb = 2 * p, 2 * p + 1
    aca, acb = _col(ac, ha), _col(ac, hb)
    la, lb = _col(aclast, ha), _col(aclast, hb)
    dt2 = jnp.where(half, _col(dt, ha), _col(dt, hb))
    xdt = x2 * dt2
    causal = ii >= jj
    sega = jnp.exp(jnp.where(causal, aca - _row(act, ha), NEG_BIG))
    segb = jnp.exp(jnp.where(causal, acb - _row(act, hb), NEG_BIG))
    ma, mb = sega * cb, segb * cb
    ydiag = jnp.where(half, _mm(ma, xdt, _NN), _mm(mb, xdt, _NN))
    cdec = jnp.where(half, jnp.exp(aca), jnp.exp(acb))
    cs = _mm(cg, s2, _NT)
    tail = jnp.where(half, jnp.exp(la - aca), jnp.exp(lb - acb))
    zt = xdt * tail
    ea, eb = jnp.exp(la), jnp.exp(lb)
    tot = jnp.where(_iota((HEAD, 1), 0) < 64, ea, eb)
    s_out = s2 * tot + _mm(zt, bg, _TN)
    return dict(half=half, dt2=dt2, xdt=xdt, sega=sega, segb=segb, ma=ma, mb=mb, ydiag=ydiag, cdec=cdec, cs=cs,
                tail=tail, zt=zt, ea=ea, eb=eb, tot=tot, s_out=s_out)


def _ssd_chunk_fwd(cx, dtp, z, dt_bias, a_log, dskip, norm_w, name):
    t = cx.shape[0]
    nc = t // CHUNK
    gw = D_INNER // SSD_GROUPS

    def body(cx_ref, dtp_ref, z_ref, db_ref, al_ref, sk_ref, nw_ref, y_ref, h_ref, st_ref, state):
        @pl.when(pl.program_id(0) == 0)
        def _():
            state[...] = jnp.zeros_like(state)

        st_ref[0] = state[...]
        dt, _, _, ac, act, aclast = _ssd_scalars(dtp_ref[...], db_ref[...], al_ref[...])
        for g in range(SSD_GROUPS):
            bg = _silu(cx_ref[:, XBC_B + g * SSD_STATE: XBC_B + (g + 1) * SSD_STATE])
            cg = _silu(cx_ref[:, XBC_C + g * SSD_STATE: XBC_C + (g + 1) * SSD_STATE])
            cb = _mm(cg, bg, _NT)

            def pair(pr, carry, g=g, bg=bg, cg=cg, cb=cb):
                p = 4 * g + pr
                x2 = _silu(cx_ref[:, _lanes(p)])
                f = _ssd_pair_fwd(x2, bg, cg, cb, dt, ac, act, aclast, p, state[p])
                state[p] = f["s_out"]
                y_ref[:, _lanes(p)] = f["ydiag"] + f["cs"] * f["cdec"] + sk_ref[:, _lanes(p)] * x2
                return carry

            lax.fori_loop(0, 4, pair, 0)
        for g in range(SSD_GROUPS):
            sl = slice(g * gw, (g + 1) * gw)
            yg = y_ref[:, sl] * _silu(z_ref[:, sl])
            rstd = lax.rsqrt(jnp.mean(yg * yg, axis=1, keepdims=True) + RMS_EPS)
            h_ref[:, sl] = yg * rstd * nw_ref[:, sl]

    rows = lambda w: pl.BlockSpec((CHUNK, w), lambda i: (i, 0))
    act_ = lambda w: jax.ShapeDtypeStruct((t, w), F32)
    return _call(body, name, (nc,),
                 [rows(SSD_CONV_DIM), rows(HEAD), rows(D_INNER), _const(1, HEAD), _const(1, HEAD),
                  _const(1, D_INNER), _const(1, D_INNER)],
                 [rows(D_INNER), rows(D_INNER), pl.BlockSpec((1, SSD_PAIRS, HEAD, SSD_STATE), lambda i: (i, 0, 0, 0))],
                 [act_(D_INNER), act_(D_INNER), jax.ShapeDtypeStruct((nc, SSD_PAIRS, HEAD, SSD_STATE), F32)],
                 [pltpu.VMEM((SSD_PAIRS, HEAD, SSD_STATE), F32)], ("arbitrary",)
                 )(cx, dtp, z, dt_bias, a_log, dskip, norm_w)


def _ssd_chunk_bwd(cx, dtp, z, dt_bias, a_log, dskip, norm_w, y, states, dh, name):
    t = cx.shape[0]
    nc = t // CHUNK
    gw = D_INNER // SSD_GROUPS

    def body(cx_ref, dtp_ref, z_ref, db_ref, al_ref, sk_ref, nw_ref, y_ref, st_ref, dh_ref,
             dcx_ref, ddtp_ref, dz_ref, wide_ref, acc_ref, dstate, dy_s):
        @pl.when(pl.program_id(0) == 0)
        def _():
            dstate[...] = jnp.zeros_like(dstate)
            wide_ref[...] = jnp.zeros_like(wide_ref)
            acc_ref[...] = jnp.zeros_like(acc_ref)

        dtp = dtp_ref[...]
        dt, a, da, ac, act, aclast = _ssd_scalars(dtp, db_ref[...], al_ref[...])
        ii, jj = _iota((CHUNK, CHUNK), 0), _iota((CHUNK, CHUNK), 1)
        last_row = _iota((CHUNK, 1), 0) == CHUNK - 1
        top = _iota((HEAD, SSD_STATE), 0) < 64
        for g in range(SSD_GROUPS):
            sl = slice(g * gw, (g + 1) * gw)
            yy, zz, dhh, nw = y_ref[:, sl], z_ref[:, sl], dh_ref[:, sl], nw_ref[:, sl]
            sz = _silu(zz)
            yg = yy * sz
            rstd = lax.rsqrt(jnp.mean(yg * yg, axis=1, keepdims=True) + RMS_EPS)
            n = yg * rstd
            dn = dhh * nw
            dyg = rstd * (dn - n * jnp.mean(dn * n, axis=1, keepdims=True))
            dy_s[:, sl] = dyg * sz
            dz_ref[:, sl] = dyg * yy * _dsilu(zz)
            wide_ref[0:1, sl] += jnp.sum(dhh * n, axis=0, keepdims=True)

        zero = jnp.zeros((CHUNK, HEAD), F32)
        scal = (zero, zero, jnp.zeros((HEAD, CHUNK), F32))
        for g in range(SSD_GROUPS):
            cxb = cx_ref[:, XBC_B + g * SSD_STATE: XBC_B + (g + 1) * SSD_STATE]
            cxc = cx_ref[:, XBC_C + g * SSD_STATE: XBC_C + (g + 1) * SSD_STATE]
            bg, cg = _silu(cxb), _silu(cxc)
            cb = _mm(cg, bg, _NT)

            def pair(pr, carry, g=g, bg=bg, cg=cg, cb=cb):
                db, dc, dcb, ddt_acc, dac_acc, drow_acc = carry
                p = 4 * g + pr
                ha, hb = 2 * p, 2 * p + 1
                cxx = cx_ref[:, _lanes(p)]
                x2 = _silu(cxx)
                s2 = st_ref[0, p]
                f = _ssd_pair_fwd(x2, bg, cg, cb, dt, ac, act, aclast, p, s2)
                half = f["half"]
                dy2 = dy_s[:, _lanes(p)]
                dsn = dstate[p]
                dx2 = dy2 * sk_ref[:, _lanes(p)]
                wide_ref[1:2, _lanes(p)] += jnp.sum(dy2 * x2, axis=0, keepdims=True)
                gg = dy2 * f["cdec"]
                dc = dc + _mm(gg, s2, _NN)
                ds_prev = dsn * f["tot"] + _mm(gg, cg, _TN)
                t1 = dy2 * f["cs"] * f["cdec"]
                t1a = jnp.sum(jnp.where(half, t1, 0.0), axis=1, keepdims=True)
                dac_a, dac_b = t1a, jnp.sum(t1, axis=1, keepdims=True) - t1a
                dya = jnp.where(half, dy2, 0.0)
                dma, dmb = _mm(dya, f["xdt"], _NT), _mm(dy2 - dya, f["xdt"], _NT)
                dxdt = jnp.where(half, _mm(f["ma"], dy2, _TN), _mm(f["mb"], dy2, _TN))
                dcb = dcb + dma * f["sega"] + dmb * f["segb"]
                ea_, eb_ = dma * f["ma"], dmb * f["mb"]
                dac_a = dac_a + jnp.sum(ea_, axis=1, keepdims=True)
                dac_b = dac_b + jnp.sum(eb_, axis=1, keepdims=True)
                drow_a, drow_b = -jnp.sum(ea_, axis=0, keepdims=True), -jnp.sum(eb_, axis=0, keepdims=True)
                dzt = _mm(bg, dsn, _NT)
                db = db + _mm(f["zt"], dsn, _NN)
                dxdt = dxdt + dzt * f["tail"]
                t2 = dzt * f["zt"]
                t2a = jnp.sum(jnp.where(half, t2, 0.0), axis=1, keepdims=True)
                t2b = jnp.sum(t2, axis=1, keepdims=True) - t2a
                dla, dlb = _total(t2a), _total(t2b)
                dac_a, dac_b = dac_a - t2a, dac_b - t2b
                t3 = dsn * s2
                t3a = _total(jnp.where(top, t3, 0.0))
                dla = dla + t3a * f["ea"]
                dlb = dlb + (_total(t3) - t3a) * f["eb"]
                dac_a = dac_a + jnp.where(last_row, dla, 0.0)
                dac_b = dac_b + jnp.where(last_row, dlb, 0.0)
                dx2 = dx2 + dxdt * f["dt2"]
                t4 = dxdt * x2
                t4a = jnp.sum(jnp.where(half, t4, 0.0), axis=1, keepdims=True)
                t4b = jnp.sum(t4, axis=1, keepdims=True) - t4a
                dcx_ref[:, _lanes(p)] = dx2 * _dsilu(cxx)
                dstate[p] = ds_prev
                ddt_acc = _put_col(_put_col(ddt_acc, t4a, ha), t4b, hb)
                dac_acc = _put_col(_put_col(dac_acc, dac_a, ha), dac_b, hb)
                drow_acc = _put_row(_put_row(drow_acc, drow_a, ha), drow_b, hb)
                return db, dc, dcb, ddt_acc, dac_acc, drow_acc

            db, dc, dcb, *scal = lax.fori_loop(0, 4, pair, (zero, zero, jnp.zeros((CHUNK, CHUNK), F32)) + tuple(scal))
            dc = dc + _mm(dcb, bg, _NN)
            db = db + _mm(dcb, cg, _TN)
            dcx_ref[:, XBC_B + g * SSD_STATE: XBC_B + (g + 1) * SSD_STATE] = db * _dsilu(cxb)
            dcx_ref[:, XBC_C + g * SSD_STATE: XBC_C + (g + 1) * SSD_STATE] = dc * _dsilu(cxc)
        ddt_acc, dac_acc, drow_acc = scal
        dac = dac_acc + _mmx(_eye(CHUNK), drow_acc, _NT)
        dda = _mmx(_chunk_mask(CHUNK, True), dac, _NN)
        ddt = ddt_acc + dda * a
        ddtp = ddt * _sig(dtp + db_ref[...])
        ddtp_ref[...] = ddtp
        acc = _put_sub(jnp.zeros((8, HEAD), F32), jnp.sum(dda * da, axis=0, keepdims=True), 0)
        acc_ref[...] += _put_sub(acc, jnp.sum(ddtp, axis=0, keepdims=True), 1)

    rows = lambda w: pl.BlockSpec((CHUNK, w), lambda i: (nc - 1 - i, 0))
    act_ = lambda w: jax.ShapeDtypeStruct((t, w), F32)
    return _call(body, name, (nc,),
                 [rows(SSD_CONV_DIM), rows(HEAD), rows(D_INNER), _const(1, HEAD), _const(1, HEAD),
                  _const(1, D_INNER), _const(1, D_INNER), rows(D_INNER),
                  pl.BlockSpec((1, SSD_PAIRS, HEAD, SSD_STATE), lambda i: (nc - 1 - i, 0, 0, 0)), rows(D_INNER)],
                 [rows(SSD_CONV_DIM), rows(HEAD), rows(D_INNER), _const(8, D_INNER), _const(8, HEAD)],
                 [act_(SSD_CONV_DIM), act_(HEAD), act_(D_INNER), jax.ShapeDtypeStruct((8, D_INNER), F32),
                  jax.ShapeDtypeStruct((8, HEAD), F32)],
                 [pltpu.VMEM((SSD_PAIRS, HEAD, SSD_STATE), F32), pltpu.VMEM((CHUNK, D_INNER), F32)], ("arbitrary",)
                 )(cx, dtp, z, dt_bias, a_log, dskip, norm_w, y, states, dh)


LN_TB = 512


def _ln_stats(x, y):
    u = ALPHA * x + y
    mu = jnp.mean(u, axis=1, keepdims=True)
    cen = u - mu
    rstd = lax.rsqrt(jnp.mean(cen * cen, axis=1, keepdims=True) + LN_EPS)
    return cen * rstd


def _ln_fwd(x, y, g, b, name):
    t, d = x.shape

    def body(x_ref, y_ref, g_ref, b_ref, o_ref):
        o_ref[...] = _ln_stats(x_ref[...], y_ref[...]) * g_ref[...] + b_ref[...]

    return _call(body, name, (t // LN_TB,), [_rows(d, LN_TB), _rows(d, LN_TB), _const(1, d), _const(1, d)],
                 _rows(d, LN_TB), jax.ShapeDtypeStruct((t, d), F32), semantics=("parallel",))(x, y, g, b)


def _ln_bwd(dout, x, y, g, name):
    t, d = x.shape

    def body(d_ref, x_ref, y_ref, g_ref, du_ref, acc_ref):
        u = ALPHA * x_ref[...] + y_ref[...]
        mu = jnp.mean(u, axis=1, keepdims=True)
        cen = u - mu
        rstd = lax.rsqrt(jnp.mean(cen * cen, axis=1, keepdims=True) + LN_EPS)
        xh = cen * rstd
        do = d_ref[...]
        dxh = do * g_ref[...]
        du_ref[...] = rstd * (dxh - jnp.mean(dxh, axis=1, keepdims=True)
                              - xh * jnp.mean(dxh * xh, axis=1, keepdims=True))
        acc = _put_sub(jnp.zeros((8, d), F32), jnp.sum(do * xh, axis=0, keepdims=True), 0)
        acc = _put_sub(acc, jnp.sum(do, axis=0, keepdims=True), 1)

        @pl.when(pl.program_id(0) == 0)
        def _():
            acc_ref[...] = jnp.zeros_like(acc_ref)

        acc_ref[...] += acc

    return _call(body, name, (t // LN_TB,), [_rows(d, LN_TB)] * 3 + [_const(1, d)],
                 [_rows(d, LN_TB), _const(8, d)],
                 [jax.ShapeDtypeStruct((t, d), F32), jax.ShapeDtypeStruct((8, d), F32)],
                 semantics=("arbitrary",))(dout, x, y, g)


def _loss_head(out, target, name):
    t, d = out.shape

    def body(o_ref, t_ref, d_ref, acc_ref):
        err = o_ref[...] - t_ref[...]
        d_ref[...] = err * (1.0 / d)

        @pl.when(pl.program_id(0) == 0)
        def _():
            acc_ref[...] = jnp.zeros_like(acc_ref)

        acc_ref[...] += _put_sub(jnp.zeros((8, d), F32), jnp.sum(err * err, axis=0, keepdims=True), 0)

    return _call(body, name, (t // LN_TB,), [_rows(d, LN_TB)] * 2, [_rows(d, LN_TB), _const(8, d)],
                 [jax.ShapeDtypeStruct((t, d), F32), jax.ShapeDtypeStruct((8, d), F32)],
                 semantics=("arbitrary",))(out, target)


def _adamw(w, gslots, m, v, name):
    r, c = w.shape
    rb = _tile_rows(r)
    c1 = 1.0 - ADAM_B1 ** ADAM_STEP
    c2 = 1.0 - ADAM_B2 ** ADAM_STEP

    def body(w_ref, g_ref, m_ref, v_ref, go_ref, d_ref, mo_ref, vo_ref):
        g = g_ref[0]
        for s in range(1, N_DEV):
            g = g + g_ref[s]
        mn = ADAM_B1 * m_ref[...] + (1.0 - ADAM_B1) * g
        vn = ADAM_B2 * v_ref[...] + (1.0 - ADAM_B2) * (g * g)
        go_ref[...] = g
        mo_ref[...] = mn
        vo_ref[...] = vn
        d_ref[...] = -ADAM_LR * ((mn / c1) / (jnp.sqrt(vn / c2) + ADAM_EPS) + ADAM_WD * w_ref[...])

    blk = pl.BlockSpec((rb, c), lambda i: (i, 0))
    sds = jax.ShapeDtypeStruct((r, c), F32)
    return _call(body, name, (r // rb,), [blk, pl.BlockSpec((N_DEV, rb, c), lambda i: (0, i, 0)), blk, blk],
                 [blk] * 4, [sds] * 4, semantics=("parallel",))(w, gslots, m, v)


def _tile_rows(r):
    for rb in (256, 128, 64, 32, 16, 8):
        if r % rb == 0:
            return rb
    return r


def _pack(arrs, lead=0):
    flats = []
    for a in arrs:
        f = a.reshape(a.shape[:lead] + (-1,)).astype(F32)
        flats.append(jnp.pad(f, [(0, 0)] * lead + [(0, (-f.shape[-1]) % 128)]))
    v = jnp.concatenate(flats, axis=-1)
    v = jnp.pad(v, [(0, 0)] * lead + [(0, (-v.shape[-1]) % 1024)])
    return v.reshape(v.shape[:lead] + (-1, 128))


def _unpack(buf, shapes, lead=0):
    flat = buf.reshape(buf.shape[:lead] + (-1,))
    outs, off = [], 0
    for s in shapes:
        n = math.prod(s)
        outs.append(flat[..., off:off + n].reshape(buf.shape[:lead] + tuple(s)))
        off += n + (-n) % 128
    return outs


def _cols_gathered(g):
    n, l, r, c = g.shape
    return g.transpose(1, 2, 0, 3).reshape(l, r, n * c)


def _cols_to_slabs(full):
    l, r, c = full.shape
    return full.reshape(l, r, N_DEV, c // N_DEV).transpose(2, 0, 1, 3)


def _rows_gathered(g):
    n, l, r, c = g.shape
    return g.transpose(1, 0, 2, 3).reshape(l, n * r, c)


def _rows_to_slabs(full):
    l, r, c = full.shape
    return full.reshape(l, N_DEV, r // N_DEV, c).transpose(1, 0, 2, 3)


def _pad_cols(w, at, width):
    return jnp.pad(w, ((0, 0), (at, width - at - w.shape[1])))


def _pad_lanes(v, width=HEAD):
    return jnp.pad(v.reshape(1, -1), ((0, 0), (0, width - v.size)))


def _taps8(w, bias=None):
    rows = [w] if bias is None else [w, bias.reshape(1, -1)]
    w8 = jnp.concatenate(rows, axis=0)
    return jnp.pad(w8, ((0, 8 - w8.shape[0]), (0, 0)))


def _gdn_forward(x, p, tag):
    pq = _matmul(x, p["w_qkv"], "nn", tag + "_in_qkv")
    z = _matmul(x, p["w_z"], "nn", tag + "_in_z")
    ba = _matmul(x, p["w_ba"], "nn", tag + "_in_ba")
    c = _conv_fwd(pq, p["conv8"], 4, tag + "_conv")
    qn, kn, v, beta, gc = _gdn_ew_fwd(c, ba, p["a_log"], p["dt_bias"], tag + "_ew")
    o, h, states = _gdn_chunk_fwd(qn, kn, v, beta, gc, z, p["norm_w"], tag + "_chunk")
    y = _matmul(h, p["w_out"], "nn", tag + "_out")
    return y, dict(pq=pq, z=z, ba=ba, c=c, qn=qn, kn=kn, v=v, beta=beta, gc=gc, o=o, h=h, states=states)


def _gdn_backward(x, du, p, s, tag):
    dh = _matmul(du, p["w_out"], "nt", tag + "_bwd_dh")
    g_out = _matmul(s["h"], du, "tn", tag + "_bwd_wout")
    dqh, dkh, dv, dz, dbeta, dgc, nacc = _gdn_chunk_bwd(
        s["qn"], s["kn"], s["v"], s["beta"], s["gc"], s["z"], p["norm_w"], s["o"], s["states"], dh, tag + "_bwd_chunk")
    dc, dba, sacc = _gdn_ew_bwd(s["c"], s["ba"], p["a_log"], p["dt_bias"], dqh, dkh, dv, dbeta, dgc, tag + "_bwd_ew")
    dpq, dconv = _conv_bwd(dc, s["pq"], p["conv8"], 4, tag + "_bwd_conv")
    dx = _matmul(dpq, p["w_qkv"], "nt", tag + "_bwd_dx_qkv", add=du, add_scale=ALPHA)
    dx = _matmul(dz, p["w_z"], "nt", tag + "_bwd_dx_z", add=dx)
    dx = _matmul(dba, p["w_ba"], "nt", tag + "_bwd_dx_ba", add=dx)
    g_qkv = _matmul(x, dpq, "tn", tag + "_bwd_w_qkv")
    g_z = _matmul(x, dz, "tn", tag + "_bwd_w_z")
    g_ba = _matmul(x, dba, "tn", tag + "_bwd_w_ba")
    g_in = jnp.concatenate([g_qkv, g_z, g_ba[:, :GDN_V_HEADS], g_ba[:, HEAD:HEAD + GDN_V_HEADS]], axis=1)
    grads = dict(w_in=g_in, w_out=g_out, conv_w=dconv[:4], a_log=sacc[0, :GDN_V_HEADS], dt_bias=sacc[1, :GDN_V_HEADS],
                 norm_w=nacc[0])
    return dx, grads


def _sc_forward(x, p, tag):
    hh = _matmul(x, p["w_h"], "nn", tag + "_in_h")
    bg = _matmul(x, p["w_b"], "nn", tag + "_in_b")
    cg = _matmul(x, p["w_c"], "nn", tag + "_in_c")
    z = _matmul(x, p["w_z"], "nn", tag + "_in_z")
    cv = _conv_fwd(cg, p["conv8"], 3, tag + "_conv", u2=hh)
    h = _sc_gate_fwd(bg, cv, z, tag + "_gate")
    y = _matmul(h, p["w_out"], "nn", tag + "_out")
    return y, dict(hh=hh, bg=bg, cg=cg, z=z, cv=cv, h=h)


def _sc_backward(x, du, p, s, tag):
    dh = _matmul(du, p["w_out"], "nt", tag + "_bwd_dh")
    g_out = _matmul(s["h"], du, "tn", tag + "_bwd_wout")
    dbg, dcv, dz = _sc_gate_bwd(dh, s["bg"], s["cv"], s["z"], tag + "_bwd_gate")
    dcg, dhh, dconv = _conv_bwd(dcv, s["cg"], p["conv8"], 3, tag + "_bwd_conv", u2=s["hh"])
    dx = _matmul(dhh, p["w_h"], "nt", tag + "_bwd_dx_h", add=du, add_scale=ALPHA)
    dx = _matmul(dbg, p["w_b"], "nt", tag + "_bwd_dx_b", add=dx)
    dx = _matmul(dcg, p["w_c"], "nt", tag + "_bwd_dx_c", add=dx)
    dx = _matmul(dz, p["w_z"], "nt", tag + "_bwd_dx_z", add=dx)
    g_in = jnp.concatenate([_matmul(x, d, "tn", tag + "_bwd_w_" + n)
                            for n, d in (("h", dhh), ("b", dbg), ("c", dcg), ("z", dz))], axis=1)
    return dx, dict(w_in=g_in, w_out=g_out, conv_w=dconv[:3])


def _ssd_forward(x, p, tag):
    z = _matmul(x, p["w_z"], "nn", tag + "_in_z")
    xbc = _matmul(x, p["w_xbc"], "nn", tag + "_in_xbc")
    dtp = _matmul(x, p["w_dt"], "nn", tag + "_in_dt")
    cx = _conv_fwd(xbc, p["conv8"], 4, tag + "_conv", bias=True)
    y, h, states = _ssd_chunk_fwd(cx, dtp, z, p["dt_bias"], p["a_log"], p["dskip"], p["norm_w"], tag + "_chunk")
    out = _matmul(h, p["w_out"], "nn", tag + "_out")
    return out, dict(z=z, xbc=xbc, dtp=dtp, cx=cx, y=y, h=h, states=states)


def _ssd_backward(x, du, p, s, tag):
    dh = _matmul(du, p["w_out"], "nt", tag + "_bwd_dh")
    g_out = _matmul(s["h"], du, "tn", tag + "_bwd_wout")
    dcx, ddtp, dz, wide, acc = _ssd_chunk_bwd(s["cx"], s["dtp"], s["z"], p["dt_bias"], p["a_log"], p["dskip"],
                                              p["norm_w"], s["y"], s["states"], dh, tag + "_bwd_chunk")
    dxbc, dconv = _conv_bwd(dcx, s["xbc"], p["conv8"], 4, tag + "_bwd_conv")
    dx = _matmul(dz, p["w_z"], "nt", tag + "_bwd_dx_z", add=du, add_scale=ALPHA)
    dx = _matmul(dxbc, p["w_xbc"], "nt", tag + "_bwd_dx_xbc", add=dx)
    dx = _matmul(ddtp, p["w_dt"], "nt", tag + "_bwd_dx_dt", add=dx)
    g_dt = _matmul(x, ddtp, "tn", tag + "_bwd_w_dt")
    g_in = jnp.concatenate([_matmul(x, dz, "tn", tag + "_bwd_w_z"), _matmul(x, dxbc, "tn", tag + "_bwd_w_xbc"),
                            g_dt[:, :32]], axis=1)
    grads = dict(w_in=g_in, w_out=g_out, conv_w=dconv[:4], conv_b=dconv[4], a_log=acc[0, :32], dt_bias=acc[1, :32],
                 d_skip=jnp.sum(wide[1].reshape(32, 64), axis=1), norm_w=wide[0])
    return dx, grads


_WEIGHTS = ['gdn_w_in', 'gdn_conv_w', 'gdn_a_log', 'gdn_dt_bias', 'gdn_norm_w', 'gdn_w_out', 'sc_w_in', 'sc_conv_w',
            'sc_w_out', 'ssd_w_in', 'ssd_conv_w', 'ssd_conv_b', 'ssd_a_log', 'ssd_dt_bias', 'ssd_d_skip',
            'ssd_norm_w', 'ssd_w_out', 'ln_g', 'ln_b']
_BIG = {'gdn_w_in': 'cols', 'gdn_w_out': 'rows', 'sc_w_in': 'cols', 'sc_w_out': 'rows', 'ssd_w_in': 'cols',
        'ssd_w_out': 'rows'}
_SMALL_SHARDED = ['gdn_conv_w', 'sc_conv_w', 'ssd_conv_w', 'ssd_conv_b', 'ssd_norm_w']
_SMALL = [n for n in _WEIGHTS if n not in _BIG]


def kernel(x, gdn_w_in, gdn_conv_w, gdn_a_log, gdn_dt_bias, gdn_norm_w, gdn_w_out, sc_w_in, sc_conv_w, sc_w_out, ssd_w_in, ssd_conv_w, ssd_conv_b, ssd_a_log, ssd_dt_bias, ssd_d_skip, ssd_norm_w, ssd_w_out, ln_g, ln_b, loss_target, m_gdn_w_in, m_gdn_conv_w, m_gdn_a_log, m_gdn_dt_bias, m_gdn_norm_w, m_gdn_w_out, m_sc_w_in, m_sc_conv_w, m_sc_w_out, m_ssd_w_in, m_ssd_conv_w, m_ssd_conv_b, m_ssd_a_log, m_ssd_dt_bias, m_ssd_d_skip, m_ssd_norm_w, m_ssd_w_out, m_ln_g, m_ln_b, v_gdn_w_in, v_gdn_conv_w, v_gdn_a_log, v_gdn_dt_bias, v_gdn_norm_w, v_gdn_w_out, v_sc_w_in, v_sc_conv_w, v_sc_w_out, v_ssd_w_in, v_ssd_conv_w, v_ssd_conv_b, v_ssd_a_log, v_ssd_dt_bias, v_ssd_d_skip, v_ssd_norm_w, v_ssd_w_out, v_ln_g, v_ln_b):
    args = locals()
    wts = {n: args[n] for n in _WEIGHTS}
    mom = {n: args["m_" + n] for n in _WEIGHTS}
    vel = {n: args["v_" + n] for n in _WEIGHTS}
    me = 4 * lax.axis_index("x") + 2 * lax.axis_index("y") + lax.axis_index("c")
    x0, target = x[0], loss_target[0]

    full = {}
    for n, how in _BIG.items():
        g = _exchange(wts[n].astype(MM_DTYPE), "gather_" + n, slabs=False)
        full[n] = _cols_gathered(g) if how == "cols" else _rows_gathered(g)
    small_shapes = [wts[n].shape for n in _SMALL_SHARDED]
    gathered = _exchange(_pack([wts[n] for n in _SMALL_SHARDED]), "gather_small", slabs=False)
    for n, g in zip(_SMALL_SHARDED, _unpack(gathered, small_shapes, lead=1)):
        full[n] = jnp.moveaxis(g, 0, -2).reshape(g.shape[1:-1] + (N_DEV * g.shape[-1],))
    for n in _SMALL:
        full.setdefault(n, wts[n])

    def gdn_params(j):
        w = full['gdn_w_in'][j]
        return dict(w_qkv=w[:, :GDN_CONV_DIM], w_z=w[:, GDN_CONV_DIM:GDN_CONV_DIM + D_INNER],
                    w_ba=jnp.concatenate([_pad_cols(w[:, 6144:6160], 0, HEAD), _pad_cols(w[:, 6160:6176], 0, HEAD)], 1),
                    conv8=_taps8(full['gdn_conv_w'][j]), a_log=_pad_lanes(full['gdn_a_log'][j]),
                    dt_bias=_pad_lanes(full['gdn_dt_bias'][j]), norm_w=full['gdn_norm_w'][j].reshape(1, HEAD),
                    w_out=full['gdn_w_out'][j])

    w = full['sc_w_in'][0]
    sc_p = dict(w_h=w[:, :2048], w_b=w[:, 2048:4096], w_c=w[:, 4096:6144], w_z=w[:, 6144:],
                conv8=_taps8(full['sc_conv_w'][0]), w_out=full['sc_w_out'][0])
    w = full['ssd_w_in'][0]
    ssd_p = dict(w_z=w[:, :D_INNER], w_xbc=w[:, D_INNER:D_INNER + SSD_CONV_DIM],
                 w_dt=_pad_cols(w[:, D_INNER + SSD_CONV_DIM:], 0, HEAD),
                 conv8=_taps8(full['ssd_conv_w'][0], full['ssd_conv_b'][0]), a_log=_pad_lanes(full['ssd_a_log'][0]),
                 dt_bias=_pad_lanes(full['ssd_dt_bias'][0]),
                 dskip=jnp.repeat(full['ssd_d_skip'][0], 64).reshape(1, D_INNER),
                 norm_w=full['ssd_norm_w'][0].reshape(1, D_INNER), w_out=full['ssd_w_out'][0])
    layers = [("gdn", _gdn_forward, _gdn_backward, gdn_params(0)), ("sc", _sc_forward, _sc_backward, sc_p),
              ("ssd", _ssd_forward, _ssd_backward, ssd_p), ("gdn", _gdn_forward, _gdn_backward, gdn_params(1))]

    acts, ys, saved = [x0], [], []
    for i, (kind, fwd, _, p) in enumerate(layers):
        y, s = fwd(acts[-1], p, "l%d_%s" % (i, kind))
        acts.append(_ln_fwd(acts[-1], y, full['ln_g'][i].reshape(1, -1), full['ln_b'][i].reshape(1, -1), "l%d_ln" % i))
        ys.append(y)
        saved.append(s)
    dact, loss_acc = _loss_head(acts[-1], target, "loss_head")
    loss = lax.psum(0.5 / D_MODEL * jnp.sum(loss_acc[0]), ("x", "y", "c"))

    lg = [None] * DEPTH
    d_ln_g, d_ln_b = [None] * DEPTH, [None] * DEPTH
    for i in reversed(range(DEPTH)):
        kind, _, bwd, p = layers[i]
        du, acc = _ln_bwd(dact, acts[i], ys[i], full['ln_g'][i].reshape(1, -1), "l%d_ln_bwd" % i)
        d_ln_g[i], d_ln_b[i] = acc[0], acc[1]
        dact, lg[i] = bwd(acts[i], du, p, saved[i], "l%d_%s" % (i, kind))
    grad_x = dact[None]

    stack = lambda k: jnp.stack([lg[0][k], lg[3][k]])
    local = {
        'gdn_w_in': stack('w_in'), 'gdn_conv_w': stack('conv_w'), 'gdn_a_log': stack('a_log'),
        'gdn_dt_bias': stack('dt_bias'), 'gdn_norm_w': stack('norm_w'), 'gdn_w_out': stack('w_out'),
        'sc_w_in': lg[1]['w_in'][None], 'sc_conv_w': lg[1]['conv_w'][None], 'sc_w_out': lg[1]['w_out'][None],
        'ssd_w_in': lg[2]['w_in'][None], 'ssd_conv_w': lg[2]['conv_w'][None], 'ssd_conv_b': lg[2]['conv_b'][None],
        'ssd_a_log': lg[2]['a_log'][None], 'ssd_dt_bias': lg[2]['dt_bias'][None], 'ssd_d_skip': lg[2]['d_skip'][None],
        'ssd_norm_w': lg[2]['norm_w'][None], 'ssd_w_out': lg[2]['w_out'][None],
        'ln_g': jnp.stack(d_ln_g), 'ln_b': jnp.stack(d_ln_b)}

    out = {}
    for n, how in _BIG.items():
        slabs = _cols_to_slabs(local[n]) if how == "cols" else _rows_to_slabs(local[n])
        recv = _exchange(slabs, "scatter_" + n, slabs=True)
        shp = wts[n].shape
        r, c = shp[0] * shp[1], shp[2]
        res = _adamw(wts[n].reshape(r, c), recv.reshape(N_DEV, r, c), mom[n].reshape(r, c), vel[n].reshape(r, c),
                     "adamw_" + n)
        out[n] = [a.reshape(shp) for a in res]
    full_shapes = [local[n].shape for n in _SMALL]
    gathered = _exchange(_pack([local[n] for n in _SMALL]), "gather_small_grads", slabs=False)
    gs = []
    for n, g in zip(_SMALL, _unpack(gathered, full_shapes, lead=1)):
        if n in _SMALL_SHARDED:
            width = wts[n].shape[-1]
            g = lax.dynamic_slice_in_dim(g, me * width, width, axis=g.ndim - 1)
        gs.append(g)
    shapes = [wts[n].shape for n in _SMALL]
    res = _adamw(_pack([wts[n] for n in _SMALL]), _pack(gs, lead=1), _pack([mom[n] for n in _SMALL]),
                 _pack([vel[n] for n in _SMALL]), "adamw_small")
    for k, n in enumerate(_SMALL):
        out[n] = [_unpack(a, shapes)[k] for a in res]

    return (loss, grad_x, *[out[n][0] for n in _WEIGHTS], *[out[n][1] for n in _WEIGHTS],
            *[out[n][2] for n in _WEIGHTS], *[out[n][3] for n in _WEIGHTS])
```

```python
import functools
import math

import jax
import jax.numpy as jnp
from jax import lax
from jax.experimental import pallas as pl
from jax.experimental.pallas import tpu as pltpu

F32 = jnp.float32
MM_DTYPE = jnp.bfloat16

N_DEV = 8
D_MODEL = 1024
D_INNER = 2048
CHUNK = 64
HEAD = 128
GDN_V_HEADS = 16
GDN_GROUP = 16
GDN_QK_HEADS = 8
GDN_QK_DIM = 1024
GDN_CONV_DIM = 4096
SSD_PAIRS = 16
SSD_GROUPS = 4
SSD_STATE = 128
SSD_CONV_DIM = 3072
DEPTH = 4
ALPHA = (2 * DEPTH) ** 0.25
RMS_EPS = 1e-6
LN_EPS = 1e-5
L2_EPS = 1e-6
ADAM_LR, ADAM_B1, ADAM_B2, ADAM_EPS, ADAM_WD, ADAM_STEP = 0.001, 0.9, 0.999, 1e-08, 0.01, 10

VMEM_LIMIT_BYTES = 48 * 1024 * 1024
NEG_BIG = -1e30

_NN = (((1,), (0,)), ((), ()))
_NT = (((1,), (1,)), ((), ()))
_TN = (((0,), (0,)), ((), ()))


def _mm(a, b, dims):
    return lax.dot_general(a.astype(MM_DTYPE), b.astype(MM_DTYPE), dims, preferred_element_type=F32)


def _mmx(a, b, dims):
    return lax.dot_general(a, b, dims, precision=lax.Precision.HIGHEST, preferred_element_type=F32)


def _iota(shape, dim):
    return lax.broadcasted_iota(jnp.int32, shape, dim)


def _eye(n):
    return (_iota((n, n), 0) == _iota((n, n), 1)).astype(F32)


def _sig(x):
    return jax.nn.sigmoid(x)


def _silu(x):
    return x * _sig(x)


def _dsilu(x):
    s = _sig(x)
    return s * (1.0 + x * (1.0 - s))


def _softplus(x):
    return jnp.maximum(x, 0.0) + jnp.log(1.0 + jnp.exp(-jnp.abs(x)))


def _col(x, h):
    return jnp.sum(jnp.where(_iota(x.shape, 1) == h, x, 0.0), axis=1, keepdims=True)


def _row(x, h):
    return jnp.sum(jnp.where(_iota(x.shape, 0) == h, x, 0.0), axis=0, keepdims=True)


def _put_col(acc, col, h):
    return jnp.where(_iota(acc.shape, 1) == h, col, acc)


def _put_row(acc, row, h):
    return jnp.where(_iota(acc.shape, 0) == h, row, acc)


def _put_sub(acc, row, j):
    return acc + jnp.where(_iota(acc.shape, 0) == j, row, 0.0)


def _lanes(h):
    return pl.ds(h * HEAD, HEAD) if isinstance(h, int) else pl.ds(pl.multiple_of(h * HEAD, HEAD), HEAD)


def _total(x):
    return jnp.sum(jnp.sum(x, axis=1, keepdims=True), axis=0, keepdims=True)


def _call(body, name, grid, in_specs, out_specs, out_shape, scratch_shapes=(), semantics=None):
    return pl.pallas_call(
        body, name=name, grid=grid, in_specs=in_specs, out_specs=out_specs, out_shape=out_shape,
        scratch_shapes=list(scratch_shapes),
        compiler_params=pltpu.CompilerParams(dimension_semantics=semantics, vmem_limit_bytes=VMEM_LIMIT_BYTES))


def _tile(n, pref):
    if n <= pref:
        return n
    t = pref
    while n % t:
        t -= 128
    return t


def _exchange(src, name, slabs):
    shape = src.shape[1:] if slabs else src.shape

    def body(src_ref, out_ref, send_sems, recv_sems, local_sem):
        x, y, c = lax.axis_index("x"), lax.axis_index("y"), lax.axis_index("c")
        me = 4 * x + 2 * y + c
        mine = src_ref.at[me] if slabs else src_ref
        local = pltpu.make_async_copy(mine, out_ref.at[me], local_sem)
        local.start()
        sends = []
        for r in range(1, N_DEV):
            px = 1 - x if r & 4 else x
            py = 1 - y if r & 2 else y
            pc = 1 - c if r & 1 else c
            peer = 4 * px + 2 * py + pc
            cp = pltpu.make_async_remote_copy(
                src_ref=src_ref.at[peer] if slabs else src_ref, dst_ref=out_ref.at[me],
                send_sem=send_sems.at[r - 1], recv_sem=recv_sems.at[r - 1],
                device_id=(px, py, pc), device_id_type=pl.DeviceIdType.MESH)
            cp.start()
            sends.append((cp, peer, (px, py, pc)))
        for r, (cp, peer, pid) in enumerate(sends):
            pltpu.make_async_remote_copy(
                src_ref=mine, dst_ref=out_ref.at[peer], send_sem=send_sems.at[r], recv_sem=recv_sems.at[r],
                device_id=pid, device_id_type=pl.DeviceIdType.MESH).wait_recv()
        for cp, _, _ in sends:
            cp.wait_send()
        local.wait()

    return pl.pallas_call(
        body, name=name,
        in_specs=[pl.BlockSpec(memory_space=pl.ANY)], out_specs=pl.BlockSpec(memory_space=pl.ANY),
        out_shape=jax.ShapeDtypeStruct((N_DEV,) + tuple(shape), src.dtype),
        scratch_shapes=[pltpu.SemaphoreType.DMA((N_DEV - 1,)), pltpu.SemaphoreType.DMA((N_DEV - 1,)),
                        pltpu.SemaphoreType.DMA(())],
    )(src)


def _matmul(a, b, mode, name, add=None, add_scale=1.0):
    if mode == "nn":
        (m, k), (_, n) = a.shape, b.shape
    elif mode == "nt":
        (m, k), (n, _) = a.shape, b.shape
    else:
        (k, m), (_, n) = a.shape, b.shape
    tm, tn, tk = _tile(m, 512), _tile(n, 1024), _tile(k, 1024)
    nk = k // tk
    dims = {"nn": _NN, "nt": _NT, "tn": _TN}[mode]

    def body(*refs):
        if add is None:
            a_ref, b_ref, o_ref, acc = refs
        else:
            a_ref, b_ref, r_ref, o_ref, acc = refs
        kk = pl.program_id(2)

        @pl.when(kk == 0)
        def _():
            acc[...] = jnp.zeros_like(acc)

        acc[...] += _mm(a_ref[...], b_ref[...], dims)

        @pl.when(kk == nk - 1)
        def _():
            if add is None:
                o_ref[...] = acc[...]
            else:
                o_ref[...] = acc[...] + add_scale * r_ref[...]

    if mode == "nn":
        specs = [pl.BlockSpec((tm, tk), lambda i, j, q: (i, q)), pl.BlockSpec((tk, tn), lambda i, j, q: (q, j))]
    elif mode == "nt":
        specs = [pl.BlockSpec((tm, tk), lambda i, j, q: (i, q)), pl.BlockSpec((tn, tk), lambda i, j, q: (j, q))]
    else:
        specs = [pl.BlockSpec((tk, tm), lambda i, j, q: (q, i)), pl.BlockSpec((tk, tn), lambda i, j, q: (q, j))]
    out_spec = pl.BlockSpec((tm, tn), lambda i, j, q: (i, j))
    args = [a, b]
    if add is not None:
        specs.append(out_spec)
        args.append(add)
    return _call(body, name, (m // tm, n // tn, nk), specs, out_spec, jax.ShapeDtypeStruct((m, n), F32),
                 [pltpu.VMEM((tm, tn), F32)], ("parallel", "parallel", "arbitrary"))(*args)


CONV_TB = 512
CONV_CB = 512
HALO = 8


def _conv_specs(t, cb_n):
    tb = min(CONV_TB, t)
    nb = tb // HALO
    blk = pl.BlockSpec((tb, cb_n), lambda c, i: (i, c))
    prev = pl.BlockSpec((HALO, cb_n), lambda c, i: (jnp.maximum(i * nb - 1, 0), c))
    nxt = pl.BlockSpec((HALO, cb_n), lambda c, i: (jnp.minimum((i + 1) * nb, t // HALO - 1), c))
    w = pl.BlockSpec((8, cb_n), lambda c, i: (0, c))
    return blk, prev, nxt, w


def _shift_down(ext, s, tb):
    return (pltpu.roll(ext, s, 0) if s else ext)[HALO:HALO + tb]


def _shift_up(ext, s, tb):
    n = ext.shape[0]
    return (pltpu.roll(ext, n - s, 0) if s else ext)[0:tb]


def _conv_fwd(u, w8, ktaps, name, u2=None, bias=False):
    t, ch = u.shape
    cb_n = min(CONV_CB, ch)
    tb = min(CONV_TB, t)
    two = u2 is not None

    def body(*refs):
        if two:
            u_ref, up_ref, v_ref, vp_ref, w_ref, o_ref = refs
        else:
            u_ref, up_ref, w_ref, o_ref = refs
        first = pl.program_id(1) == 0
        blk, halo = u_ref[...], up_ref[...]
        if two:
            blk, halo = blk * v_ref[...], halo * vp_ref[...]
        ext = jnp.concatenate([jnp.where(first, 0.0, halo), blk], axis=0)
        acc = jnp.zeros((tb, cb_n), F32)
        for j in range(ktaps):
            acc = acc + w_ref[j:j + 1, :] * _shift_down(ext, ktaps - 1 - j, tb)
        if bias:
            acc = acc + w_ref[ktaps:ktaps + 1, :]
        o_ref[...] = acc

    blk, prev, _, wspec = _conv_specs(t, cb_n)
    specs, args = [blk, prev], [u, u]
    if two:
        specs += [blk, prev]
        args += [u2, u2]
    specs.append(wspec)
    args.append(w8)
    return _call(body, name, (ch // cb_n, t // tb), specs, blk, jax.ShapeDtypeStruct((t, ch), F32),
                 semantics=("parallel", "parallel"))(*args)


def _conv_bwd(dc, u, w8, ktaps, name, u2=None):
    t, ch = u.shape
    cb_n = min(CONV_CB, ch)
    tb = min(CONV_TB, t)
    two = u2 is not None

    def body(*refs):
        if two:
            dc_ref, dn_ref, u_ref, up_ref, v_ref, vp_ref, w_ref, du_ref, dv_ref, dw_ref = refs
        else:
            dc_ref, dn_ref, u_ref, up_ref, w_ref, du_ref, dw_ref = refs
        i = pl.program_id(1)
        first, last = i == 0, i == t // tb - 1
        d = dc_ref[...]
        dext = jnp.concatenate([d, jnp.where(last, 0.0, dn_ref[...])], axis=0)
        blk, halo = u_ref[...], up_ref[...]
        if two:
            blk, halo = blk * v_ref[...], halo * vp_ref[...]
        uext = jnp.concatenate([jnp.where(first, 0.0, halo), blk], axis=0)
        du = jnp.zeros((tb, cb_n), F32)
        dw = jnp.zeros((8, cb_n), F32)
        for j in range(ktaps):
            s = ktaps - 1 - j
            du = du + w_ref[j:j + 1, :] * _shift_up(dext, s, tb)
            dw = _put_sub(dw, jnp.sum(d * _shift_down(uext, s, tb), axis=0, keepdims=True), j)
        dw = _put_sub(dw, jnp.sum(d, axis=0, keepdims=True), ktaps)
        if two:
            du_ref[...] = du * v_ref[...]
            dv_ref[...] = du * u_ref[...]
        else:
            du_ref[...] = du

        @pl.when(first)
        def _():
            dw_ref[...] = jnp.zeros_like(dw_ref)

        dw_ref[...] += dw

    blk, prev, nxt, wspec = _conv_specs(t, cb_n)
    specs, args = [blk, nxt, blk, prev], [dc, dc, u, u]
    if two:
        specs += [blk, prev]
        args += [u2, u2]
    specs.append(wspec)
    args.append(w8)
    act = jax.ShapeDtypeStruct((t, ch), F32)
    outs = ([blk, blk, wspec], [act, act, jax.ShapeDtypeStruct((8, ch), F32)]) if two else \
        ([blk, wspec], [act, jax.ShapeDtypeStruct((8, ch), F32)])
    return _call(body, name, (ch // cb_n, t // tb), specs, outs[0], outs[1],
                 semantics=("parallel", "arbitrary"))(*args)


EW_TB = 256


def _chunk_mask(n, upper):
    i, j = _iota((n, n), 0), _iota((n, n), 1)
    same = jnp.right_shift(i, 6) == jnp.right_shift(j, 6)
    return (same & ((j >= i) if upper else (i >= j))).astype(F32)


def _rows(width, tb=EW_TB):
    return pl.BlockSpec((tb, width), lambda i: (i, 0))


def _const(rows, width):
    return pl.BlockSpec((rows, width), lambda i: (0, 0))


def _gdn_ew_fwd(c, ba, a_log, dt_bias, name):
    t = c.shape[0]
    tb = EW_TB

    def body(c_ref, ba_ref, al_ref, db_ref, q_ref, k_ref, v_ref, beta_ref, gc_ref):
        for h in range(GDN_QK_HEADS):
            for base, ref, scale in ((0, q_ref, HEAD ** -0.5), (GDN_QK_DIM, k_ref, 1.0)):
                s = _silu(c_ref[:, base + h * HEAD: base + (h + 1) * HEAD])
                r = lax.rsqrt(jnp.sum(s * s, axis=1, keepdims=True) + L2_EPS)
                ref[:, h * HEAD:(h + 1) * HEAD] = s * (r * scale)
        v_ref[...] = _silu(c_ref[:, 2 * GDN_QK_DIM:])
        beta_ref[...] = _sig(ba_ref[:, :HEAD])
        g = -jnp.exp(al_ref[...]) * _softplus(ba_ref[:, HEAD:] + db_ref[...])
        gc_ref[...] = _mmx(_chunk_mask(tb, False), g, _NN)

    act = lambda w: jax.ShapeDtypeStruct((t, w), F32)
    return _call(body, name, (t // tb,),
                 [_rows(GDN_CONV_DIM), _rows(2 * HEAD), _const(1, HEAD), _const(1, HEAD)],
                 [_rows(GDN_QK_DIM), _rows(GDN_QK_DIM), _rows(D_INNER), _rows(HEAD), _rows(HEAD)],
                 [act(GDN_QK_DIM), act(GDN_QK_DIM), act(D_INNER), act(HEAD), act(HEAD)],
                 semantics=("parallel",))(c, ba, a_log, dt_bias)


def _gdn_ew_bwd(c, ba, a_log, dt_bias, dqh, dkh, dv, dbeta, dgc, name):
    t = c.shape[0]
    tb = EW_TB

    def body(c_ref, ba_ref, al_ref, db_ref, dq_ref, dk_ref, dv_ref, dbeta_ref, dgc_ref, dc_ref, dba_ref, acc_ref):
        for h in range(GDN_QK_HEADS):
            for base, ref, scale in ((0, dq_ref, HEAD ** -0.5), (GDN_QK_DIM, dk_ref, 1.0)):
                cq = c_ref[:, base + h * HEAD: base + (h + 1) * HEAD]
                s = _silu(cq)
                r = lax.rsqrt(jnp.sum(s * s, axis=1, keepdims=True) + L2_EPS)
                dn = (ref[:, 2 * h * HEAD:(2 * h + 1) * HEAD] + ref[:, (2 * h + 1) * HEAD:(2 * h + 2) * HEAD]) * scale
                ds = r * dn - s * (r * r * r) * jnp.sum(dn * s, axis=1, keepdims=True)
                dc_ref[:, base + h * HEAD: base + (h + 1) * HEAD] = ds * _dsilu(cq)
        dc_ref[:, 2 * GDN_QK_DIM:] = dv_ref[...] * _dsilu(c_ref[:, 2 * GDN_QK_DIM:])
        beta = _sig(ba_ref[:, :HEAD])
        dba_ref[:, :HEAD] = dbeta_ref[...] * beta * (1.0 - beta)
        pre = ba_ref[:, HEAD:] + db_ref[...]
        ea = jnp.exp(al_ref[...])
        g = -ea * _softplus(pre)
        dg = _mmx(_chunk_mask(tb, True), dgc_ref[...], _NN)
        da_raw = dg * (-ea) * _sig(pre)
        dba_ref[:, HEAD:] = da_raw
        acc = jnp.zeros((8, HEAD), F32)
        acc = _put_sub(acc, jnp.sum(dg * g, axis=0, keepdims=True), 0)
        acc = _put_sub(acc, jnp.sum(da_raw, axis=0, keepdims=True), 1)

        @pl.when(pl.program_id(0) == 0)
        def _():
            acc_ref[...] = jnp.zeros_like(acc_ref)

        acc_ref[...] += acc

    act = lambda w: jax.ShapeDtypeStruct((t, w), F32)
    return _call(body, name, (t // tb,),
                 [_rows(GDN_CONV_DIM), _rows(2 * HEAD), _const(1, HEAD), _const(1, HEAD),
                  _rows(D_INNER), _rows(D_INNER), _rows(D_INNER), _rows(HEAD), _rows(HEAD)],
                 [_rows(GDN_CONV_DIM), _rows(2 * HEAD), _const(8, HEAD)],
                 [act(GDN_CONV_DIM), act(2 * HEAD), jax.ShapeDtypeStruct((8, HEAD), F32)],
                 semantics=("arbitrary",))(c, ba, a_log, dt_bias, dqh, dkh, dv, dbeta, dgc)


def _zip(fn, *lists):
    return [fn(*xs) for xs in zip(*lists)]


def _mms(xs, ys, dims):
    return [_mm(x, y, dims) for x, y in zip(xs, ys)]


def _gdn_heads_fwd(q, k, v, bcol, gcol, grow, glast, s_in):
    ii, jj = _iota((CHUNK, CHUNK), 0), _iota((CHUNK, CHUNK), 1)
    eye = _eye(CHUNK)
    mul = lambda x, y: x * y
    eg = [jnp.exp(g) for g in gcol]
    decay = _zip(lambda gc, gr: jnp.exp(jnp.where(ii >= jj, gc - gr, NEG_BIG)), gcol, grow)
    kb = _zip(mul, k, bcol)
    p = _mms(kb, k, _NT)
    a = _zip(lambda x, d: jnp.where(ii > jj, x * d, 0.0), p, decay)
    inv, pw = [eye - x for x in a], a
    for _ in range(5):
        pw = _mms(pw, pw, _NN)
        inv = _zip(lambda x, y: x + y, inv, _mms(inv, pw, _NN))
    rv, rk = _zip(mul, v, bcol), _zip(mul, kb, eg)
    u, w = _mms(inv, rv, _NN), _mms(inv, rk, _NN)
    vn = _zip(lambda x, y: x - y, u, _mms(w, s_in, _NN))
    qk = _mms(q, k, _NT)
    att = _zip(mul, qk, decay)
    qd = _zip(mul, q, eg)
    out = _zip(lambda x, y: x + y, _mms(qd, s_in, _NN), _mms(att, vn, _NN))
    ekt = _zip(lambda gl, gc: jnp.exp(gl - gc), glast, gcol)
    kt = _zip(mul, k, ekt)
    el = [jnp.exp(g) for g in glast]
    s_out = _zip(lambda s, e, y: s * e + y, s_in, el, _mms(kt, vn, _TN))
    return dict(eg=eg, decay=decay, kb=kb, p=p, inv=inv, rv=rv, rk=rk, u=u, w=w, vn=vn, qk=qk, att=att, qd=qd,
                out=out, ekt=ekt, kt=kt, el=el, s_out=s_out)


def _head_groups(group, init):
    if GDN_GROUP == GDN_V_HEADS:
        return group(0, init)
    return lax.fori_loop(0, GDN_V_HEADS // GDN_GROUP, lambda gi, c: group(GDN_GROUP * gi, c), init)


def _half(h):
    return h // 2 if isinstance(h, int) else jnp.right_shift(h, 1)


def _gdn_chunk_fwd(qn, kn, v, beta, gc, z, norm_w, name):
    t = qn.shape[0]
    nc = t // CHUNK

    def body(q_ref, k_ref, v_ref, beta_ref, gc_ref, z_ref, nw_ref, o_ref, h_ref, st_ref, state):
        @pl.when(pl.program_id(0) == 0)
        def _():
            state[...] = jnp.zeros_like(state)

        st_ref[0] = state[...]
        gc_all, beta_all = gc_ref[...], beta_ref[...]
        gct = _mmx(_eye(HEAD), gc_all, _NT)
        glast_all = gc_ref[CHUNK - 1:CHUNK, :]
        nw = nw_ref[...]

        def group(h0, carry):
            heads = [h0 + s for s in range(GDN_GROUP)]
            f = _gdn_heads_fwd([q_ref[:, _lanes(_half(h))] for h in heads], [k_ref[:, _lanes(_half(h))] for h in heads],
                               [v_ref[:, _lanes(h)] for h in heads], [_col(beta_all, h) for h in heads],
                               [_col(gc_all, h) for h in heads], [_row(gct, h) for h in heads],
                               [_col(glast_all, h) for h in heads], [state[h] for h in heads])
            for h, s_out, o in zip(heads, f["s_out"], f["out"]):
                state[h] = s_out
                o_ref[:, _lanes(h)] = o
                rstd = lax.rsqrt(jnp.mean(o * o, axis=1, keepdims=True) + RMS_EPS)
                h_ref[:, _lanes(h)] = o * rstd * nw * _silu(z_ref[:, _lanes(h)])
            return carry

        _head_groups(group, 0)

    rows = lambda w: pl.BlockSpec((CHUNK, w), lambda i: (i, 0))
    act = lambda w: jax.ShapeDtypeStruct((t, w), F32)
    return _call(body, name, (nc,),
                 [rows(GDN_QK_DIM), rows(GDN_QK_DIM), rows(D_INNER), rows(HEAD), rows(HEAD), rows(D_INNER),
                  _const(1, HEAD)],
                 [rows(D_INNER), rows(D_INNER), pl.BlockSpec((1, GDN_V_HEADS, HEAD, HEAD), lambda i: (i, 0, 0, 0))],
                 [act(D_INNER), act(D_INNER), jax.ShapeDtypeStruct((nc, GDN_V_HEADS, HEAD, HEAD), F32)],
                 [pltpu.VMEM((GDN_V_HEADS, HEAD, HEAD), F32)], ("arbitrary",))(qn, kn, v, beta, gc, z, norm_w)


def _gdn_chunk_bwd(qn, kn, v, beta, gc, z, norm_w, o, states, dh, name):
    t = qn.shape[0]
    nc = t // CHUNK

    def body(q_ref, k_ref, v_ref, beta_ref, gc_ref, z_ref, nw_ref, o_ref, st_ref, dh_ref,
             dq_ref, dk_ref, dv_ref, dz_ref, dbeta_ref, dgc_ref, acc_ref, dstate):
        @pl.when(pl.program_id(0) == 0)
        def _():
            dstate[...] = jnp.zeros_like(dstate)
            acc_ref[...] = jnp.zeros_like(acc_ref)

        gc_all, beta_all = gc_ref[...], beta_ref[...]
        gct = _mmx(_eye(HEAD), gc_all, _NT)
        glast_all = gc_ref[CHUNK - 1:CHUNK, :]
        nw = nw_ref[...]
        ii, jj = _iota((CHUNK, CHUNK), 0), _iota((CHUNK, CHUNK), 1)
        last_row = _iota((CHUNK, 1), 0) == CHUNK - 1

        def group(h0, carry):
            dbeta_acc, dgc_acc, dgrow_acc, dnw_acc = carry
            heads = [h0 + s for s in range(GDN_GROUP)]
            mul, add, sub = (lambda x, y: x * y), (lambda x, y: x + y), (lambda x, y: x - y)
            rowsum = lambda x, y: jnp.sum(x * y, axis=1, keepdims=True)
            q, k = [q_ref[:, _lanes(_half(h))] for h in heads], [k_ref[:, _lanes(_half(h))] for h in heads]
            vv = [v_ref[:, _lanes(h)] for h in heads]
            bcol, gcol = [_col(beta_all, h) for h in heads], [_col(gc_all, h) for h in heads]
            s_in, dsn = [st_ref[0, h] for h in heads], [dstate[h] for h in heads]
            f = _gdn_heads_fwd(q, k, vv, bcol, gcol, [_row(gct, h) for h in heads],
                               [_col(glast_all, h) for h in heads], s_in)
            do = []
            for h in heads:
                oo, zz, dhh = o_ref[:, _lanes(h)], z_ref[:, _lanes(h)], dh_ref[:, _lanes(h)]
                rstd = lax.rsqrt(jnp.mean(oo * oo, axis=1, keepdims=True) + RMS_EPS)
                on, sz = oo * rstd, _silu(zz)
                dnw_acc = dnw_acc + jnp.sum(dhh * on * sz, axis=0, keepdims=True)
                dz_ref[:, _lanes(h)] = dhh * on * nw * _dsilu(zz)
                don = dhh * nw * sz
                do.append(rstd * (don - on * jnp.mean(don * on, axis=1, keepdims=True)))
            decay, eg, inv = f["decay"], f["eg"], f["inv"]
            d_glast = _zip(lambda d, s, e: _total(d * s) * e, dsn, s_in, f["el"])
            dkt = _mms(f["vn"], dsn, _NT)
            dvn = _mms(f["kt"], dsn, _NN)
            dqd = _mms(do, s_in, _NT)
            ds_prev = _zip(lambda d, e, y: d * e + y, dsn, f["el"], _mms(f["qd"], do, _TN))
            datt = _mms(do, f["vn"], _NT)
            dvn = _zip(add, dvn, _mms(f["att"], do, _TN))
            dqk = _zip(mul, datt, decay)
            dq = _zip(lambda x, e, y: x * e + y, dqd, eg, _mms(dqk, k, _NN))
            dk = _mms(dqk, q, _TN)
            ddecay = _zip(mul, datt, f["qk"])
            dgcol = _zip(rowsum, dqd, f["qd"])
            dw = [-x for x in _mms(dvn, s_in, _NT)]
            ds_prev = _zip(sub, ds_prev, _mms(f["w"], dvn, _TN))
            drv, drk = _mms(inv, dvn, _TN), _mms(inv, dw, _TN)
            da = _zip(lambda x, y: jnp.where(ii > jj, -(x + y), 0.0), _mms(drv, f["u"], _NT), _mms(drk, f["w"], _NT))
            dp = _zip(mul, da, decay)
            ddecay = _zip(lambda x, y, z_: x + y * z_, ddecay, da, f["p"])
            dkb = _zip(lambda x, y, e: x + y * e, _mms(dp, k, _NN), drk, eg)
            dk = _zip(add, dk, _mms(dp, f["kb"], _TN))
            dbeta = _zip(add, _zip(rowsum, drv, vv), _zip(rowsum, dkb, k))
            dgcol = _zip(add, dgcol, _zip(rowsum, drk, f["rk"]))
            dk = _zip(lambda x, y, b_, z_, e: x + y * b_ + z_ * e, dk, dkb, bcol, dkt, f["ekt"])
            tail = _zip(mul, dkt, f["kt"])
            d_glast = _zip(lambda x, y: x + _total(y), d_glast, tail)
            e_ = _zip(mul, ddecay, decay)
            dgcol = _zip(lambda x, t_, e, gl: x - jnp.sum(t_, axis=1, keepdims=True) + jnp.sum(e, axis=1, keepdims=True)
                         + jnp.where(last_row, gl, 0.0), dgcol, tail, e_, d_glast)
            for i_, h in enumerate(heads):
                dstate[h] = ds_prev[i_]
                dq_ref[:, _lanes(h)] = dq[i_]
                dk_ref[:, _lanes(h)] = dk[i_]
                dv_ref[:, _lanes(h)] = drv[i_] * bcol[i_]
                dbeta_acc = _put_col(dbeta_acc, dbeta[i_], h)
                dgc_acc = _put_col(dgc_acc, dgcol[i_], h)
                dgrow_acc = _put_row(dgrow_acc, -jnp.sum(e_[i_], axis=0, keepdims=True), h)
            return dbeta_acc, dgc_acc, dgrow_acc, dnw_acc

        zero = jnp.zeros((CHUNK, HEAD), F32)
        dbeta_acc, dgc_acc, dgrow_acc, dnw_acc = _head_groups(
            group, (zero, zero, jnp.zeros((HEAD, CHUNK), F32), jnp.zeros((1, HEAD), F32)))
        dbeta_ref[...] = dbeta_acc
        dgc_ref[...] = dgc_acc + _mmx(_eye(CHUNK), dgrow_acc, _NT)
        acc_ref[...] += _put_sub(jnp.zeros((8, HEAD), F32), dnw_acc, 0)

    rows = lambda w: pl.BlockSpec((CHUNK, w), lambda i: (nc - 1 - i, 0))
    act = lambda w: jax.ShapeDtypeStruct((t, w), F32)
    return _call(body, name, (nc,),
                 [rows(GDN_QK_DIM), rows(GDN_QK_DIM), rows(D_INNER), rows(HEAD), rows(HEAD), rows(D_INNER),
                  _const(1, HEAD), rows(D_INNER),
                  pl.BlockSpec((1, GDN_V_HEADS, HEAD, HEAD), lambda i: (nc - 1 - i, 0, 0, 0)), rows(D_INNER)],
                 [rows(D_INNER), rows(D_INNER), rows(D_INNER), rows(D_INNER), rows(HEAD), rows(HEAD), _const(8, HEAD)],
                 [act(D_INNER), act(D_INNER), act(D_INNER), act(D_INNER), act(HEAD), act(HEAD),
                  jax.ShapeDtypeStruct((8, HEAD), F32)],
                 [pltpu.VMEM((GDN_V_HEADS, HEAD, HEAD), F32)], ("arbitrary",)
                 )(qn, kn, v, beta, gc, z, norm_w, o, states, dh)


def _sc_gate_fwd(bg, cv, z, name):
    t, w = bg.shape

    def body(b_ref, c_ref, z_ref, o_ref):
        o_ref[...] = b_ref[...] * c_ref[...] * _silu(z_ref[...])

    return _call(body, name, (t // EW_TB,), [_rows(w)] * 3, _rows(w), jax.ShapeDtypeStruct((t, w), F32),
                 semantics=("parallel",))(bg, cv, z)


def _sc_gate_bwd(dh, bg, cv, z, name):
    t, w = bg.shape

    def body(d_ref, b_ref, c_ref, z_ref, db_ref, dc_ref, dz_ref):
        d, b, c, zz = d_ref[...], b_ref[...], c_ref[...], z_ref[...]
        sz = _silu(zz)
        db_ref[...] = d * c * sz
        dc_ref[...] = d * b * sz
        dz_ref[...] = d * b * c * _dsilu(zz)

    act = jax.ShapeDtypeStruct((t, w), F32)
    return _call(body, name, (t // EW_TB,), [_rows(w)] * 4, [_rows(w)] * 3, [act] * 3,
                 semantics=("parallel",))(dh, bg, cv, z)


XBC_B = D_INNER
XBC_C = D_INNER + SSD_GROUPS * SSD_STATE


def _ssd_scalars(dtp, dt_bias, a_log):
    dt = _softplus(dtp + dt_bias)
    a = -jnp.exp(a_log)
    da = dt * a
    ac = _mmx(_chunk_mask(CHUNK, False), da, _NN)
    act = _mmx(_eye(HEAD), ac, _NT)
    aclast = jnp.sum(jnp.where(_iota(ac.shape, 0) == CHUNK - 1, ac, 0.0), axis=0, keepdims=True)
    return dt, a, da, ac, act, aclast


def _ssd_pairs_fwd(x2, bg, cg, cb, dt, ac, act, aclast, s2):
    ii, jj = _iota((CHUNK, CHUNK), 0), _iota((CHUNK, CHUNK), 1)
    half = _iota((CHUNK, HEAD), 1) < 64
    causal = ii >= jj
    pairs = range(len(x2))
    mul = lambda x, y: x * y
    pick = lambda a, b: jnp.where(half, a, b)
    aca, acb = [_col(ac, 2 * p) for p in pairs], [_col(ac, 2 * p + 1) for p in pairs]
    la, lb = [_col(aclast, 2 * p) for p in pairs], [_col(aclast, 2 * p + 1) for p in pairs]
    dt2 = [pick(_col(dt, 2 * p), _col(dt, 2 * p + 1)) for p in pairs]
    xdt = _zip(mul, x2, dt2)
    sega = [jnp.exp(jnp.where(causal, aca[p] - _row(act, 2 * p), NEG_BIG)) for p in pairs]
    segb = [jnp.exp(jnp.where(causal, acb[p] - _row(act, 2 * p + 1), NEG_BIG)) for p in pairs]
    ma, mb = _zip(mul, sega, cb), _zip(mul, segb, cb)
    ydiag = _zip(pick, _mms(ma, xdt, _NN), _mms(mb, xdt, _NN))
    cdec = _zip(lambda a, b: pick(jnp.exp(a), jnp.exp(b)), aca, acb)
    cs = _mms(cg, s2, _NT)
    tail = _zip(lambda l1, a, l2, b: pick(jnp.exp(l1 - a), jnp.exp(l2 - b)), la, aca, lb, acb)
    zt = _zip(mul, xdt, tail)
    ea, eb = [jnp.exp(x) for x in la], [jnp.exp(x) for x in lb]
    tot = _zip(lambda a, b: jnp.where(_iota((HEAD, 1), 0) < 64, a, b), ea, eb)
    s_out = _zip(lambda s, t_, y: s * t_ + y, s2, tot, _mms(zt, bg, _TN))
    return dict(half=half, dt2=dt2, xdt=xdt, sega=sega, segb=segb, ma=ma, mb=mb, ydiag=ydiag, cdec=cdec, cs=cs,
                tail=tail, zt=zt, ea=ea, eb=eb, tot=tot, s_out=s_out)


def _ssd_group_inputs(cx_ref):
    cxb = [cx_ref[:, XBC_B + g * SSD_STATE: XBC_B + (g + 1) * SSD_STATE] for g in range(SSD_GROUPS)]
    cxc = [cx_ref[:, XBC_C + g * SSD_STATE: XBC_C + (g + 1) * SSD_STATE] for g in range(SSD_GROUPS)]
    bg, cg = [_silu(x) for x in cxb], [_silu(x) for x in cxc]
    return cxb, cxc, bg, cg, _mms(cg, bg, _NT)


def _per_pair(group_list):
    return [group_list[p // (SSD_PAIRS // SSD_GROUPS)] for p in range(SSD_PAIRS)]


def _ssd_chunk_fwd(cx, dtp, z, dt_bias, a_log, dskip, norm_w, name):
    t = cx.shape[0]
    nc = t // CHUNK
    gw = D_INNER // SSD_GROUPS

    def body(cx_ref, dtp_ref, z_ref, db_ref, al_ref, sk_ref, nw_ref, y_ref, h_ref, st_ref, state):
        @pl.when(pl.program_id(0) == 0)
        def _():
            state[...] = jnp.zeros_like(state)

        st_ref[0] = state[...]
        dt, _, _, ac, act, aclast = _ssd_scalars(dtp_ref[...], db_ref[...], al_ref[...])
        _, _, bg, cg, cb = _ssd_group_inputs(cx_ref)
        x2 = [_silu(cx_ref[:, _lanes(p)]) for p in range(SSD_PAIRS)]
        f = _ssd_pairs_fwd(x2, _per_pair(bg), _per_pair(cg), _per_pair(cb), dt, ac, act, aclast,
                           [state[p] for p in range(SSD_PAIRS)])
        for p in range(SSD_PAIRS):
            state[p] = f["s_out"][p]
            y_ref[:, _lanes(p)] = f["ydiag"][p] + f["cs"][p] * f["cdec"][p] + sk_ref[:, _lanes(p)] * x2[p]
        for g in range(SSD_GROUPS):
            sl = slice(g * gw, (g + 1) * gw)
            yg = y_ref[:, sl] * _silu(z_ref[:, sl])
            rstd = lax.rsqrt(jnp.mean(yg * yg, axis=1, keepdims=True) + RMS_EPS)
            h_ref[:, sl] = yg * rstd * nw_ref[:, sl]

    rows = lambda w: pl.BlockSpec((CHUNK, w), lambda i: (i, 0))
    act_ = lambda w: jax.ShapeDtypeStruct((t, w), F32)
    return _call(body, name, (nc,),
                 [rows(SSD_CONV_DIM), rows(HEAD), rows(D_INNER), _const(1, HEAD), _const(1, HEAD),
                  _const(1, D_INNER), _const(1, D_INNER)],
                 [rows(D_INNER), rows(D_INNER), pl.BlockSpec((1, SSD_PAIRS, HEAD, SSD_STATE), lambda i: (i, 0, 0, 0))],
                 [act_(D_INNER), act_(D_INNER), jax.ShapeDtypeStruct((nc, SSD_PAIRS, HEAD, SSD_STATE), F32)],
                 [pltpu.VMEM((SSD_PAIRS, HEAD, SSD_STATE), F32)], ("arbitrary",)
                 )(cx, dtp, z, dt_bias, a_log, dskip, norm_w)


def _ssd_chunk_bwd(cx, dtp, z, dt_bias, a_log, dskip, norm_w, y, states, dh, name):
    t = cx.shape[0]
    nc = t // CHUNK
    gw = D_INNER // SSD_GROUPS

    def body(cx_ref, dtp_ref, z_ref, db_ref, al_ref, sk_ref, nw_ref, y_ref, st_ref, dh_ref,
             dcx_ref, ddtp_ref, dz_ref, wide_ref, acc_ref, dstate, dy_s):
        @pl.when(pl.program_id(0) == 0)
        def _():
            dstate[...] = jnp.zeros_like(dstate)
            wide_ref[...] = jnp.zeros_like(wide_ref)
            acc_ref[...] = jnp.zeros_like(acc_ref)

        dtp = dtp_ref[...]
        dt, a, da, ac, act, aclast = _ssd_scalars(dtp, db_ref[...], al_ref[...])
        ii, jj = _iota((CHUNK, CHUNK), 0), _iota((CHUNK, CHUNK), 1)
        last_row = _iota((CHUNK, 1), 0) == CHUNK - 1
        top = _iota((HEAD, SSD_STATE), 0) < 64
        for g in range(SSD_GROUPS):
            sl = slice(g * gw, (g + 1) * gw)
            yy, zz, dhh, nw = y_ref[:, sl], z_ref[:, sl], dh_ref[:, sl], nw_ref[:, sl]
            sz = _silu(zz)
            yg = yy * sz
            rstd = lax.rsqrt(jnp.mean(yg * yg, axis=1, keepdims=True) + RMS_EPS)
            n = yg * rstd
            dn = dhh * nw
            dyg = rstd * (dn - n * jnp.mean(dn * n, axis=1, keepdims=True))
            dy_s[:, sl] = dyg * sz
            dz_ref[:, sl] = dyg * yy * _dsilu(zz)
            wide_ref[0:1, sl] += jnp.sum(dhh * n, axis=0, keepdims=True)

        pairs = range(SSD_PAIRS)
        mul, add, sub = (lambda x, y: x * y), (lambda x, y: x + y), (lambda x, y: x - y)
        rowsum = lambda x: jnp.sum(x, axis=1, keepdims=True)
        cxb, cxc, bg, cg, cb = _ssd_group_inputs(cx_ref)
        bgp, cgp = _per_pair(bg), _per_pair(cg)
        cxx = [cx_ref[:, _lanes(p)] for p in pairs]
        x2 = [_silu(x) for x in cxx]
        s2, dsn = [st_ref[0, p] for p in pairs], [dstate[p] for p in pairs]
        dy2 = [dy_s[:, _lanes(p)] for p in pairs]
        f = _ssd_pairs_fwd(x2, bgp, cgp, _per_pair(cb), dt, ac, act, aclast, s2)
        half = f["half"]
        lo = lambda x: jnp.where(half, x, 0.0)
        dx2 = [dy2[p] * sk_ref[:, _lanes(p)] for p in pairs]
        for p in pairs:
            wide_ref[1:2, _lanes(p)] += jnp.sum(dy2[p] * x2[p], axis=0, keepdims=True)
        gg = _zip(mul, dy2, f["cdec"])
        dc_p = _mms(gg, s2, _NN)
        ds_prev = _zip(lambda d, t_, y: d * t_ + y, dsn, f["tot"], _mms(gg, cgp, _TN))
        t1 = _zip(lambda d, c, e: d * c * e, dy2, f["cs"], f["cdec"])
        dac_a = [rowsum(lo(x)) for x in t1]
        dac_b = _zip(lambda x, a_: rowsum(x) - a_, t1, dac_a)
        dya = [lo(x) for x in dy2]
        dma, dmb = _mms(dya, f["xdt"], _NT), _mms(_zip(sub, dy2, dya), f["xdt"], _NT)
        dxdt = _zip(lambda a_, b_: jnp.where(half, a_, b_), _mms(f["ma"], dy2, _TN), _mms(f["mb"], dy2, _TN))
        dcb_p = _zip(lambda a_, sa, b_, sb: a_ * sa + b_ * sb, dma, f["sega"], dmb, f["segb"])
        ea_, eb_ = _zip(mul, dma, f["ma"]), _zip(mul, dmb, f["mb"])
        dac_a = _zip(lambda x, e: x + rowsum(e), dac_a, ea_)
        dac_b = _zip(lambda x, e: x + rowsum(e), dac_b, eb_)
        dzt = _mms(bgp, dsn, _NT)
        db_p = _mms(f["zt"], dsn, _NN)
        dxdt = _zip(lambda x, d, t_: x + d * t_, dxdt, dzt, f["tail"])
        t2 = _zip(mul, dzt, f["zt"])
        t2a = [rowsum(lo(x)) for x in t2]
        t2b = _zip(lambda x, a_: rowsum(x) - a_, t2, t2a)
        t3 = _zip(mul, dsn, s2)
        t3a = [_total(jnp.where(top, x, 0.0)) for x in t3]
        dla = _zip(lambda x, y, e: _total(x) + y * e, t2a, t3a, f["ea"])
        dlb = _zip(lambda x, y, ya, e: _total(x) + (_total(y) - ya) * e, t2b, t3, t3a, f["eb"])
        dac_a = _zip(lambda x, y, l: x - y + jnp.where(last_row, l, 0.0), dac_a, t2a, dla)
        dac_b = _zip(lambda x, y, l: x - y + jnp.where(last_row, l, 0.0), dac_b, t2b, dlb)
        dx2 = _zip(lambda x, d, t_: x + d * t_, dx2, dxdt, f["dt2"])
        t4 = _zip(mul, dxdt, x2)
        t4a = [rowsum(lo(x)) for x in t4]
        t4b = _zip(lambda x, a_: rowsum(x) - a_, t4, t4a)
        zero = jnp.zeros((CHUNK, HEAD), F32)
        ddt_acc, dac_acc, drow_acc = zero, zero, jnp.zeros((HEAD, CHUNK), F32)
        for p in pairs:
            dcx_ref[:, _lanes(p)] = dx2[p] * _dsilu(cxx[p])
            dstate[p] = ds_prev[p]
            ddt_acc = _put_col(_put_col(ddt_acc, t4a[p], 2 * p), t4b[p], 2 * p + 1)
            dac_acc = _put_col(_put_col(dac_acc, dac_a[p], 2 * p), dac_b[p], 2 * p + 1)
            drow_acc = _put_row(_put_row(drow_acc, -jnp.sum(ea_[p], axis=0, keepdims=True), 2 * p),
                                -jnp.sum(eb_[p], axis=0, keepdims=True), 2 * p + 1)
        per = SSD_PAIRS // SSD_GROUPS
        gsum = lambda xs: [functools.reduce(add, xs[g * per:(g + 1) * per]) for g in range(SSD_GROUPS)]
        dcb = gsum(dcb_p)
        dc = _zip(add, gsum(dc_p), _mms(dcb, bg, _NN))
        db = _zip(add, gsum(db_p), _mms(dcb, cg, _TN))
        for g in range(SSD_GROUPS):
            dcx_ref[:, XBC_B + g * SSD_STATE: XBC_B + (g + 1) * SSD_STATE] = db[g] * _dsilu(cxb[g])
            dcx_ref[:, XBC_C + g * SSD_STATE: XBC_C + (g + 1) * SSD_STATE] = dc[g] * _dsilu(cxc[g])
        dac = dac_acc + _mmx(_eye(CHUNK), drow_acc, _NT)
        dda = _mmx(_chunk_mask(CHUNK, True), dac, _NN)
        ddt = ddt_acc + dda * a
        ddtp = ddt * _sig(dtp + db_ref[...])
        ddtp_ref[...] = ddtp
        acc = _put_sub(jnp.zeros((8, HEAD), F32), jnp.sum(dda * da, axis=0, keepdims=True), 0)
        acc_ref[...] += _put_sub(acc, jnp.sum(ddtp, axis=0, keepdims=True), 1)

    rows = lambda w: pl.BlockSpec((CHUNK, w), lambda i: (nc - 1 - i, 0))
    act_ = lambda w: jax.ShapeDtypeStruct((t, w), F32)
    return _call(body, name, (nc,),
                 [rows(SSD_CONV_DIM), rows(HEAD), rows(D_INNER), _const(1, HEAD), _const(1, HEAD),
                  _const(1, D_INNER), _const(1, D_INNER), rows(D_INNER),
                  pl.BlockSpec((1, SSD_PAIRS, HEAD, SSD_STATE), lambda i: (nc - 1 - i, 0, 0, 0)), rows(D_INNER)],
                 [rows(SSD_CONV_DIM), rows(HEAD), rows(D_INNER), _const(8, D_INNER), _const(8, HEAD)],
                 [act_(SSD_CONV_DIM), act_(HEAD), act_(D_INNER), jax.ShapeDtypeStruct((8, D_INNER), F32),
                  jax.ShapeDtypeStruct((8, HEAD), F32)],
                 [pltpu.VMEM((SSD_PAIRS, HEAD, SSD_STATE), F32), pltpu.VMEM((CHUNK, D_INNER), F32)], ("arbitrary",)
                 )(cx, dtp, z, dt_bias, a_log, dskip, norm_w, y, states, dh)


LN_TB = 512


def _ln_stats(x, y):
    u = ALPHA * x + y
    mu = jnp.mean(u, axis=1, keepdims=True)
    cen = u - mu
    rstd = lax.rsqrt(jnp.mean(cen * cen, axis=1, keepdims=True) + LN_EPS)
    return cen * rstd


def _ln_fwd(x, y, g, b, name):
    t, d = x.shape

    def body(x_ref, y_ref, g_ref, b_ref, o_ref):
        o_ref[...] = _ln_stats(x_ref[...], y_ref[...]) * g_ref[...] + b_ref[...]

    return _call(body, name, (t // LN_TB,), [_rows(d, LN_TB), _rows(d, LN_TB), _const(1, d), _const(1, d)],
                 _rows(d, LN_TB), jax.ShapeDtypeStruct((t, d), F32), semantics=("parallel",))(x, y, g, b)


def _ln_bwd(dout, x, y, g, name):
    t, d = x.shape

    def body(d_ref, x_ref, y_ref, g_ref, du_ref, acc_ref):
        u = ALPHA * x_ref[...] + y_ref[...]
        mu = jnp.mean(u, axis=1, keepdims=True)
        cen = u - mu
        rstd = lax.rsqrt(jnp.mean(cen * cen, axis=1, keepdims=True) + LN_EPS)
        xh = cen * rstd
        do = d_ref[...]
        dxh = do * g_ref[...]
        du_ref[...] = rstd * (dxh - jnp.mean(dxh, axis=1, keepdims=True)
                              - xh * jnp.mean(dxh * xh, axis=1, keepdims=True))
        acc = _put_sub(jnp.zeros((8, d), F32), jnp.sum(do * xh, axis=0, keepdims=True), 0)
        acc = _put_sub(acc, jnp.sum(do, axis=0, keepdims=True), 1)

        @pl.when(pl.program_id(0) == 0)
        def _():
            acc_ref[...] = jnp.zeros_like(acc_ref)

        acc_ref[...] += acc

    return _call(body, name, (t // LN_TB,), [_rows(d, LN_TB)] * 3 + [_const(1, d)],
                 [_rows(d, LN_TB), _const(8, d)],
                 [jax.ShapeDtypeStruct((t, d), F32), jax.ShapeDtypeStruct((8, d), F32)],
                 semantics=("arbitrary",))(dout, x, y, g)


def _loss_head(out, target, name):
    t, d = out.shape

    def body(o_ref, t_ref, d_ref, acc_ref):
        err = o_ref[...] - t_ref[...]
        d_ref[...] = err * (1.0 / d)

        @pl.when(pl.program_id(0) == 0)
        def _():
            acc_ref[...] = jnp.zeros_like(acc_ref)

        acc_ref[...] += _put_sub(jnp.zeros((8, d), F32), jnp.sum(err * err, axis=0, keepdims=True), 0)

    return _call(body, name, (t // LN_TB,), [_rows(d, LN_TB)] * 2, [_rows(d, LN_TB), _const(8, d)],
                 [jax.ShapeDtypeStruct((t, d), F32), jax.ShapeDtypeStruct((8, d), F32)],
                 semantics=("arbitrary",))(out, target)


def _adamw(w, gslots, m, v, name):
    r, c = w.shape
    rb = _tile_rows(r)
    c1 = 1.0 - ADAM_B1 ** ADAM_STEP
    c2 = 1.0 - ADAM_B2 ** ADAM_STEP

    def body(w_ref, g_ref, m_ref, v_ref, go_ref, d_ref, mo_ref, vo_ref):
        g = g_ref[0]
        for s in range(1, N_DEV):
            g = g + g_ref[s]
        mn = ADAM_B1 * m_ref[...] + (1.0 - ADAM_B1) * g
        vn = ADAM_B2 * v_ref[...] + (1.0 - ADAM_B2) * (g * g)
        go_ref[...] = g
        mo_ref[...] = mn
        vo_ref[...] = vn
        d_ref[...] = -ADAM_LR * ((mn / c1) / (jnp.sqrt(vn / c2) + ADAM_EPS) + ADAM_WD * w_ref[...])

    blk = pl.BlockSpec((rb, c), lambda i: (i, 0))
    sds = jax.ShapeDtypeStruct((r, c), F32)
    return _call(body, name, (r // rb,), [blk, pl.BlockSpec((N_DEV, rb, c), lambda i: (0, i, 0)), blk, blk],
                 [blk] * 4, [sds] * 4, semantics=("parallel",))(w, gslots, m, v)


def _tile_rows(r):
    for rb in (256, 128, 64, 32, 16, 8):
        if r % rb == 0:
            return rb
    return r


def _pack(arrs, lead=0):
    flats = []
    for a in arrs:
        f = a.reshape(a.shape[:lead] + (-1,)).astype(F32)
        flats.append(jnp.pad(f, [(0, 0)] * lead + [(0, (-f.shape[-1]) % 128)]))
    v = jnp.concatenate(flats, axis=-1)
    v = jnp.pad(v, [(0, 0)] * lead + [(0, (-v.shape[-1]) % 1024)])
    return v.reshape(v.shape[:lead] + (-1, 128))


def _unpack(buf, shapes, lead=0):
    flat = buf.reshape(buf.shape[:lead] + (-1,))
    outs, off = [], 0
    for s in shapes:
        n = math.prod(s)
        outs.append(flat[..., off:off + n].reshape(buf.shape[:lead] + tuple(s)))
        off += n + (-n) % 128
    return outs


def _cols_gathered(g):
    n, l, r, c = g.shape
    return g.transpose(1, 2, 0, 3).reshape(l, r, n * c)


def _cols_to_slabs(full):
    l, r, c = full.shape
    return full.reshape(l, r, N_DEV, c // N_DEV).transpose(2, 0, 1, 3)


def _rows_gathered(g):
    n, l, r, c = g.shape
    return g.transpose(1, 0, 2, 3).reshape(l, n * r, c)


def _rows_to_slabs(full):
    l, r, c = full.shape
    return full.reshape(l, N_DEV, r // N_DEV, c).transpose(1, 0, 2, 3)


def _pad_cols(w, at, width):
    return jnp.pad(w, ((0, 0), (at, width - at - w.shape[1])))


def _pad_lanes(v, width=HEAD):
    return jnp.pad(v.reshape(1, -1), ((0, 0), (0, width - v.size)))


def _taps8(w, bias=None):
    rows = [w] if bias is None else [w, bias.reshape(1, -1)]
    w8 = jnp.concatenate(rows, axis=0)
    return jnp.pad(w8, ((0, 8 - w8.shape[0]), (0, 0)))


def _gdn_forward(x, p, tag):
    pq = _matmul(x, p["w_qkv"], "nn", tag + "_in_qkv")
    z = _matmul(x, p["w_z"], "nn", tag + "_in_z")
    ba = _matmul(x, p["w_ba"], "nn", tag + "_in_ba")
    c = _conv_fwd(pq, p["conv8"], 4, tag + "_conv")
    qn, kn, v, beta, gc = _gdn_ew_fwd(c, ba, p["a_log"], p["dt_bias"], tag + "_ew")
    o, h, states = _gdn_chunk_fwd(qn, kn, v, beta, gc, z, p["norm_w"], tag + "_chunk")
    y = _matmul(h, p["w_out"], "nn", tag + "_out")
    return y, dict(pq=pq, z=z, ba=ba, c=c, qn=qn, kn=kn, v=v, beta=beta, gc=gc, o=o, h=h, states=states)


def _gdn_backward(x, du, p, s, tag):
    dh = _matmul(du, p["w_out"], "nt", tag + "_bwd_dh")
    g_out = _matmul(s["h"], du, "tn", tag + "_bwd_wout")
    dqh, dkh, dv, dz, dbeta, dgc, nacc = _gdn_chunk_bwd(
        s["qn"], s["kn"], s["v"], s["beta"], s["gc"], s["z"], p["norm_w"], s["o"], s["states"], dh, tag + "_bwd_chunk")
    dc, dba, sacc = _gdn_ew_bwd(s["c"], s["ba"], p["a_log"], p["dt_bias"], dqh, dkh, dv, dbeta, dgc, tag + "_bwd_ew")
    dpq, dconv = _conv_bwd(dc, s["pq"], p["conv8"], 4, tag + "_bwd_conv")
    dx = _matmul(dpq, p["w_qkv"], "nt", tag + "_bwd_dx_qkv", add=du, add_scale=ALPHA)
    dx = _matmul(dz, p["w_z"], "nt", tag + "_bwd_dx_z", add=dx)
    dx = _matmul(dba, p["w_ba"], "nt", tag + "_bwd_dx_ba", add=dx)
    g_qkv = _matmul(x, dpq, "tn", tag + "_bwd_w_qkv")
    g_z = _matmul(x, dz, "tn", tag + "_bwd_w_z")
    g_ba = _matmul(x, dba, "tn", tag + "_bwd_w_ba")
    g_in = jnp.concatenate([g_qkv, g_z, g_ba[:, :GDN_V_HEADS], g_ba[:, HEAD:HEAD + GDN_V_HEADS]], axis=1)
    grads = dict(w_in=g_in, w_out=g_out, conv_w=dconv[:4], a_log=sacc[0, :GDN_V_HEADS], dt_bias=sacc[1, :GDN_V_HEADS],
                 norm_w=nacc[0])
    return dx, grads


def _sc_forward(x, p, tag):
    hh = _matmul(x, p["w_h"], "nn", tag + "_in_h")
    bg = _matmul(x, p["w_b"], "nn", tag + "_in_b")
    cg = _matmul(x, p["w_c"], "nn", tag + "_in_c")
    z = _matmul(x, p["w_z"], "nn", tag + "_in_z")
    cv = _conv_fwd(cg, p["conv8"], 3, tag + "_conv", u2=hh)
    h = _sc_gate_fwd(bg, cv, z, tag + "_gate")
    y = _matmul(h, p["w_out"], "nn", tag + "_out")
    return y, dict(hh=hh, bg=bg, cg=cg, z=z, cv=cv, h=h)


def _sc_backward(x, du, p, s, tag):
    dh = _matmul(du, p["w_out"], "nt", tag + "_bwd_dh")
    g_out = _matmul(s["h"], du, "tn", tag + "_bwd_wout")
    dbg, dcv, dz = _sc_gate_bwd(dh, s["bg"], s["cv"], s["z"], tag + "_bwd_gate")
    dcg, dhh, dconv = _conv_bwd(dcv, s["cg"], p["conv8"], 3, tag + "_bwd_conv", u2=s["hh"])
    dx = _matmul(dhh, p["w_h"], "nt", tag + "_bwd_dx_h", add=du, add_scale=ALPHA)
    dx = _matmul(dbg, p["w_b"], "nt", tag + "_bwd_dx_b", add=dx)
    dx = _matmul(dcg, p["w_c"], "nt", tag + "_bwd_dx_c", add=dx)
    dx = _matmul(dz, p["w_z"], "nt", tag + "_bwd_dx_z", add=dx)
    g_in = jnp.concatenate([_matmul(x, d, "tn", tag + "_bwd_w_" + n)
                            for n, d in (("h", dhh), ("b", dbg), ("c", dcg), ("z", dz))], axis=1)
    return dx, dict(w_in=g_in, w_out=g_out, conv_w=dconv[:3])


def _ssd_forward(x, p, tag):
    z = _matmul(x, p["w_z"], "nn", tag + "_in_z")
    xbc = _matmul(x, p["w_xbc"], "nn", tag + "_in_xbc")
    dtp = _matmul(x, p["w_dt"], "nn", tag + "_in_dt")
    cx = _conv_fwd(xbc, p["conv8"], 4, tag + "_conv", bias=True)
    y, h, states = _ssd_chunk_fwd(cx, dtp, z, p["dt_bias"], p["a_log"], p["dskip"], p["norm_w"], tag + "_chunk")
    out = _matmul(h, p["w_out"], "nn", tag + "_out")
    return out, dict(z=z, xbc=xbc, dtp=dtp, cx=cx, y=y, h=h, states=states)


def _ssd_backward(x, du, p, s, tag):
    dh = _matmul(du, p["w_out"], "nt", tag + "_bwd_dh")
    g_out = _matmul(s["h"], du, "tn", tag + "_bwd_wout")
    dcx, ddtp, dz, wide, acc = _ssd_chunk_bwd(s["cx"], s["dtp"], s["z"], p["dt_bias"], p["a_log"], p["dskip"],
                                              p["norm_w"], s["y"], s["states"], dh, tag + "_bwd_chunk")
    dxbc, dconv = _conv_bwd(dcx, s["xbc"], p["conv8"], 4, tag + "_bwd_conv")
    dx = _matmul(dz, p["w_z"], "nt", tag + "_bwd_dx_z", add=du, add_scale=ALPHA)
    dx = _matmul(dxbc, p["w_xbc"], "nt", tag + "_bwd_dx_xbc", add=dx)
    dx = _matmul(ddtp, p["w_dt"], "nt", tag + "_bwd_dx_dt", add=dx)
    g_dt = _matmul(x, ddtp, "tn", tag + "_bwd_w_dt")
    g_in = jnp.concatenate([_matmul(x, dz, "tn", tag + "_bwd_w_z"), _matmul(x, dxbc, "tn", tag + "_bwd_w_xbc"),
                            g_dt[:, :32]], axis=1)
    grads = dict(w_in=g_in, w_out=g_out, conv_w=dconv[:4], conv_b=dconv[4], a_log=acc[0, :32], dt_bias=acc[1, :32],
                 d_skip=jnp.sum(wide[1].reshape(32, 64), axis=1), norm_w=wide[0])
    return dx, grads


_WEIGHTS = ['gdn_w_in', 'gdn_conv_w', 'gdn_a_log', 'gdn_dt_bias', 'gdn_norm_w', 'gdn_w_out', 'sc_w_in', 'sc_conv_w',
            'sc_w_out', 'ssd_w_in', 'ssd_conv_w', 'ssd_conv_b', 'ssd_a_log', 'ssd_dt_bias', 'ssd_d_skip',
            'ssd_norm_w', 'ssd_w_out', 'ln_g', 'ln_b']
_BIG = {'gdn_w_in': 'cols', 'gdn_w_out': 'rows', 'sc_w_in': 'cols', 'sc_w_out': 'rows', 'ssd_w_in': 'cols',
        'ssd_w_out': 'rows'}
_SMALL_SHARDED = ['gdn_conv_w', 'sc_conv_w', 'ssd_conv_w', 'ssd_conv_b', 'ssd_norm_w']
_SMALL = [n for n in _WEIGHTS if n not in _BIG]


def kernel(x, gdn_w_in, gdn_conv_w, gdn_a_log, gdn_dt_bias, gdn_norm_w, gdn_w_out, sc_w_in, sc_conv_w, sc_w_out, ssd_w_in, ssd_conv_w, ssd_conv_b, ssd_a_log, ssd_dt_bias, ssd_d_skip, ssd_norm_w, ssd_w_out, ln_g, ln_b, loss_target, m_gdn_w_in, m_gdn_conv_w, m_gdn_a_log, m_gdn_dt_bias, m_gdn_norm_w, m_gdn_w_out, m_sc_w_in, m_sc_conv_w, m_sc_w_out, m_ssd_w_in, m_ssd_conv_w, m_ssd_conv_b, m_ssd_a_log, m_ssd_dt_bias, m_ssd_d_skip, m_ssd_norm_w, m_ssd_w_out, m_ln_g, m_ln_b, v_gdn_w_in, v_gdn_conv_w, v_gdn_a_log, v_gdn_dt_bias, v_gdn_norm_w, v_gdn_w_out, v_sc_w_in, v_sc_conv_w, v_sc_w_out, v_ssd_w_in, v_ssd_conv_w, v_ssd_conv_b, v_ssd_a_log, v_ssd_dt_bias, v_ssd_d_skip, v_ssd_norm_w, v_ssd_w_out, v_ln_g, v_ln_b):
    args = locals()
    wts = {n: args[n] for n in _WEIGHTS}
    mom = {n: args["m_" + n] for n in _WEIGHTS}
    vel = {n: args["v_" + n] for n in _WEIGHTS}
    me = 4 * lax.axis_index("x") + 2 * lax.axis_index("y") + lax.axis_index("c")
    x0, target = x[0], loss_target[0]

    full = {}
    for n, how in _BIG.items():
        g = _exchange(wts[n].astype(MM_DTYPE), "gather_" + n, slabs=False)
        full[n] = _cols_gathered(g) if how == "cols" else _rows_gathered(g)
    small_shapes = [wts[n].shape for n in _SMALL_SHARDED]
    gathered = _exchange(_pack([wts[n] for n in _SMALL_SHARDED]), "gather_small", slabs=False)
    for n, g in zip(_SMALL_SHARDED, _unpack(gathered, small_shapes, lead=1)):
        full[n] = jnp.moveaxis(g, 0, -2).reshape(g.shape[1:-1] + (N_DEV * g.shape[-1],))
    for n in _SMALL:
        full.setdefault(n, wts[n])

    def gdn_params(j):
        w = full['gdn_w_in'][j]
        return dict(w_qkv=w[:, :GDN_CONV_DIM], w_z=w[:, GDN_CONV_DIM:GDN_CONV_DIM + D_INNER],
                    w_ba=jnp.concatenate([_pad_cols(w[:, 6144:6160], 0, HEAD), _pad_cols(w[:, 6160:6176], 0, HEAD)], 1),
                    conv8=_taps8(full['gdn_conv_w'][j]), a_log=_pad_lanes(full['gdn_a_log'][j]),
                    dt_bias=_pad_lanes(full['gdn_dt_bias'][j]), norm_w=full['gdn_norm_w'][j].reshape(1, HEAD),
                    w_out=full['gdn_w_out'][j])

    w = full['sc_w_in'][0]
    sc_p = dict(w_h=w[:, :2048], w_b=w[:, 2048:4096], w_c=w[:, 4096:6144], w_z=w[:, 6144:],
                conv8=_taps8(full['sc_conv_w'][0]), w_out=full['sc_w_out'][0])
    w = full['ssd_w_in'][0]
    ssd_p = dict(w_z=w[:, :D_INNER], w_xbc=w[:, D_INNER:D_INNER + SSD_CONV_DIM],
                 w_dt=_pad_cols(w[:, D_INNER + SSD_CONV_DIM:], 0, HEAD),
                 conv8=_taps8(full['ssd_conv_w'][0], full['ssd_conv_b'][0]), a_log=_pad_lanes(full['ssd_a_log'][0]),
                 dt_bias=_pad_lanes(full['ssd_dt_bias'][0]),
                 dskip=jnp.repeat(full['ssd_d_skip'][0], 64).reshape(1, D_INNER),
                 norm_w=full['ssd_norm_w'][0].reshape(1, D_INNER), w_out=full['ssd_w_out'][0])
    layers = [("gdn", _gdn_forward, _gdn_backward, gdn_params(0)), ("sc", _sc_forward, _sc_backward, sc_p),
              ("ssd", _ssd_forward, _ssd_backward, ssd_p), ("gdn", _gdn_forward, _gdn_backward, gdn_params(1))]

    acts, ys, saved = [x0], [], []
    for i, (kind, fwd, _, p) in enumerate(layers):
        y, s = fwd(acts[-1], p, "l%d_%s" % (i, kind))
        acts.append(_ln_fwd(acts[-1], y, full['ln_g'][i].reshape(1, -1), full['ln_b'][i].reshape(1, -1), "l%d_ln" % i))
        ys.append(y)
        saved.append(s)
    dact, loss_acc = _loss_head(acts[-1], target, "loss_head")
    loss = lax.psum(0.5 / D_MODEL * jnp.sum(loss_acc[0]), ("x", "y", "c"))

    lg = [None] * DEPTH
    d_ln_g, d_ln_b = [None] * DEPTH, [None] * DEPTH
    for i in reversed(range(DEPTH)):
        kind, _, bwd, p = layers[i]
        du, acc = _ln_bwd(dact, acts[i], ys[i], full['ln_g'][i].reshape(1, -1), "l%d_ln_bwd" % i)
        d_ln_g[i], d_ln_b[i] = acc[0], acc[1]
        dact, lg[i] = bwd(acts[i], du, p, saved[i], "l%d_%s" % (i, kind))
    grad_x = dact[None]

    stack = lambda k: jnp.stack([lg[0][k], lg[3][k]])
    local = {
        'gdn_w_in': stack('w_in'), 'gdn_conv_w': stack('conv_w'), 'gdn_a_log': stack('a_log'),
        'gdn_dt_bias': stack('dt_bias'), 'gdn_norm_w': stack('norm_w'), 'gdn_w_out': stack('w_out'),
        'sc_w_in': lg[1]['w_in'][None], 'sc_conv_w': lg[1]['conv_w'][None], 'sc_w_out': lg[1]['w_out'][None],
        'ssd_w_in': lg[2]['w_in'][None], 'ssd_conv_w': lg[2]['conv_w'][None], 'ssd_conv_b': lg[2]['conv_b'][None],
        'ssd_a_log': lg[2]['a_log'][None], 'ssd_dt_bias': lg[2]['dt_bias'][None], 'ssd_d_skip': lg[2]['d_skip'][None],
        'ssd_norm_w': lg[2]['norm_w'][None], 'ssd_w_out': lg[2]['w_out'][None],
        'ln_g': jnp.stack(d_ln_g), 'ln_b': jnp.stack(d_ln_b)}

    out = {}
    for n, how in _BIG.items():
        slabs = _cols_to_slabs(local[n]) if how == "cols" else _rows_to_slabs(local[n])
        recv = _exchange(slabs, "scatter_" + n, slabs=True)
        shp = wts[n].shape
        r, c = shp[0] * shp[1], shp[2]
        res = _adamw(wts[n].reshape(r, c), recv.reshape(N_DEV, r, c), mom[n].reshape(r, c), vel[n].reshape(r, c),
                     "adamw_" + n)
        out[n] = [a.reshape(shp) for a in res]
    full_shapes = [local[n].shape for n in _SMALL]
    gathered = _exchange(_pack([local[n] for n in _SMALL]), "gather_small_grads", slabs=False)
    gs = []
    for n, g in zip(_SMALL, _unpack(gathered, full_shapes, lead=1)):
        if n in _SMALL_SHARDED:
            width = wts[n].shape[-1]
            g = lax.dynamic_slice_in_dim(g, me * width, width, axis=g.ndim - 1)
        gs.append(g)
    shapes = [wts[n].shape for n in _SMALL]
    res = _adamw(_pack([wts[n] for n in _SMALL]), _pack(gs, lead=1), _pack([mom[n] for n in _SMALL]),
                 _pack([vel[n] for n in _SMALL]), "adamw_small")
    for k, n in enumerate(_SMALL):
        out[n] = [_unpack(a, shapes)[k] for a in res]

    return (loss, grad_x, *[out[n][0] for n in _WEIGHTS], *[out[n][1] for n in _WEIGHTS],
            *[out[n][2] for n in _WEIGHTS], *[out[n][3] for n in _WEIGHTS])
```

```python
import functools
import math

import jax
import jax.numpy as jnp
from jax import lax
from jax.experimental import pallas as pl
from jax.experimental.pallas import tpu as pltpu

F32 = jnp.float32
MM_DTYPE = jnp.bfloat16

N_DEV = 8
D_MODEL = 1024
D_INNER = 2048
CHUNK = 64
HEAD = 128
GDN_V_HEADS = 16
GDN_GROUP = 16
GDN_QK_HEADS = 8
GDN_QK_DIM = 1024
GDN_CONV_DIM = 4096
SSD_PAIRS = 16
SSD_GROUPS = 4
SSD_STATE = 128
SSD_CONV_DIM = 3072
DEPTH = 4
ALPHA = (2 * DEPTH) ** 0.25
RMS_EPS = 1e-6
LN_EPS = 1e-5
L2_EPS = 1e-6
ADAM_LR, ADAM_B1, ADAM_B2, ADAM_EPS, ADAM_WD, ADAM_STEP = 0.001, 0.9, 0.999, 1e-08, 0.01, 10

VMEM_LIMIT_BYTES = 48 * 1024 * 1024
NEG_BIG = -1e30

_NN = (((1,), (0,)), ((), ()))
_NT = (((1,), (1,)), ((), ()))
_TN = (((0,), (0,)), ((), ()))


def _mm(a, b, dims):
    return lax.dot_general(a.astype(MM_DTYPE), b.astype(MM_DTYPE), dims, preferred_element_type=F32)


def _mmx(a, b, dims):
    return lax.dot_general(a, b, dims, precision=lax.Precision.HIGHEST, preferred_element_type=F32)


def _iota(shape, dim):
    return lax.broadcasted_iota(jnp.int32, shape, dim)


def _eye(n):
    return (_iota((n, n), 0) == _iota((n, n), 1)).astype(F32)


def _sig(x):
    return jax.nn.sigmoid(x)


def _silu(x):
    return x * _sig(x)


def _dsilu(x):
    s = _sig(x)
    return s * (1.0 + x * (1.0 - s))


def _softplus(x):
    return jnp.maximum(x, 0.0) + jnp.log(1.0 + jnp.exp(-jnp.abs(x)))


def _col(x, h):
    return jnp.sum(jnp.where(_iota(x.shape, 1) == h, x, 0.0), axis=1, keepdims=True)


def _row(x, h):
    return jnp.sum(jnp.where(_iota(x.shape, 0) == h, x, 0.0), axis=0, keepdims=True)


def _put_col(acc, col, h):
    return jnp.where(_iota(acc.shape, 1) == h, col, acc)


def _put_row(acc, row, h):
    return jnp.where(_iota(acc.shape, 0) == h, row, acc)


def _put_sub(acc, row, j):
    return acc + jnp.where(_iota(acc.shape, 0) == j, row, 0.0)


def _lanes(h):
    return pl.ds(h * HEAD, HEAD) if isinstance(h, int) else pl.ds(pl.multiple_of(h * HEAD, HEAD), HEAD)


def _total(x):
    return jnp.sum(jnp.sum(x, axis=1, keepdims=True), axis=0, keepdims=True)


def _call(body, name, grid, in_specs, out_specs, out_shape, scratch_shapes=(), semantics=None):
    return pl.pallas_call(
        body, name=name, grid=grid, in_specs=in_specs, out_specs=out_specs, out_shape=out_shape,
        scratch_shapes=list(scratch_shapes),
        compiler_params=pltpu.CompilerParams(dimension_semantics=semantics, vmem_limit_bytes=VMEM_LIMIT_BYTES))


def _tile(n, pref):
    if n <= pref:
        return n
    t = pref
    while n % t:
        t -= 128
    return t


def _exchange(src, name, slabs):
    shape = src.shape[1:] if slabs else src.shape

    def body(src_ref, out_ref, send_sems, recv_sems, local_sem):
        x, y, c = lax.axis_index("x"), lax.axis_index("y"), lax.axis_index("c")
        me = 4 * x + 2 * y + c
        mine = src_ref.at[me] if slabs else src_ref
        local = pltpu.make_async_copy(mine, out_ref.at[me], local_sem)
        local.start()
        sends = []
        for r in range(1, N_DEV):
            px = 1 - x if r & 4 else x
            py = 1 - y if r & 2 else y
            pc = 1 - c if r & 1 else c
            peer = 4 * px + 2 * py + pc
            cp = pltpu.make_async_remote_copy(
                src_ref=src_ref.at[peer] if slabs else src_ref, dst_ref=out_ref.at[me],
                send_sem=send_sems.at[r - 1], recv_sem=recv_sems.at[r - 1],
                device_id=(px, py, pc), device_id_type=pl.DeviceIdType.MESH)
            cp.start()
            sends.append((cp, peer, (px, py, pc)))
        for r, (cp, peer, pid) in enumerate(sends):
            pltpu.make_async_remote_copy(
                src_ref=mine, dst_ref=out_ref.at[peer], send_sem=send_sems.at[r], recv_sem=recv_sems.at[r],
                device_id=pid, device_id_type=pl.DeviceIdType.MESH).wait_recv()
        for cp, _, _ in sends:
            cp.wait_send()
        local.wait()

    return pl.pallas_call(
        body, name=name,
        in_specs=[pl.BlockSpec(memory_space=pl.ANY)], out_specs=pl.BlockSpec(memory_space=pl.ANY),
        out_shape=jax.ShapeDtypeStruct((N_DEV,) + tuple(shape), src.dtype),
        scratch_shapes=[pltpu.SemaphoreType.DMA((N_DEV - 1,)), pltpu.SemaphoreType.DMA((N_DEV - 1,)),
                        pltpu.SemaphoreType.DMA(())],
    )(src)


MM_TM, MM_TN, MM_TK = 1024, 1024, 1024


def _matmul(a, b, mode, name, add=None, add_scale=1.0):
    if mode == "nn":
        (m, k), (_, n) = a.shape, b.shape
    elif mode == "nt":
        (m, k), (n, _) = a.shape, b.shape
    else:
        (k, m), (_, n) = a.shape, b.shape
    tm, tn, tk = _tile(m, MM_TM), _tile(n, MM_TN), _tile(k, MM_TK)
    nk = k // tk
    dims = {"nn": _NN, "nt": _NT, "tn": _TN}[mode]

    def body(*refs):
        a_ref, b_ref, o_ref = refs[0], refs[1], refs[-1]
        part = _mm(a_ref[...], b_ref[...], dims)
        first = part if add is None else part + add_scale * refs[2][...]
        if nk == 1:
            o_ref[...] = first
        else:
            @pl.when(pl.program_id(2) == 0)
            def _():
                o_ref[...] = first

            @pl.when(pl.program_id(2) > 0)
            def _():
                o_ref[...] += part

    if mode == "nn":
        specs = [pl.BlockSpec((tm, tk), lambda i, j, q: (i, q)), pl.BlockSpec((tk, tn), lambda i, j, q: (q, j))]
    elif mode == "nt":
        specs = [pl.BlockSpec((tm, tk), lambda i, j, q: (i, q)), pl.BlockSpec((tn, tk), lambda i, j, q: (j, q))]
    else:
        specs = [pl.BlockSpec((tk, tm), lambda i, j, q: (q, i)), pl.BlockSpec((tk, tn), lambda i, j, q: (q, j))]
    out_spec = pl.BlockSpec((tm, tn), lambda i, j, q: (i, j))
    args = [a, b]
    if add is not None:
        specs.append(out_spec)
        args.append(add)
    return _call(body, name, (m // tm, n // tn, nk), specs, out_spec, jax.ShapeDtypeStruct((m, n), F32),
                 semantics=("parallel", "parallel", "arbitrary"))(*args)


CONV_TB = 512
CONV_CB = 1024
HALO = 8


def _conv_specs(t, cb_n):
    tb = min(CONV_TB, t)
    nb = tb // HALO
    blk = pl.BlockSpec((tb, cb_n), lambda c, i: (i, c))
    prev = pl.BlockSpec((HALO, cb_n), lambda c, i: (jnp.maximum(i * nb - 1, 0), c))
    nxt = pl.BlockSpec((HALO, cb_n), lambda c, i: (jnp.minimum((i + 1) * nb, t // HALO - 1), c))
    w = pl.BlockSpec((8, cb_n), lambda c, i: (0, c))
    return blk, prev, nxt, w


def _shift_down(ext, s, tb):
    return (pltpu.roll(ext, s, 0) if s else ext)[HALO:HALO + tb]


def _shift_up(ext, s, tb):
    n = ext.shape[0]
    return (pltpu.roll(ext, n - s, 0) if s else ext)[0:tb]


def _conv_fwd(u, w8, ktaps, name, u2=None, bias=False):
    t, ch = u.shape
    cb_n = min(CONV_CB, ch)
    tb = min(CONV_TB, t)
    two = u2 is not None

    def body(*refs):
        if two:
            u_ref, up_ref, v_ref, vp_ref, w_ref, o_ref = refs
        else:
            u_ref, up_ref, w_ref, o_ref = refs
        first = pl.program_id(1) == 0
        blk, halo = u_ref[...], up_ref[...]
        if two:
            blk, halo = blk * v_ref[...], halo * vp_ref[...]
        ext = jnp.concatenate([jnp.where(first, 0.0, halo), blk], axis=0)
        acc = jnp.zeros((tb, cb_n), F32)
        for j in range(ktaps):
            acc = acc + w_ref[j:j + 1, :] * _shift_down(ext, ktaps - 1 - j, tb)
        if bias:
            acc = acc + w_ref[ktaps:ktaps + 1, :]
        o_ref[...] = acc

    blk, prev, _, wspec = _conv_specs(t, cb_n)
    specs, args = [blk, prev], [u, u]
    if two:
        specs += [blk, prev]
        args += [u2, u2]
    specs.append(wspec)
    args.append(w8)
    return _call(body, name, (ch // cb_n, t // tb), specs, blk, jax.ShapeDtypeStruct((t, ch), F32),
                 semantics=("parallel", "parallel"))(*args)


def _conv_bwd(dc, u, w8, ktaps, name, u2=None):
    t, ch = u.shape
    cb_n = min(CONV_CB, ch)
    tb = min(CONV_TB, t)
    two = u2 is not None

    def body(*refs):
        if two:
            dc_ref, dn_ref, u_ref, up_ref, v_ref, vp_ref, w_ref, du_ref, dv_ref, dw_ref = refs
        else:
            dc_ref, dn_ref, u_ref, up_ref, w_ref, du_ref, dw_ref = refs
        i = pl.program_id(1)
        first, last = i == 0, i == t // tb - 1
        d = dc_ref[...]
        dext = jnp.concatenate([d, jnp.where(last, 0.0, dn_ref[...])], axis=0)
        blk, halo = u_ref[...], up_ref[...]
        if two:
            blk, halo = blk * v_ref[...], halo * vp_ref[...]
        uext = jnp.concatenate([jnp.where(first, 0.0, halo), blk], axis=0)
        du = jnp.zeros((tb, cb_n), F32)
        dw = jnp.zeros((8, cb_n), F32)
        for j in range(ktaps):
            s = ktaps - 1 - j
            du = du + w_ref[j:j + 1, :] * _shift_up(dext, s, tb)
            dw = _put_sub(dw, jnp.sum(d * _shift_down(uext, s, tb), axis=0, keepdims=True), j)
        dw = _put_sub(dw, jnp.sum(d, axis=0, keepdims=True), ktaps)
        if two:
            du_ref[...] = (du * v_ref[...]).astype(du_ref.dtype)
            dv_ref[...] = (du * u_ref[...]).astype(dv_ref.dtype)
        else:
            du_ref[...] = du.astype(du_ref.dtype)

        @pl.when(first)
        def _():
            dw_ref[...] = jnp.zeros_like(dw_ref)

        dw_ref[...] += dw

    blk, prev, nxt, wspec = _conv_specs(t, cb_n)
    specs, args = [blk, nxt, blk, prev], [dc, dc, u, u]
    if two:
        specs += [blk, prev]
        args += [u2, u2]
    specs.append(wspec)
    args.append(w8)
    act = jax.ShapeDtypeStruct((t, ch), MM_DTYPE)
    outs = ([blk, blk, wspec], [act, act, jax.ShapeDtypeStruct((8, ch), F32)]) if two else \
        ([blk, wspec], [act, jax.ShapeDtypeStruct((8, ch), F32)])
    return _call(body, name, (ch // cb_n, t // tb), specs, outs[0], outs[1],
                 semantics=("parallel", "arbitrary"))(*args)


EW_TB = 256


def _chunk_mask(n, upper):
    i, j = _iota((n, n), 0), _iota((n, n), 1)
    same = jnp.right_shift(i, 6) == jnp.right_shift(j, 6)
    return (same & ((j >= i) if upper else (i >= j))).astype(F32)


def _rows(width, tb=EW_TB):
    return pl.BlockSpec((tb, width), lambda i: (i, 0))


def _const(rows, width):
    return pl.BlockSpec((rows, width), lambda i: (0, 0))


def _gdn_ew_fwd(c, ba, a_log, dt_bias, name):
    t = c.shape[0]
    tb = EW_TB

    def body(c_ref, ba_ref, al_ref, db_ref, q_ref, k_ref, v_ref, beta_ref, gc_ref):
        for h in range(GDN_QK_HEADS):
            for base, ref, scale in ((0, q_ref, HEAD ** -0.5), (GDN_QK_DIM, k_ref, 1.0)):
                s = _silu(c_ref[:, base + h * HEAD: base + (h + 1) * HEAD])
                r = lax.rsqrt(jnp.sum(s * s, axis=1, keepdims=True) + L2_EPS)
                ref[:, h * HEAD:(h + 1) * HEAD] = s * (r * scale)
        v_ref[...] = _silu(c_ref[:, 2 * GDN_QK_DIM:])
        beta_ref[...] = _sig(ba_ref[:, :HEAD])
        g = -jnp.exp(al_ref[...]) * _softplus(ba_ref[:, HEAD:] + db_ref[...])
        gc_ref[...] = _mmx(_chunk_mask(tb, False), g, _NN)

    act = lambda w: jax.ShapeDtypeStruct((t, w), F32)
    return _call(body, name, (t // tb,),
                 [_rows(GDN_CONV_DIM), _rows(2 * HEAD), _const(1, HEAD), _const(1, HEAD)],
                 [_rows(GDN_QK_DIM), _rows(GDN_QK_DIM), _rows(D_INNER), _rows(HEAD), _rows(HEAD)],
                 [act(GDN_QK_DIM), act(GDN_QK_DIM), act(D_INNER), act(HEAD), act(HEAD)],
                 semantics=("parallel",))(c, ba, a_log, dt_bias)


def _gdn_ew_bwd(c, ba, a_log, dt_bias, dqh, dkh, dv, dbeta, dgc, name):
    t = c.shape[0]
    tb = EW_TB

    def body(c_ref, ba_ref, al_ref, db_ref, dq_ref, dk_ref, dv_ref, dbeta_ref, dgc_ref, dc_ref, dba_ref, acc_ref):
        for h in range(GDN_QK_HEADS):
            for base, ref, scale in ((0, dq_ref, HEAD ** -0.5), (GDN_QK_DIM, dk_ref, 1.0)):
                cq = c_ref[:, base + h * HEAD: base + (h + 1) * HEAD]
                s = _silu(cq)
                r = lax.rsqrt(jnp.sum(s * s, axis=1, keepdims=True) + L2_EPS)
                dn = (ref[:, 2 * h * HEAD:(2 * h + 1) * HEAD] + ref[:, (2 * h + 1) * HEAD:(2 * h + 2) * HEAD]) * scale
                ds = r * dn - s * (r * r * r) * jnp.sum(dn * s, axis=1, keepdims=True)
                dc_ref[:, base + h * HEAD: base + (h + 1) * HEAD] = ds * _dsilu(cq)
        dc_ref[:, 2 * GDN_QK_DIM:] = dv_ref[...] * _dsilu(c_ref[:, 2 * GDN_QK_DIM:])
        beta = _sig(ba_ref[:, :HEAD])
        dba_ref[:, :HEAD] = (dbeta_ref[...] * beta * (1.0 - beta)).astype(dba_ref.dtype)
        pre = ba_ref[:, HEAD:] + db_ref[...]
        ea = jnp.exp(al_ref[...])
        g = -ea * _softplus(pre)
        dg = _mmx(_chunk_mask(tb, True), dgc_ref[...], _NN)
        da_raw = dg * (-ea) * _sig(pre)
        dba_ref[:, HEAD:] = da_raw.astype(dba_ref.dtype)
        acc = jnp.zeros((8, HEAD), F32)
        acc = _put_sub(acc, jnp.sum(dg * g, axis=0, keepdims=True), 0)
        acc = _put_sub(acc, jnp.sum(da_raw, axis=0, keepdims=True), 1)

        @pl.when(pl.program_id(0) == 0)
        def _():
            acc_ref[...] = jnp.zeros_like(acc_ref)

        acc_ref[...] += acc

    act = lambda w: jax.ShapeDtypeStruct((t, w), F32)
    return _call(body, name, (t // tb,),
                 [_rows(GDN_CONV_DIM), _rows(2 * HEAD), _const(1, HEAD), _const(1, HEAD),
                  _rows(D_INNER), _rows(D_INNER), _rows(D_INNER), _rows(HEAD), _rows(HEAD)],
                 [_rows(GDN_CONV_DIM), _rows(2 * HEAD), _const(8, HEAD)],
                 [act(GDN_CONV_DIM), jax.ShapeDtypeStruct((t, 2 * HEAD), MM_DTYPE), jax.ShapeDtypeStruct((8, HEAD), F32)],
                 semantics=("arbitrary",))(c, ba, a_log, dt_bias, dqh, dkh, dv, dbeta, dgc)


def _zip(fn, *lists):
    return [fn(*xs) for xs in zip(*lists)]


def _mms(xs, ys, dims):
    return [_mm(x, y, dims) for x, y in zip(xs, ys)]


def _gdn_heads_fwd(q, k, v, bcol, gcol, grow, glast, s_in):
    ii, jj = _iota((CHUNK, CHUNK), 0), _iota((CHUNK, CHUNK), 1)
    eye = _eye(CHUNK)
    mul = lambda x, y: x * y
    eg = [jnp.exp(g) for g in gcol]
    decay = _zip(lambda gc, gr: jnp.exp(jnp.where(ii >= jj, gc - gr, NEG_BIG)), gcol, grow)
    kb = _zip(mul, k, bcol)
    p = _mms(kb, k, _NT)
    a = _zip(lambda x, d: jnp.where(ii > jj, x * d, 0.0), p, decay)
    inv, pw = [eye - x for x in a], a
    for _ in range(5):
        pw = _mms(pw, pw, _NN)
        inv = _zip(lambda x, y: x + y, inv, _mms(inv, pw, _NN))
    rv, rk = _zip(mul, v, bcol), _zip(mul, kb, eg)
    u, w = _mms(inv, rv, _NN), _mms(inv, rk, _NN)
    vn = _zip(lambda x, y: x - y, u, _mms(w, s_in, _NN))
    qk = _mms(q, k, _NT)
    att = _zip(mul, qk, decay)
    qd = _zip(mul, q, eg)
    out = _zip(lambda x, y: x + y, _mms(qd, s_in, _NN), _mms(att, vn, _NN))
    ekt = _zip(lambda gl, gc: jnp.exp(gl - gc), glast, gcol)
    kt = _zip(mul, k, ekt)
    el = [jnp.exp(g) for g in glast]
    s_out = _zip(lambda s, e, y: s * e + y, s_in, el, _mms(kt, vn, _TN))
    return dict(eg=eg, decay=decay, kb=kb, p=p, inv=inv, rv=rv, rk=rk, u=u, w=w, vn=vn, qk=qk, att=att, qd=qd,
                out=out, ekt=ekt, kt=kt, el=el, s_out=s_out)


def _head_groups(group, init):
    if GDN_GROUP == GDN_V_HEADS:
        return group(0, init)
    return lax.fori_loop(0, GDN_V_HEADS // GDN_GROUP, lambda gi, c: group(GDN_GROUP * gi, c), init)


def _half(h):
    return h // 2 if isinstance(h, int) else jnp.right_shift(h, 1)


def _gdn_chunk_fwd(qn, kn, v, beta, gc, z, norm_w, name):
    t = qn.shape[0]
    nc = t // CHUNK

    def body(q_ref, k_ref, v_ref, beta_ref, gc_ref, z_ref, nw_ref, o_ref, h_ref, st_ref, state):
        @pl.when(pl.program_id(0) == 0)
        def _():
            state[...] = jnp.zeros_like(state)

        st_ref[0] = state[...]
        gc_all, beta_all = gc_ref[...], beta_ref[...]
        gct = _mmx(_eye(HEAD), gc_all, _NT)
        glast_all = gc_ref[CHUNK - 1:CHUNK, :]
        nw = nw_ref[...]

        def group(h0, carry):
            heads = [h0 + s for s in range(GDN_GROUP)]
            f = _gdn_heads_fwd([q_ref[:, _lanes(_half(h))] for h in heads], [k_ref[:, _lanes(_half(h))] for h in heads],
                               [v_ref[:, _lanes(h)] for h in heads], [_col(beta_all, h) for h in heads],
                               [_col(gc_all, h) for h in heads], [_row(gct, h) for h in heads],
                               [_col(glast_all, h) for h in heads], [state[h] for h in heads])
            for h, s_out, o in zip(heads, f["s_out"], f["out"]):
                state[h] = s_out
                o_ref[:, _lanes(h)] = o
                rstd = lax.rsqrt(jnp.mean(o * o, axis=1, keepdims=True) + RMS_EPS)
                h_ref[:, _lanes(h)] = (o * rstd * nw * _silu(z_ref[:, _lanes(h)])).astype(h_ref.dtype)
            return carry

        _head_groups(group, 0)

    rows = lambda w: pl.BlockSpec((CHUNK, w), lambda i: (i, 0))
    act = lambda w: jax.ShapeDtypeStruct((t, w), F32)
    return _call(body, name, (nc,),
                 [rows(GDN_QK_DIM), rows(GDN_QK_DIM), rows(D_INNER), rows(HEAD), rows(HEAD), rows(D_INNER),
                  _const(1, HEAD)],
                 [rows(D_INNER), rows(D_INNER), pl.BlockSpec((1, GDN_V_HEADS, HEAD, HEAD), lambda i: (i, 0, 0, 0))],
                 [act(D_INNER), jax.ShapeDtypeStruct((t, D_INNER), MM_DTYPE),
                  jax.ShapeDtypeStruct((nc, GDN_V_HEADS, HEAD, HEAD), F32)],
                 [pltpu.VMEM((GDN_V_HEADS, HEAD, HEAD), F32)], ("arbitrary",))(qn, kn, v, beta, gc, z, norm_w)


def _gdn_chunk_bwd(qn, kn, v, beta, gc, z, norm_w, o, states, dh, name):
    t = qn.shape[0]
    nc = t // CHUNK

    def body(q_ref, k_ref, v_ref, beta_ref, gc_ref, z_ref, nw_ref, o_ref, st_ref, dh_ref,
             dq_ref, dk_ref, dv_ref, dz_ref, dbeta_ref, dgc_ref, acc_ref, dstate):
        @pl.when(pl.program_id(0) == 0)
        def _():
            dstate[...] = jnp.zeros_like(dstate)
            acc_ref[...] = jnp.zeros_like(acc_ref)

        gc_all, beta_all = gc_ref[...], beta_ref[...]
        gct = _mmx(_eye(HEAD), gc_all, _NT)
        glast_all = gc_ref[CHUNK - 1:CHUNK, :]
        nw = nw_ref[...]
        ii, jj = _iota((CHUNK, CHUNK), 0), _iota((CHUNK, CHUNK), 1)
        last_row = _iota((CHUNK, 1), 0) == CHUNK - 1

        def group(h0, carry):
            dbeta_acc, dgc_acc, dgrow_acc, dnw_acc = carry
            heads = [h0 + s for s in range(GDN_GROUP)]
            mul, add, sub = (lambda x, y: x * y), (lambda x, y: x + y), (lambda x, y: x - y)
            rowsum = lambda x, y: jnp.sum(x * y, axis=1, keepdims=True)
            q, k = [q_ref[:, _lanes(_half(h))] for h in heads], [k_ref[:, _lanes(_half(h))] for h in heads]
            vv = [v_ref[:, _lanes(h)] for h in heads]
            bcol, gcol = [_col(beta_all, h) for h in heads], [_col(gc_all, h) for h in heads]
            s_in, dsn = [st_ref[0, h] for h in heads], [dstate[h] for h in heads]
            f = _gdn_heads_fwd(q, k, vv, bcol, gcol, [_row(gct, h) for h in heads],
                               [_col(glast_all, h) for h in heads], s_in)
            do = []
            for h in heads:
                oo, zz, dhh = o_ref[:, _lanes(h)], z_ref[:, _lanes(h)], dh_ref[:, _lanes(h)]
                rstd = lax.rsqrt(jnp.mean(oo * oo, axis=1, keepdims=True) + RMS_EPS)
                on, sz = oo * rstd, _silu(zz)
                dnw_acc = dnw_acc + jnp.sum(dhh * on * sz, axis=0, keepdims=True)
                dz_ref[:, _lanes(h)] = (dhh * on * nw * _dsilu(zz)).astype(dz_ref.dtype)
                don = dhh * nw * sz
                do.append(rstd * (don - on * jnp.mean(don * on, axis=1, keepdims=True)))
            decay, eg, inv = f["decay"], f["eg"], f["inv"]
            d_glast = _zip(lambda d, s, e: _total(d * s) * e, dsn, s_in, f["el"])
            dkt = _mms(f["vn"], dsn, _NT)
            dvn = _mms(f["kt"], dsn, _NN)
            dqd = _mms(do, s_in, _NT)
            ds_prev = _zip(lambda d, e, y: d * e + y, dsn, f["el"], _mms(f["qd"], do, _TN))
            datt = _mms(do, f["vn"], _NT)
            dvn = _zip(add, dvn, _mms(f["att"], do, _TN))
            dqk = _zip(mul, datt, decay)
            dq = _zip(lambda x, e, y: x * e + y, dqd, eg, _mms(dqk, k, _NN))
            dk = _mms(dqk, q, _TN)
            ddecay = _zip(mul, datt, f["qk"])
            dgcol = _zip(rowsum, dqd, f["qd"])
            dw = [-x for x in _mms(dvn, s_in, _NT)]
            ds_prev = _zip(sub, ds_prev, _mms(f["w"], dvn, _TN))
            drv, drk = _mms(inv, dvn, _TN), _mms(inv, dw, _TN)
            da = _zip(lambda x, y: jnp.where(ii > jj, -(x + y), 0.0), _mms(drv, f["u"], _NT), _mms(drk, f["w"], _NT))
            dp = _zip(mul, da, decay)
            ddecay = _zip(lambda x, y, z_: x + y * z_, ddecay, da, f["p"])
            dkb = _zip(lambda x, y, e: x + y * e, _mms(dp, k, _NN), drk, eg)
            dk = _zip(add, dk, _mms(dp, f["kb"], _TN))
            dbeta = _zip(add, _zip(rowsum, drv, vv), _zip(rowsum, dkb, k))
            dgcol = _zip(add, dgcol, _zip(rowsum, drk, f["rk"]))
            dk = _zip(lambda x, y, b_, z_, e: x + y * b_ + z_ * e, dk, dkb, bcol, dkt, f["ekt"])
            tail = _zip(mul, dkt, f["kt"])
            d_glast = _zip(lambda x, y: x + _total(y), d_glast, tail)
            e_ = _zip(mul, ddecay, decay)
            dgcol = _zip(lambda x, t_, e, gl: x - jnp.sum(t_, axis=1, keepdims=True) + jnp.sum(e, axis=1, keepdims=True)
                         + jnp.where(last_row, gl, 0.0), dgcol, tail, e_, d_glast)
            for i_, h in enumerate(heads):
                dstate[h] = ds_prev[i_]
                dq_ref[:, _lanes(h)] = dq[i_]
                dk_ref[:, _lanes(h)] = dk[i_]
                dv_ref[:, _lanes(h)] = drv[i_] * bcol[i_]
                dbeta_acc = _put_col(dbeta_acc, dbeta[i_], h)
                dgc_acc = _put_col(dgc_acc, dgcol[i_], h)
                dgrow_acc = _put_row(dgrow_acc, -jnp.sum(e_[i_], axis=0, keepdims=True), h)
            return dbeta_acc, dgc_acc, dgrow_acc, dnw_acc

        zero = jnp.zeros((CHUNK, HEAD), F32)
        dbeta_acc, dgc_acc, dgrow_acc, dnw_acc = _head_groups(
            group, (zero, zero, jnp.zeros((HEAD, CHUNK), F32), jnp.zeros((1, HEAD), F32)))
        dbeta_ref[...] = dbeta_acc
        dgc_ref[...] = dgc_acc + _mmx(_eye(CHUNK), dgrow_acc, _NT)
        acc_ref[...] += _put_sub(jnp.zeros((8, HEAD), F32), dnw_acc, 0)

    rows = lambda w: pl.BlockSpec((CHUNK, w), lambda i: (nc - 1 - i, 0))
    act = lambda w: jax.ShapeDtypeStruct((t, w), F32)
    return _call(body, name, (nc,),
                 [rows(GDN_QK_DIM), rows(GDN_QK_DIM), rows(D_INNER), rows(HEAD), rows(HEAD), rows(D_INNER),
                  _const(1, HEAD), rows(D_INNER),
                  pl.BlockSpec((1, GDN_V_HEADS, HEAD, HEAD), lambda i: (nc - 1 - i, 0, 0, 0)), rows(D_INNER)],
                 [rows(D_INNER), rows(D_INNER), rows(D_INNER), rows(D_INNER), rows(HEAD), rows(HEAD), _const(8, HEAD)],
                 [act(D_INNER), act(D_INNER), act(D_INNER), jax.ShapeDtypeStruct((t, D_INNER), MM_DTYPE), act(HEAD),
                  act(HEAD), jax.ShapeDtypeStruct((8, HEAD), F32)],
                 [pltpu.VMEM((GDN_V_HEADS, HEAD, HEAD), F32)], ("arbitrary",)
                 )(qn, kn, v, beta, gc, z, norm_w, o, states, dh)


def _sc_gate_fwd(bg, cv, z, name):
    t, w = bg.shape

    def body(b_ref, c_ref, z_ref, o_ref):
        o_ref[...] = (b_ref[...] * c_ref[...] * _silu(z_ref[...])).astype(o_ref.dtype)

    return _call(body, name, (t // EW_TB,), [_rows(w)] * 3, _rows(w), jax.ShapeDtypeStruct((t, w), MM_DTYPE),
                 semantics=("parallel",))(bg, cv, z)


def _sc_gate_bwd(dh, bg, cv, z, name):
    t, w = bg.shape

    def body(d_ref, b_ref, c_ref, z_ref, db_ref, dc_ref, dz_ref):
        d, b, c, zz = d_ref[...], b_ref[...], c_ref[...], z_ref[...]
        sz = _silu(zz)
        db_ref[...] = (d * c * sz).astype(db_ref.dtype)
        dc_ref[...] = d * b * sz
        dz_ref[...] = (d * b * c * _dsilu(zz)).astype(dz_ref.dtype)

    act, act_mm = jax.ShapeDtypeStruct((t, w), F32), jax.ShapeDtypeStruct((t, w), MM_DTYPE)
    return _call(body, name, (t // EW_TB,), [_rows(w)] * 4, [_rows(w)] * 3, [act_mm, act, act_mm],
                 semantics=("parallel",))(dh, bg, cv, z)


XBC_B = D_INNER
XBC_C = D_INNER + SSD_GROUPS * SSD_STATE


def _ssd_scalars(dtp, dt_bias, a_log):
    dt = _softplus(dtp + dt_bias)
    a = -jnp.exp(a_log)
    da = dt * a
    ac = _mmx(_chunk_mask(CHUNK, False), da, _NN)
    act = _mmx(_eye(HEAD), ac, _NT)
    aclast = jnp.sum(jnp.where(_iota(ac.shape, 0) == CHUNK - 1, ac, 0.0), axis=0, keepdims=True)
    return dt, a, da, ac, act, aclast


def _ssd_pairs_fwd(x2, bg, cg, cb, dt, ac, act, aclast, s2):
    ii, jj = _iota((CHUNK, CHUNK), 0), _iota((CHUNK, CHUNK), 1)
    half = _iota((CHUNK, HEAD), 1) < 64
    causal = ii >= jj
    pairs = range(len(x2))
    mul = lambda x, y: x * y
    pick = lambda a, b: jnp.where(half, a, b)
    aca, acb = [_col(ac, 2 * p) for p in pairs], [_col(ac, 2 * p + 1) for p in pairs]
    la, lb = [_col(aclast, 2 * p) for p in pairs], [_col(aclast, 2 * p + 1) for p in pairs]
    dt2 = [pick(_col(dt, 2 * p), _col(dt, 2 * p + 1)) for p in pairs]
    xdt = _zip(mul, x2, dt2)
    sega = [jnp.exp(jnp.where(causal, aca[p] - _row(act, 2 * p), NEG_BIG)) for p in pairs]
    segb = [jnp.exp(jnp.where(causal, acb[p] - _row(act, 2 * p + 1), NEG_BIG)) for p in pairs]
    ma, mb = _zip(mul, sega, cb), _zip(mul, segb, cb)
    ydiag = _zip(pick, _mms(ma, xdt, _NN), _mms(mb, xdt, _NN))
    cdec = _zip(lambda a, b: pick(jnp.exp(a), jnp.exp(b)), aca, acb)
    cs = _mms(cg, s2, _NT)
    tail = _zip(lambda l1, a, l2, b: pick(jnp.exp(l1 - a), jnp.exp(l2 - b)), la, aca, lb, acb)
    zt = _zip(mul, xdt, tail)
    ea, eb = [jnp.exp(x) for x in la], [jnp.exp(x) for x in lb]
    tot = _zip(lambda a, b: jnp.where(_iota((HEAD, 1), 0) < 64, a, b), ea, eb)
    s_out = _zip(lambda s, t_, y: s * t_ + y, s2, tot, _mms(zt, bg, _TN))
    return dict(half=half, dt2=dt2, xdt=xdt, sega=sega, segb=segb, ma=ma, mb=mb, ydiag=ydiag, cdec=cdec, cs=cs,
                tail=tail, zt=zt, ea=ea, eb=eb, tot=tot, s_out=s_out)


def _ssd_group_inputs(cx_ref):
    cxb = [cx_ref[:, XBC_B + g * SSD_STATE: XBC_B + (g + 1) * SSD_STATE] for g in range(SSD_GROUPS)]
    cxc = [cx_ref[:, XBC_C + g * SSD_STATE: XBC_C + (g + 1) * SSD_STATE] for g in range(SSD_GROUPS)]
    bg, cg = [_silu(x) for x in cxb], [_silu(x) for x in cxc]
    return cxb, cxc, bg, cg, _mms(cg, bg, _NT)


def _per_pair(group_list):
    return [group_list[p // (SSD_PAIRS // SSD_GROUPS)] for p in range(SSD_PAIRS)]


def _ssd_chunk_fwd(cx, dtp, z, dt_bias, a_log, dskip, norm_w, name):
    t = cx.shape[0]
    nc = t // CHUNK
    gw = D_INNER // SSD_GROUPS

    def body(cx_ref, dtp_ref, z_ref, db_ref, al_ref, sk_ref, nw_ref, y_ref, h_ref, st_ref, state):
        @pl.when(pl.program_id(0) == 0)
        def _():
            state[...] = jnp.zeros_like(state)

        st_ref[0] = state[...]
        dt, _, _, ac, act, aclast = _ssd_scalars(dtp_ref[...], db_ref[...], al_ref[...])
        _, _, bg, cg, cb = _ssd_group_inputs(cx_ref)
        x2 = [_silu(cx_ref[:, _lanes(p)]) for p in range(SSD_PAIRS)]
        f = _ssd_pairs_fwd(x2, _per_pair(bg), _per_pair(cg), _per_pair(cb), dt, ac, act, aclast,
                           [state[p] for p in range(SSD_PAIRS)])
        for p in range(SSD_PAIRS):
            state[p] = f["s_out"][p]
            y_ref[:, _lanes(p)] = f["ydiag"][p] + f["cs"][p] * f["cdec"][p] + sk_ref[:, _lanes(p)] * x2[p]
        for g in range(SSD_GROUPS):
            sl = slice(g * gw, (g + 1) * gw)
            yg = y_ref[:, sl] * _silu(z_ref[:, sl])
            rstd = lax.rsqrt(jnp.mean(yg * yg, axis=1, keepdims=True) + RMS_EPS)
            h_ref[:, sl] = (yg * rstd * nw_ref[:, sl]).astype(h_ref.dtype)

    rows = lambda w: pl.BlockSpec((CHUNK, w), lambda i: (i, 0))
    act_ = lambda w: jax.ShapeDtypeStruct((t, w), F32)
    return _call(body, name, (nc,),
                 [rows(SSD_CONV_DIM), rows(HEAD), rows(D_INNER), _const(1, HEAD), _const(1, HEAD),
                  _const(1, D_INNER), _const(1, D_INNER)],
                 [rows(D_INNER), rows(D_INNER), pl.BlockSpec((1, SSD_PAIRS, HEAD, SSD_STATE), lambda i: (i, 0, 0, 0))],
                 [act_(D_INNER), jax.ShapeDtypeStruct((t, D_INNER), MM_DTYPE),
                  jax.ShapeDtypeStruct((nc, SSD_PAIRS, HEAD, SSD_STATE), F32)],
                 [pltpu.VMEM((SSD_PAIRS, HEAD, SSD_STATE), F32)], ("arbitrary",)
                 )(cx, dtp, z, dt_bias, a_log, dskip, norm_w)


def _ssd_chunk_bwd(cx, dtp, z, dt_bias, a_log, dskip, norm_w, y, states, dh, name):
    t = cx.shape[0]
    nc = t // CHUNK
    gw = D_INNER // SSD_GROUPS

    def body(cx_ref, dtp_ref, z_ref, db_ref, al_ref, sk_ref, nw_ref, y_ref, st_ref, dh_ref,
             dcx_ref, ddtp_ref, dz_ref, wide_ref, acc_ref, dstate, dy_s):
        @pl.when(pl.program_id(0) == 0)
        def _():
            dstate[...] = jnp.zeros_like(dstate)
            wide_ref[...] = jnp.zeros_like(wide_ref)
            acc_ref[...] = jnp.zeros_like(acc_ref)

        dtp = dtp_ref[...]
        dt, a, da, ac, act, aclast = _ssd_scalars(dtp, db_ref[...], al_ref[...])
        ii, jj = _iota((CHUNK, CHUNK), 0), _iota((CHUNK, CHUNK), 1)
        last_row = _iota((CHUNK, 1), 0) == CHUNK - 1
        top = _iota((HEAD, SSD_STATE), 0) < 64
        for g in range(SSD_GROUPS):
            sl = slice(g * gw, (g + 1) * gw)
            yy, zz, dhh, nw = y_ref[:, sl], z_ref[:, sl], dh_ref[:, sl], nw_ref[:, sl]
            sz = _silu(zz)
            yg = yy * sz
            rstd = lax.rsqrt(jnp.mean(yg * yg, axis=1, keepdims=True) + RMS_EPS)
            n = yg * rstd
            dn = dhh * nw
            dyg = rstd * (dn - n * jnp.mean(dn * n, axis=1, keepdims=True))
            dy_s[:, sl] = dyg * sz
            dz_ref[:, sl] = (dyg * yy * _dsilu(zz)).astype(dz_ref.dtype)
            wide_ref[0:1, sl] += jnp.sum(dhh * n, axis=0, keepdims=True)

        pairs = range(SSD_PAIRS)
        mul, add, sub = (lambda x, y: x * y), (lambda x, y: x + y), (lambda x, y: x - y)
        rowsum = lambda x: jnp.sum(x, axis=1, keepdims=True)
        cxb, cxc, bg, cg, cb = _ssd_group_inputs(cx_ref)
        bgp, cgp = _per_pair(bg), _per_pair(cg)
        cxx = [cx_ref[:, _lanes(p)] for p in pairs]
        x2 = [_silu(x) for x in cxx]
        s2, dsn = [st_ref[0, p] for p in pairs], [dstate[p] for p in pairs]
        dy2 = [dy_s[:, _lanes(p)] for p in pairs]
        f = _ssd_pairs_fwd(x2, bgp, cgp, _per_pair(cb), dt, ac, act, aclast, s2)
        half = f["half"]
        lo = lambda x: jnp.where(half, x, 0.0)
        dx2 = [dy2[p] * sk_ref[:, _lanes(p)] for p in pairs]
        for p in pairs:
            wide_ref[1:2, _lanes(p)] += jnp.sum(dy2[p] * x2[p], axis=0, keepdims=True)
        gg = _zip(mul, dy2, f["cdec"])
        dc_p = _mms(gg, s2, _NN)
        ds_prev = _zip(lambda d, t_, y: d * t_ + y, dsn, f["tot"], _mms(gg, cgp, _TN))
        t1 = _zip(lambda d, c, e: d * c * e, dy2, f["cs"], f["cdec"])
        dac_a = [rowsum(lo(x)) for x in t1]
        dac_b = _zip(lambda x, a_: rowsum(x) - a_, t1, dac_a)
        dya = [lo(x) for x in dy2]
        dma, dmb = _mms(dya, f["xdt"], _NT), _mms(_zip(sub, dy2, dya), f["xdt"], _NT)
        dxdt = _zip(lambda a_, b_: jnp.where(half, a_, b_), _mms(f["ma"], dy2, _TN), _mms(f["mb"], dy2, _TN))
        dcb_p = _zip(lambda a_, sa, b_, sb: a_ * sa + b_ * sb, dma, f["sega"], dmb, f["segb"])
        ea_, eb_ = _zip(mul, dma, f["ma"]), _zip(mul, dmb, f["mb"])
        dac_a = _zip(lambda x, e: x + rowsum(e), dac_a, ea_)
        dac_b = _zip(lambda x, e: x + rowsum(e), dac_b, eb_)
        dzt = _mms(bgp, dsn, _NT)
        db_p = _mms(f["zt"], dsn, _NN)
        dxdt = _zip(lambda x, d, t_: x + d * t_, dxdt, dzt, f["tail"])
        t2 = _zip(mul, dzt, f["zt"])
        t2a = [rowsum(lo(x)) for x in t2]
        t2b = _zip(lambda x, a_: rowsum(x) - a_, t2, t2a)
        t3 = _zip(mul, dsn, s2)
        t3a = [_total(jnp.where(top, x, 0.0)) for x in t3]
        dla = _zip(lambda x, y, e: _total(x) + y * e, t2a, t3a, f["ea"])
        dlb = _zip(lambda x, y, ya, e: _total(x) + (_total(y) - ya) * e, t2b, t3, t3a, f["eb"])
        dac_a = _zip(lambda x, y, l: x - y + jnp.where(last_row, l, 0.0), dac_a, t2a, dla)
        dac_b = _zip(lambda x, y, l: x - y + jnp.where(last_row, l, 0.0), dac_b, t2b, dlb)
        dx2 = _zip(lambda x, d, t_: x + d * t_, dx2, dxdt, f["dt2"])
        t4 = _zip(mul, dxdt, x2)
        t4a = [rowsum(lo(x)) for x in t4]
        t4b = _zip(lambda x, a_: rowsum(x) - a_, t4, t4a)
        zero = jnp.zeros((CHUNK, HEAD), F32)
        ddt_acc, dac_acc, drow_acc = zero, zero, jnp.zeros((HEAD, CHUNK), F32)
        for p in pairs:
            dcx_ref[:, _lanes(p)] = dx2[p] * _dsilu(cxx[p])
            dstate[p] = ds_prev[p]
            ddt_acc = _put_col(_put_col(ddt_acc, t4a[p], 2 * p), t4b[p], 2 * p + 1)
            dac_acc = _put_col(_put_col(dac_acc, dac_a[p], 2 * p), dac_b[p], 2 * p + 1)
            drow_acc = _put_row(_put_row(drow_acc, -jnp.sum(ea_[p], axis=0, keepdims=True), 2 * p),
                                -jnp.sum(eb_[p], axis=0, keepdims=True), 2 * p + 1)
        per = SSD_PAIRS // SSD_GROUPS
        gsum = lambda xs: [functools.reduce(add, xs[g * per:(g + 1) * per]) for g in range(SSD_GROUPS)]
        dcb = gsum(dcb_p)
        dc = _zip(add, gsum(dc_p), _mms(dcb, bg, _NN))
        db = _zip(add, gsum(db_p), _mms(dcb, cg, _TN))
        for g in range(SSD_GROUPS):
            dcx_ref[:, XBC_B + g * SSD_STATE: XBC_B + (g + 1) * SSD_STATE] = db[g] * _dsilu(cxb[g])
            dcx_ref[:, XBC_C + g * SSD_STATE: XBC_C + (g + 1) * SSD_STATE] = dc[g] * _dsilu(cxc[g])
        dac = dac_acc + _mmx(_eye(CHUNK), drow_acc, _NT)
        dda = _mmx(_chunk_mask(CHUNK, True), dac, _NN)
        ddt = ddt_acc + dda * a
        ddtp = ddt * _sig(dtp + db_ref[...])
        ddtp_ref[...] = ddtp.astype(ddtp_ref.dtype)
        acc = _put_sub(jnp.zeros((8, HEAD), F32), jnp.sum(dda * da, axis=0, keepdims=True), 0)
        acc_ref[...] += _put_sub(acc, jnp.sum(ddtp, axis=0, keepdims=True), 1)

    rows = lambda w: pl.BlockSpec((CHUNK, w), lambda i: (nc - 1 - i, 0))
    act_ = lambda w: jax.ShapeDtypeStruct((t, w), F32)
    return _call(body, name, (nc,),
                 [rows(SSD_CONV_DIM), rows(HEAD), rows(D_INNER), _const(1, HEAD), _const(1, HEAD),
                  _const(1, D_INNER), _const(1, D_INNER), rows(D_INNER),
                  pl.BlockSpec((1, SSD_PAIRS, HEAD, SSD_STATE), lambda i: (nc - 1 - i, 0, 0, 0)), rows(D_INNER)],
                 [rows(SSD_CONV_DIM), rows(HEAD), rows(D_INNER), _const(8, D_INNER), _const(8, HEAD)],
                 [act_(SSD_CONV_DIM), jax.ShapeDtypeStruct((t, HEAD), MM_DTYPE), jax.ShapeDtypeStruct((t, D_INNER), MM_DTYPE),
                  jax.ShapeDtypeStruct((8, D_INNER), F32),
                  jax.ShapeDtypeStruct((8, HEAD), F32)],
                 [pltpu.VMEM((SSD_PAIRS, HEAD, SSD_STATE), F32), pltpu.VMEM((CHUNK, D_INNER), F32)], ("arbitrary",)
                 )(cx, dtp, z, dt_bias, a_log, dskip, norm_w, y, states, dh)


LN_TB = 512


def _ln_stats(x, y):
    u = ALPHA * x + y
    mu = jnp.mean(u, axis=1, keepdims=True)
    cen = u - mu
    rstd = lax.rsqrt(jnp.mean(cen * cen, axis=1, keepdims=True) + LN_EPS)
    return cen * rstd


def _ln_fwd(x, y, g, b, name):
    t, d = x.shape

    def body(x_ref, y_ref, g_ref, b_ref, o_ref, omm_ref):
        out = _ln_stats(x_ref[...], y_ref[...]) * g_ref[...] + b_ref[...]
        o_ref[...] = out
        omm_ref[...] = out.astype(omm_ref.dtype)

    return _call(body, name, (t // LN_TB,), [_rows(d, LN_TB), _rows(d, LN_TB), _const(1, d), _const(1, d)],
                 [_rows(d, LN_TB)] * 2, [jax.ShapeDtypeStruct((t, d), F32), jax.ShapeDtypeStruct((t, d), MM_DTYPE)],
                 semantics=("parallel",))(x, y, g, b)


def _ln_bwd(dout, x, y, g, name):
    t, d = x.shape

    def body(d_ref, x_ref, y_ref, g_ref, du_ref, dumm_ref, acc_ref):
        u = ALPHA * x_ref[...] + y_ref[...]
        mu = jnp.mean(u, axis=1, keepdims=True)
        cen = u - mu
        rstd = lax.rsqrt(jnp.mean(cen * cen, axis=1, keepdims=True) + LN_EPS)
        xh = cen * rstd
        do = d_ref[...]
        dxh = do * g_ref[...]
        du = rstd * (dxh - jnp.mean(dxh, axis=1, keepdims=True) - xh * jnp.mean(dxh * xh, axis=1, keepdims=True))
        du_ref[...] = du
        dumm_ref[...] = du.astype(dumm_ref.dtype)
        acc = _put_sub(jnp.zeros((8, d), F32), jnp.sum(do * xh, axis=0, keepdims=True), 0)
        acc = _put_sub(acc, jnp.sum(do, axis=0, keepdims=True), 1)

        @pl.when(pl.program_id(0) == 0)
        def _():
            acc_ref[...] = jnp.zeros_like(acc_ref)

        acc_ref[...] += acc

    return _call(body, name, (t // LN_TB,), [_rows(d, LN_TB)] * 3 + [_const(1, d)],
                 [_rows(d, LN_TB), _rows(d, LN_TB), _const(8, d)],
                 [jax.ShapeDtypeStruct((t, d), F32), jax.ShapeDtypeStruct((t, d), MM_DTYPE),
                  jax.ShapeDtypeStruct((8, d), F32)],
                 semantics=("arbitrary",))(dout, x, y, g)


def _loss_head(out, target, name):
    t, d = out.shape

    def body(o_ref, t_ref, d_ref, acc_ref):
        err = o_ref[...] - t_ref[...]
        d_ref[...] = err * (1.0 / d)

        @pl.when(pl.program_id(0) == 0)
        def _():
            acc_ref[...] = jnp.zeros_like(acc_ref)

        acc_ref[...] += _put_sub(jnp.zeros((8, d), F32), jnp.sum(err * err, axis=0, keepdims=True), 0)

    return _call(body, name, (t // LN_TB,), [_rows(d, LN_TB)] * 2, [_rows(d, LN_TB), _const(8, d)],
                 [jax.ShapeDtypeStruct((t, d), F32), jax.ShapeDtypeStruct((8, d), F32)],
                 semantics=("arbitrary",))(out, target)


def _adamw(w, gslots, m, v, name):
    r, c = w.shape
    rb = _tile_rows(r)
    c1 = 1.0 - ADAM_B1 ** ADAM_STEP
    c2 = 1.0 - ADAM_B2 ** ADAM_STEP

    def body(w_ref, g_ref, m_ref, v_ref, go_ref, d_ref, mo_ref, vo_ref):
        g = g_ref[0].astype(F32)
        for s in range(1, N_DEV):
            g = g + g_ref[s].astype(F32)
        mn = ADAM_B1 * m_ref[...] + (1.0 - ADAM_B1) * g
        vn = ADAM_B2 * v_ref[...] + (1.0 - ADAM_B2) * (g * g)
        go_ref[...] = g
        mo_ref[...] = mn
        vo_ref[...] = vn
        d_ref[...] = -ADAM_LR * ((mn / c1) / (jnp.sqrt(vn / c2) + ADAM_EPS) + ADAM_WD * w_ref[...])

    blk = pl.BlockSpec((rb, c), lambda i: (i, 0))
    sds = jax.ShapeDtypeStruct((r, c), F32)
    return _call(body, name, (r // rb,), [blk, pl.BlockSpec((N_DEV, rb, c), lambda i: (0, i, 0)), blk, blk],
                 [blk] * 4, [sds] * 4, semantics=("parallel",))(w, gslots, m, v)


def _tile_rows(r):
    for rb in (256, 128, 64, 32, 16, 8):
        if r % rb == 0:
            return rb
    return r


def _pack(arrs, lead=0):
    flats = []
    for a in arrs:
        f = a.reshape(a.shape[:lead] + (-1,)).astype(F32)
        flats.append(jnp.pad(f, [(0, 0)] * lead + [(0, (-f.shape[-1]) % 128)]))
    v = jnp.concatenate(flats, axis=-1)
    v = jnp.pad(v, [(0, 0)] * lead + [(0, (-v.shape[-1]) % 1024)])
    return v.reshape(v.shape[:lead] + (-1, 128))


def _unpack(buf, shapes, lead=0):
    flat = buf.reshape(buf.shape[:lead] + (-1,))
    outs, off = [], 0
    for s in shapes:
        n = math.prod(s)
        outs.append(flat[..., off:off + n].reshape(buf.shape[:lead] + tuple(s)))
        off += n + (-n) % 128
    return outs


def _cols_gathered(g):
    n, l, r, c = g.shape
    return g.transpose(1, 2, 0, 3).reshape(l, r, n * c)


def _cols_to_slabs(full):
    l, r, c = full.shape
    return full.reshape(l, r, N_DEV, c // N_DEV).transpose(2, 0, 1, 3)


def _rows_gathered(g):
    n, l, r, c = g.shape
    return g.transpose(1, 0, 2, 3).reshape(l, n * r, c)


def _rows_to_slabs(full):
    l, r, c = full.shape
    return full.reshape(l, N_DEV, r // N_DEV, c).transpose(1, 0, 2, 3)


def _pad_cols(w, at, width):
    return jnp.pad(w, ((0, 0), (at, width - at - w.shape[1])))


def _pad_lanes(v, width=HEAD):
    return jnp.pad(v.reshape(1, -1), ((0, 0), (0, width - v.size)))


def _taps8(w, bias=None):
    rows = [w] if bias is None else [w, bias.reshape(1, -1)]
    w8 = jnp.concatenate(rows, axis=0)
    return jnp.pad(w8, ((0, 8 - w8.shape[0]), (0, 0)))


def _gdn_forward(x, p, tag):
    pq = _matmul(x, p["w_qkv"], "nn", tag + "_in_qkv")
    z = _matmul(x, p["w_z"], "nn", tag + "_in_z")
    ba = _matmul(x, p["w_ba"], "nn", tag + "_in_ba")
    c = _conv_fwd(pq, p["conv8"], 4, tag + "_conv")
    qn, kn, v, beta, gc = _gdn_ew_fwd(c, ba, p["a_log"], p["dt_bias"], tag + "_ew")
    o, h, states = _gdn_chunk_fwd(qn, kn, v, beta, gc, z, p["norm_w"], tag + "_chunk")
    y = _matmul(h, p["w_out"], "nn", tag + "_out")
    return y, dict(pq=pq, z=z, ba=ba, c=c, qn=qn, kn=kn, v=v, beta=beta, gc=gc, o=o, h=h, states=states)


def _gdn_backward(x, du, du_mm, p, s, tag):
    dh = _matmul(du_mm, p["w_out"], "nt", tag + "_bwd_dh")
    g_out = _matmul(s["h"], du_mm, "tn", tag + "_bwd_wout")
    dqh, dkh, dv, dz, dbeta, dgc, nacc = _gdn_chunk_bwd(
        s["qn"], s["kn"], s["v"], s["beta"], s["gc"], s["z"], p["norm_w"], s["o"], s["states"], dh, tag + "_bwd_chunk")
    dc, dba, sacc = _gdn_ew_bwd(s["c"], s["ba"], p["a_log"], p["dt_bias"], dqh, dkh, dv, dbeta, dgc, tag + "_bwd_ew")
    dpq, dconv = _conv_bwd(dc, s["pq"], p["conv8"], 4, tag + "_bwd_conv")
    dx = _matmul(dpq, p["w_qkv"], "nt", tag + "_bwd_dx_qkv", add=du, add_scale=ALPHA)
    dx = _matmul(dz, p["w_z"], "nt", tag + "_bwd_dx_z", add=dx)
    dx = _matmul(dba, p["w_ba"], "nt", tag + "_bwd_dx_ba", add=dx)
    g_qkv = _matmul(x, dpq, "tn", tag + "_bwd_w_qkv")
    g_z = _matmul(x, dz, "tn", tag + "_bwd_w_z")
    g_ba = _matmul(x, dba, "tn", tag + "_bwd_w_ba")
    g_in = jnp.concatenate([g_qkv, g_z, g_ba[:, :GDN_V_HEADS], g_ba[:, HEAD:HEAD + GDN_V_HEADS]], axis=1)
    grads = dict(w_in=g_in, w_out=g_out, conv_w=dconv[:4], a_log=sacc[0, :GDN_V_HEADS], dt_bias=sacc[1, :GDN_V_HEADS],
                 norm_w=nacc[0])
    return dx, grads


def _sc_forward(x, p, tag):
    hh = _matmul(x, p["w_h"], "nn", tag + "_in_h")
    bg = _matmul(x, p["w_b"], "nn", tag + "_in_b")
    cg = _matmul(x, p["w_c"], "nn", tag + "_in_c")
    z = _matmul(x, p["w_z"], "nn", tag + "_in_z")
    cv = _conv_fwd(cg, p["conv8"], 3, tag + "_conv", u2=hh)
    h = _sc_gate_fwd(bg, cv, z, tag + "_gate")
    y = _matmul(h, p["w_out"], "nn", tag + "_out")
    return y, dict(hh=hh, bg=bg, cg=cg, z=z, cv=cv, h=h)


def _sc_backward(x, du, du_mm, p, s, tag):
    dh = _matmul(du_mm, p["w_out"], "nt", tag + "_bwd_dh")
    g_out = _matmul(s["h"], du_mm, "tn", tag + "_bwd_wout")
    dbg, dcv, dz = _sc_gate_bwd(dh, s["bg"], s["cv"], s["z"], tag + "_bwd_gate")
    dcg, dhh, dconv = _conv_bwd(dcv, s["cg"], p["conv8"], 3, tag + "_bwd_conv", u2=s["hh"])
    dx = _matmul(dhh, p["w_h"], "nt", tag + "_bwd_dx_h", add=du, add_scale=ALPHA)
    dx = _matmul(dbg, p["w_b"], "nt", tag + "_bwd_dx_b", add=dx)
    dx = _matmul(dcg, p["w_c"], "nt", tag + "_bwd_dx_c", add=dx)
    dx = _matmul(dz, p["w_z"], "nt", tag + "_bwd_dx_z", add=dx)
    g_in = jnp.concatenate([_matmul(x, d, "tn", tag + "_bwd_w_" + n)
                            for n, d in (("h", dhh), ("b", dbg), ("c", dcg), ("z", dz))], axis=1)
    return dx, dict(w_in=g_in, w_out=g_out, conv_w=dconv[:3])


def _ssd_forward(x, p, tag):
    z = _matmul(x, p["w_z"], "nn", tag + "_in_z")
    xbc = _matmul(x, p["w_xbc"], "nn", tag + "_in_xbc")
    dtp = _matmul(x, p["w_dt"], "nn", tag + "_in_dt")
    cx = _conv_fwd(xbc, p["conv8"], 4, tag + "_conv", bias=True)
    y, h, states = _ssd_chunk_fwd(cx, dtp, z, p["dt_bias"], p["a_log"], p["dskip"], p["norm_w"], tag + "_chunk")
    out = _matmul(h, p["w_out"], "nn", tag + "_out")
    return out, dict(z=z, xbc=xbc, dtp=dtp, cx=cx, y=y, h=h, states=states)


def _ssd_backward(x, du, du_mm, p, s, tag):
    dh = _matmul(du_mm, p["w_out"], "nt", tag + "_bwd_dh")
    g_out = _matmul(s["h"], du_mm, "tn", tag + "_bwd_wout")
    dcx, ddtp, dz, wide, acc = _ssd_chunk_bwd(s["cx"], s["dtp"], s["z"], p["dt_bias"], p["a_log"], p["dskip"],
                                              p["norm_w"], s["y"], s["states"], dh, tag + "_bwd_chunk")
    dxbc, dconv = _conv_bwd(dcx, s["xbc"], p["conv8"], 4, tag + "_bwd_conv")
    dx = _matmul(dz, p["w_z"], "nt", tag + "_bwd_dx_z", add=du, add_scale=ALPHA)
    dx = _matmul(dxbc, p["w_xbc"], "nt", tag + "_bwd_dx_xbc", add=dx)
    dx = _matmul(ddtp, p["w_dt"], "nt", tag + "_bwd_dx_dt", add=dx)
    g_dt = _matmul(x, ddtp, "tn", tag + "_bwd_w_dt")
    g_in = jnp.concatenate([_matmul(x, dz, "tn", tag + "_bwd_w_z"), _matmul(x, dxbc, "tn", tag + "_bwd_w_xbc"),
                            g_dt[:, :32]], axis=1)
    grads = dict(w_in=g_in, w_out=g_out, conv_w=dconv[:4], conv_b=dconv[4], a_log=acc[0, :32], dt_bias=acc[1, :32],
                 d_skip=jnp.sum(wide[1].reshape(32, 64), axis=1), norm_w=wide[0])
    return dx, grads


_WEIGHTS = ['gdn_w_in', 'gdn_conv_w', 'gdn_a_log', 'gdn_dt_bias', 'gdn_norm_w', 'gdn_w_out', 'sc_w_in', 'sc_conv_w',
            'sc_w_out', 'ssd_w_in', 'ssd_conv_w', 'ssd_conv_b', 'ssd_a_log', 'ssd_dt_bias', 'ssd_d_skip',
            'ssd_norm_w', 'ssd_w_out', 'ln_g', 'ln_b']
_BIG = {'gdn_w_in': 'cols', 'gdn_w_out': 'rows', 'sc_w_in': 'cols', 'sc_w_out': 'rows', 'ssd_w_in': 'cols',
        'ssd_w_out': 'rows'}
_SMALL_SHARDED = ['gdn_conv_w', 'sc_conv_w', 'ssd_conv_w', 'ssd_conv_b', 'ssd_norm_w']
_SMALL = [n for n in _WEIGHTS if n not in _BIG]


def kernel(x, gdn_w_in, gdn_conv_w, gdn_a_log, gdn_dt_bias, gdn_norm_w, gdn_w_out, sc_w_in, sc_conv_w, sc_w_out, ssd_w_in, ssd_conv_w, ssd_conv_b, ssd_a_log, ssd_dt_bias, ssd_d_skip, ssd_norm_w, ssd_w_out, ln_g, ln_b, loss_target, m_gdn_w_in, m_gdn_conv_w, m_gdn_a_log, m_gdn_dt_bias, m_gdn_norm_w, m_gdn_w_out, m_sc_w_in, m_sc_conv_w, m_sc_w_out, m_ssd_w_in, m_ssd_conv_w, m_ssd_conv_b, m_ssd_a_log, m_ssd_dt_bias, m_ssd_d_skip, m_ssd_norm_w, m_ssd_w_out, m_ln_g, m_ln_b, v_gdn_w_in, v_gdn_conv_w, v_gdn_a_log, v_gdn_dt_bias, v_gdn_norm_w, v_gdn_w_out, v_sc_w_in, v_sc_conv_w, v_sc_w_out, v_ssd_w_in, v_ssd_conv_w, v_ssd_conv_b, v_ssd_a_log, v_ssd_dt_bias, v_ssd_d_skip, v_ssd_norm_w, v_ssd_w_out, v_ln_g, v_ln_b):
    args = locals()
    wts = {n: args[n] for n in _WEIGHTS}
    mom = {n: args["m_" + n] for n in _WEIGHTS}
    vel = {n: args["v_" + n] for n in _WEIGHTS}
    me = 4 * lax.axis_index("x") + 2 * lax.axis_index("y") + lax.axis_index("c")
    x0, target = x[0], loss_target[0]

    full = {}
    for n, how in _BIG.items():
        g = _exchange(wts[n].astype(MM_DTYPE), "gather_" + n, slabs=False)
        full[n] = _cols_gathered(g) if how == "cols" else _rows_gathered(g)
    small_shapes = [wts[n].shape for n in _SMALL_SHARDED]
    gathered = _exchange(_pack([wts[n] for n in _SMALL_SHARDED]), "gather_small", slabs=False)
    for n, g in zip(_SMALL_SHARDED, _unpack(gathered, small_shapes, lead=1)):
        full[n] = jnp.moveaxis(g, 0, -2).reshape(g.shape[1:-1] + (N_DEV * g.shape[-1],))
    for n in _SMALL:
        full.setdefault(n, wts[n])

    def gdn_params(j):
        w = full['gdn_w_in'][j]
        return dict(w_qkv=w[:, :GDN_CONV_DIM], w_z=w[:, GDN_CONV_DIM:GDN_CONV_DIM + D_INNER],
                    w_ba=jnp.concatenate([_pad_cols(w[:, 6144:6160], 0, HEAD), _pad_cols(w[:, 6160:6176], 0, HEAD)], 1),
                    conv8=_taps8(full['gdn_conv_w'][j]), a_log=_pad_lanes(full['gdn_a_log'][j]),
                    dt_bias=_pad_lanes(full['gdn_dt_bias'][j]), norm_w=full['gdn_norm_w'][j].reshape(1, HEAD),
                    w_out=full['gdn_w_out'][j])

    w = full['sc_w_in'][0]
    sc_p = dict(w_h=w[:, :2048], w_b=w[:, 2048:4096], w_c=w[:, 4096:6144], w_z=w[:, 6144:],
                conv8=_taps8(full['sc_conv_w'][0]), w_out=full['sc_w_out'][0])
    w = full['ssd_w_in'][0]
    ssd_p = dict(w_z=w[:, :D_INNER], w_xbc=w[:, D_INNER:D_INNER + SSD_CONV_DIM],
                 w_dt=_pad_cols(w[:, D_INNER + SSD_CONV_DIM:], 0, HEAD),
                 conv8=_taps8(full['ssd_conv_w'][0], full['ssd_conv_b'][0]), a_log=_pad_lanes(full['ssd_a_log'][0]),
                 dt_bias=_pad_lanes(full['ssd_dt_bias'][0]),
                 dskip=jnp.repeat(full['ssd_d_skip'][0], 64).reshape(1, D_INNER),
                 norm_w=full['ssd_norm_w'][0].reshape(1, D_INNER), w_out=full['ssd_w_out'][0])
    layers = [("gdn", _gdn_forward, _gdn_backward, gdn_params(0)), ("sc", _sc_forward, _sc_backward, sc_p),
              ("ssd", _ssd_forward, _ssd_backward, ssd_p), ("gdn", _gdn_forward, _gdn_backward, gdn_params(1))]

    acts, acts_mm, ys, saved = [x0], [x0.astype(MM_DTYPE)], [], []
    for i, (kind, fwd, _, p) in enumerate(layers):
        y, s = fwd(acts_mm[-1], p, "l%d_%s" % (i, kind))
        out, out_mm = _ln_fwd(acts[-1], y, full['ln_g'][i].reshape(1, -1), full['ln_b'][i].reshape(1, -1), "l%d_ln" % i)
        acts.append(out)
        acts_mm.append(out_mm)
        ys.append(y)
        saved.append(s)
    dact, loss_acc = _loss_head(acts[-1], target, "loss_head")
    loss = lax.psum(0.5 / D_MODEL * jnp.sum(loss_acc[0]), ("x", "y", "c"))

    lg = [None] * DEPTH
    d_ln_g, d_ln_b = [None] * DEPTH, [None] * DEPTH
    for i in reversed(range(DEPTH)):
        kind, _, bwd, p = layers[i]
        du, du_mm, acc = _ln_bwd(dact, acts[i], ys[i], full['ln_g'][i].reshape(1, -1), "l%d_ln_bwd" % i)
        d_ln_g[i], d_ln_b[i] = acc[0], acc[1]
        dact, lg[i] = bwd(acts_mm[i], du, du_mm, p, saved[i], "l%d_%s" % (i, kind))
    grad_x = dact[None]

    stack = lambda k: jnp.stack([lg[0][k], lg[3][k]])
    local = {
        'gdn_w_in': stack('w_in'), 'gdn_conv_w': stack('conv_w'), 'gdn_a_log': stack('a_log'),
        'gdn_dt_bias': stack('dt_bias'), 'gdn_norm_w': stack('norm_w'), 'gdn_w_out': stack('w_out'),
        'sc_w_in': lg[1]['w_in'][None], 'sc_conv_w': lg[1]['conv_w'][None], 'sc_w_out': lg[1]['w_out'][None],
        'ssd_w_in': lg[2]['w_in'][None], 'ssd_conv_w': lg[2]['conv_w'][None], 'ssd_conv_b': lg[2]['conv_b'][None],
        'ssd_a_log': lg[2]['a_log'][None], 'ssd_dt_bias': lg[2]['dt_bias'][None], 'ssd_d_skip': lg[2]['d_skip'][None],
        'ssd_norm_w': lg[2]['norm_w'][None], 'ssd_w_out': lg[2]['w_out'][None],
        'ln_g': jnp.stack(d_ln_g), 'ln_b': jnp.stack(d_ln_b)}

    out = {}
    for n, how in _BIG.items():
        slabs = _cols_to_slabs(local[n]) if how == "cols" else _rows_to_slabs(local[n])
        recv = _exchange(slabs.astype(MM_DTYPE), "scatter_" + n, slabs=True)
        shp = wts[n].shape
        r, c = shp[0] * shp[1], shp[2]
        res = _adamw(wts[n].reshape(r, c), recv.reshape(N_DEV, r, c), mom[n].reshape(r, c), vel[n].reshape(r, c),
                     "adamw_" + n)
        out[n] = [a.reshape(shp) for a in res]
    full_shapes = [local[n].shape for n in _SMALL]
    gathered = _exchange(_pack([local[n] for n in _SMALL]), "gather_small_grads", slabs=False)
    gs = []
    for n, g in zip(_SMALL, _unpack(gathered, full_shapes, lead=1)):
        if n in _SMALL_SHARDED:
            width = wts[n].shape[-1]
            g = lax.dynamic_slice_in_dim(g, me * width, width, axis=g.ndim - 1)
        gs.append(g)
    shapes = [wts[n].shape for n in _SMALL]
    res = _adamw(_pack([wts[n] for n in _SMALL]), _pack(gs, lead=1), _pack([mom[n] for n in _SMALL]),
                 _pack([vel[n] for n in _SMALL]), "adamw_small")
    for k, n in enumerate(_SMALL):
        out[n] = [_unpack(a, shapes)[k] for a in res]

    return (loss, grad_x, *[out[n][0] for n in _WEIGHTS], *[out[n][1] for n in _WEIGHTS],
            *[out[n][2] for n in _WEIGHTS], *[out[n][3] for n in _WEIGHTS])
```

```python
import functools
import math

import jax
import jax.numpy as jnp
from jax import lax
from jax.experimental import pallas as pl
from jax.experimental.pallas import tpu as pltpu

F32 = jnp.float32
MM_DTYPE = jnp.bfloat16

N_DEV = 8
D_MODEL = 1024
D_INNER = 2048
CHUNK = 64
HEAD = 128
GDN_V_HEADS = 16
GDN_GROUP = 16
GDN_QK_HEADS = 8
GDN_QK_DIM = 1024
GDN_CONV_DIM = 4096
SSD_PAIRS = 16
SSD_GROUPS = 4
SSD_STATE = 128
SSD_CONV_DIM = 3072
DEPTH = 4
ALPHA = (2 * DEPTH) ** 0.25
RMS_EPS = 1e-6
LN_EPS = 1e-5
L2_EPS = 1e-6
ADAM_LR, ADAM_B1, ADAM_B2, ADAM_EPS, ADAM_WD, ADAM_STEP = 0.001, 0.9, 0.999, 1e-08, 0.01, 10

VMEM_LIMIT_BYTES = 48 * 1024 * 1024
NEG_BIG = -1e30

_NN = (((1,), (0,)), ((), ()))
_NT = (((1,), (1,)), ((), ()))
_TN = (((0,), (0,)), ((), ()))


def _mm(a, b, dims):
    return lax.dot_general(a.astype(MM_DTYPE), b.astype(MM_DTYPE), dims, preferred_element_type=F32)


def _mmx(a, b, dims):
    return lax.dot_general(a, b, dims, precision=lax.Precision.HIGHEST, preferred_element_type=F32)


def _iota(shape, dim):
    return lax.broadcasted_iota(jnp.int32, shape, dim)


def _eye(n):
    return (_iota((n, n), 0) == _iota((n, n), 1)).astype(F32)


def _sig(x):
    return jax.nn.sigmoid(x)


def _silu(x):
    return x * _sig(x)


def _dsilu(x):
    s = _sig(x)
    return s * (1.0 + x * (1.0 - s))


def _softplus(x):
    return jnp.maximum(x, 0.0) + jnp.log(1.0 + jnp.exp(-jnp.abs(x)))


def _col(x, h):
    return jnp.sum(jnp.where(_iota(x.shape, 1) == h, x, 0.0), axis=1, keepdims=True)


def _row(x, h):
    return jnp.sum(jnp.where(_iota(x.shape, 0) == h, x, 0.0), axis=0, keepdims=True)


def _put_col(acc, col, h):
    return jnp.where(_iota(acc.shape, 1) == h, col, acc)


def _put_row(acc, row, h):
    return jnp.where(_iota(acc.shape, 0) == h, row, acc)


def _put_sub(acc, row, j):
    return acc + jnp.where(_iota(acc.shape, 0) == j, row, 0.0)


def _lanes(h):
    return pl.ds(h * HEAD, HEAD) if isinstance(h, int) else pl.ds(pl.multiple_of(h * HEAD, HEAD), HEAD)


def _total(x):
    return jnp.sum(jnp.sum(x, axis=1, keepdims=True), axis=0, keepdims=True)


def _call(body, name, grid, in_specs, out_specs, out_shape, scratch_shapes=(), semantics=None):
    return pl.pallas_call(
        body, name=name, grid=grid, in_specs=in_specs, out_specs=out_specs, out_shape=out_shape,
        scratch_shapes=list(scratch_shapes),
        compiler_params=pltpu.CompilerParams(dimension_semantics=semantics, vmem_limit_bytes=VMEM_LIMIT_BYTES))


def _tile(n, pref):
    if n <= pref:
        return n
    t = pref
    while n % t:
        t -= 128
    return t


def _exchange_copies(src_ref, out_ref, send_sems, recv_sems, local_sem, slabs):
    x, y, c = lax.axis_index("x"), lax.axis_index("y"), lax.axis_index("c")
    me = 4 * x + 2 * y + c
    mine = src_ref.at[me] if slabs else src_ref
    local = pltpu.make_async_copy(mine, out_ref.at[me], local_sem)
    sends, recvs = [], []
    for r in range(1, N_DEV):
        px = 1 - x if r & 4 else x
        py = 1 - y if r & 2 else y
        pc = 1 - c if r & 1 else c
        peer = 4 * px + 2 * py + pc
        kw = dict(send_sem=send_sems.at[r - 1], recv_sem=recv_sems.at[r - 1], device_id=(px, py, pc),
                  device_id_type=pl.DeviceIdType.MESH)
        sends.append(pltpu.make_async_remote_copy(src_ref=src_ref.at[peer] if slabs else src_ref,
                                                  dst_ref=out_ref.at[me], **kw))
        recvs.append(pltpu.make_async_remote_copy(src_ref=mine, dst_ref=out_ref.at[peer], **kw))
    return local, sends, recvs


def _exchange_start(*refs, slabs):
    local, sends, _ = _exchange_copies(*refs, slabs=slabs)
    local.start()
    for cp in sends:
        cp.start()


def _exchange_wait(*refs, slabs):
    local, sends, recvs = _exchange_copies(*refs, slabs=slabs)
    for cp in recvs:
        cp.wait_recv()
    for cp in sends:
        cp.wait_send()
    local.wait()


def _exchange_sems():
    return [pltpu.SemaphoreType.DMA((N_DEV - 1,)), pltpu.SemaphoreType.DMA((N_DEV - 1,)), pltpu.SemaphoreType.DMA(())]


def _exchange_shape(src, slabs):
    return jax.ShapeDtypeStruct((N_DEV,) + tuple(src.shape[1:] if slabs else src.shape), src.dtype)


def _exchange(src, name, slabs):
    def body(*refs):
        _exchange_start(*refs, slabs=slabs)
        _exchange_wait(*refs, slabs=slabs)

    return pl.pallas_call(
        body, name=name,
        in_specs=[pl.BlockSpec(memory_space=pl.ANY)], out_specs=pl.BlockSpec(memory_space=pl.ANY),
        out_shape=_exchange_shape(src, slabs), scratch_shapes=_exchange_sems(),
    )(src)


MM_TM, MM_TN, MM_TK = 1024, 1024, 1024


def _matmul(a, b, mode, name, add=None, add_scale=1.0, carry=None):
    if mode == "nn":
        (m, k), (_, n) = a.shape, b.shape
    elif mode == "nt":
        (m, k), (n, _) = a.shape, b.shape
    else:
        (k, m), (_, n) = a.shape, b.shape
    tm, tn, tk = _tile(m, MM_TM), _tile(n, MM_TN), _tile(k, MM_TK)
    nk = k // tk
    grid = (m // tm, n // tn, nk)
    dims = {"nn": _NN, "nt": _NT, "tn": _TN}[mode]
    n_in = 2 + (add is not None)

    def body(*refs):
        a_ref, b_ref, o_ref = refs[0], refs[1], refs[n_in + (carry is not None)]
        if carry is not None:
            ex = (refs[n_in], refs[n_in + 2]) + tuple(refs[n_in + 3:])
            step = (pl.program_id(0) * grid[1] + pl.program_id(1)) * grid[2] + pl.program_id(2)

            @pl.when(step == 0)
            def _():
                _exchange_start(*ex, slabs=carry[1])

        part = _mm(a_ref[...], b_ref[...], dims)
        first = part if add is None else part + add_scale * refs[2][...]
        if nk == 1:
            o_ref[...] = first
        else:
            @pl.when(pl.program_id(2) == 0)
            def _():
                o_ref[...] = first

            @pl.when(pl.program_id(2) > 0)
            def _():
                o_ref[...] += part

        if carry is not None:
            @pl.when(step == grid[0] * grid[1] * grid[2] - 1)
            def _():
                _exchange_wait(*ex, slabs=carry[1])

    if mode == "nn":
        specs = [pl.BlockSpec((tm, tk), lambda i, j, q: (i, q)), pl.BlockSpec((tk, tn), lambda i, j, q: (q, j))]
    elif mode == "nt":
        specs = [pl.BlockSpec((tm, tk), lambda i, j, q: (i, q)), pl.BlockSpec((tn, tk), lambda i, j, q: (j, q))]
    else:
        specs = [pl.BlockSpec((tk, tm), lambda i, j, q: (q, i)), pl.BlockSpec((tk, tn), lambda i, j, q: (q, j))]
    out_spec = pl.BlockSpec((tm, tn), lambda i, j, q: (i, j))
    args = [a, b]
    if add is not None:
        specs.append(out_spec)
        args.append(add)
    out_shape = jax.ShapeDtypeStruct((m, n), F32)
    if carry is None:
        return _call(body, name, grid, specs, out_spec, out_shape, semantics=("parallel", "parallel", "arbitrary"))(*args)
    hbm = pl.BlockSpec(memory_space=pl.ANY)
    return _call(body, name, grid, specs + [hbm], [out_spec, hbm], [out_shape, _exchange_shape(*carry)],
                 _exchange_sems(), ("arbitrary", "arbitrary", "arbitrary"))(*args, carry[0])


CONV_TB = 512
CONV_CB = 1024
HALO = 8


def _conv_specs(t, cb_n):
    tb = min(CONV_TB, t)
    nb = tb // HALO
    blk = pl.BlockSpec((tb, cb_n), lambda c, i: (i, c))
    prev = pl.BlockSpec((HALO, cb_n), lambda c, i: (jnp.maximum(i * nb - 1, 0), c))
    nxt = pl.BlockSpec((HALO, cb_n), lambda c, i: (jnp.minimum((i + 1) * nb, t // HALO - 1), c))
    w = pl.BlockSpec((8, cb_n), lambda c, i: (0, c))
    return blk, prev, nxt, w


def _shift_down(ext, s, tb):
    return (pltpu.roll(ext, s, 0) if s else ext)[HALO:HALO + tb]


def _shift_up(ext, s, tb):
    n = ext.shape[0]
    return (pltpu.roll(ext, n - s, 0) if s else ext)[0:tb]


def _conv_fwd(u, w8, ktaps, name, u2=None, bias=False):
    t, ch = u.shape
    cb_n = min(CONV_CB, ch)
    tb = min(CONV_TB, t)
    two = u2 is not None

    def body(*refs):
        if two:
            u_ref, up_ref, v_ref, vp_ref, w_ref, o_ref = refs
        else:
            u_ref, up_ref, w_ref, o_ref = refs
        first = pl.program_id(1) == 0
        blk, halo = u_ref[...], up_ref[...]
        if two:
            blk, halo = blk * v_ref[...], halo * vp_ref[...]
        ext = jnp.concatenate([jnp.where(first, 0.0, halo), blk], axis=0)
        acc = jnp.zeros((tb, cb_n), F32)
        for j in range(ktaps):
            acc = acc + w_ref[j:j + 1, :] * _shift_down(ext, ktaps - 1 - j, tb)
        if bias:
            acc = acc + w_ref[ktaps:ktaps + 1, :]
        o_ref[...] = acc

    blk, prev, _, wspec = _conv_specs(t, cb_n)
    specs, args = [blk, prev], [u, u]
    if two:
        specs += [blk, prev]
        args += [u2, u2]
    specs.append(wspec)
    args.append(w8)
    return _call(body, name, (ch // cb_n, t // tb), specs, blk, jax.ShapeDtypeStruct((t, ch), F32),
                 semantics=("parallel", "parallel"))(*args)


def _conv_bwd(dc, u, w8, ktaps, name, u2=None):
    t, ch = u.shape
    cb_n = min(CONV_CB, ch)
    tb = min(CONV_TB, t)
    two = u2 is not None

    def body(*refs):
        if two:
            dc_ref, dn_ref, u_ref, up_ref, v_ref, vp_ref, w_ref, du_ref, dv_ref, dw_ref = refs
        else:
            dc_ref, dn_ref, u_ref, up_ref, w_ref, du_ref, dw_ref = refs
        i = pl.program_id(1)
        first, last = i == 0, i == t // tb - 1
        d = dc_ref[...]
        dext = jnp.concatenate([d, jnp.where(last, 0.0, dn_ref[...])], axis=0)
        blk, halo = u_ref[...], up_ref[...]
        if two:
            blk, halo = blk * v_ref[...], halo * vp_ref[...]
        uext = jnp.concatenate([jnp.where(first, 0.0, halo), blk], axis=0)
        du = jnp.zeros((tb, cb_n), F32)
        dw = jnp.zeros((8, cb_n), F32)
        for j in range(ktaps):
            s = ktaps - 1 - j
            du = du + w_ref[j:j + 1, :] * _shift_up(dext, s, tb)
            dw = _put_sub(dw, jnp.sum(d * _shift_down(uext, s, tb), axis=0, keepdims=True), j)
        dw = _put_sub(dw, jnp.sum(d, axis=0, keepdims=True), ktaps)
        if two:
            du_ref[...] = (du * v_ref[...]).astype(du_ref.dtype)
            dv_ref[...] = (du * u_ref[...]).astype(dv_ref.dtype)
        else:
            du_ref[...] = du.astype(du_ref.dtype)

        @pl.when(first)
        def _():
            dw_ref[...] = jnp.zeros_like(dw_ref)

        dw_ref[...] += dw

    blk, prev, nxt, wspec = _conv_specs(t, cb_n)
    specs, args = [blk, nxt, blk, prev], [dc, dc, u, u]
    if two:
        specs += [blk, prev]
        args += [u2, u2]
    specs.append(wspec)
    args.append(w8)
    act = jax.ShapeDtypeStruct((t, ch), MM_DTYPE)
    outs = ([blk, blk, wspec], [act, act, jax.ShapeDtypeStruct((8, ch), F32)]) if two else \
        ([blk, wspec], [act, jax.ShapeDtypeStruct((8, ch), F32)])
    return _call(body, name, (ch // cb_n, t // tb), specs, outs[0], outs[1],
                 semantics=("parallel", "arbitrary"))(*args)


EW_TB = 256


def _chunk_mask(n, upper):
    i, j = _iota((n, n), 0), _iota((n, n), 1)
    same = jnp.right_shift(i, 6) == jnp.right_shift(j, 6)
    return (same & ((j >= i) if upper else (i >= j))).astype(F32)


def _rows(width, tb=EW_TB):
    return pl.BlockSpec((tb, width), lambda i: (i, 0))


def _const(rows, width):
    return pl.BlockSpec((rows, width), lambda i: (0, 0))


def _gdn_ew_fwd(c, ba, a_log, dt_bias, name):
    t = c.shape[0]
    tb = EW_TB

    def body(c_ref, ba_ref, al_ref, db_ref, q_ref, k_ref, v_ref, beta_ref, gc_ref):
        for h in range(GDN_QK_HEADS):
            for base, ref, scale in ((0, q_ref, HEAD ** -0.5), (GDN_QK_DIM, k_ref, 1.0)):
                s = _silu(c_ref[:, base + h * HEAD: base + (h + 1) * HEAD])
                r = lax.rsqrt(jnp.sum(s * s, axis=1, keepdims=True) + L2_EPS)
                ref[:, h * HEAD:(h + 1) * HEAD] = s * (r * scale)
        v_ref[...] = _silu(c_ref[:, 2 * GDN_QK_DIM:])
        beta_ref[...] = _sig(ba_ref[:, :HEAD])
        g = -jnp.exp(al_ref[...]) * _softplus(ba_ref[:, HEAD:] + db_ref[...])
        gc_ref[...] = _mmx(_chunk_mask(tb, False), g, _NN)

    act = lambda w: jax.ShapeDtypeStruct((t, w), F32)
    return _call(body, name, (t // tb,),
                 [_rows(GDN_CONV_DIM), _rows(2 * HEAD), _const(1, HEAD), _const(1, HEAD)],
                 [_rows(GDN_QK_DIM), _rows(GDN_QK_DIM), _rows(D_INNER), _rows(HEAD), _rows(HEAD)],
                 [act(GDN_QK_DIM), act(GDN_QK_DIM), act(D_INNER), act(HEAD), act(HEAD)],
                 semantics=("parallel",))(c, ba, a_log, dt_bias)


def _gdn_ew_bwd(c, ba, a_log, dt_bias, dqh, dkh, dv, dbeta, dgc, name):
    t = c.shape[0]
    tb = EW_TB

    def body(c_ref, ba_ref, al_ref, db_ref, dq_ref, dk_ref, dv_ref, dbeta_ref, dgc_ref, dc_ref, dba_ref, acc_ref):
        for h in range(GDN_QK_HEADS):
            for base, ref, scale in ((0, dq_ref, HEAD ** -0.5), (GDN_QK_DIM, dk_ref, 1.0)):
                cq = c_ref[:, base + h * HEAD: base + (h + 1) * HEAD]
                s = _silu(cq)
                r = lax.rsqrt(jnp.sum(s * s, axis=1, keepdims=True) + L2_EPS)
                dn = (ref[:, 2 * h * HEAD:(2 * h + 1) * HEAD] + ref[:, (2 * h + 1) * HEAD:(2 * h + 2) * HEAD]) * scale
                ds = r * dn - s * (r * r * r) * jnp.sum(dn * s, axis=1, keepdims=True)
                dc_ref[:, base + h * HEAD: base + (h + 1) * HEAD] = ds * _dsilu(cq)
        dc_ref[:, 2 * GDN_QK_DIM:] = dv_ref[...] * _dsilu(c_ref[:, 2 * GDN_QK_DIM:])
        beta = _sig(ba_ref[:, :HEAD])
        dba_ref[:, :HEAD] = (dbeta_ref[...] * beta * (1.0 - beta)).astype(dba_ref.dtype)
        pre = ba_ref[:, HEAD:] + db_ref[...]
        ea = jnp.exp(al_ref[...])
        g = -ea * _softplus(pre)
        dg = _mmx(_chunk_mask(tb, True), dgc_ref[...], _NN)
        da_raw = dg * (-ea) * _sig(pre)
        dba_ref[:, HEAD:] = da_raw.astype(dba_ref.dtype)
        acc = jnp.zeros((8, HEAD), F32)
        acc = _put_sub(acc, jnp.sum(dg * g, axis=0, keepdims=True), 0)
        acc = _put_sub(acc, jnp.sum(da_raw, axis=0, keepdims=True), 1)

        @pl.when(pl.program_id(0) == 0)
        def _():
            acc_ref[...] = jnp.zeros_like(acc_ref)

        acc_ref[...] += acc

    act = lambda w: jax.ShapeDtypeStruct((t, w), F32)
    return _call(body, name, (t // tb,),
                 [_rows(GDN_CONV_DIM), _rows(2 * HEAD), _const(1, HEAD), _const(1, HEAD),
                  _rows(D_INNER), _rows(D_INNER), _rows(D_INNER), _rows(HEAD), _rows(HEAD)],
                 [_rows(GDN_CONV_DIM), _rows(2 * HEAD), _const(8, HEAD)],
                 [act(GDN_CONV_DIM), jax.ShapeDtypeStruct((t, 2 * HEAD), MM_DTYPE), jax.ShapeDtypeStruct((8, HEAD), F32)],
                 semantics=("arbitrary",))(c, ba, a_log, dt_bias, dqh, dkh, dv, dbeta, dgc)


def _zip(fn, *lists):
    return [fn(*xs) for xs in zip(*lists)]


def _mms(xs, ys, dims):
    return [_mm(x, y, dims) for x, y in zip(xs, ys)]


def _gdn_heads_fwd(q, k, v, bcol, gcol, grow, glast, s_in):
    ii, jj = _iota((CHUNK, CHUNK), 0), _iota((CHUNK, CHUNK), 1)
    eye = _eye(CHUNK)
    mul = lambda x, y: x * y
    eg = [jnp.exp(g) for g in gcol]
    decay = _zip(lambda gc, gr: jnp.exp(jnp.where(ii >= jj, gc - gr, NEG_BIG)), gcol, grow)
    kb = _zip(mul, k, bcol)
    p = _mms(kb, k, _NT)
    a = _zip(lambda x, d: jnp.where(ii > jj, x * d, 0.0), p, decay)
    inv, pw = [eye - x for x in a], a
    for _ in range(5):
        pw = _mms(pw, pw, _NN)
        inv = _zip(lambda x, y: x + y, inv, _mms(inv, pw, _NN))
    rv, rk = _zip(mul, v, bcol), _zip(mul, kb, eg)
    u, w = _mms(inv, rv, _NN), _mms(inv, rk, _NN)
    vn = _zip(lambda x, y: x - y, u, _mms(w, s_in, _NN))
    qk = _mms(q, k, _NT)
    att = _zip(mul, qk, decay)
    qd = _zip(mul, q, eg)
    out = _zip(lambda x, y: x + y, _mms(qd, s_in, _NN), _mms(att, vn, _NN))
    ekt = _zip(lambda gl, gc: jnp.exp(gl - gc), glast, gcol)
    kt = _zip(mul, k, ekt)
    el = [jnp.exp(g) for g in glast]
    s_out = _zip(lambda s, e, y: s * e + y, s_in, el, _mms(kt, vn, _TN))
    return dict(eg=eg, decay=decay, kb=kb, p=p, inv=inv, rv=rv, rk=rk, u=u, w=w, vn=vn, qk=qk, att=att, qd=qd,
                out=out, ekt=ekt, kt=kt, el=el, s_out=s_out)


def _head_groups(group, init):
    if GDN_GROUP == GDN_V_HEADS:
        return group(0, init)
    return lax.fori_loop(0, GDN_V_HEADS // GDN_GROUP, lambda gi, c: group(GDN_GROUP * gi, c), init)


def _half(h):
    return h // 2 if isinstance(h, int) else jnp.right_shift(h, 1)


def _gdn_chunk_fwd(qn, kn, v, beta, gc, z, norm_w, name):
    t = qn.shape[0]
    nc = t // CHUNK

    def body(q_ref, k_ref, v_ref, beta_ref, gc_ref, z_ref, nw_ref, o_ref, h_ref, st_ref, state):
        @pl.when(pl.program_id(0) == 0)
        def _():
            state[...] = jnp.zeros_like(state)

        st_ref[0] = state[...]
        gc_all, beta_all = gc_ref[...], beta_ref[...]
        gct = _mmx(_eye(HEAD), gc_all, _NT)
        glast_all = gc_ref[CHUNK - 1:CHUNK, :]
        nw = nw_ref[...]

        def group(h0, carry):
            heads = [h0 + s for s in range(GDN_GROUP)]
            f = _gdn_heads_fwd([q_ref[:, _lanes(_half(h))] for h in heads], [k_ref[:, _lanes(_half(h))] for h in heads],
                               [v_ref[:, _lanes(h)] for h in heads], [_col(beta_all, h) for h in heads],
                               [_col(gc_all, h) for h in heads], [_row(gct, h) for h in heads],
                               [_col(glast_all, h) for h in heads], [state[h] for h in heads])
            for h, s_out, o in zip(heads, f["s_out"], f["out"]):
                state[h] = s_out
                o_ref[:, _lanes(h)] = o
                rstd = lax.rsqrt(jnp.mean(o * o, axis=1, keepdims=True) + RMS_EPS)
                h_ref[:, _lanes(h)] = (o * rstd * nw * _silu(z_ref[:, _lanes(h)])).astype(h_ref.dtype)
            return carry

        _head_groups(group, 0)

    rows = lambda w: pl.BlockSpec((CHUNK, w), lambda i: (i, 0))
    act = lambda w: jax.ShapeDtypeStruct((t, w), F32)
    return _call(body, name, (nc,),
                 [rows(GDN_QK_DIM), rows(GDN_QK_DIM), rows(D_INNER), rows(HEAD), rows(HEAD), rows(D_INNER),
                  _const(1, HEAD)],
                 [rows(D_INNER), rows(D_INNER), pl.BlockSpec((1, GDN_V_HEADS, HEAD, HEAD), lambda i: (i, 0, 0, 0))],
                 [act(D_INNER), jax.ShapeDtypeStruct((t, D_INNER), MM_DTYPE),
                  jax.ShapeDtypeStruct((nc, GDN_V_HEADS, HEAD, HEAD), F32)],
                 [pltpu.VMEM((GDN_V_HEADS, HEAD, HEAD), F32)], ("arbitrary",))(qn, kn, v, beta, gc, z, norm_w)


def _gdn_chunk_bwd(qn, kn, v, beta, gc, z, norm_w, o, states, dh, name):
    t = qn.shape[0]
    nc = t // CHUNK

    def body(q_ref, k_ref, v_ref, beta_ref, gc_ref, z_ref, nw_ref, o_ref, st_ref, dh_ref,
             dq_ref, dk_ref, dv_ref, dz_ref, dbeta_ref, dgc_ref, acc_ref, dstate):
        @pl.when(pl.program_id(0) == 0)
        def _():
            dstate[...] = jnp.zeros_like(dstate)
            acc_ref[...] = jnp.zeros_like(acc_ref)

        gc_all, beta_all = gc_ref[...], beta_ref[...]
        gct = _mmx(_eye(HEAD), gc_all, _NT)
        glast_all = gc_ref[CHUNK - 1:CHUNK, :]
        nw = nw_ref[...]
        ii, jj = _iota((CHUNK, CHUNK), 0), _iota((CHUNK, CHUNK), 1)
        last_row = _iota((CHUNK, 1), 0) == CHUNK - 1

        def group(h0, carry):
            dbeta_acc, dgc_acc, dgrow_acc, dnw_acc = carry
            heads = [h0 + s for s in range(GDN_GROUP)]
            mul, add, sub = (lambda x, y: x * y), (lambda x, y: x + y), (lambda x, y: x - y)
            rowsum = lambda x, y: jnp.sum(x * y, axis=1, keepdims=True)
            q, k = [q_ref[:, _lanes(_half(h))] for h in heads], [k_ref[:, _lanes(_half(h))] for h in heads]
            vv = [v_ref[:, _lanes(h)] for h in heads]
            bcol, gcol = [_col(beta_all, h) for h in heads], [_col(gc_all, h) for h in heads]
            s_in, dsn = [st_ref[0, h] for h in heads], [dstate[h] for h in heads]
            f = _gdn_heads_fwd(q, k, vv, bcol, gcol, [_row(gct, h) for h in heads],
                               [_col(glast_all, h) for h in heads], s_in)
            do = []
            for h in heads:
                oo, zz, dhh = o_ref[:, _lanes(h)], z_ref[:, _lanes(h)], dh_ref[:, _lanes(h)]
                rstd = lax.rsqrt(jnp.mean(oo * oo, axis=1, keepdims=True) + RMS_EPS)
                on, sz = oo * rstd, _silu(zz)
                dnw_acc = dnw_acc + jnp.sum(dhh * on * sz, axis=0, keepdims=True)
                dz_ref[:, _lanes(h)] = (dhh * on * nw * _dsilu(zz)).astype(dz_ref.dtype)
                don = dhh * nw * sz
                do.append(rstd * (don - on * jnp.mean(don * on, axis=1, keepdims=True)))
            decay, eg, inv = f["decay"], f["eg"], f["inv"]
            d_glast = _zip(lambda d, s, e: _total(d * s) * e, dsn, s_in, f["el"])
            dkt = _mms(f["vn"], dsn, _NT)
            dvn = _mms(f["kt"], dsn, _NN)
            dqd = _mms(do, s_in, _NT)
            ds_prev = _zip(lambda d, e, y: d * e + y, dsn, f["el"], _mms(f["qd"], do, _TN))
            datt = _mms(do, f["vn"], _NT)
            dvn = _zip(add, dvn, _mms(f["att"], do, _TN))
            dqk = _zip(mul, datt, decay)
            dq = _zip(lambda x, e, y: x * e + y, dqd, eg, _mms(dqk, k, _NN))
            dk = _mms(dqk, q, _TN)
            ddecay = _zip(mul, datt, f["qk"])
            dgcol = _zip(rowsum, dqd, f["qd"])
            dw = [-x for x in _mms(dvn, s_in, _NT)]
            ds_prev = _zip(sub, ds_prev, _mms(f["w"], dvn, _TN))
            drv, drk = _mms(inv, dvn, _TN), _mms(inv, dw, _TN)
            da = _zip(lambda x, y: jnp.where(ii > jj, -(x + y), 0.0), _mms(drv, f["u"], _NT), _mms(drk, f["w"], _NT))
            dp = _zip(mul, da, decay)
            ddecay = _zip(lambda x, y, z_: x + y * z_, ddecay, da, f["p"])
            dkb = _zip(lambda x, y, e: x + y * e, _mms(dp, k, _NN), drk, eg)
            dk = _zip(add, dk, _mms(dp, f["kb"], _TN))
            dbeta = _zip(add, _zip(rowsum, drv, vv), _zip(rowsum, dkb, k))
            dgcol = _zip(add, dgcol, _zip(rowsum, drk, f["rk"]))
            dk = _zip(lambda x, y, b_, z_, e: x + y * b_ + z_ * e, dk, dkb, bcol, dkt, f["ekt"])
            tail = _zip(mul, dkt, f["kt"])
            d_glast = _zip(lambda x, y: x + _total(y), d_glast, tail)
            e_ = _zip(mul, ddecay, decay)
            dgcol = _zip(lambda x, t_, e, gl: x - jnp.sum(t_, axis=1, keepdims=True) + jnp.sum(e, axis=1, keepdims=True)
                         + jnp.where(last_row, gl, 0.0), dgcol, tail, e_, d_glast)
            for i_, h in enumerate(heads):
                dstate[h] = ds_prev[i_]
                dq_ref[:, _lanes(h)] = dq[i_]
                dk_ref[:, _lanes(h)] = dk[i_]
                dv_ref[:, _lanes(h)] = drv[i_] * bcol[i_]
                dbeta_acc = _put_col(dbeta_acc, dbeta[i_], h)
                dgc_acc = _put_col(dgc_acc, dgcol[i_], h)
                dgrow_acc = _put_row(dgrow_acc, -jnp.sum(e_[i_], axis=0, keepdims=True), h)
            return dbeta_acc, dgc_acc, dgrow_acc, dnw_acc

        zero = jnp.zeros((CHUNK, HEAD), F32)
        dbeta_acc, dgc_acc, dgrow_acc, dnw_acc = _head_groups(
            group, (zero, zero, jnp.zeros((HEAD, CHUNK), F32), jnp.zeros((1, HEAD), F32)))
        dbeta_ref[...] = dbeta_acc
        dgc_ref[...] = dgc_acc + _mmx(_eye(CHUNK), dgrow_acc, _NT)
        acc_ref[...] += _put_sub(jnp.zeros((8, HEAD), F32), dnw_acc, 0)

    rows = lambda w: pl.BlockSpec((CHUNK, w), lambda i: (nc - 1 - i, 0))
    act = lambda w: jax.ShapeDtypeStruct((t, w), F32)
    return _call(body, name, (nc,),
                 [rows(GDN_QK_DIM), rows(GDN_QK_DIM), rows(D_INNER), rows(HEAD), rows(HEAD), rows(D_INNER),
                  _const(1, HEAD), rows(D_INNER),
                  pl.BlockSpec((1, GDN_V_HEADS, HEAD, HEAD), lambda i: (nc - 1 - i, 0, 0, 0)), rows(D_INNER)],
                 [rows(D_INNER), rows(D_INNER), rows(D_INNER), rows(D_INNER), rows(HEAD), rows(HEAD), _const(8, HEAD)],
                 [act(D_INNER), act(D_INNER), act(D_INNER), jax.ShapeDtypeStruct((t, D_INNER), MM_DTYPE), act(HEAD),
                  act(HEAD), jax.ShapeDtypeStruct((8, HEAD), F32)],
                 [pltpu.VMEM((GDN_V_HEADS, HEAD, HEAD), F32)], ("arbitrary",)
                 )(qn, kn, v, beta, gc, z, norm_w, o, states, dh)


def _sc_gate_fwd(bg, cv, z, name):
    t, w = bg.shape

    def body(b_ref, c_ref, z_ref, o_ref):
        o_ref[...] = (b_ref[...] * c_ref[...] * _silu(z_ref[...])).astype(o_ref.dtype)

    return _call(body, name, (t // EW_TB,), [_rows(w)] * 3, _rows(w), jax.ShapeDtypeStruct((t, w), MM_DTYPE),
                 semantics=("parallel",))(bg, cv, z)


def _sc_gate_bwd(dh, bg, cv, z, name):
    t, w = bg.shape

    def body(d_ref, b_ref, c_ref, z_ref, db_ref, dc_ref, dz_ref):
        d, b, c, zz = d_ref[...], b_ref[...], c_ref[...], z_ref[...]
        sz = _silu(zz)
        db_ref[...] = (d * c * sz).astype(db_ref.dtype)
        dc_ref[...] = d * b * sz
        dz_ref[...] = (d * b * c * _dsilu(zz)).astype(dz_ref.dtype)

    act, act_mm = jax.ShapeDtypeStruct((t, w), F32), jax.ShapeDtypeStruct((t, w), MM_DTYPE)
    return _call(body, name, (t // EW_TB,), [_rows(w)] * 4, [_rows(w)] * 3, [act_mm, act, act_mm],
                 semantics=("parallel",))(dh, bg, cv, z)


XBC_B = D_INNER
XBC_C = D_INNER + SSD_GROUPS * SSD_STATE


def _ssd_scalars(dtp, dt_bias, a_log):
    dt = _softplus(dtp + dt_bias)
    a = -jnp.exp(a_log)
    da = dt * a
    ac = _mmx(_chunk_mask(CHUNK, False), da, _NN)
    act = _mmx(_eye(HEAD), ac, _NT)
    aclast = jnp.sum(jnp.where(_iota(ac.shape, 0) == CHUNK - 1, ac, 0.0), axis=0, keepdims=True)
    return dt, a, da, ac, act, aclast


def _ssd_pairs_fwd(x2, bg, cg, cb, dt, ac, act, aclast, s2):
    ii, jj = _iota((CHUNK, CHUNK), 0), _iota((CHUNK, CHUNK), 1)
    half = _iota((CHUNK, HEAD), 1) < 64
    causal = ii >= jj
    pairs = range(len(x2))
    mul = lambda x, y: x * y
    pick = lambda a, b: jnp.where(half, a, b)
    aca, acb = [_col(ac, 2 * p) for p in pairs], [_col(ac, 2 * p + 1) for p in pairs]
    la, lb = [_col(aclast, 2 * p) for p in pairs], [_col(aclast, 2 * p + 1) for p in pairs]
    dt2 = [pick(_col(dt, 2 * p), _col(dt, 2 * p + 1)) for p in pairs]
    xdt = _zip(mul, x2, dt2)
    sega = [jnp.exp(jnp.where(causal, aca[p] - _row(act, 2 * p), NEG_BIG)) for p in pairs]
    segb = [jnp.exp(jnp.where(causal, acb[p] - _row(act, 2 * p + 1), NEG_BIG)) for p in pairs]
    ma, mb = _zip(mul, sega, cb), _zip(mul, segb, cb)
    ydiag = _zip(pick, _mms(ma, xdt, _NN), _mms(mb, xdt, _NN))
    cdec = _zip(lambda a, b: pick(jnp.exp(a), jnp.exp(b)), aca, acb)
    cs = _mms(cg, s2, _NT)
    tail = _zip(lambda l1, a, l2, b: pick(jnp.exp(l1 - a), jnp.exp(l2 - b)), la, aca, lb, acb)
    zt = _zip(mul, xdt, tail)
    ea, eb = [jnp.exp(x) for x in la], [jnp.exp(x) for x in lb]
    tot = _zip(lambda a, b: jnp.where(_iota((HEAD, 1), 0) < 64, a, b), ea, eb)
    s_out = _zip(lambda s, t_, y: s * t_ + y, s2, tot, _mms(zt, bg, _TN))
    return dict(half=half, dt2=dt2, xdt=xdt, sega=sega, segb=segb, ma=ma, mb=mb, ydiag=ydiag, cdec=cdec, cs=cs,
                tail=tail, zt=zt, ea=ea, eb=eb, tot=tot, s_out=s_out)


def _ssd_group_inputs(cx_ref):
    cxb = [cx_ref[:, XBC_B + g * SSD_STATE: XBC_B + (g + 1) * SSD_STATE] for g in range(SSD_GROUPS)]
    cxc = [cx_ref[:, XBC_C + g * SSD_STATE: XBC_C + (g + 1) * SSD_STATE] for g in range(SSD_GROUPS)]
    bg, cg = [_silu(x) for x in cxb], [_silu(x) for x in cxc]
    return cxb, cxc, bg, cg, _mms(cg, bg, _NT)


def _per_pair(group_list):
    return [group_list[p // (SSD_PAIRS // SSD_GROUPS)] for p in range(SSD_PAIRS)]


def _ssd_chunk_fwd(cx, dtp, z, dt_bias, a_log, dskip, norm_w, name):
    t = cx.shape[0]
    nc = t // CHUNK
    gw = D_INNER // SSD_GROUPS

    def body(cx_ref, dtp_ref, z_ref, db_ref, al_ref, sk_ref, nw_ref, y_ref, h_ref, st_ref, state):
        @pl.when(pl.program_id(0) == 0)
        def _():
            state[...] = jnp.zeros_like(state)

        st_ref[0] = state[...]
        dt, _, _, ac, act, aclast = _ssd_scalars(dtp_ref[...], db_ref[...], al_ref[...])
        _, _, bg, cg, cb = _ssd_group_inputs(cx_ref)
        x2 = [_silu(cx_ref[:, _lanes(p)]) for p in range(SSD_PAIRS)]
        f = _ssd_pairs_fwd(x2, _per_pair(bg), _per_pair(cg), _per_pair(cb), dt, ac, act, aclast,
                           [state[p] for p in range(SSD_PAIRS)])
        for p in range(SSD_PAIRS):
            state[p] = f["s_out"][p]
            y_ref[:, _lanes(p)] = f["ydiag"][p] + f["cs"][p] * f["cdec"][p] + sk_ref[:, _lanes(p)] * x2[p]
        for g in range(SSD_GROUPS):
            sl = slice(g * gw, (g + 1) * gw)
            yg = y_ref[:, sl] * _silu(z_ref[:, sl])
            rstd = lax.rsqrt(jnp.mean(yg * yg, axis=1, keepdims=True) + RMS_EPS)
            h_ref[:, sl] = (yg * rstd * nw_ref[:, sl]).astype(h_ref.dtype)

    rows = lambda w: pl.BlockSpec((CHUNK, w), lambda i: (i, 0))
    act_ = lambda w: jax.ShapeDtypeStruct((t, w), F32)
    return _call(body, name, (nc,),
                 [rows(SSD_CONV_DIM), rows(HEAD), rows(D_INNER), _const(1, HEAD), _const(1, HEAD),
                  _const(1, D_INNER), _const(1, D_INNER)],
                 [rows(D_INNER), rows(D_INNER), pl.BlockSpec((1, SSD_PAIRS, HEAD, SSD_STATE), lambda i: (i, 0, 0, 0))],
                 [act_(D_INNER), jax.ShapeDtypeStruct((t, D_INNER), MM_DTYPE),
                  jax.ShapeDtypeStruct((nc, SSD_PAIRS, HEAD, SSD_STATE), F32)],
                 [pltpu.VMEM((SSD_PAIRS, HEAD, SSD_STATE), F32)], ("arbitrary",)
                 )(cx, dtp, z, dt_bias, a_log, dskip, norm_w)


def _ssd_chunk_bwd(cx, dtp, z, dt_bias, a_log, dskip, norm_w, y, states, dh, name):
    t = cx.shape[0]
    nc = t // CHUNK
    gw = D_INNER // SSD_GROUPS

    def body(cx_ref, dtp_ref, z_ref, db_ref, al_ref, sk_ref, nw_ref, y_ref, st_ref, dh_ref,
             dcx_ref, ddtp_ref, dz_ref, wide_ref, acc_ref, dstate, dy_s):
        @pl.when(pl.program_id(0) == 0)
        def _():
            dstate[...] = jnp.zeros_like(dstate)
            wide_ref[...] = jnp.zeros_like(wide_ref)
            acc_ref[...] = jnp.zeros_like(acc_ref)

        dtp = dtp_ref[...]
        dt, a, da, ac, act, aclast = _ssd_scalars(dtp, db_ref[...], al_ref[...])
        ii, jj = _iota((CHUNK, CHUNK), 0), _iota((CHUNK, CHUNK), 1)
        last_row = _iota((CHUNK, 1), 0) == CHUNK - 1
        top = _iota((HEAD, SSD_STATE), 0) < 64
        for g in range(SSD_GROUPS):
            sl = slice(g * gw, (g + 1) * gw)
            yy, zz, dhh, nw = y_ref[:, sl], z_ref[:, sl], dh_ref[:, sl], nw_ref[:, sl]
            sz = _silu(zz)
            yg = yy * sz
            rstd = lax.rsqrt(jnp.mean(yg * yg, axis=1, keepdims=True) + RMS_EPS)
            n = yg * rstd
            dn = dhh * nw
            dyg = rstd * (dn - n * jnp.mean(dn * n, axis=1, keepdims=True))
            dy_s[:, sl] = dyg * sz
            dz_ref[:, sl] = (dyg * yy * _dsilu(zz)).astype(dz_ref.dtype)
            wide_ref[0:1, sl] += jnp.sum(dhh * n, axis=0, keepdims=True)

        pairs = range(SSD_PAIRS)
        mul, add, sub = (lambda x, y: x * y), (lambda x, y: x + y), (lambda x, y: x - y)
        rowsum = lambda x: jnp.sum(x, axis=1, keepdims=True)
        cxb, cxc, bg, cg, cb = _ssd_group_inputs(cx_ref)
        bgp, cgp = _per_pair(bg), _per_pair(cg)
        cxx = [cx_ref[:, _lanes(p)] for p in pairs]
        x2 = [_silu(x) for x in cxx]
        s2, dsn = [st_ref[0, p] for p in pairs], [dstate[p] for p in pairs]
        dy2 = [dy_s[:, _lanes(p)] for p in pairs]
        f = _ssd_pairs_fwd(x2, bgp, cgp, _per_pair(cb), dt, ac, act, aclast, s2)
        half = f["half"]
        lo = lambda x: jnp.where(half, x, 0.0)
        dx2 = [dy2[p] * sk_ref[:, _lanes(p)] for p in pairs]
        for p in pairs:
            wide_ref[1:2, _lanes(p)] += jnp.sum(dy2[p] * x2[p], axis=0, keepdims=True)
        gg = _zip(mul, dy2, f["cdec"])
        dc_p = _mms(gg, s2, _NN)
        ds_prev = _zip(lambda d, t_, y: d * t_ + y, dsn, f["tot"], _mms(gg, cgp, _TN))
        t1 = _zip(lambda d, c, e: d * c * e, dy2, f["cs"], f["cdec"])
        dac_a = [rowsum(lo(x)) for x in t1]
        dac_b = _zip(lambda x, a_: rowsum(x) - a_, t1, dac_a)
        dya = [lo(x) for x in dy2]
        dma, dmb = _mms(dya, f["xdt"], _NT), _mms(_zip(sub, dy2, dya), f["xdt"], _NT)
        dxdt = _zip(lambda a_, b_: jnp.where(half, a_, b_), _mms(f["ma"], dy2, _TN), _mms(f["mb"], dy2, _TN))
        dcb_p = _zip(lambda a_, sa, b_, sb: a_ * sa + b_ * sb, dma, f["sega"], dmb, f["segb"])
        ea_, eb_ = _zip(mul, dma, f["ma"]), _zip(mul, dmb, f["mb"])
        dac_a = _zip(lambda x, e: x + rowsum(e), dac_a, ea_)
        dac_b = _zip(lambda x, e: x + rowsum(e), dac_b, eb_)
        dzt = _mms(bgp, dsn, _NT)
        db_p = _mms(f["zt"], dsn, _NN)
        dxdt = _zip(lambda x, d, t_: x + d * t_, dxdt, dzt, f["tail"])
        t2 = _zip(mul, dzt, f["zt"])
        t2a = [rowsum(lo(x)) for x in t2]
        t2b = _zip(lambda x, a_: rowsum(x) - a_, t2, t2a)
        t3 = _zip(mul, dsn, s2)
        t3a = [_total(jnp.where(top, x, 0.0)) for x in t3]
        dla = _zip(lambda x, y, e: _total(x) + y * e, t2a, t3a, f["ea"])
        dlb = _zip(lambda x, y, ya, e: _total(x) + (_total(y) - ya) * e, t2b, t3, t3a, f["eb"])
        dac_a = _zip(lambda x, y, l: x - y + jnp.where(last_row, l, 0.0), dac_a, t2a, dla)
        dac_b = _zip(lambda x, y, l: x - y + jnp.where(last_row, l, 0.0), dac_b, t2b, dlb)
        dx2 = _zip(lambda x, d, t_: x + d * t_, dx2, dxdt, f["dt2"])
        t4 = _zip(mul, dxdt, x2)
        t4a = [rowsum(lo(x)) for x in t4]
        t4b = _zip(lambda x, a_: rowsum(x) - a_, t4, t4a)
        zero = jnp.zeros((CHUNK, HEAD), F32)
        ddt_acc, dac_acc, drow_acc = zero, zero, jnp.zeros((HEAD, CHUNK), F32)
        for p in pairs:
            dcx_ref[:, _lanes(p)] = dx2[p] * _dsilu(cxx[p])
            dstate[p] = ds_prev[p]
            ddt_acc = _put_col(_put_col(ddt_acc, t4a[p], 2 * p), t4b[p], 2 * p + 1)
            dac_acc = _put_col(_put_col(dac_acc, dac_a[p], 2 * p), dac_b[p], 2 * p + 1)
            drow_acc = _put_row(_put_row(drow_acc, -jnp.sum(ea_[p], axis=0, keepdims=True), 2 * p),
                                -jnp.sum(eb_[p], axis=0, keepdims=True), 2 * p + 1)
        per = SSD_PAIRS // SSD_GROUPS
        gsum = lambda xs: [functools.reduce(add, xs[g * per:(g + 1) * per]) for g in range(SSD_GROUPS)]
        dcb = gsum(dcb_p)
        dc = _zip(add, gsum(dc_p), _mms(dcb, bg, _NN))
        db = _zip(add, gsum(db_p), _mms(dcb, cg, _TN))
        for g in range(SSD_GROUPS):
            dcx_ref[:, XBC_B + g * SSD_STATE: XBC_B + (g + 1) * SSD_STATE] = db[g] * _dsilu(cxb[g])
            dcx_ref[:, XBC_C + g * SSD_STATE: XBC_C + (g + 1) * SSD_STATE] = dc[g] * _dsilu(cxc[g])
        dac = dac_acc + _mmx(_eye(CHUNK), drow_acc, _NT)
        dda = _mmx(_chunk_mask(CHUNK, True), dac, _NN)
        ddt = ddt_acc + dda * a
        ddtp = ddt * _sig(dtp + db_ref[...])
        ddtp_ref[...] = ddtp.astype(ddtp_ref.dtype)
        acc = _put_sub(jnp.zeros((8, HEAD), F32), jnp.sum(dda * da, axis=0, keepdims=True), 0)
        acc_ref[...] += _put_sub(acc, jnp.sum(ddtp, axis=0, keepdims=True), 1)

    rows = lambda w: pl.BlockSpec((CHUNK, w), lambda i: (nc - 1 - i, 0))
    act_ = lambda w: jax.ShapeDtypeStruct((t, w), F32)
    return _call(body, name, (nc,),
                 [rows(SSD_CONV_DIM), rows(HEAD), rows(D_INNER), _const(1, HEAD), _const(1, HEAD),
                  _const(1, D_INNER), _const(1, D_INNER), rows(D_INNER),
                  pl.BlockSpec((1, SSD_PAIRS, HEAD, SSD_STATE), lambda i: (nc - 1 - i, 0, 0, 0)), rows(D_INNER)],
                 [rows(SSD_CONV_DIM), rows(HEAD), rows(D_INNER), _const(8, D_INNER), _const(8, HEAD)],
                 [act_(SSD_CONV_DIM), jax.ShapeDtypeStruct((t, HEAD), MM_DTYPE), jax.ShapeDtypeStruct((t, D_INNER), MM_DTYPE),
                  jax.ShapeDtypeStruct((8, D_INNER), F32),
                  jax.ShapeDtypeStruct((8, HEAD), F32)],
                 [pltpu.VMEM((SSD_PAIRS, HEAD, SSD_STATE), F32), pltpu.VMEM((CHUNK, D_INNER), F32)], ("arbitrary",)
                 )(cx, dtp, z, dt_bias, a_log, dskip, norm_w, y, states, dh)


LN_TB = 512


def _ln_stats(x, y):
    u = ALPHA * x + y
    mu = jnp.mean(u, axis=1, keepdims=True)
    cen = u - mu
    rstd = lax.rsqrt(jnp.mean(cen * cen, axis=1, keepdims=True) + LN_EPS)
    return cen * rstd


def _ln_fwd(x, y, g, b, name):
    t, d = x.shape

    def body(x_ref, y_ref, g_ref, b_ref, o_ref, omm_ref):
        out = _ln_stats(x_ref[...], y_ref[...]) * g_ref[...] + b_ref[...]
        o_ref[...] = out
        omm_ref[...] = out.astype(omm_ref.dtype)

    return _call(body, name, (t // LN_TB,), [_rows(d, LN_TB), _rows(d, LN_TB), _const(1, d), _const(1, d)],
                 [_rows(d, LN_TB)] * 2, [jax.ShapeDtypeStruct((t, d), F32), jax.ShapeDtypeStruct((t, d), MM_DTYPE)],
                 semantics=("parallel",))(x, y, g, b)


def _ln_bwd(dout, x, y, g, name):
    t, d = x.shape

    def body(d_ref, x_ref, y_ref, g_ref, du_ref, dumm_ref, acc_ref):
        u = ALPHA * x_ref[...] + y_ref[...]
        mu = jnp.mean(u, axis=1, keepdims=True)
        cen = u - mu
        rstd = lax.rsqrt(jnp.mean(cen * cen, axis=1, keepdims=True) + LN_EPS)
        xh = cen * rstd
        do = d_ref[...]
        dxh = do * g_ref[...]
        du = rstd * (dxh - jnp.mean(dxh, axis=1, keepdims=True) - xh * jnp.mean(dxh * xh, axis=1, keepdims=True))
        du_ref[...] = du
        dumm_ref[...] = du.astype(dumm_ref.dtype)
        acc = _put_sub(jnp.zeros((8, d), F32), jnp.sum(do * xh, axis=0, keepdims=True), 0)
        acc = _put_sub(acc, jnp.sum(do, axis=0, keepdims=True), 1)

        @pl.when(pl.program_id(0) == 0)
        def _():
            acc_ref[...] = jnp.zeros_like(acc_ref)

        acc_ref[...] += acc

    return _call(body, name, (t // LN_TB,), [_rows(d, LN_TB)] * 3 + [_const(1, d)],
                 [_rows(d, LN_TB), _rows(d, LN_TB), _const(8, d)],
                 [jax.ShapeDtypeStruct((t, d), F32), jax.ShapeDtypeStruct((t, d), MM_DTYPE),
                  jax.ShapeDtypeStruct((8, d), F32)],
                 semantics=("arbitrary",))(dout, x, y, g)


def _loss_head(out, target, name):
    t, d = out.shape

    def body(o_ref, t_ref, d_ref, acc_ref):
        err = o_ref[...] - t_ref[...]
        d_ref[...] = err * (1.0 / d)

        @pl.when(pl.program_id(0) == 0)
        def _():
            acc_ref[...] = jnp.zeros_like(acc_ref)

        acc_ref[...] += _put_sub(jnp.zeros((8, d), F32), jnp.sum(err * err, axis=0, keepdims=True), 0)

    return _call(body, name, (t // LN_TB,), [_rows(d, LN_TB)] * 2, [_rows(d, LN_TB), _const(8, d)],
                 [jax.ShapeDtypeStruct((t, d), F32), jax.ShapeDtypeStruct((8, d), F32)],
                 semantics=("arbitrary",))(out, target)


def _adamw(w, gslots, m, v, name):
    r, c = w.shape
    rb = _tile_rows(r)
    c1 = 1.0 - ADAM_B1 ** ADAM_STEP
    c2 = 1.0 - ADAM_B2 ** ADAM_STEP

    def body(w_ref, g_ref, m_ref, v_ref, go_ref, d_ref, mo_ref, vo_ref):
        g = g_ref[0].astype(F32)
        for s in range(1, N_DEV):
            g = g + g_ref[s].astype(F32)
        mn = ADAM_B1 * m_ref[...] + (1.0 - ADAM_B1) * g
        vn = ADAM_B2 * v_ref[...] + (1.0 - ADAM_B2) * (g * g)
        go_ref[...] = g
        mo_ref[...] = mn
        vo_ref[...] = vn
        d_ref[...] = -ADAM_LR * ((mn / c1) / (jnp.sqrt(vn / c2) + ADAM_EPS) + ADAM_WD * w_ref[...])

    blk = pl.BlockSpec((rb, c), lambda i: (i, 0))
    sds = jax.ShapeDtypeStruct((r, c), F32)
    return _call(body, name, (r // rb,), [blk, pl.BlockSpec((N_DEV, rb, c), lambda i: (0, i, 0)), blk, blk],
                 [blk] * 4, [sds] * 4, semantics=("parallel",))(w, gslots, m, v)


def _tile_rows(r):
    for rb in (256, 128, 64, 32, 16, 8):
        if r % rb == 0:
            return rb
    return r


def _pack(arrs, lead=0):
    flats = []
    for a in arrs:
        f = a.reshape(a.shape[:lead] + (-1,)).astype(F32)
        flats.append(jnp.pad(f, [(0, 0)] * lead + [(0, (-f.shape[-1]) % 128)]))
    v = jnp.concatenate(flats, axis=-1)
    v = jnp.pad(v, [(0, 0)] * lead + [(0, (-v.shape[-1]) % 1024)])
    return v.reshape(v.shape[:lead] + (-1, 128))


def _unpack(buf, shapes, lead=0):
    flat = buf.reshape(buf.shape[:lead] + (-1,))
    outs, off = [], 0
    for s in shapes:
        n = math.prod(s)
        outs.append(flat[..., off:off + n].reshape(buf.shape[:lead] + tuple(s)))
        off += n + (-n) % 128
    return outs


def _cols_gathered(g):
    n, l, r, c = g.shape
    return g.transpose(1, 2, 0, 3).reshape(l, r, n * c)


def _cols_to_slabs(full):
    l, r, c = full.shape
    return full.reshape(l, r, N_DEV, c // N_DEV).transpose(2, 0, 1, 3)


def _rows_gathered(g):
    n, l, r, c = g.shape
    return g.transpose(1, 0, 2, 3).reshape(l, n * r, c)


def _rows_to_slabs(full):
    l, r, c = full.shape
    return full.reshape(l, N_DEV, r // N_DEV, c).transpose(1, 0, 2, 3)


def _pad_cols(w, at, width):
    return jnp.pad(w, ((0, 0), (at, width - at - w.shape[1])))


def _pad_lanes(v, width=HEAD):
    return jnp.pad(v.reshape(1, -1), ((0, 0), (0, width - v.size)))


def _taps8(w, bias=None):
    rows = [w] if bias is None else [w, bias.reshape(1, -1)]
    w8 = jnp.concatenate(rows, axis=0)
    return jnp.pad(w8, ((0, 8 - w8.shape[0]), (0, 0)))


class _Carrier:
    def __init__(self):
        self.jobs, self.got = {}, {}

    def put(self, matmul_name, key, src, slabs):
        self.jobs[matmul_name] = (key, src, slabs)

    def matmul(self, a, b, mode, name, **kw):
        job = self.jobs.pop(name, None)
        if job is None:
            return _matmul(a, b, mode, name, **kw)
        key, src, slabs = job
        out, self.got[key] = _matmul(a, b, mode, name, carry=(src, slabs), **kw)
        return out


def _gdn_forward(x, p, tag, mm):
    pq = mm(x, p["w_qkv"], "nn", tag + "_in_qkv")
    z = mm(x, p["w_z"], "nn", tag + "_in_z")
    ba = mm(x, p["w_ba"], "nn", tag + "_in_ba")
    c = _conv_fwd(pq, p["conv8"], 4, tag + "_conv")
    qn, kn, v, beta, gc = _gdn_ew_fwd(c, ba, p["a_log"], p["dt_bias"], tag + "_ew")
    o, h, states = _gdn_chunk_fwd(qn, kn, v, beta, gc, z, p["norm_w"], tag + "_chunk")
    y = mm(h, p["w_out"], "nn", tag + "_out")
    return y, dict(pq=pq, z=z, ba=ba, c=c, qn=qn, kn=kn, v=v, beta=beta, gc=gc, o=o, h=h, states=states)


def _gdn_backward(x, du, du_mm, p, s, tag, mm):
    dh = mm(du_mm, p["w_out"], "nt", tag + "_bwd_dh")
    g_out = mm(s["h"], du_mm, "tn", tag + "_bwd_wout")
    dqh, dkh, dv, dz, dbeta, dgc, nacc = _gdn_chunk_bwd(
        s["qn"], s["kn"], s["v"], s["beta"], s["gc"], s["z"], p["norm_w"], s["o"], s["states"], dh, tag + "_bwd_chunk")
    dc, dba, sacc = _gdn_ew_bwd(s["c"], s["ba"], p["a_log"], p["dt_bias"], dqh, dkh, dv, dbeta, dgc, tag + "_bwd_ew")
    dpq, dconv = _conv_bwd(dc, s["pq"], p["conv8"], 4, tag + "_bwd_conv")
    dx = mm(dpq, p["w_qkv"], "nt", tag + "_bwd_dx_qkv", add=du, add_scale=ALPHA)
    dx = mm(dz, p["w_z"], "nt", tag + "_bwd_dx_z", add=dx)
    dx = mm(dba, p["w_ba"], "nt", tag + "_bwd_dx_ba", add=dx)
    g_qkv = mm(x, dpq, "tn", tag + "_bwd_w_qkv")
    g_z = mm(x, dz, "tn", tag + "_bwd_w_z")
    g_ba = mm(x, dba, "tn", tag + "_bwd_w_ba")
    g_in = jnp.concatenate([g_qkv, g_z, g_ba[:, :GDN_V_HEADS], g_ba[:, HEAD:HEAD + GDN_V_HEADS]], axis=1)
    grads = dict(w_in=g_in, w_out=g_out, conv_w=dconv[:4], a_log=sacc[0, :GDN_V_HEADS], dt_bias=sacc[1, :GDN_V_HEADS],
                 norm_w=nacc[0])
    return dx, grads


def _sc_forward(x, p, tag, mm):
    hh = mm(x, p["w_h"], "nn", tag + "_in_h")
    bg = mm(x, p["w_b"], "nn", tag + "_in_b")
    cg = mm(x, p["w_c"], "nn", tag + "_in_c")
    z = mm(x, p["w_z"], "nn", tag + "_in_z")
    cv = _conv_fwd(cg, p["conv8"], 3, tag + "_conv", u2=hh)
    h = _sc_gate_fwd(bg, cv, z, tag + "_gate")
    y = mm(h, p["w_out"], "nn", tag + "_out")
    return y, dict(hh=hh, bg=bg, cg=cg, z=z, cv=cv, h=h)


def _sc_backward(x, du, du_mm, p, s, tag, mm):
    dh = mm(du_mm, p["w_out"], "nt", tag + "_bwd_dh")
    g_out = mm(s["h"], du_mm, "tn", tag + "_bwd_wout")
    dbg, dcv, dz = _sc_gate_bwd(dh, s["bg"], s["cv"], s["z"], tag + "_bwd_gate")
    dcg, dhh, dconv = _conv_bwd(dcv, s["cg"], p["conv8"], 3, tag + "_bwd_conv", u2=s["hh"])
    dx = mm(dhh, p["w_h"], "nt", tag + "_bwd_dx_h", add=du, add_scale=ALPHA)
    dx = mm(dbg, p["w_b"], "nt", tag + "_bwd_dx_b", add=dx)
    dx = mm(dcg, p["w_c"], "nt", tag + "_bwd_dx_c", add=dx)
    dx = mm(dz, p["w_z"], "nt", tag + "_bwd_dx_z", add=dx)
    g_in = jnp.concatenate([mm(x, d, "tn", tag + "_bwd_w_" + n)
                            for n, d in (("h", dhh), ("b", dbg), ("c", dcg), ("z", dz))], axis=1)
    return dx, dict(w_in=g_in, w_out=g_out, conv_w=dconv[:3])


def _ssd_forward(x, p, tag, mm):
    z = mm(x, p["w_z"], "nn", tag + "_in_z")
    xbc = mm(x, p["w_xbc"], "nn", tag + "_in_xbc")
    dtp = mm(x, p["w_dt"], "nn", tag + "_in_dt")
    cx = _conv_fwd(xbc, p["conv8"], 4, tag + "_conv", bias=True)
    y, h, states = _ssd_chunk_fwd(cx, dtp, z, p["dt_bias"], p["a_log"], p["dskip"], p["norm_w"], tag + "_chunk")
    out = mm(h, p["w_out"], "nn", tag + "_out")
    return out, dict(z=z, xbc=xbc, dtp=dtp, cx=cx, y=y, h=h, states=states)


def _ssd_backward(x, du, du_mm, p, s, tag, mm):
    dh = mm(du_mm, p["w_out"], "nt", tag + "_bwd_dh")
    g_out = mm(s["h"], du_mm, "tn", tag + "_bwd_wout")
    dcx, ddtp, dz, wide, acc = _ssd_chunk_bwd(s["cx"], s["dtp"], s["z"], p["dt_bias"], p["a_log"], p["dskip"],
                                              p["norm_w"], s["y"], s["states"], dh, tag + "_bwd_chunk")
    dxbc, dconv = _conv_bwd(dcx, s["xbc"], p["conv8"], 4, tag + "_bwd_conv")
    dx = mm(dz, p["w_z"], "nt", tag + "_bwd_dx_z", add=du, add_scale=ALPHA)
    dx = mm(dxbc, p["w_xbc"], "nt", tag + "_bwd_dx_xbc", add=dx)
    dx = mm(ddtp, p["w_dt"], "nt", tag + "_bwd_dx_dt", add=dx)
    g_dt = mm(x, ddtp, "tn", tag + "_bwd_w_dt")
    g_in = jnp.concatenate([mm(x, dz, "tn", tag + "_bwd_w_z"), mm(x, dxbc, "tn", tag + "_bwd_w_xbc"),
                            g_dt[:, :32]], axis=1)
    grads = dict(w_in=g_in, w_out=g_out, conv_w=dconv[:4], conv_b=dconv[4], a_log=acc[0, :32], dt_bias=acc[1, :32],
                 d_skip=jnp.sum(wide[1].reshape(32, 64), axis=1), norm_w=wide[0])
    return dx, grads


_WEIGHTS = ['gdn_w_in', 'gdn_conv_w', 'gdn_a_log', 'gdn_dt_bias', 'gdn_norm_w', 'gdn_w_out', 'sc_w_in', 'sc_conv_w',
            'sc_w_out', 'ssd_w_in', 'ssd_conv_w', 'ssd_conv_b', 'ssd_a_log', 'ssd_dt_bias', 'ssd_d_skip',
            'ssd_norm_w', 'ssd_w_out', 'ln_g', 'ln_b']
_BIG = {'gdn_w_in': 'cols', 'gdn_w_out': 'rows', 'sc_w_in': 'cols', 'sc_w_out': 'rows', 'ssd_w_in': 'cols',
        'ssd_w_out': 'rows'}
_SMALL_SHARDED = ['gdn_conv_w', 'sc_conv_w', 'ssd_conv_w', 'ssd_conv_b', 'ssd_norm_w']
_SMALL = [n for n in _WEIGHTS if n not in _BIG]


def kernel(x, gdn_w_in, gdn_conv_w, gdn_a_log, gdn_dt_bias, gdn_norm_w, gdn_w_out, sc_w_in, sc_conv_w, sc_w_out, ssd_w_in, ssd_conv_w, ssd_conv_b, ssd_a_log, ssd_dt_bias, ssd_d_skip, ssd_norm_w, ssd_w_out, ln_g, ln_b, loss_target, m_gdn_w_in, m_gdn_conv_w, m_gdn_a_log, m_gdn_dt_bias, m_gdn_norm_w, m_gdn_w_out, m_sc_w_in, m_sc_conv_w, m_sc_w_out, m_ssd_w_in, m_ssd_conv_w, m_ssd_conv_b, m_ssd_a_log, m_ssd_dt_bias, m_ssd_d_skip, m_ssd_norm_w, m_ssd_w_out, m_ln_g, m_ln_b, v_gdn_w_in, v_gdn_conv_w, v_gdn_a_log, v_gdn_dt_bias, v_gdn_norm_w, v_gdn_w_out, v_sc_w_in, v_sc_conv_w, v_sc_w_out, v_ssd_w_in, v_ssd_conv_w, v_ssd_conv_b, v_ssd_a_log, v_ssd_dt_bias, v_ssd_d_skip, v_ssd_norm_w, v_ssd_w_out, v_ln_g, v_ln_b):
    args = locals()
    wts = {n: args[n] for n in _WEIGHTS}
    mom = {n: args["m_" + n] for n in _WEIGHTS}
    vel = {n: args["v_" + n] for n in _WEIGHTS}
    me = 4 * lax.axis_index("x") + 2 * lax.axis_index("y") + lax.axis_index("c")
    x0, target = x[0], loss_target[0]

    car = _Carrier()
    shard = lambda n, j: wts[n][j:j + 1].astype(MM_DTYPE)
    gathered_w = lambda n, j: (_cols_gathered if _BIG[n] == "cols" else _rows_gathered)(car.got[n, j])[0]

    for n in ('gdn_w_in', 'gdn_w_out'):
        car.got[n, 0] = _exchange(shard(n, 0), "gather_%s0" % n, slabs=False)
    riders = {0: [("l0_gdn_in_qkv", 'sc_w_in', 0), ("l0_gdn_in_z", 'sc_w_out', 0), ("l0_gdn_out", 'ssd_w_out', 0)],
              1: [("l1_sc_in_h", 'ssd_w_in', 0)],
              2: [("l2_ssd_in_xbc", 'gdn_w_in', 1), ("l2_ssd_in_z", 'gdn_w_out', 1)]}
    full = {}
    small_shapes = [wts[n].shape for n in _SMALL_SHARDED]
    gathered = _exchange(_pack([wts[n] for n in _SMALL_SHARDED]), "gather_small", slabs=False)
    for n, g in zip(_SMALL_SHARDED, _unpack(gathered, small_shapes, lead=1)):
        full[n] = jnp.moveaxis(g, 0, -2).reshape(g.shape[1:-1] + (N_DEV * g.shape[-1],))
    for n in _SMALL:
        full.setdefault(n, wts[n])

    def gdn_params(j):
        w = gathered_w('gdn_w_in', j)
        return dict(w_qkv=w[:, :GDN_CONV_DIM], w_z=w[:, GDN_CONV_DIM:GDN_CONV_DIM + D_INNER],
                    w_ba=jnp.concatenate([_pad_cols(w[:, 6144:6160], 0, HEAD), _pad_cols(w[:, 6160:6176], 0, HEAD)], 1),
                    conv8=_taps8(full['gdn_conv_w'][j]), a_log=_pad_lanes(full['gdn_a_log'][j]),
                    dt_bias=_pad_lanes(full['gdn_dt_bias'][j]), norm_w=full['gdn_norm_w'][j].reshape(1, HEAD),
                    w_out=gathered_w('gdn_w_out', j))

    def sc_params():
        w = gathered_w('sc_w_in', 0)
        return dict(w_h=w[:, :2048], w_b=w[:, 2048:4096], w_c=w[:, 4096:6144], w_z=w[:, 6144:],
                    conv8=_taps8(full['sc_conv_w'][0]), w_out=gathered_w('sc_w_out', 0))

    def ssd_params():
        w = gathered_w('ssd_w_in', 0)
        return dict(w_z=w[:, :D_INNER], w_xbc=w[:, D_INNER:D_INNER + SSD_CONV_DIM],
                    w_dt=_pad_cols(w[:, D_INNER + SSD_CONV_DIM:], 0, HEAD),
                    conv8=_taps8(full['ssd_conv_w'][0], full['ssd_conv_b'][0]), a_log=_pad_lanes(full['ssd_a_log'][0]),
                    dt_bias=_pad_lanes(full['ssd_dt_bias'][0]),
                    dskip=jnp.repeat(full['ssd_d_skip'][0], 64).reshape(1, D_INNER),
                    norm_w=full['ssd_norm_w'][0].reshape(1, D_INNER), w_out=gathered_w('ssd_w_out', 0))

    layers = [("gdn", _gdn_forward, _gdn_backward, lambda: gdn_params(0)), ("sc", _sc_forward, _sc_backward, sc_params),
              ("ssd", _ssd_forward, _ssd_backward, ssd_params), ("gdn", _gdn_forward, _gdn_backward, lambda: gdn_params(1))]

    acts, acts_mm, ys, saved, params = [x0], [x0.astype(MM_DTYPE)], [], [], []
    for i, (kind, fwd, _, make_params) in enumerate(layers):
        params.append(make_params())
        for matmul_name, n, j in riders.get(i, ()):
            car.put(matmul_name, (n, j), shard(n, j), False)
        y, s = fwd(acts_mm[-1], params[i], "l%d_%s" % (i, kind), car.matmul)
        out, out_mm = _ln_fwd(acts[-1], y, full['ln_g'][i].reshape(1, -1), full['ln_b'][i].reshape(1, -1), "l%d_ln" % i)
        acts.append(out)
        acts_mm.append(out_mm)
        ys.append(y)
        saved.append(s)
    dact, loss_acc = _loss_head(acts[-1], target, "loss_head")
    loss = lax.psum(0.5 / D_MODEL * jnp.sum(loss_acc[0]), ("x", "y", "c"))

    def slabs_of(i, key):
        g = lg[i][key][None]
        return (_cols_to_slabs(g) if key == 'w_in' else _rows_to_slabs(g)).astype(MM_DTYPE)

    grad_riders = {3: ("l2_ssd_bwd_dx_xbc", "l2_ssd_bwd_dh"), 2: ("l1_sc_bwd_dx_h", "l1_sc_bwd_dh"),
                   1: ("l0_gdn_bwd_dx_qkv", "l0_gdn_bwd_dh")}
    lg = [None] * DEPTH
    d_ln_g, d_ln_b = [None] * DEPTH, [None] * DEPTH
    for i in reversed(range(DEPTH)):
        kind, _, bwd, _ = layers[i]
        du, du_mm, acc = _ln_bwd(dact, acts[i], ys[i], full['ln_g'][i].reshape(1, -1), "l%d_ln_bwd" % i)
        d_ln_g[i], d_ln_b[i] = acc[0], acc[1]
        dact, lg[i] = bwd(acts_mm[i], du, du_mm, params[i], saved[i], "l%d_%s" % (i, kind), car.matmul)
        if i in grad_riders:
            car.put(grad_riders[i][0], ('grad', i, 'w_in'), slabs_of(i, 'w_in'), True)
            car.put(grad_riders[i][1], ('grad', i, 'w_out'), slabs_of(i, 'w_out'), True)
    for key in ('w_in', 'w_out'):
        car.got['grad', 0, key] = _exchange(slabs_of(0, key), "scatter_gdn_%s0" % key, slabs=True)
    grad_x = dact[None]

    stack = lambda k: jnp.stack([lg[0][k], lg[3][k]])
    local = {
        'gdn_conv_w': stack('conv_w'), 'gdn_a_log': stack('a_log'),
        'gdn_dt_bias': stack('dt_bias'), 'gdn_norm_w': stack('norm_w'),
        'sc_conv_w': lg[1]['conv_w'][None],
        'ssd_conv_w': lg[2]['conv_w'][None], 'ssd_conv_b': lg[2]['conv_b'][None],
        'ssd_a_log': lg[2]['a_log'][None], 'ssd_dt_bias': lg[2]['dt_bias'][None], 'ssd_d_skip': lg[2]['d_skip'][None],
        'ssd_norm_w': lg[2]['norm_w'][None],
        'ln_g': jnp.stack(d_ln_g), 'ln_b': jnp.stack(d_ln_b)}

    out = {}
    layers_of = {'gdn': (0, 3), 'sc': (1,), 'ssd': (2,)}
    for n in _BIG:
        kind, key = n.split('_', 1)
        recv = jnp.concatenate([car.got['grad', i, key] for i in layers_of[kind]], axis=1)
        shp = wts[n].shape
        r, c = shp[0] * shp[1], shp[2]
        res = _adamw(wts[n].reshape(r, c), recv.reshape(N_DEV, r, c), mom[n].reshape(r, c), vel[n].reshape(r, c),
                     "adamw_" + n)
        out[n] = [a.reshape(shp) for a in res]
    full_shapes = [local[n].shape for n in _SMALL]
    gathered = _exchange(_pack([local[n] for n in _SMALL]), "gather_small_grads", slabs=False)
    gs = []
    for n, g in zip(_SMALL, _unpack(gathered, full_shapes, lead=1)):
        if n in _SMALL_SHARDED:
            width = wts[n].shape[-1]
            g = lax.dynamic_slice_in_dim(g, me * width, width, axis=g.ndim - 1)
        gs.append(g)
    shapes = [wts[n].shape for n in _SMALL]
    res = _adamw(_pack([wts[n] for n in _SMALL]), _pack(gs, lead=1), _pack([mom[n] for n in _SMALL]),
                 _pack([vel[n] for n in _SMALL]), "adamw_small")
    for k, n in enumerate(_SMALL):
        out[n] = [_unpack(a, shapes)[k] for a in res]

    return (loss, grad_x, *[out[n][0] for n in _WEIGHTS], *[out[n][1] for n in _WEIGHTS],
            *[out[n][2] for n in _WEIGHTS], *[out[n][3] for n in _WEIGHTS])
```

```python
import functools
import math

import jax
import jax.numpy as jnp
from jax import lax
from jax.experimental import pallas as pl
from jax.experimental.pallas import tpu as pltpu

F32 = jnp.float32
MM_DTYPE = jnp.bfloat16

N_DEV = 8
D_MODEL = 1024
D_INNER = 2048
CHUNK = 64
HEAD = 128
GDN_V_HEADS = 16
GDN_GROUP = 16
GDN_QK_HEADS = 8
GDN_QK_DIM = 1024
GDN_CONV_DIM = 4096
SSD_PAIRS = 16
SSD_GROUPS = 4
SSD_STATE = 128
SSD_CONV_DIM = 3072
DEPTH = 4
ALPHA = (2 * DEPTH) ** 0.25
RMS_EPS = 1e-6
LN_EPS = 1e-5
L2_EPS = 1e-6
ADAM_LR, ADAM_B1, ADAM_B2, ADAM_EPS, ADAM_WD, ADAM_STEP = 0.001, 0.9, 0.999, 1e-08, 0.01, 10

VMEM_LIMIT_BYTES = 48 * 1024 * 1024
NEG_BIG = -1e30

_NN = (((1,), (0,)), ((), ()))
_NT = (((1,), (1,)), ((), ()))
_TN = (((0,), (0,)), ((), ()))


def _mm(a, b, dims):
    return lax.dot_general(a.astype(MM_DTYPE), b.astype(MM_DTYPE), dims, preferred_element_type=F32)


def _mmx(a, b, dims):
    return lax.dot_general(a, b, dims, precision=lax.Precision.HIGHEST, preferred_element_type=F32)


def _iota(shape, dim):
    return lax.broadcasted_iota(jnp.int32, shape, dim)


def _eye(n):
    return (_iota((n, n), 0) == _iota((n, n), 1)).astype(F32)


def _sig(x):
    return jax.nn.sigmoid(x)


def _silu(x):
    return x * _sig(x)


def _dsilu(x):
    s = _sig(x)
    return s * (1.0 + x * (1.0 - s))


def _softplus(x):
    return jnp.maximum(x, 0.0) + jnp.log(1.0 + jnp.exp(-jnp.abs(x)))


def _col(x, h):
    return jnp.sum(jnp.where(_iota(x.shape, 1) == h, x, 0.0), axis=1, keepdims=True)


def _row(x, h):
    return jnp.sum(jnp.where(_iota(x.shape, 0) == h, x, 0.0), axis=0, keepdims=True)


def _put_col(acc, col, h):
    return jnp.where(_iota(acc.shape, 1) == h, col, acc)


def _put_row(acc, row, h):
    return jnp.where(_iota(acc.shape, 0) == h, row, acc)


def _put_sub(acc, row, j):
    return acc + jnp.where(_iota(acc.shape, 0) == j, row, 0.0)


def _lanes(h):
    return pl.ds(h * HEAD, HEAD) if isinstance(h, int) else pl.ds(pl.multiple_of(h * HEAD, HEAD), HEAD)


def _total(x):
    return jnp.sum(jnp.sum(x, axis=1, keepdims=True), axis=0, keepdims=True)


def _call(body, name, grid, in_specs, out_specs, out_shape, scratch_shapes=(), semantics=None):
    return pl.pallas_call(
        body, name=name, grid=grid, in_specs=in_specs, out_specs=out_specs, out_shape=out_shape,
        scratch_shapes=list(scratch_shapes),
        compiler_params=pltpu.CompilerParams(dimension_semantics=semantics, vmem_limit_bytes=VMEM_LIMIT_BYTES))


def _tile(n, pref):
    if n <= pref:
        return n
    t = pref
    while n % t:
        t -= 128
    return t


def _exchange_copies(src_ref, out_ref, send_sems, recv_sems, local_sem, slabs):
    x, y, c = lax.axis_index("x"), lax.axis_index("y"), lax.axis_index("c")
    me = 4 * x + 2 * y + c
    mine = src_ref.at[me] if slabs else src_ref
    local = pltpu.make_async_copy(mine, out_ref.at[me], local_sem)
    sends, recvs = [], []
    for r in range(1, N_DEV):
        px = 1 - x if r & 4 else x
        py = 1 - y if r & 2 else y
        pc = 1 - c if r & 1 else c
        peer = 4 * px + 2 * py + pc
        kw = dict(send_sem=send_sems.at[r - 1], recv_sem=recv_sems.at[r - 1], device_id=(px, py, pc),
                  device_id_type=pl.DeviceIdType.MESH)
        sends.append(pltpu.make_async_remote_copy(src_ref=src_ref.at[peer] if slabs else src_ref,
                                                  dst_ref=out_ref.at[me], **kw))
        recvs.append(pltpu.make_async_remote_copy(src_ref=mine, dst_ref=out_ref.at[peer], **kw))
    return local, sends, recvs


def _exchange_start(*refs, slabs):
    local, sends, _ = _exchange_copies(*refs, slabs=slabs)
    local.start()
    for cp in sends:
        cp.start()


def _exchange_wait(*refs, slabs):
    local, sends, recvs = _exchange_copies(*refs, slabs=slabs)
    for cp in recvs:
        cp.wait_recv()
    for cp in sends:
        cp.wait_send()
    local.wait()


def _exchange_sems():
    return [pltpu.SemaphoreType.DMA((N_DEV - 1,)), pltpu.SemaphoreType.DMA((N_DEV - 1,)), pltpu.SemaphoreType.DMA(())]


def _exchange_shape(src, slabs):
    return jax.ShapeDtypeStruct((N_DEV,) + tuple(src.shape[1:] if slabs else src.shape), src.dtype)


def _exchange(src, name, slabs):
    def body(*refs):
        _exchange_start(*refs, slabs=slabs)
        _exchange_wait(*refs, slabs=slabs)

    return pl.pallas_call(
        body, name=name,
        in_specs=[pl.BlockSpec(memory_space=pl.ANY)], out_specs=pl.BlockSpec(memory_space=pl.ANY),
        out_shape=_exchange_shape(src, slabs), scratch_shapes=_exchange_sems(),
    )(src)


MM_TM, MM_TN, MM_TK = 1024, 1024, 1024


def _matmul(a, b, mode, name, add=None, add_scale=1.0, carry=None):
    if mode == "nn":
        (m, k), (_, n) = a.shape, b.shape
    elif mode == "nt":
        (m, k), (n, _) = a.shape, b.shape
    else:
        (k, m), (_, n) = a.shape, b.shape
    tm, tn, tk = _tile(m, 2 * MM_TM if mode == "nn" and add is None else MM_TM), _tile(n, MM_TN), _tile(k, MM_TK)
    nk = k // tk
    grid = (m // tm, n // tn, nk)
    dims = {"nn": _NN, "nt": _NT, "tn": _TN}[mode]
    n_in = 2 + (add is not None)

    def body(*refs):
        a_ref, b_ref, o_ref = refs[0], refs[1], refs[n_in + (carry is not None)]
        if carry is not None:
            ex = (refs[n_in], refs[n_in + 2]) + tuple(refs[n_in + 3:])
            step = (pl.program_id(0) * grid[1] + pl.program_id(1)) * grid[2] + pl.program_id(2)

            @pl.when(step == 0)
            def _():
                _exchange_start(*ex, slabs=carry[1])

        part = _mm(a_ref[...], b_ref[...], dims)
        first = part if add is None else part + add_scale * refs[2][...]
        if nk == 1:
            o_ref[...] = first
        else:
            @pl.when(pl.program_id(2) == 0)
            def _():
                o_ref[...] = first

            @pl.when(pl.program_id(2) > 0)
            def _():
                o_ref[...] += part

        if carry is not None:
            @pl.when(step == grid[0] * grid[1] * grid[2] - 1)
            def _():
                _exchange_wait(*ex, slabs=carry[1])

    if mode == "nn":
        specs = [pl.BlockSpec((tm, tk), lambda i, j, q: (i, q)), pl.BlockSpec((tk, tn), lambda i, j, q: (q, j))]
    elif mode == "nt":
        specs = [pl.BlockSpec((tm, tk), lambda i, j, q: (i, q)), pl.BlockSpec((tn, tk), lambda i, j, q: (j, q))]
    else:
        specs = [pl.BlockSpec((tk, tm), lambda i, j, q: (q, i)), pl.BlockSpec((tk, tn), lambda i, j, q: (q, j))]
    out_spec = pl.BlockSpec((tm, tn), lambda i, j, q: (i, j))
    args = [a, b]
    if add is not None:
        specs.append(out_spec)
        args.append(add)
    out_shape = jax.ShapeDtypeStruct((m, n), F32)
    if carry is None:
        return _call(body, name, grid, specs, out_spec, out_shape, semantics=("parallel", "parallel", "arbitrary"))(*args)
    hbm = pl.BlockSpec(memory_space=pl.ANY)
    return _call(body, name, grid, specs + [hbm], [out_spec, hbm], [out_shape, _exchange_shape(*carry)],
                 _exchange_sems(), ("arbitrary", "arbitrary", "arbitrary"))(*args, carry[0])


CONV_TB = 512
CONV_CB = 1024
HALO = 8


def _conv_specs(t, cb_n):
    tb = min(CONV_TB, t)
    nb = tb // HALO
    blk = pl.BlockSpec((tb, cb_n), lambda c, i: (i, c))
    prev = pl.BlockSpec((HALO, cb_n), lambda c, i: (jnp.maximum(i * nb - 1, 0), c))
    nxt = pl.BlockSpec((HALO, cb_n), lambda c, i: (jnp.minimum((i + 1) * nb, t // HALO - 1), c))
    w = pl.BlockSpec((8, cb_n), lambda c, i: (0, c))
    return blk, prev, nxt, w


def _shift_down(ext, s, tb):
    return (pltpu.roll(ext, s, 0) if s else ext)[HALO:HALO + tb]


def _shift_up(ext, s, tb):
    n = ext.shape[0]
    return (pltpu.roll(ext, n - s, 0) if s else ext)[0:tb]


def _conv_fwd(u, w8, ktaps, name, u2=None, bias=False):
    t, ch = u.shape
    cb_n = min(CONV_CB, ch)
    tb = min(CONV_TB, t)
    two = u2 is not None

    def body(*refs):
        if two:
            u_ref, up_ref, v_ref, vp_ref, w_ref, o_ref = refs
        else:
            u_ref, up_ref, w_ref, o_ref = refs
        first = pl.program_id(1) == 0
        blk, halo = u_ref[...], up_ref[...]
        if two:
            blk, halo = blk * v_ref[...], halo * vp_ref[...]
        ext = jnp.concatenate([jnp.where(first, 0.0, halo), blk], axis=0)
        acc = jnp.zeros((tb, cb_n), F32)
        for j in range(ktaps):
            acc = acc + w_ref[j:j + 1, :] * _shift_down(ext, ktaps - 1 - j, tb)
        if bias:
            acc = acc + w_ref[ktaps:ktaps + 1, :]
        o_ref[...] = acc

    blk, prev, _, wspec = _conv_specs(t, cb_n)
    specs, args = [blk, prev], [u, u]
    if two:
        specs += [blk, prev]
        args += [u2, u2]
    specs.append(wspec)
    args.append(w8)
    return _call(body, name, (ch // cb_n, t // tb), specs, blk, jax.ShapeDtypeStruct((t, ch), F32),
                 semantics=("parallel", "parallel"))(*args)


def _conv_bwd(dc, u, w8, ktaps, name, u2=None):
    t, ch = u.shape
    cb_n = min(CONV_CB, ch)
    tb = min(CONV_TB, t)
    two = u2 is not None

    def body(*refs):
        if two:
            dc_ref, dn_ref, u_ref, up_ref, v_ref, vp_ref, w_ref, du_ref, dv_ref, dw_ref = refs
        else:
            dc_ref, dn_ref, u_ref, up_ref, w_ref, du_ref, dw_ref = refs
        i = pl.program_id(1)
        first, last = i == 0, i == t // tb - 1
        d = dc_ref[...]
        dext = jnp.concatenate([d, jnp.where(last, 0.0, dn_ref[...])], axis=0)
        blk, halo = u_ref[...], up_ref[...]
        if two:
            blk, halo = blk * v_ref[...], halo * vp_ref[...]
        uext = jnp.concatenate([jnp.where(first, 0.0, halo), blk], axis=0)
        du = jnp.zeros((tb, cb_n), F32)
        dw = jnp.zeros((8, cb_n), F32)
        for j in range(ktaps):
            s = ktaps - 1 - j
            du = du + w_ref[j:j + 1, :] * _shift_up(dext, s, tb)
            dw = _put_sub(dw, jnp.sum(d * _shift_down(uext, s, tb), axis=0, keepdims=True), j)
        dw = _put_sub(dw, jnp.sum(d, axis=0, keepdims=True), ktaps)
        if two:
            du_ref[...] = (du * v_ref[...]).astype(du_ref.dtype)
            dv_ref[...] = (du * u_ref[...]).astype(dv_ref.dtype)
        else:
            du_ref[...] = du.astype(du_ref.dtype)

        @pl.when(first)
        def _():
            dw_ref[...] = jnp.zeros_like(dw_ref)

        dw_ref[...] += dw

    blk, prev, nxt, wspec = _conv_specs(t, cb_n)
    specs, args = [blk, nxt, blk, prev], [dc, dc, u, u]
    if two:
        specs += [blk, prev]
        args += [u2, u2]
    specs.append(wspec)
    args.append(w8)
    act = jax.ShapeDtypeStruct((t, ch), MM_DTYPE)
    outs = ([blk, blk, wspec], [act, act, jax.ShapeDtypeStruct((8, ch), F32)]) if two else \
        ([blk, wspec], [act, jax.ShapeDtypeStruct((8, ch), F32)])
    return _call(body, name, (ch // cb_n, t // tb), specs, outs[0], outs[1],
                 semantics=("parallel", "arbitrary"))(*args)


EW_TB = 256


def _chunk_mask(n, upper):
    i, j = _iota((n, n), 0), _iota((n, n), 1)
    same = jnp.right_shift(i, 6) == jnp.right_shift(j, 6)
    return (same & ((j >= i) if upper else (i >= j))).astype(F32)


def _rows(width, tb=EW_TB):
    return pl.BlockSpec((tb, width), lambda i: (i, 0))


def _const(rows, width):
    return pl.BlockSpec((rows, width), lambda i: (0, 0))


def _gdn_ew_fwd(c, ba, a_log, dt_bias, name):
    t = c.shape[0]
    tb = EW_TB

    def body(c_ref, ba_ref, al_ref, db_ref, q_ref, k_ref, v_ref, beta_ref, gc_ref):
        for h in range(GDN_QK_HEADS):
            for base, ref, scale in ((0, q_ref, HEAD ** -0.5), (GDN_QK_DIM, k_ref, 1.0)):
                s = _silu(c_ref[:, base + h * HEAD: base + (h + 1) * HEAD])
                r = lax.rsqrt(jnp.sum(s * s, axis=1, keepdims=True) + L2_EPS)
                ref[:, h * HEAD:(h + 1) * HEAD] = s * (r * scale)
        v_ref[...] = _silu(c_ref[:, 2 * GDN_QK_DIM:])
        beta_ref[...] = _sig(ba_ref[:, :HEAD])
        g = -jnp.exp(al_ref[...]) * _softplus(ba_ref[:, HEAD:] + db_ref[...])
        gc_ref[...] = _mmx(_chunk_mask(tb, False), g, _NN)

    act = lambda w: jax.ShapeDtypeStruct((t, w), F32)
    return _call(body, name, (t // tb,),
                 [_rows(GDN_CONV_DIM), _rows(2 * HEAD), _const(1, HEAD), _const(1, HEAD)],
                 [_rows(GDN_QK_DIM), _rows(GDN_QK_DIM), _rows(D_INNER), _rows(HEAD), _rows(HEAD)],
                 [act(GDN_QK_DIM), act(GDN_QK_DIM), act(D_INNER), act(HEAD), act(HEAD)],
                 semantics=("parallel",))(c, ba, a_log, dt_bias)


def _gdn_ew_bwd(c, ba, a_log, dt_bias, dqh, dkh, dv, dbeta, dgc, name):
    t = c.shape[0]
    tb = EW_TB

    def body(c_ref, ba_ref, al_ref, db_ref, dq_ref, dk_ref, dv_ref, dbeta_ref, dgc_ref, dc_ref, dba_ref, acc_ref):
        for h in range(GDN_QK_HEADS):
            for base, ref, scale in ((0, dq_ref, HEAD ** -0.5), (GDN_QK_DIM, dk_ref, 1.0)):
                cq = c_ref[:, base + h * HEAD: base + (h + 1) * HEAD]
                s = _silu(cq)
                r = lax.rsqrt(jnp.sum(s * s, axis=1, keepdims=True) + L2_EPS)
                dn = ref[:, h * HEAD:(h + 1) * HEAD] * scale
                ds = r * dn - s * (r * r * r) * jnp.sum(dn * s, axis=1, keepdims=True)
                dc_ref[:, base + h * HEAD: base + (h + 1) * HEAD] = ds * _dsilu(cq)
        dc_ref[:, 2 * GDN_QK_DIM:] = dv_ref[...] * _dsilu(c_ref[:, 2 * GDN_QK_DIM:])
        beta = _sig(ba_ref[:, :HEAD])
        dba_ref[:, :HEAD] = (dbeta_ref[...] * beta * (1.0 - beta)).astype(dba_ref.dtype)
        pre = ba_ref[:, HEAD:] + db_ref[...]
        ea = jnp.exp(al_ref[...])
        g = -ea * _softplus(pre)
        dg = _mmx(_chunk_mask(tb, True), dgc_ref[...], _NN)
        da_raw = dg * (-ea) * _sig(pre)
        dba_ref[:, HEAD:] = da_raw.astype(dba_ref.dtype)
        acc = jnp.zeros((8, HEAD), F32)
        acc = _put_sub(acc, jnp.sum(dg * g, axis=0, keepdims=True), 0)
        acc = _put_sub(acc, jnp.sum(da_raw, axis=0, keepdims=True), 1)

        @pl.when(pl.program_id(0) == 0)
        def _():
            acc_ref[...] = jnp.zeros_like(acc_ref)

        acc_ref[...] += acc

    act = lambda w: jax.ShapeDtypeStruct((t, w), F32)
    return _call(body, name, (t // tb,),
                 [_rows(GDN_CONV_DIM), _rows(2 * HEAD), _const(1, HEAD), _const(1, HEAD),
                  _rows(GDN_QK_DIM), _rows(GDN_QK_DIM), _rows(D_INNER), _rows(HEAD), _rows(HEAD)],
                 [_rows(GDN_CONV_DIM), _rows(2 * HEAD), _const(8, HEAD)],
                 [act(GDN_CONV_DIM), jax.ShapeDtypeStruct((t, 2 * HEAD), MM_DTYPE), jax.ShapeDtypeStruct((8, HEAD), F32)],
                 semantics=("arbitrary",))(c, ba, a_log, dt_bias, dqh, dkh, dv, dbeta, dgc)


def _zip(fn, *lists):
    return [fn(*xs) for xs in zip(*lists)]


def _mms(xs, ys, dims):
    return [_mm(x, y, dims) for x, y in zip(xs, ys)]


def _side_by_side(a, b):
    return jnp.concatenate([a, b], axis=1)


def _interleave(*gens):
    results, live = [None] * len(gens), list(range(len(gens)))
    while live:
        for i in list(live):
            try:
                next(gens[i])
            except StopIteration as stop:
                results[i] = stop.value
                live.remove(i)
    return results


def _gdn_local_stages(q, k, v, bcol, gcol, grow, glast):
    ii, jj = _iota((CHUNK, CHUNK), 0), _iota((CHUNK, CHUNK), 1)
    eye = _eye(CHUNK)
    mul = lambda x, y: x * y
    eg = [jnp.exp(g) for g in gcol]
    decay = _zip(lambda gc, gr: jnp.exp(jnp.where(ii >= jj, gc - gr, NEG_BIG)), gcol, grow)
    kb = _zip(mul, k, bcol)
    p, qk = _mms(kb, k, _NT), _mms(q, k, _NT)
    yield
    a = _zip(lambda x, d: jnp.where(ii > jj, x * d, 0.0), p, decay)
    inv, pw = [eye - x for x in a], a
    for _ in range(5):
        pw = _mms(pw, pw, _NN)
        yield
        inv = _zip(lambda x, y: x + y, inv, _mms(inv, pw, _NN))
        yield
    rv, rk = _zip(mul, v, bcol), _zip(mul, kb, eg)
    uw = _mms(inv, _zip(_side_by_side, rv, rk), _NN)
    u, w = [x[:, :HEAD] for x in uw], [x[:, HEAD:] for x in uw]
    yield
    att = _zip(mul, qk, decay)
    qd = _zip(mul, q, eg)
    ekt = _zip(lambda gl, gc: jnp.exp(gl - gc), glast, gcol)
    kt = _zip(mul, k, ekt)
    el = [jnp.exp(g) for g in glast]
    return dict(eg=eg, decay=decay, kb=kb, p=p, inv=inv, rv=rv, rk=rk, u=u, w=w, qk=qk, att=att, qd=qd, ekt=ekt, kt=kt,
                el=el)


def _gdn_state_stages(u, w, att, qd, kt, el, s_in):
    ws, qs = _mms(w, s_in, _NN), _mms(qd, s_in, _NN)
    yield
    vn = _zip(lambda x, y: x - y, u, ws)
    av, kv = _mms(att, vn, _NN), _mms(kt, vn, _TN)
    yield
    out = _zip(lambda x, y: x + y, qs, av)
    s_out = _zip(lambda s, e, y: s * e + y, s_in, el, kv)
    return dict(vn=vn, out=out, s_out=s_out)


def _gdn_heads_fwd(q, k, v, bcol, gcol, grow, glast, s_in):
    f, = _interleave(_gdn_local_stages(q, k, v, bcol, gcol, grow, glast))
    g, = _interleave(_gdn_state_stages(f["u"], f["w"], f["att"], f["qd"], f["kt"], f["el"], s_in))
    return {**f, **g}


def _head_groups(group, init):
    if GDN_GROUP == GDN_V_HEADS:
        return group(0, init)
    return lax.fori_loop(0, GDN_V_HEADS // GDN_GROUP, lambda gi, c: group(GDN_GROUP * gi, c), init)


def _half(h):
    return h // 2 if isinstance(h, int) else jnp.right_shift(h, 1)


def _gdn_chunk_fwd(qn, kn, v, beta, gc, z, norm_w, name):
    t = qn.shape[0]
    nc = t // CHUNK

    def body(q_ref, k_ref, v_ref, beta_ref, gc_ref, z_ref, nw_ref, o_ref, h_ref, st_ref, state):
        @pl.when(pl.program_id(0) == 0)
        def _():
            state[...] = jnp.zeros_like(state)

        st_ref[0] = state[...]
        gc_all, beta_all = gc_ref[...], beta_ref[...]
        gct = _mmx(_eye(HEAD), gc_all, _NT)
        glast_all = gc_ref[CHUNK - 1:CHUNK, :]
        nw = nw_ref[...]

        def group(h0, carry):
            heads = [h0 + s for s in range(GDN_GROUP)]
            f = _gdn_heads_fwd([q_ref[:, _lanes(_half(h))] for h in heads], [k_ref[:, _lanes(_half(h))] for h in heads],
                               [v_ref[:, _lanes(h)] for h in heads], [_col(beta_all, h) for h in heads],
                               [_col(gc_all, h) for h in heads], [_row(gct, h) for h in heads],
                               [_col(glast_all, h) for h in heads], [state[h] for h in heads])
            for h, s_out, o in zip(heads, f["s_out"], f["out"]):
                state[h] = s_out
                o_ref[:, _lanes(h)] = o
                rstd = lax.rsqrt(jnp.mean(o * o, axis=1, keepdims=True) + RMS_EPS)
                h_ref[:, _lanes(h)] = (o * rstd * nw * _silu(z_ref[:, _lanes(h)])).astype(h_ref.dtype)
            return carry

        _head_groups(group, 0)

    rows = lambda w: pl.BlockSpec((CHUNK, w), lambda i: (i, 0))
    act = lambda w: jax.ShapeDtypeStruct((t, w), F32)
    return _call(body, name, (nc,),
                 [rows(GDN_QK_DIM), rows(GDN_QK_DIM), rows(D_INNER), rows(HEAD), rows(HEAD), rows(D_INNER),
                  _const(1, HEAD)],
                 [rows(D_INNER), rows(D_INNER), pl.BlockSpec((1, GDN_V_HEADS, HEAD, HEAD), lambda i: (i, 0, 0, 0))],
                 [act(D_INNER), jax.ShapeDtypeStruct((t, D_INNER), MM_DTYPE),
                  jax.ShapeDtypeStruct((nc, GDN_V_HEADS, HEAD, HEAD), F32)],
                 [pltpu.VMEM((GDN_V_HEADS, HEAD, HEAD), F32)], ("arbitrary",))(qn, kn, v, beta, gc, z, norm_w)


def _gdn_chunk_bwd(qn, kn, v, beta, gc, z, norm_w, o, states, dh, name):
    t = qn.shape[0]
    nc = t // CHUNK

    def body(q_ref, k_ref, v_ref, beta_ref, gc_ref, z_ref, nw_ref, o_ref, st_ref, dh_ref,
             dq_ref, dk_ref, dv_ref, dz_ref, dbeta_ref, dgc_ref, acc_ref, dstate):
        @pl.when(pl.program_id(0) == 0)
        def _():
            dstate[...] = jnp.zeros_like(dstate)
            acc_ref[...] = jnp.zeros_like(acc_ref)

        gc_all, beta_all = gc_ref[...], beta_ref[...]
        gct = _mmx(_eye(HEAD), gc_all, _NT)
        glast_all = gc_ref[CHUNK - 1:CHUNK, :]
        nw = nw_ref[...]
        ii, jj = _iota((CHUNK, CHUNK), 0), _iota((CHUNK, CHUNK), 1)
        last_row = _iota((CHUNK, 1), 0) == CHUNK - 1

        def group(h0, carry):
            dbeta_acc, dgc_acc, dgrow_acc, dnw_acc = carry
            heads = [h0 + s for s in range(GDN_GROUP)]
            mul, add, sub = (lambda x, y: x * y), (lambda x, y: x + y), (lambda x, y: x - y)
            rowsum = lambda x, y: jnp.sum(x * y, axis=1, keepdims=True)
            q, k = [q_ref[:, _lanes(_half(h))] for h in heads], [k_ref[:, _lanes(_half(h))] for h in heads]
            vv = [v_ref[:, _lanes(h)] for h in heads]
            bcol, gcol = [_col(beta_all, h) for h in heads], [_col(gc_all, h) for h in heads]
            s_in, dsn = [st_ref[0, h] for h in heads], [dstate[h] for h in heads]
            f = _gdn_heads_fwd(q, k, vv, bcol, gcol, [_row(gct, h) for h in heads],
                               [_col(glast_all, h) for h in heads], s_in)
            do = []
            for h in heads:
                oo, zz, dhh = o_ref[:, _lanes(h)], z_ref[:, _lanes(h)], dh_ref[:, _lanes(h)]
                rstd = lax.rsqrt(jnp.mean(oo * oo, axis=1, keepdims=True) + RMS_EPS)
                on, sz = oo * rstd, _silu(zz)
                dnw_acc = dnw_acc + jnp.sum(dhh * on * sz, axis=0, keepdims=True)
                dz_ref[:, _lanes(h)] = (dhh * on * nw * _dsilu(zz)).astype(dz_ref.dtype)
                don = dhh * nw * sz
                do.append(rstd * (don - on * jnp.mean(don * on, axis=1, keepdims=True)))
            decay, eg, inv = f["decay"], f["eg"], f["inv"]
            d_glast = _zip(lambda d, s, e: _total(d * s) * e, dsn, s_in, f["el"])
            dkt = _mms(f["vn"], dsn, _NT)
            dvn = _mms(f["kt"], dsn, _NN)
            dqd = _mms(do, s_in, _NT)
            ds_prev = _zip(lambda d, e, y: d * e + y, dsn, f["el"], _mms(f["qd"], do, _TN))
            datt = _mms(do, f["vn"], _NT)
            dvn = _zip(add, dvn, _mms(f["att"], do, _TN))
            dqk = _zip(mul, datt, decay)
            dq = _zip(lambda x, e, y: x * e + y, dqd, eg, _mms(dqk, k, _NN))
            dk = _mms(dqk, q, _TN)
            ddecay = _zip(mul, datt, f["qk"])
            dgcol = _zip(rowsum, dqd, f["qd"])
            dw = [-x for x in _mms(dvn, s_in, _NT)]
            ds_prev = _zip(sub, ds_prev, _mms(f["w"], dvn, _TN))
            drv, drk = _mms(inv, dvn, _TN), _mms(inv, dw, _TN)
            da = [jnp.where(ii > jj, -x, 0.0) for x in
                  _mms(_zip(_side_by_side, drv, drk), _zip(_side_by_side, f["u"], f["w"]), _NT)]
            dp = _zip(mul, da, decay)
            ddecay = _zip(lambda x, y, z_: x + y * z_, ddecay, da, f["p"])
            dkb = _zip(lambda x, y, e: x + y * e, _mms(dp, k, _NN), drk, eg)
            dk = _zip(add, dk, _mms(dp, f["kb"], _TN))
            dbeta = _zip(add, _zip(rowsum, drv, vv), _zip(rowsum, dkb, k))
            dgcol = _zip(add, dgcol, _zip(rowsum, drk, f["rk"]))
            dk = _zip(lambda x, y, b_, z_, e: x + y * b_ + z_ * e, dk, dkb, bcol, dkt, f["ekt"])
            tail = _zip(mul, dkt, f["kt"])
            d_glast = _zip(lambda x, y: x + _total(y), d_glast, tail)
            e_ = _zip(mul, ddecay, decay)
            dgcol = _zip(lambda x, t_, e, gl: x - jnp.sum(t_, axis=1, keepdims=True) + jnp.sum(e, axis=1, keepdims=True)
                         + jnp.where(last_row, gl, 0.0), dgcol, tail, e_, d_glast)
            for i_ in range(0, len(heads), 2):
                dq_ref[:, _lanes(_half(heads[i_]))] = dq[i_] + dq[i_ + 1]
                dk_ref[:, _lanes(_half(heads[i_]))] = dk[i_] + dk[i_ + 1]
            for i_, h in enumerate(heads):
                dstate[h] = ds_prev[i_]
                dv_ref[:, _lanes(h)] = drv[i_] * bcol[i_]
                dbeta_acc = _put_col(dbeta_acc, dbeta[i_], h)
                dgc_acc = _put_col(dgc_acc, dgcol[i_], h)
                dgrow_acc = _put_row(dgrow_acc, -jnp.sum(e_[i_], axis=0, keepdims=True), h)
            return dbeta_acc, dgc_acc, dgrow_acc, dnw_acc

        zero = jnp.zeros((CHUNK, HEAD), F32)
        dbeta_acc, dgc_acc, dgrow_acc, dnw_acc = _head_groups(
            group, (zero, zero, jnp.zeros((HEAD, CHUNK), F32), jnp.zeros((1, HEAD), F32)))
        dbeta_ref[...] = dbeta_acc
        dgc_ref[...] = dgc_acc + _mmx(_eye(CHUNK), dgrow_acc, _NT)
        acc_ref[...] += _put_sub(jnp.zeros((8, HEAD), F32), dnw_acc, 0)

    rows = lambda w: pl.BlockSpec((CHUNK, w), lambda i: (nc - 1 - i, 0))
    act = lambda w: jax.ShapeDtypeStruct((t, w), F32)
    return _call(body, name, (nc,),
                 [rows(GDN_QK_DIM), rows(GDN_QK_DIM), rows(D_INNER), rows(HEAD), rows(HEAD), rows(D_INNER),
                  _const(1, HEAD), rows(D_INNER),
                  pl.BlockSpec((1, GDN_V_HEADS, HEAD, HEAD), lambda i: (nc - 1 - i, 0, 0, 0)), rows(D_INNER)],
                 [rows(GDN_QK_DIM), rows(GDN_QK_DIM), rows(D_INNER), rows(D_INNER), rows(HEAD), rows(HEAD), _const(8, HEAD)],
                 [act(GDN_QK_DIM), act(GDN_QK_DIM), act(D_INNER), jax.ShapeDtypeStruct((t, D_INNER), MM_DTYPE), act(HEAD),
                  act(HEAD), jax.ShapeDtypeStruct((8, HEAD), F32)],
                 [pltpu.VMEM((GDN_V_HEADS, HEAD, HEAD), F32)], ("arbitrary",)
                 )(qn, kn, v, beta, gc, z, norm_w, o, states, dh)


def _sc_gate_fwd(bg, cv, z, name):
    t, w = bg.shape

    def body(b_ref, c_ref, z_ref, o_ref):
        o_ref[...] = (b_ref[...] * c_ref[...] * _silu(z_ref[...])).astype(o_ref.dtype)

    return _call(body, name, (t // EW_TB,), [_rows(w)] * 3, _rows(w), jax.ShapeDtypeStruct((t, w), MM_DTYPE),
                 semantics=("parallel",))(bg, cv, z)


def _sc_gate_bwd(dh, bg, cv, z, name):
    t, w = bg.shape

    def body(d_ref, b_ref, c_ref, z_ref, db_ref, dc_ref, dz_ref):
        d, b, c, zz = d_ref[...], b_ref[...], c_ref[...], z_ref[...]
        sz = _silu(zz)
        db_ref[...] = (d * c * sz).astype(db_ref.dtype)
        dc_ref[...] = d * b * sz
        dz_ref[...] = (d * b * c * _dsilu(zz)).astype(dz_ref.dtype)

    act, act_mm = jax.ShapeDtypeStruct((t, w), F32), jax.ShapeDtypeStruct((t, w), MM_DTYPE)
    return _call(body, name, (t // EW_TB,), [_rows(w)] * 4, [_rows(w)] * 3, [act_mm, act, act_mm],
                 semantics=("parallel",))(dh, bg, cv, z)


XBC_B = D_INNER
XBC_C = D_INNER + SSD_GROUPS * SSD_STATE


def _ssd_scalars(dtp, dt_bias, a_log):
    dt = _softplus(dtp + dt_bias)
    a = -jnp.exp(a_log)
    da = dt * a
    ac = _mmx(_chunk_mask(CHUNK, False), da, _NN)
    act = _mmx(_eye(HEAD), ac, _NT)
    aclast = jnp.sum(jnp.where(_iota(ac.shape, 0) == CHUNK - 1, ac, 0.0), axis=0, keepdims=True)
    return dt, a, da, ac, act, aclast


def _ssd_pairs_fwd(x2, bg, cg, cb, dt, ac, act, aclast, s2):
    ii, jj = _iota((CHUNK, CHUNK), 0), _iota((CHUNK, CHUNK), 1)
    half = _iota((CHUNK, HEAD), 1) < 64
    causal = ii >= jj
    pairs = range(len(x2))
    mul = lambda x, y: x * y
    pick = lambda a, b: jnp.where(half, a, b)
    aca, acb = [_col(ac, 2 * p) for p in pairs], [_col(ac, 2 * p + 1) for p in pairs]
    la, lb = [_col(aclast, 2 * p) for p in pairs], [_col(aclast, 2 * p + 1) for p in pairs]
    dt2 = [pick(_col(dt, 2 * p), _col(dt, 2 * p + 1)) for p in pairs]
    xdt = _zip(mul, x2, dt2)
    sega = [jnp.exp(jnp.where(causal, aca[p] - _row(act, 2 * p), NEG_BIG)) for p in pairs]
    segb = [jnp.exp(jnp.where(causal, acb[p] - _row(act, 2 * p + 1), NEG_BIG)) for p in pairs]
    ma, mb = _zip(mul, sega, cb), _zip(mul, segb, cb)
    ydiag = _zip(pick, _mms(ma, xdt, _NN), _mms(mb, xdt, _NN))
    cdec = _zip(lambda a, b: pick(jnp.exp(a), jnp.exp(b)), aca, acb)
    cs = _mms(cg, s2, _NT)
    tail = _zip(lambda l1, a, l2, b: pick(jnp.exp(l1 - a), jnp.exp(l2 - b)), la, aca, lb, acb)
    zt = _zip(mul, xdt, tail)
    ea, eb = [jnp.exp(x) for x in la], [jnp.exp(x) for x in lb]
    tot = _zip(lambda a, b: jnp.where(_iota((HEAD, 1), 0) < 64, a, b), ea, eb)
    s_out = _zip(lambda s, t_, y: s * t_ + y, s2, tot, _mms(zt, bg, _TN))
    return dict(half=half, dt2=dt2, xdt=xdt, sega=sega, segb=segb, ma=ma, mb=mb, ydiag=ydiag, cdec=cdec, cs=cs,
                tail=tail, zt=zt, ea=ea, eb=eb, tot=tot, s_out=s_out)


def _ssd_group_inputs(cx_ref):
    cxb = [cx_ref[:, XBC_B + g * SSD_STATE: XBC_B + (g + 1) * SSD_STATE] for g in range(SSD_GROUPS)]
    cxc = [cx_ref[:, XBC_C + g * SSD_STATE: XBC_C + (g + 1) * SSD_STATE] for g in range(SSD_GROUPS)]
    bg, cg = [_silu(x) for x in cxb], [_silu(x) for x in cxc]
    return cxb, cxc, bg, cg, _mms(cg, bg, _NT)


def _per_pair(group_list):
    return [group_list[p // (SSD_PAIRS // SSD_GROUPS)] for p in range(SSD_PAIRS)]


def _ssd_chunk_fwd(cx, dtp, z, dt_bias, a_log, dskip, norm_w, name):
    t = cx.shape[0]
    nc = t // CHUNK
    gw = D_INNER // SSD_GROUPS

    def body(cx_ref, dtp_ref, z_ref, db_ref, al_ref, sk_ref, nw_ref, y_ref, h_ref, st_ref, state):
        @pl.when(pl.program_id(0) == 0)
        def _():
            state[...] = jnp.zeros_like(state)

        st_ref[0] = state[...]
        dt, _, _, ac, act, aclast = _ssd_scalars(dtp_ref[...], db_ref[...], al_ref[...])
        _, _, bg, cg, cb = _ssd_group_inputs(cx_ref)
        x2 = [_silu(cx_ref[:, _lanes(p)]) for p in range(SSD_PAIRS)]
        f = _ssd_pairs_fwd(x2, _per_pair(bg), _per_pair(cg), _per_pair(cb), dt, ac, act, aclast,
                           [state[p] for p in range(SSD_PAIRS)])
        for p in range(SSD_PAIRS):
            state[p] = f["s_out"][p]
            y_ref[:, _lanes(p)] = f["ydiag"][p] + f["cs"][p] * f["cdec"][p] + sk_ref[:, _lanes(p)] * x2[p]
        for g in range(SSD_GROUPS):
            sl = slice(g * gw, (g + 1) * gw)
            yg = y_ref[:, sl] * _silu(z_ref[:, sl])
            rstd = lax.rsqrt(jnp.mean(yg * yg, axis=1, keepdims=True) + RMS_EPS)
            h_ref[:, sl] = (yg * rstd * nw_ref[:, sl]).astype(h_ref.dtype)

    rows = lambda w: pl.BlockSpec((CHUNK, w), lambda i: (i, 0))
    act_ = lambda w: jax.ShapeDtypeStruct((t, w), F32)
    return _call(body, name, (nc,),
                 [rows(SSD_CONV_DIM), rows(HEAD), rows(D_INNER), _const(1, HEAD), _const(1, HEAD),
                  _const(1, D_INNER), _const(1, D_INNER)],
                 [rows(D_INNER), rows(D_INNER), pl.BlockSpec((1, SSD_PAIRS, HEAD, SSD_STATE), lambda i: (i, 0, 0, 0))],
                 [act_(D_INNER), jax.ShapeDtypeStruct((t, D_INNER), MM_DTYPE),
                  jax.ShapeDtypeStruct((nc, SSD_PAIRS, HEAD, SSD_STATE), F32)],
                 [pltpu.VMEM((SSD_PAIRS, HEAD, SSD_STATE), F32)], ("arbitrary",)
                 )(cx, dtp, z, dt_bias, a_log, dskip, norm_w)


def _ssd_chunk_bwd(cx, dtp, z, dt_bias, a_log, dskip, norm_w, y, states, dh, name):
    t = cx.shape[0]
    nc = t // CHUNK
    gw = D_INNER // SSD_GROUPS

    def body(cx_ref, dtp_ref, z_ref, db_ref, al_ref, sk_ref, nw_ref, y_ref, st_ref, dh_ref,
             dcx_ref, ddtp_ref, dz_ref, wide_ref, acc_ref, dstate, dy_s):
        @pl.when(pl.program_id(0) == 0)
        def _():
            dstate[...] = jnp.zeros_like(dstate)
            wide_ref[...] = jnp.zeros_like(wide_ref)
            acc_ref[...] = jnp.zeros_like(acc_ref)

        dtp = dtp_ref[...]
        dt, a, da, ac, act, aclast = _ssd_scalars(dtp, db_ref[...], al_ref[...])
        ii, jj = _iota((CHUNK, CHUNK), 0), _iota((CHUNK, CHUNK), 1)
        last_row = _iota((CHUNK, 1), 0) == CHUNK - 1
        top = _iota((HEAD, SSD_STATE), 0) < 64
        for g in range(SSD_GROUPS):
            sl = slice(g * gw, (g + 1) * gw)
            yy, zz, dhh, nw = y_ref[:, sl], z_ref[:, sl], dh_ref[:, sl], nw_ref[:, sl]
            sz = _silu(zz)
            yg = yy * sz
            rstd = lax.rsqrt(jnp.mean(yg * yg, axis=1, keepdims=True) + RMS_EPS)
            n = yg * rstd
            dn = dhh * nw
            dyg = rstd * (dn - n * jnp.mean(dn * n, axis=1, keepdims=True))
            dy_s[:, sl] = dyg * sz
            dz_ref[:, sl] = (dyg * yy * _dsilu(zz)).astype(dz_ref.dtype)
            wide_ref[0:1, sl] += jnp.sum(dhh * n, axis=0, keepdims=True)

        pairs = range(SSD_PAIRS)
        mul, add, sub = (lambda x, y: x * y), (lambda x, y: x + y), (lambda x, y: x - y)
        rowsum = lambda x: jnp.sum(x, axis=1, keepdims=True)
        cxb, cxc, bg, cg, cb = _ssd_group_inputs(cx_ref)
        bgp, cgp = _per_pair(bg), _per_pair(cg)
        cxx = [cx_ref[:, _lanes(p)] for p in pairs]
        x2 = [_silu(x) for x in cxx]
        s2, dsn = [st_ref[0, p] for p in pairs], [dstate[p] for p in pairs]
        dy2 = [dy_s[:, _lanes(p)] for p in pairs]
        f = _ssd_pairs_fwd(x2, bgp, cgp, _per_pair(cb), dt, ac, act, aclast, s2)
        half = f["half"]
        lo = lambda x: jnp.where(half, x, 0.0)
        dx2 = [dy2[p] * sk_ref[:, _lanes(p)] for p in pairs]
        for p in pairs:
            wide_ref[1:2, _lanes(p)] += jnp.sum(dy2[p] * x2[p], axis=0, keepdims=True)
        gg = _zip(mul, dy2, f["cdec"])
        dc_p = _mms(gg, s2, _NN)
        ds_prev = _zip(lambda d, t_, y: d * t_ + y, dsn, f["tot"], _mms(gg, cgp, _TN))
        t1 = _zip(lambda d, c, e: d * c * e, dy2, f["cs"], f["cdec"])
        dac_a = [rowsum(lo(x)) for x in t1]
        dac_b = _zip(lambda x, a_: rowsum(x) - a_, t1, dac_a)
        dya = [lo(x) for x in dy2]
        dma, dmb = _mms(dya, f["xdt"], _NT), _mms(_zip(sub, dy2, dya), f["xdt"], _NT)
        dxdt = _zip(lambda a_, b_: jnp.where(half, a_, b_), _mms(f["ma"], dy2, _TN), _mms(f["mb"], dy2, _TN))
        dcb_p = _zip(lambda a_, sa, b_, sb: a_ * sa + b_ * sb, dma, f["sega"], dmb, f["segb"])
        ea_, eb_ = _zip(mul, dma, f["ma"]), _zip(mul, dmb, f["mb"])
        dac_a = _zip(lambda x, e: x + rowsum(e), dac_a, ea_)
        dac_b = _zip(lambda x, e: x + rowsum(e), dac_b, eb_)
        dzt = _mms(bgp, dsn, _NT)
        db_p = _mms(f["zt"], dsn, _NN)
        dxdt = _zip(lambda x, d, t_: x + d * t_, dxdt, dzt, f["tail"])
        t2 = _zip(mul, dzt, f["zt"])
        t2a = [rowsum(lo(x)) for x in t2]
        t2b = _zip(lambda x, a_: rowsum(x) - a_, t2, t2a)
        t3 = _zip(mul, dsn, s2)
        t3a = [_total(jnp.where(top, x, 0.0)) for x in t3]
        dla = _zip(lambda x, y, e: _total(x) + y * e, t2a, t3a, f["ea"])
        dlb = _zip(lambda x, y, ya, e: _total(x) + (_total(y) - ya) * e, t2b, t3, t3a, f["eb"])
        dac_a = _zip(lambda x, y, l: x - y + jnp.where(last_row, l, 0.0), dac_a, t2a, dla)
        dac_b = _zip(lambda x, y, l: x - y + jnp.where(last_row, l, 0.0), dac_b, t2b, dlb)
        dx2 = _zip(lambda x, d, t_: x + d * t_, dx2, dxdt, f["dt2"])
        t4 = _zip(mul, dxdt, x2)
        t4a = [rowsum(lo(x)) for x in t4]
        t4b = _zip(lambda x, a_: rowsum(x) - a_, t4, t4a)
        zero = jnp.zeros((CHUNK, HEAD), F32)
        ddt_acc, dac_acc, drow_acc = zero, zero, jnp.zeros((HEAD, CHUNK), F32)
        for p in pairs:
            dcx_ref[:, _lanes(p)] = dx2[p] * _dsilu(cxx[p])
            dstate[p] = ds_prev[p]
            ddt_acc = _put_col(_put_col(ddt_acc, t4a[p], 2 * p), t4b[p], 2 * p + 1)
            dac_acc = _put_col(_put_col(dac_acc, dac_a[p], 2 * p), dac_b[p], 2 * p + 1)
            drow_acc = _put_row(_put_row(drow_acc, -jnp.sum(ea_[p], axis=0, keepdims=True), 2 * p),
                                -jnp.sum(eb_[p], axis=0, keepdims=True), 2 * p + 1)
        per = SSD_PAIRS // SSD_GROUPS
        gsum = lambda xs: [functools.reduce(add, xs[g * per:(g + 1) * per]) for g in range(SSD_GROUPS)]
        dcb = gsum(dcb_p)
        dc = _zip(add, gsum(dc_p), _mms(dcb, bg, _NN))
        db = _zip(add, gsum(db_p), _mms(dcb, cg, _TN))
        for g in range(SSD_GROUPS):
            dcx_ref[:, XBC_B + g * SSD_STATE: XBC_B + (g + 1) * SSD_STATE] = db[g] * _dsilu(cxb[g])
            dcx_ref[:, XBC_C + g * SSD_STATE: XBC_C + (g + 1) * SSD_STATE] = dc[g] * _dsilu(cxc[g])
        dac = dac_acc + _mmx(_eye(CHUNK), drow_acc, _NT)
        dda = _mmx(_chunk_mask(CHUNK, True), dac, _NN)
        ddt = ddt_acc + dda * a
        ddtp = ddt * _sig(dtp + db_ref[...])
        ddtp_ref[...] = ddtp.astype(ddtp_ref.dtype)
        acc = _put_sub(jnp.zeros((8, HEAD), F32), jnp.sum(dda * da, axis=0, keepdims=True), 0)
        acc_ref[...] += _put_sub(acc, jnp.sum(ddtp, axis=0, keepdims=True), 1)

    rows = lambda w: pl.BlockSpec((CHUNK, w), lambda i: (nc - 1 - i, 0))
    act_ = lambda w: jax.ShapeDtypeStruct((t, w), F32)
    return _call(body, name, (nc,),
                 [rows(SSD_CONV_DIM), rows(HEAD), rows(D_INNER), _const(1, HEAD), _const(1, HEAD),
                  _const(1, D_INNER), _const(1, D_INNER), rows(D_INNER),
                  pl.BlockSpec((1, SSD_PAIRS, HEAD, SSD_STATE), lambda i: (nc - 1 - i, 0, 0, 0)), rows(D_INNER)],
                 [rows(SSD_CONV_DIM), rows(HEAD), rows(D_INNER), _const(8, D_INNER), _const(8, HEAD)],
                 [act_(SSD_CONV_DIM), jax.ShapeDtypeStruct((t, HEAD), MM_DTYPE), jax.ShapeDtypeStruct((t, D_INNER), MM_DTYPE),
                  jax.ShapeDtypeStruct((8, D_INNER), F32),
                  jax.ShapeDtypeStruct((8, HEAD), F32)],
                 [pltpu.VMEM((SSD_PAIRS, HEAD, SSD_STATE), F32), pltpu.VMEM((CHUNK, D_INNER), F32)], ("arbitrary",)
                 )(cx, dtp, z, dt_bias, a_log, dskip, norm_w, y, states, dh)


LN_TB = 512


def _ln_stats(x, y):
    u = ALPHA * x + y
    mu = jnp.mean(u, axis=1, keepdims=True)
    cen = u - mu
    rstd = lax.rsqrt(jnp.mean(cen * cen, axis=1, keepdims=True) + LN_EPS)
    return cen * rstd


def _ln_fwd(x, y, g, b, name):
    t, d = x.shape

    def body(x_ref, y_ref, g_ref, b_ref, o_ref, omm_ref):
        out = _ln_stats(x_ref[...], y_ref[...]) * g_ref[...] + b_ref[...]
        o_ref[...] = out
        omm_ref[...] = out.astype(omm_ref.dtype)

    return _call(body, name, (t // LN_TB,), [_rows(d, LN_TB), _rows(d, LN_TB), _const(1, d), _const(1, d)],
                 [_rows(d, LN_TB)] * 2, [jax.ShapeDtypeStruct((t, d), F32), jax.ShapeDtypeStruct((t, d), MM_DTYPE)],
                 semantics=("parallel",))(x, y, g, b)


def _ln_bwd(dout, x, y, g, name):
    t, d = x.shape

    def body(d_ref, x_ref, y_ref, g_ref, du_ref, dumm_ref, acc_ref):
        u = ALPHA * x_ref[...] + y_ref[...]
        mu = jnp.mean(u, axis=1, keepdims=True)
        cen = u - mu
        rstd = lax.rsqrt(jnp.mean(cen * cen, axis=1, keepdims=True) + LN_EPS)
        xh = cen * rstd
        do = d_ref[...]
        dxh = do * g_ref[...]
        du = rstd * (dxh - jnp.mean(dxh, axis=1, keepdims=True) - xh * jnp.mean(dxh * xh, axis=1, keepdims=True))
        du_ref[...] = du
        dumm_ref[...] = du.astype(dumm_ref.dtype)
        acc = _put_sub(jnp.zeros((8, d), F32), jnp.sum(do * xh, axis=0, keepdims=True), 0)
        acc = _put_sub(acc, jnp.sum(do, axis=0, keepdims=True), 1)

        @pl.when(pl.program_id(0) == 0)
        def _():
            acc_ref[...] = jnp.zeros_like(acc_ref)

        acc_ref[...] += acc

    return _call(body, name, (t // LN_TB,), [_rows(d, LN_TB)] * 3 + [_const(1, d)],
                 [_rows(d, LN_TB), _rows(d, LN_TB), _const(8, d)],
                 [jax.ShapeDtypeStruct((t, d), F32), jax.ShapeDtypeStruct((t, d), MM_DTYPE),
                  jax.ShapeDtypeStruct((8, d), F32)],
                 semantics=("arbitrary",))(dout, x, y, g)


def _loss_head(out, target, name):
    t, d = out.shape

    def body(o_ref, t_ref, d_ref, acc_ref):
        err = o_ref[...] - t_ref[...]
        d_ref[...] = err * (1.0 / d)

        @pl.when(pl.program_id(0) == 0)
        def _():
            acc_ref[...] = jnp.zeros_like(acc_ref)

        acc_ref[...] += _put_sub(jnp.zeros((8, d), F32), jnp.sum(err * err, axis=0, keepdims=True), 0)

    return _call(body, name, (t // LN_TB,), [_rows(d, LN_TB)] * 2, [_rows(d, LN_TB), _const(8, d)],
                 [jax.ShapeDtypeStruct((t, d), F32), jax.ShapeDtypeStruct((8, d), F32)],
                 semantics=("arbitrary",))(out, target)


def _adamw(w, gslots, m, v, name):
    r, c = w.shape
    rb = _tile_rows(r)
    c1 = 1.0 - ADAM_B1 ** ADAM_STEP
    c2 = 1.0 - ADAM_B2 ** ADAM_STEP

    def body(w_ref, g_ref, m_ref, v_ref, go_ref, d_ref, mo_ref, vo_ref):
        g = g_ref[0].astype(F32)
        for s in range(1, N_DEV):
            g = g + g_ref[s].astype(F32)
        mn = ADAM_B1 * m_ref[...] + (1.0 - ADAM_B1) * g
        vn = ADAM_B2 * v_ref[...] + (1.0 - ADAM_B2) * (g * g)
        go_ref[...] = g
        mo_ref[...] = mn
        vo_ref[...] = vn
        d_ref[...] = -ADAM_LR * ((mn / c1) / (jnp.sqrt(vn / c2) + ADAM_EPS) + ADAM_WD * w_ref[...])

    blk = pl.BlockSpec((rb, c), lambda i: (i, 0))
    sds = jax.ShapeDtypeStruct((r, c), F32)
    return _call(body, name, (r // rb,), [blk, pl.BlockSpec((N_DEV, rb, c), lambda i: (0, i, 0)), blk, blk],
                 [blk] * 4, [sds] * 4, semantics=("parallel",))(w, gslots, m, v)


def _tile_rows(r):
    for rb in (256, 128, 64, 32, 16, 8):
        if r % rb == 0:
            return rb
    return r


def _pack(arrs, lead=0):
    flats = []
    for a in arrs:
        f = a.reshape(a.shape[:lead] + (-1,)).astype(F32)
        flats.append(jnp.pad(f, [(0, 0)] * lead + [(0, (-f.shape[-1]) % 128)]))
    v = jnp.concatenate(flats, axis=-1)
    v = jnp.pad(v, [(0, 0)] * lead + [(0, (-v.shape[-1]) % 1024)])
    return v.reshape(v.shape[:lead] + (-1, 128))


def _unpack(buf, shapes, lead=0):
    flat = buf.reshape(buf.shape[:lead] + (-1,))
    outs, off = [], 0
    for s in shapes:
        n = math.prod(s)
        outs.append(flat[..., off:off + n].reshape(buf.shape[:lead] + tuple(s)))
        off += n + (-n) % 128
    return outs


def _cols_gathered(g):
    n, l, r, c = g.shape
    return g.transpose(1, 2, 0, 3).reshape(l, r, n * c)


def _cols_to_slabs(full):
    l, r, c = full.shape
    return full.reshape(l, r, N_DEV, c // N_DEV).transpose(2, 0, 1, 3)


def _rows_gathered(g):
    n, l, r, c = g.shape
    return g.transpose(1, 0, 2, 3).reshape(l, n * r, c)


def _rows_to_slabs(full):
    l, r, c = full.shape
    return full.reshape(l, N_DEV, r // N_DEV, c).transpose(1, 0, 2, 3)


def _pad_cols(w, at, width):
    return jnp.pad(w, ((0, 0), (at, width - at - w.shape[1])))


def _pad_lanes(v, width=HEAD):
    return jnp.pad(v.reshape(1, -1), ((0, 0), (0, width - v.size)))


def _taps8(w, bias=None):
    rows = [w] if bias is None else [w, bias.reshape(1, -1)]
    w8 = jnp.concatenate(rows, axis=0)
    return jnp.pad(w8, ((0, 8 - w8.shape[0]), (0, 0)))


class _Carrier:
    def __init__(self):
        self.jobs, self.got = {}, {}

    def put(self, matmul_name, key, src, slabs):
        self.jobs[matmul_name] = (key, src, slabs)

    def matmul(self, a, b, mode, name, **kw):
        job = self.jobs.pop(name, None)
        if job is None:
            return _matmul(a, b, mode, name, **kw)
        key, src, slabs = job
        out, self.got[key] = _matmul(a, b, mode, name, carry=(src, slabs), **kw)
        return out


def _gdn_forward(x, p, tag, mm):
    pq = mm(x, p["w_qkv"], "nn", tag + "_in_qkv")
    z = mm(x, p["w_z"], "nn", tag + "_in_z")
    ba = mm(x, p["w_ba"], "nn", tag + "_in_ba")
    c = _conv_fwd(pq, p["conv8"], 4, tag + "_conv")
    qn, kn, v, beta, gc = _gdn_ew_fwd(c, ba, p["a_log"], p["dt_bias"], tag + "_ew")
    o, h, states = _gdn_chunk_fwd(qn, kn, v, beta, gc, z, p["norm_w"], tag + "_chunk")
    y = mm(h, p["w_out"], "nn", tag + "_out")
    return y, dict(pq=pq, z=z, ba=ba, c=c, qn=qn, kn=kn, v=v, beta=beta, gc=gc, o=o, h=h, states=states)


def _gdn_backward(x, du, du_mm, p, s, tag, mm, ship):
    dh = mm(du_mm, p["w_out"], "nt", tag + "_bwd_dh")
    g_out = mm(s["h"], du_mm, "tn", tag + "_bwd_wout")
    ship("w_out", g_out)
    dq, dk, dv, dz, dbeta, dgc, nacc = _gdn_chunk_bwd(
        s["qn"], s["kn"], s["v"], s["beta"], s["gc"], s["z"], p["norm_w"], s["o"], s["states"], dh, tag + "_bwd_chunk")
    dc, dba, sacc = _gdn_ew_bwd(s["c"], s["ba"], p["a_log"], p["dt_bias"], dq, dk, dv, dbeta, dgc, tag + "_bwd_ew")
    dpq, dconv = _conv_bwd(dc, s["pq"], p["conv8"], 4, tag + "_bwd_conv")
    g_qkv = mm(x, dpq, "tn", tag + "_bwd_w_qkv")
    g_z = mm(x, dz, "tn", tag + "_bwd_w_z")
    g_ba = mm(x, dba, "tn", tag + "_bwd_w_ba")
    g_in = jnp.concatenate([g_qkv, g_z, g_ba[:, :GDN_V_HEADS], g_ba[:, HEAD:HEAD + GDN_V_HEADS]], axis=1)
    ship("w_in", g_in)
    dx = mm(dpq, p["w_qkv"], "nt", tag + "_bwd_dx_qkv", add=du, add_scale=ALPHA)
    dx = mm(dz, p["w_z"], "nt", tag + "_bwd_dx_z", add=dx)
    dx = mm(dba, p["w_ba"], "nt", tag + "_bwd_dx_ba", add=dx)
    grads = dict(w_in=g_in, w_out=g_out, conv_w=dconv[:4], a_log=sacc[0, :GDN_V_HEADS], dt_bias=sacc[1, :GDN_V_HEADS],
                 norm_w=nacc[0])
    return dx, grads


def _sc_forward(x, p, tag, mm):
    hh = mm(x, p["w_h"], "nn", tag + "_in_h")
    bg = mm(x, p["w_b"], "nn", tag + "_in_b")
    cg = mm(x, p["w_c"], "nn", tag + "_in_c")
    z = mm(x, p["w_z"], "nn", tag + "_in_z")
    cv = _conv_fwd(cg, p["conv8"], 3, tag + "_conv", u2=hh)
    h = _sc_gate_fwd(bg, cv, z, tag + "_gate")
    y = mm(h, p["w_out"], "nn", tag + "_out")
    return y, dict(hh=hh, bg=bg, cg=cg, z=z, cv=cv, h=h)


def _sc_backward(x, du, du_mm, p, s, tag, mm, ship):
    dh = mm(du_mm, p["w_out"], "nt", tag + "_bwd_dh")
    g_out = mm(s["h"], du_mm, "tn", tag + "_bwd_wout")
    ship("w_out", g_out)
    dbg, dcv, dz = _sc_gate_bwd(dh, s["bg"], s["cv"], s["z"], tag + "_bwd_gate")
    dcg, dhh, dconv = _conv_bwd(dcv, s["cg"], p["conv8"], 3, tag + "_bwd_conv", u2=s["hh"])
    g_in = jnp.concatenate([mm(x, d, "tn", tag + "_bwd_w_" + n)
                            for n, d in (("h", dhh), ("b", dbg), ("c", dcg), ("z", dz))], axis=1)
    ship("w_in", g_in)
    dx = mm(dhh, p["w_h"], "nt", tag + "_bwd_dx_h", add=du, add_scale=ALPHA)
    dx = mm(dbg, p["w_b"], "nt", tag + "_bwd_dx_b", add=dx)
    dx = mm(dcg, p["w_c"], "nt", tag + "_bwd_dx_c", add=dx)
    dx = mm(dz, p["w_z"], "nt", tag + "_bwd_dx_z", add=dx)
    return dx, dict(w_in=g_in, w_out=g_out, conv_w=dconv[:3])


def _ssd_forward(x, p, tag, mm):
    z = mm(x, p["w_z"], "nn", tag + "_in_z")
    xbc = mm(x, p["w_xbc"], "nn", tag + "_in_xbc")
    dtp = mm(x, p["w_dt"], "nn", tag + "_in_dt")
    cx = _conv_fwd(xbc, p["conv8"], 4, tag + "_conv", bias=True)
    y, h, states = _ssd_chunk_fwd(cx, dtp, z, p["dt_bias"], p["a_log"], p["dskip"], p["norm_w"], tag + "_chunk")
    out = mm(h, p["w_out"], "nn", tag + "_out")
    return out, dict(z=z, xbc=xbc, dtp=dtp, cx=cx, y=y, h=h, states=states)


def _ssd_backward(x, du, du_mm, p, s, tag, mm, ship):
    dh = mm(du_mm, p["w_out"], "nt", tag + "_bwd_dh")
    g_out = mm(s["h"], du_mm, "tn", tag + "_bwd_wout")
    ship("w_out", g_out)
    dcx, ddtp, dz, wide, acc = _ssd_chunk_bwd(s["cx"], s["dtp"], s["z"], p["dt_bias"], p["a_log"], p["dskip"],
                                              p["norm_w"], s["y"], s["states"], dh, tag + "_bwd_chunk")
    dxbc, dconv = _conv_bwd(dcx, s["xbc"], p["conv8"], 4, tag + "_bwd_conv")
    g_dt = mm(x, ddtp, "tn", tag + "_bwd_w_dt")
    g_in = jnp.concatenate([mm(x, dz, "tn", tag + "_bwd_w_z"), mm(x, dxbc, "tn", tag + "_bwd_w_xbc"),
                            g_dt[:, :32]], axis=1)
    ship("w_in", g_in)
    dx = mm(dz, p["w_z"], "nt", tag + "_bwd_dx_z", add=du, add_scale=ALPHA)
    dx = mm(dxbc, p["w_xbc"], "nt", tag + "_bwd_dx_xbc", add=dx)
    dx = mm(ddtp, p["w_dt"], "nt", tag + "_bwd_dx_dt", add=dx)
    grads = dict(w_in=g_in, w_out=g_out, conv_w=dconv[:4], conv_b=dconv[4], a_log=acc[0, :32], dt_bias=acc[1, :32],
                 d_skip=jnp.sum(wide[1].reshape(32, 64), axis=1), norm_w=wide[0])
    return dx, grads


_WEIGHTS = ['gdn_w_in', 'gdn_conv_w', 'gdn_a_log', 'gdn_dt_bias', 'gdn_norm_w', 'gdn_w_out', 'sc_w_in', 'sc_conv_w',
            'sc_w_out', 'ssd_w_in', 'ssd_conv_w', 'ssd_conv_b', 'ssd_a_log', 'ssd_dt_bias', 'ssd_d_skip',
            'ssd_norm_w', 'ssd_w_out', 'ln_g', 'ln_b']
_BIG = {'gdn_w_in': 'cols', 'gdn_w_out': 'rows', 'sc_w_in': 'cols', 'sc_w_out': 'rows', 'ssd_w_in': 'cols',
        'ssd_w_out': 'rows'}
_SMALL_SHARDED = ['gdn_conv_w', 'sc_conv_w', 'ssd_conv_w', 'ssd_conv_b', 'ssd_norm_w']
_SMALL = [n for n in _WEIGHTS if n not in _BIG]


def kernel(x, gdn_w_in, gdn_conv_w, gdn_a_log, gdn_dt_bias, gdn_norm_w, gdn_w_out, sc_w_in, sc_conv_w, sc_w_out, ssd_w_in, ssd_conv_w, ssd_conv_b, ssd_a_log, ssd_dt_bias, ssd_d_skip, ssd_norm_w, ssd_w_out, ln_g, ln_b, loss_target, m_gdn_w_in, m_gdn_conv_w, m_gdn_a_log, m_gdn_dt_bias, m_gdn_norm_w, m_gdn_w_out, m_sc_w_in, m_sc_conv_w, m_sc_w_out, m_ssd_w_in, m_ssd_conv_w, m_ssd_conv_b, m_ssd_a_log, m_ssd_dt_bias, m_ssd_d_skip, m_ssd_norm_w, m_ssd_w_out, m_ln_g, m_ln_b, v_gdn_w_in, v_gdn_conv_w, v_gdn_a_log, v_gdn_dt_bias, v_gdn_norm_w, v_gdn_w_out, v_sc_w_in, v_sc_conv_w, v_sc_w_out, v_ssd_w_in, v_ssd_conv_w, v_ssd_conv_b, v_ssd_a_log, v_ssd_dt_bias, v_ssd_d_skip, v_ssd_norm_w, v_ssd_w_out, v_ln_g, v_ln_b):
    args = locals()
    wts = {n: args[n] for n in _WEIGHTS}
    mom = {n: args["m_" + n] for n in _WEIGHTS}
    vel = {n: args["v_" + n] for n in _WEIGHTS}
    me = 4 * lax.axis_index("x") + 2 * lax.axis_index("y") + lax.axis_index("c")
    x0, target = x[0], loss_target[0]

    car = _Carrier()
    shard = lambda n, j: wts[n][j:j + 1].astype(MM_DTYPE)
    gathered_w = lambda n, j: (_cols_gathered if _BIG[n] == "cols" else _rows_gathered)(car.got[n, j])[0]

    for n in ('gdn_w_in', 'gdn_w_out'):
        car.got[n, 0] = _exchange(shard(n, 0), "gather_%s0" % n, slabs=False)
    riders = {0: [("l0_gdn_in_qkv", 'sc_w_in', 0), ("l0_gdn_in_z", 'sc_w_out', 0), ("l0_gdn_out", 'ssd_w_out', 0)],
              1: [("l1_sc_in_h", 'ssd_w_in', 0)],
              2: [("l2_ssd_in_xbc", 'gdn_w_in', 1), ("l2_ssd_in_z", 'gdn_w_out', 1)]}
    full = {}
    small_shapes = [wts[n].shape for n in _SMALL_SHARDED]
    gathered = _exchange(_pack([wts[n] for n in _SMALL_SHARDED]), "gather_small", slabs=False)
    for n, g in zip(_SMALL_SHARDED, _unpack(gathered, small_shapes, lead=1)):
        full[n] = jnp.moveaxis(g, 0, -2).reshape(g.shape[1:-1] + (N_DEV * g.shape[-1],))
    for n in _SMALL:
        full.setdefault(n, wts[n])

    def gdn_params(j):
        w = gathered_w('gdn_w_in', j)
        return dict(w_qkv=w[:, :GDN_CONV_DIM], w_z=w[:, GDN_CONV_DIM:GDN_CONV_DIM + D_INNER],
                    w_ba=jnp.concatenate([_pad_cols(w[:, 6144:6160], 0, HEAD), _pad_cols(w[:, 6160:6176], 0, HEAD)], 1),
                    conv8=_taps8(full['gdn_conv_w'][j]), a_log=_pad_lanes(full['gdn_a_log'][j]),
                    dt_bias=_pad_lanes(full['gdn_dt_bias'][j]), norm_w=full['gdn_norm_w'][j].reshape(1, HEAD),
                    w_out=gathered_w('gdn_w_out', j))

    def sc_params():
        w = gathered_w('sc_w_in', 0)
        return dict(w_h=w[:, :2048], w_b=w[:, 2048:4096], w_c=w[:, 4096:6144], w_z=w[:, 6144:],
                    conv8=_taps8(full['sc_conv_w'][0]), w_out=gathered_w('sc_w_out', 0))

    def ssd_params():
        w = gathered_w('ssd_w_in', 0)
        return dict(w_z=w[:, :D_INNER], w_xbc=w[:, D_INNER:D_INNER + SSD_CONV_DIM],
                    w_dt=_pad_cols(w[:, D_INNER + SSD_CONV_DIM:], 0, HEAD),
                    conv8=_taps8(full['ssd_conv_w'][0], full['ssd_conv_b'][0]), a_log=_pad_lanes(full['ssd_a_log'][0]),
                    dt_bias=_pad_lanes(full['ssd_dt_bias'][0]),
                    dskip=jnp.repeat(full['ssd_d_skip'][0], 64).reshape(1, D_INNER),
                    norm_w=full['ssd_norm_w'][0].reshape(1, D_INNER), w_out=gathered_w('ssd_w_out', 0))

    layers = [("gdn", _gdn_forward, _gdn_backward, lambda: gdn_params(0)), ("sc", _sc_forward, _sc_backward, sc_params),
              ("ssd", _ssd_forward, _ssd_backward, ssd_params), ("gdn", _gdn_forward, _gdn_backward, lambda: gdn_params(1))]

    acts, acts_mm, ys, saved, params = [x0], [x0.astype(MM_DTYPE)], [], [], []
    for i, (kind, fwd, _, make_params) in enumerate(layers):
        params.append(make_params())
        for matmul_name, n, j in riders.get(i, ()):
            car.put(matmul_name, (n, j), shard(n, j), False)
        y, s = fwd(acts_mm[-1], params[i], "l%d_%s" % (i, kind), car.matmul)
        out, out_mm = _ln_fwd(acts[-1], y, full['ln_g'][i].reshape(1, -1), full['ln_b'][i].reshape(1, -1), "l%d_ln" % i)
        acts.append(out)
        acts_mm.append(out_mm)
        ys.append(y)
        saved.append(s)
    dact, loss_acc = _loss_head(acts[-1], target, "loss_head")
    loss = lax.psum(0.5 / D_MODEL * jnp.sum(loss_acc[0]), ("x", "y", "c"))

    grad_riders = {3: dict(w_out="l3_gdn_bwd_w_qkv", w_in="l3_gdn_bwd_dx_qkv"),
                   2: dict(w_out="l2_ssd_bwd_w_z", w_in="l2_ssd_bwd_dx_xbc"),
                   1: dict(w_out="l1_sc_bwd_w_h", w_in="l0_gdn_bwd_w_qkv"),
                   0: dict(w_out="l0_gdn_bwd_w_z", w_in="l0_gdn_bwd_dx_qkv")}

    def shipper(i):
        def ship(key, g):
            slabs = _cols_to_slabs(g[None]) if key == 'w_in' else _rows_to_slabs(g[None])
            car.put(grad_riders[i][key], ('grad', i, key), slabs.astype(MM_DTYPE), True)
        return ship

    lg = [None] * DEPTH
    d_ln_g, d_ln_b = [None] * DEPTH, [None] * DEPTH
    for i in reversed(range(DEPTH)):
        kind, _, bwd, _ = layers[i]
        du, du_mm, acc = _ln_bwd(dact, acts[i], ys[i], full['ln_g'][i].reshape(1, -1), "l%d_ln_bwd" % i)
        d_ln_g[i], d_ln_b[i] = acc[0], acc[1]
        dact, lg[i] = bwd(acts_mm[i], du, du_mm, params[i], saved[i], "l%d_%s" % (i, kind), car.matmul, shipper(i))
    assert not car.jobs, car.jobs
    grad_x = dact[None]

    stack = lambda k: jnp.stack([lg[0][k], lg[3][k]])
    local = {
        'gdn_conv_w': stack('conv_w'), 'gdn_a_log': stack('a_log'),
        'gdn_dt_bias': stack('dt_bias'), 'gdn_norm_w': stack('norm_w'),
        'sc_conv_w': lg[1]['conv_w'][None],
        'ssd_conv_w': lg[2]['conv_w'][None], 'ssd_conv_b': lg[2]['conv_b'][None],
        'ssd_a_log': lg[2]['a_log'][None], 'ssd_dt_bias': lg[2]['dt_bias'][None], 'ssd_d_skip': lg[2]['d_skip'][None],
        'ssd_norm_w': lg[2]['norm_w'][None],
        'ln_g': jnp.stack(d_ln_g), 'ln_b': jnp.stack(d_ln_b)}

    out = {}
    layers_of = {'gdn': (0, 3), 'sc': (1,), 'ssd': (2,)}
    for n in _BIG:
        kind, key = n.split('_', 1)
        recv = jnp.concatenate([car.got['grad', i, key] for i in layers_of[kind]], axis=1)
        shp = wts[n].shape
        r, c = shp[0] * shp[1], shp[2]
        res = _adamw(wts[n].reshape(r, c), recv.reshape(N_DEV, r, c), mom[n].reshape(r, c), vel[n].reshape(r, c),
                     "adamw_" + n)
        out[n] = [a.reshape(shp) for a in res]
    full_shapes = [local[n].shape for n in _SMALL]
    gathered = _exchange(_pack([local[n] for n in _SMALL]), "gather_small_grads", slabs=False)
    gs = []
    for n, g in zip(_SMALL, _unpack(gathered, full_shapes, lead=1)):
        if n in _SMALL_SHARDED:
            width = wts[n].shape[-1]
            g = lax.dynamic_slice_in_dim(g, me * width, width, axis=g.ndim - 1)
        gs.append(g)
    shapes = [wts[n].shape for n in _SMALL]
    res = _adamw(_pack([wts[n] for n in _SMALL]), _pack(gs, lead=1), _pack([mom[n] for n in _SMALL]),
                 _pack([vel[n] for n in _SMALL]), "adamw_small")
    for k, n in enumerate(_SMALL):
        out[n] = [_unpack(a, shapes)[k] for a in res]

    return (loss, grad_x, *[out[n][0] for n in _WEIGHTS], *[out[n][1] for n in _WEIGHTS],
            *[out[n][2] for n in _WEIGHTS], *[out[n][3] for n in _WEIGHTS])
```

```python
import functools
import math

import jax
import jax.numpy as jnp
from jax import lax
from jax.experimental import pallas as pl
from jax.experimental.pallas import tpu as pltpu

F32 = jnp.float32
MM_DTYPE = jnp.bfloat16

N_DEV = 8
D_MODEL = 1024
D_INNER = 2048
CHUNK = 64
HEAD = 128
GDN_V_HEADS = 16
GDN_GROUP = 16
GDN_QK_HEADS = 8
GDN_QK_DIM = 1024
GDN_CONV_DIM = 4096
SSD_PAIRS = 16
SSD_GROUPS = 4
SSD_STATE = 128
SSD_CONV_DIM = 3072
DEPTH = 4
ALPHA = (2 * DEPTH) ** 0.25
RMS_EPS = 1e-6
LN_EPS = 1e-5
L2_EPS = 1e-6
ADAM_LR, ADAM_B1, ADAM_B2, ADAM_EPS, ADAM_WD, ADAM_STEP = 0.001, 0.9, 0.999, 1e-08, 0.01, 10

VMEM_LIMIT_BYTES = 48 * 1024 * 1024
NEG_BIG = -1e30

_NN = (((1,), (0,)), ((), ()))
_NT = (((1,), (1,)), ((), ()))
_TN = (((0,), (0,)), ((), ()))


def _mm(a, b, dims):
    return lax.dot_general(a.astype(MM_DTYPE), b.astype(MM_DTYPE), dims, preferred_element_type=F32)


def _mmx(a, b, dims):
    return lax.dot_general(a, b, dims, precision=lax.Precision.HIGHEST, preferred_element_type=F32)


def _iota(shape, dim):
    return lax.broadcasted_iota(jnp.int32, shape, dim)


def _eye(n):
    return (_iota((n, n), 0) == _iota((n, n), 1)).astype(F32)


def _sig(x):
    return jax.nn.sigmoid(x)


def _silu(x):
    return x * _sig(x)


def _dsilu(x):
    s = _sig(x)
    return s * (1.0 + x * (1.0 - s))


def _softplus(x):
    return jnp.maximum(x, 0.0) + jnp.log(1.0 + jnp.exp(-jnp.abs(x)))


def _col(x, h):
    return jnp.sum(jnp.where(_iota(x.shape, 1) == h, x, 0.0), axis=1, keepdims=True)


def _row(x, h):
    return jnp.sum(jnp.where(_iota(x.shape, 0) == h, x, 0.0), axis=0, keepdims=True)


def _put_col(acc, col, h):
    return jnp.where(_iota(acc.shape, 1) == h, col, acc)


def _put_row(acc, row, h):
    return jnp.where(_iota(acc.shape, 0) == h, row, acc)


def _put_sub(acc, row, j):
    return acc + jnp.where(_iota(acc.shape, 0) == j, row, 0.0)


def _lanes(h):
    return pl.ds(h * HEAD, HEAD) if isinstance(h, int) else pl.ds(pl.multiple_of(h * HEAD, HEAD), HEAD)


def _total(x):
    return jnp.sum(jnp.sum(x, axis=0, keepdims=True), axis=1, keepdims=True)


def _call(body, name, grid, in_specs, out_specs, out_shape, scratch_shapes=(), semantics=None):
    return pl.pallas_call(
        body, name=name, grid=grid, in_specs=in_specs, out_specs=out_specs, out_shape=out_shape,
        scratch_shapes=list(scratch_shapes),
        compiler_params=pltpu.CompilerParams(dimension_semantics=semantics, vmem_limit_bytes=VMEM_LIMIT_BYTES))


def _tile(n, pref):
    if n <= pref:
        return n
    t = pref
    while n % t:
        t -= 128
    return t


def _exchange_copies(src_ref, out_ref, send_sems, recv_sems, local_sem, slabs):
    x, y, c = lax.axis_index("x"), lax.axis_index("y"), lax.axis_index("c")
    me = 4 * x + 2 * y + c
    mine = src_ref.at[me] if slabs else src_ref
    local = pltpu.make_async_copy(mine, out_ref.at[me], local_sem)
    sends, recvs = [], []
    for r in range(1, N_DEV):
        px = 1 - x if r & 4 else x
        py = 1 - y if r & 2 else y
        pc = 1 - c if r & 1 else c
        peer = 4 * px + 2 * py + pc
        kw = dict(send_sem=send_sems.at[r - 1], recv_sem=recv_sems.at[r - 1], device_id=(px, py, pc),
                  device_id_type=pl.DeviceIdType.MESH)
        sends.append(pltpu.make_async_remote_copy(src_ref=src_ref.at[peer] if slabs else src_ref,
                                                  dst_ref=out_ref.at[me], **kw))
        recvs.append(pltpu.make_async_remote_copy(src_ref=mine, dst_ref=out_ref.at[peer], **kw))
    return local, sends, recvs


def _gather_copies(src_ref, out_ref, send_sems, recv_sems, local_sem):
    x, y, c = lax.axis_index("x"), lax.axis_index("y"), lax.axis_index("c")
    chips = [(1 - x, y), (x, 1 - y), (1 - x, 1 - y)]
    sibling = (x, y, 1 - c)

    def slot(px, py, pc):
        return out_ref.at[4 * px + 2 * py + pc]

    def copy(k, src, dst, to):
        return pltpu.make_async_remote_copy(src_ref=src, dst_ref=dst, send_sem=send_sems.at[k], recv_sem=recv_sems.at[k],
                                            device_id=to, device_id_type=pl.DeviceIdType.MESH)

    mine = slot(x, y, c)
    local = pltpu.make_async_copy(src_ref, mine, local_sem)
    first = [copy(0, src_ref, mine, sibling)] + [copy(1 + j, src_ref, mine, (*chip, c)) for j, chip in enumerate(chips)]
    landed = [copy(1 + j, src_ref, slot(*chip, c), sibling) for j, chip in enumerate(chips)]
    passed = [copy(4 + j, slot(*chip, c), slot(*chip, c), sibling) for j, chip in enumerate(chips)]
    from_sibling = [copy(0, src_ref, slot(x, y, 1 - c), sibling)] + \
        [copy(4 + j, src_ref, slot(*chip, 1 - c), sibling) for j, chip in enumerate(chips)]
    return local, first, landed, passed, from_sibling


def _exchange_start(*refs, slabs):
    if not slabs:
        local, first = _gather_copies(*refs)[:2]
        local.start()
        for cp in first:
            cp.start()
        return
    local, sends, _ = _exchange_copies(*refs, slabs=slabs)
    local.start()
    for cp in sends:
        cp.start()


def _exchange_wait(*refs, slabs):
    if not slabs:
        local, first, landed, passed, from_sibling = _gather_copies(*refs)
        for arrived, onward in zip(landed, passed):
            arrived.wait_recv()
            onward.start()
        for cp in from_sibling:
            cp.wait_recv()
        for cp in first + passed:
            cp.wait_send()
        local.wait()
        return
    local, sends, recvs = _exchange_copies(*refs, slabs=slabs)
    for cp in recvs:
        cp.wait_recv()
    for cp in sends:
        cp.wait_send()
    local.wait()


def _exchange_sems():
    return [pltpu.SemaphoreType.DMA((N_DEV - 1,)), pltpu.SemaphoreType.DMA((N_DEV - 1,)), pltpu.SemaphoreType.DMA(())]


def _exchange_shape(src, slabs):
    return jax.ShapeDtypeStruct((N_DEV,) + tuple(src.shape[1:] if slabs else src.shape), src.dtype)


def _exchange(src, name, slabs):
    def body(*refs):
        _exchange_start(*refs, slabs=slabs)
        _exchange_wait(*refs, slabs=slabs)

    return pl.pallas_call(
        body, name=name,
        in_specs=[pl.BlockSpec(memory_space=pl.ANY)], out_specs=pl.BlockSpec(memory_space=pl.ANY),
        out_shape=_exchange_shape(src, slabs), scratch_shapes=_exchange_sems(),
    )(src)


MM_TM, MM_TN, MM_TK = 1024, 1024, 1024


def _matmul(a, b, mode, name, add=None, add_scale=1.0, carry=None):
    if mode == "nn":
        (m, k), (_, n) = a.shape, b.shape
    elif mode == "nt":
        (m, k), (n, _) = a.shape, b.shape
    else:
        (k, m), (_, n) = a.shape, b.shape
    tm, tn, tk = _tile(m, 2 * MM_TM if mode == "nn" and add is None else MM_TM), _tile(n, MM_TN), _tile(k, MM_TK)
    nk = k // tk
    grid = (m // tm, n // tn, nk)
    dims = {"nn": _NN, "nt": _NT, "tn": _TN}[mode]
    n_in = 2 + (add is not None)

    def body(*refs):
        a_ref, b_ref, o_ref = refs[0], refs[1], refs[n_in + (carry is not None)]
        if carry is not None:
            ex = (refs[n_in], refs[n_in + 2]) + tuple(refs[n_in + 3:])
            step = (pl.program_id(0) * grid[1] + pl.program_id(1)) * grid[2] + pl.program_id(2)

            @pl.when(step == 0)
            def _():
                _exchange_start(*ex, slabs=carry[1])

        part = _mm(a_ref[...], b_ref[...], dims)
        first = part if add is None else part + add_scale * refs[2][...]
        if nk == 1:
            o_ref[...] = first
        else:
            @pl.when(pl.program_id(2) == 0)
            def _():
                o_ref[...] = first

            @pl.when(pl.program_id(2) > 0)
            def _():
                o_ref[...] += part

        if carry is not None:
            @pl.when(step == grid[0] * grid[1] * grid[2] - 1)
            def _():
                _exchange_wait(*ex, slabs=carry[1])

    if mode == "nn":
        specs = [pl.BlockSpec((tm, tk), lambda i, j, q: (i, q)), pl.BlockSpec((tk, tn), lambda i, j, q: (q, j))]
    elif mode == "nt":
        specs = [pl.BlockSpec((tm, tk), lambda i, j, q: (i, q)), pl.BlockSpec((tn, tk), lambda i, j, q: (j, q))]
    else:
        specs = [pl.BlockSpec((tk, tm), lambda i, j, q: (q, i)), pl.BlockSpec((tk, tn), lambda i, j, q: (q, j))]
    out_spec = pl.BlockSpec((tm, tn), lambda i, j, q: (i, j))
    args = [a, b]
    if add is not None:
        specs.append(out_spec)
        args.append(add)
    out_shape = jax.ShapeDtypeStruct((m, n), F32)
    if carry is None:
        return _call(body, name, grid, specs, out_spec, out_shape, semantics=("parallel", "parallel", "arbitrary"))(*args)
    hbm = pl.BlockSpec(memory_space=pl.ANY)
    return _call(body, name, grid, specs + [hbm], [out_spec, hbm], [out_shape, _exchange_shape(*carry)],
                 _exchange_sems(), ("arbitrary", "arbitrary", "arbitrary"))(*args, carry[0])


CONV_TB = 512
CONV_CB = 1024
HALO = 8


def _conv_specs(t, cb_n):
    tb = min(CONV_TB, t)
    nb = tb // HALO
    blk = pl.BlockSpec((tb, cb_n), lambda c, i: (i, c))
    prev = pl.BlockSpec((HALO, cb_n), lambda c, i: (jnp.maximum(i * nb - 1, 0), c))
    nxt = pl.BlockSpec((HALO, cb_n), lambda c, i: (jnp.minimum((i + 1) * nb, t // HALO - 1), c))
    w = pl.BlockSpec((8, cb_n), lambda c, i: (0, c))
    return blk, prev, nxt, w


def _shift_down(ext, s, tb):
    return (pltpu.roll(ext, s, 0) if s else ext)[HALO:HALO + tb]


def _shift_up(ext, s, tb):
    n = ext.shape[0]
    return (pltpu.roll(ext, n - s, 0) if s else ext)[0:tb]


def _conv_fwd(u, w8, ktaps, name, u2=None, bias=False):
    t, ch = u.shape
    cb_n = min(CONV_CB, ch)
    tb = min(CONV_TB, t)
    two = u2 is not None

    def body(*refs):
        if two:
            u_ref, up_ref, v_ref, vp_ref, w_ref, o_ref = refs
        else:
            u_ref, up_ref, w_ref, o_ref = refs
        first = pl.program_id(1) == 0
        blk, halo = u_ref[...], up_ref[...]
        if two:
            blk, halo = blk * v_ref[...], halo * vp_ref[...]
        ext = jnp.concatenate([jnp.where(first, 0.0, halo), blk], axis=0)
        acc = jnp.zeros((tb, cb_n), F32)
        for j in range(ktaps):
            acc = acc + w_ref[j:j + 1, :] * _shift_down(ext, ktaps - 1 - j, tb)
        if bias:
            acc = acc + w_ref[ktaps:ktaps + 1, :]
        o_ref[...] = acc

    blk, prev, _, wspec = _conv_specs(t, cb_n)
    specs, args = [blk, prev], [u, u]
    if two:
        specs += [blk, prev]
        args += [u2, u2]
    specs.append(wspec)
    args.append(w8)
    return _call(body, name, (ch // cb_n, t // tb), specs, blk, jax.ShapeDtypeStruct((t, ch), F32),
                 semantics=("parallel", "parallel"))(*args)


def _conv_bwd(dc, u, w8, ktaps, name, u2=None):
    t, ch = u.shape
    cb_n = min(CONV_CB, ch)
    tb = min(CONV_TB, t)
    two = u2 is not None

    def body(*refs):
        if two:
            dc_ref, dn_ref, u_ref, up_ref, v_ref, vp_ref, w_ref, du_ref, dv_ref, dw_ref = refs
        else:
            dc_ref, dn_ref, u_ref, up_ref, w_ref, du_ref, dw_ref = refs
        i = pl.program_id(1)
        first, last = i == 0, i == t // tb - 1
        d = dc_ref[...]
        dext = jnp.concatenate([d, jnp.where(last, 0.0, dn_ref[...])], axis=0)
        blk, halo = u_ref[...], up_ref[...]
        if two:
            blk, halo = blk * v_ref[...], halo * vp_ref[...]
        uext = jnp.concatenate([jnp.where(first, 0.0, halo), blk], axis=0)
        du = jnp.zeros((tb, cb_n), F32)
        dw = jnp.zeros((8, cb_n), F32)
        for j in range(ktaps):
            s = ktaps - 1 - j
            du = du + w_ref[j:j + 1, :] * _shift_up(dext, s, tb)
            dw = _put_sub(dw, jnp.sum(d * _shift_down(uext, s, tb), axis=0, keepdims=True), j)
        dw = _put_sub(dw, jnp.sum(d, axis=0, keepdims=True), ktaps)
        if two:
            du_ref[...] = (du * v_ref[...]).astype(du_ref.dtype)
            dv_ref[...] = (du * u_ref[...]).astype(dv_ref.dtype)
        else:
            du_ref[...] = du.astype(du_ref.dtype)

        @pl.when(first)
        def _():
            dw_ref[...] = jnp.zeros_like(dw_ref)

        dw_ref[...] += dw

    blk, prev, nxt, wspec = _conv_specs(t, cb_n)
    specs, args = [blk, nxt, blk, prev], [dc, dc, u, u]
    if two:
        specs += [blk, prev]
        args += [u2, u2]
    specs.append(wspec)
    args.append(w8)
    act = jax.ShapeDtypeStruct((t, ch), MM_DTYPE)
    outs = ([blk, blk, wspec], [act, act, jax.ShapeDtypeStruct((8, ch), F32)]) if two else \
        ([blk, wspec], [act, jax.ShapeDtypeStruct((8, ch), F32)])
    return _call(body, name, (ch // cb_n, t // tb), specs, outs[0], outs[1],
                 semantics=("parallel", "arbitrary"))(*args)


EW_TB = 256


def _chunk_mask(n, upper):
    i, j = _iota((n, n), 0), _iota((n, n), 1)
    same = jnp.right_shift(i, 6) == jnp.right_shift(j, 6)
    return (same & ((j >= i) if upper else (i >= j))).astype(F32)


def _rows(width, tb=EW_TB):
    return pl.BlockSpec((tb, width), lambda i: (i, 0))


def _const(rows, width):
    return pl.BlockSpec((rows, width), lambda i: (0, 0))


def _gdn_ew_fwd(c, ba, a_log, dt_bias, name):
    t = c.shape[0]
    tb = EW_TB

    def body(c_ref, ba_ref, al_ref, db_ref, q_ref, k_ref, v_ref, beta_ref, gc_ref):
        for h in range(GDN_QK_HEADS):
            for base, ref, scale in ((0, q_ref, HEAD ** -0.5), (GDN_QK_DIM, k_ref, 1.0)):
                s = _silu(c_ref[:, base + h * HEAD: base + (h + 1) * HEAD])
                r = lax.rsqrt(jnp.sum(s * s, axis=1, keepdims=True) + L2_EPS)
                ref[:, h * HEAD:(h + 1) * HEAD] = s * (r * scale)
        v_ref[...] = _silu(c_ref[:, 2 * GDN_QK_DIM:])
        beta_ref[...] = _sig(ba_ref[:, :HEAD])
        g = -jnp.exp(al_ref[...]) * _softplus(ba_ref[:, HEAD:] + db_ref[...])
        gc_ref[...] = _mmx(_chunk_mask(tb, False), g, _NN)

    act = lambda w: jax.ShapeDtypeStruct((t, w), F32)
    return _call(body, name, (t // tb,),
                 [_rows(GDN_CONV_DIM), _rows(2 * HEAD), _const(1, HEAD), _const(1, HEAD)],
                 [_rows(GDN_QK_DIM), _rows(GDN_QK_DIM), _rows(D_INNER), _rows(HEAD), _rows(HEAD)],
                 [act(GDN_QK_DIM), act(GDN_QK_DIM), act(D_INNER), act(HEAD), act(HEAD)],
                 semantics=("parallel",))(c, ba, a_log, dt_bias)


def _gdn_ew_bwd(c, ba, a_log, dt_bias, dqh, dkh, dv, dbeta, dgc, name):
    t = c.shape[0]
    tb = EW_TB

    def body(c_ref, ba_ref, al_ref, db_ref, dq_ref, dk_ref, dv_ref, dbeta_ref, dgc_ref, dc_ref, dba_ref, acc_ref):
        for h in range(GDN_QK_HEADS):
            for base, ref, scale in ((0, dq_ref, HEAD ** -0.5), (GDN_QK_DIM, dk_ref, 1.0)):
                cq = c_ref[:, base + h * HEAD: base + (h + 1) * HEAD]
                s = _silu(cq)
                r = lax.rsqrt(jnp.sum(s * s, axis=1, keepdims=True) + L2_EPS)
                dn = ref[:, h * HEAD:(h + 1) * HEAD] * scale
                ds = r * dn - s * (r * r * r) * jnp.sum(dn * s, axis=1, keepdims=True)
                dc_ref[:, base + h * HEAD: base + (h + 1) * HEAD] = ds * _dsilu(cq)
        dc_ref[:, 2 * GDN_QK_DIM:] = dv_ref[...] * _dsilu(c_ref[:, 2 * GDN_QK_DIM:])
        beta = _sig(ba_ref[:, :HEAD])
        dba_ref[:, :HEAD] = (dbeta_ref[...] * beta * (1.0 - beta)).astype(dba_ref.dtype)
        pre = ba_ref[:, HEAD:] + db_ref[...]
        ea = jnp.exp(al_ref[...])
        g = -ea * _softplus(pre)
        dg = _mmx(_chunk_mask(tb, True), dgc_ref[...], _NN)
        da_raw = dg * (-ea) * _sig(pre)
        dba_ref[:, HEAD:] = da_raw.astype(dba_ref.dtype)
        acc = jnp.zeros((8, HEAD), F32)
        acc = _put_sub(acc, jnp.sum(dg * g, axis=0, keepdims=True), 0)
        acc = _put_sub(acc, jnp.sum(da_raw, axis=0, keepdims=True), 1)

        @pl.when(pl.program_id(0) == 0)
        def _():
            acc_ref[...] = jnp.zeros_like(acc_ref)

        acc_ref[...] += acc

    act = lambda w: jax.ShapeDtypeStruct((t, w), F32)
    return _call(body, name, (t // tb,),
                 [_rows(GDN_CONV_DIM), _rows(2 * HEAD), _const(1, HEAD), _const(1, HEAD),
                  _rows(GDN_QK_DIM), _rows(GDN_QK_DIM), _rows(D_INNER), _rows(HEAD), _rows(HEAD)],
                 [_rows(GDN_CONV_DIM), _rows(2 * HEAD), _const(8, HEAD)],
                 [act(GDN_CONV_DIM), jax.ShapeDtypeStruct((t, 2 * HEAD), MM_DTYPE), jax.ShapeDtypeStruct((8, HEAD), F32)],
                 semantics=("arbitrary",))(c, ba, a_log, dt_bias, dqh, dkh, dv, dbeta, dgc)


def _zip(fn, *lists):
    return [fn(*xs) for xs in zip(*lists)]


def _mms(xs, ys, dims):
    return [_mm(x, y, dims) for x, y in zip(xs, ys)]


def _side_by_side(a, b):
    return jnp.concatenate([a, b], axis=1)


def _interleave(*gens):
    results, live = [None] * len(gens), list(range(len(gens)))
    while live:
        for i in list(live):
            try:
                next(gens[i])
            except StopIteration as stop:
                results[i] = stop.value
                live.remove(i)
    return results


def _gdn_local_stages(q, k, v, bcol, gcol, grow, glast):
    ii, jj = _iota((CHUNK, CHUNK), 0), _iota((CHUNK, CHUNK), 1)
    eye = _eye(CHUNK)
    mul = lambda x, y: x * y
    eg = [jnp.exp(g) for g in gcol]
    decay = _zip(lambda gc, gr: jnp.exp(jnp.where(ii >= jj, gc - gr, NEG_BIG)), gcol, grow)
    kb = _zip(mul, k, bcol)
    p, qk = _mms(kb, k, _NT), _mms(q, k, _NT)
    yield
    a = _zip(lambda x, d: jnp.where(ii > jj, x * d, 0.0), p, decay)
    inv, pw = [eye - x for x in a], a
    for _ in range(5):
        pw = _mms(pw, pw, _NN)
        yield
        inv = _zip(lambda x, y: x + y, inv, _mms(inv, pw, _NN))
        yield
    rv, rk = _zip(mul, v, bcol), _zip(mul, kb, eg)
    uw = _mms(inv, _zip(_side_by_side, rv, rk), _NN)
    u, w = [x[:, :HEAD] for x in uw], [x[:, HEAD:] for x in uw]
    yield
    att = _zip(mul, qk, decay)
    qd = _zip(mul, q, eg)
    ekt = _zip(lambda gl, gc: jnp.exp(gl - gc), glast, gcol)
    kt = _zip(mul, k, ekt)
    el = [jnp.exp(g) for g in glast]
    return dict(eg=eg, decay=decay, kb=kb, p=p, inv=inv, rv=rv, rk=rk, u=u, w=w, qk=qk, att=att, qd=qd, ekt=ekt, kt=kt,
                el=el)


def _gdn_state_stages(u, w, att, qd, kt, el, s_in):
    ws, qs = _mms(w, s_in, _NN), _mms(qd, s_in, _NN)
    yield
    vn = _zip(lambda x, y: x - y, u, ws)
    av, kv = _mms(att, vn, _NN), _mms(kt, vn, _TN)
    yield
    out = _zip(lambda x, y: x + y, qs, av)
    s_out = _zip(lambda s, e, y: s * e + y, s_in, el, kv)
    return dict(vn=vn, out=out, s_out=s_out)


def _gdn_heads_fwd(q, k, v, bcol, gcol, grow, glast, s_in):
    f, = _interleave(_gdn_local_stages(q, k, v, bcol, gcol, grow, glast))
    g, = _interleave(_gdn_state_stages(f["u"], f["w"], f["att"], f["qd"], f["kt"], f["el"], s_in))
    return {**f, **g}


def _head_groups(group, init):
    if GDN_GROUP == GDN_V_HEADS:
        return group(0, init)
    return lax.fori_loop(0, GDN_V_HEADS // GDN_GROUP, lambda gi, c: group(GDN_GROUP * gi, c), init)


def _half(h):
    return h // 2 if isinstance(h, int) else jnp.right_shift(h, 1)


def _gdn_chunk_fwd(qn, kn, v, beta, gc, z, norm_w, name):
    t = qn.shape[0]
    nc = t // CHUNK

    def body(q_ref, k_ref, v_ref, beta_ref, gc_ref, z_ref, nw_ref, o_ref, h_ref, st_ref, state):
        @pl.when(pl.program_id(0) == 0)
        def _():
            state[...] = jnp.zeros_like(state)

        st_ref[0] = state[...]
        gc_all, beta_all = gc_ref[...], beta_ref[...]
        gct = _mmx(_eye(HEAD), gc_all, _NT)
        glast_all = gc_ref[CHUNK - 1:CHUNK, :]
        nw = nw_ref[...]

        def group(h0, carry):
            heads = [h0 + s for s in range(GDN_GROUP)]
            f = _gdn_heads_fwd([q_ref[:, _lanes(_half(h))] for h in heads], [k_ref[:, _lanes(_half(h))] for h in heads],
                               [v_ref[:, _lanes(h)] for h in heads], [_col(beta_all, h) for h in heads],
                               [_col(gc_all, h) for h in heads], [_row(gct, h) for h in heads],
                               [_col(glast_all, h) for h in heads], [state[h] for h in heads])
            for h, s_out, o in zip(heads, f["s_out"], f["out"]):
                state[h] = s_out
                o_ref[:, _lanes(h)] = o
                rstd = lax.rsqrt(jnp.mean(o * o, axis=1, keepdims=True) + RMS_EPS)
                h_ref[:, _lanes(h)] = (o * rstd * nw * _silu(z_ref[:, _lanes(h)])).astype(h_ref.dtype)
            return carry

        _head_groups(group, 0)

    rows = lambda w: pl.BlockSpec((CHUNK, w), lambda i: (i, 0))
    act = lambda w: jax.ShapeDtypeStruct((t, w), F32)
    return _call(body, name, (nc,),
                 [rows(GDN_QK_DIM), rows(GDN_QK_DIM), rows(D_INNER), rows(HEAD), rows(HEAD), rows(D_INNER),
                  _const(1, HEAD)],
                 [rows(D_INNER), rows(D_INNER), pl.BlockSpec((1, GDN_V_HEADS, HEAD, HEAD), lambda i: (i, 0, 0, 0))],
                 [act(D_INNER), jax.ShapeDtypeStruct((t, D_INNER), MM_DTYPE),
                  jax.ShapeDtypeStruct((nc, GDN_V_HEADS, HEAD, HEAD), F32)],
                 [pltpu.VMEM((GDN_V_HEADS, HEAD, HEAD), F32)], ("arbitrary",))(qn, kn, v, beta, gc, z, norm_w)


def _gdn_chunk_bwd(qn, kn, v, beta, gc, z, norm_w, o, states, dh, name):
    t = qn.shape[0]
    nc = t // CHUNK

    def body(q_ref, k_ref, v_ref, beta_ref, gc_ref, z_ref, nw_ref, o_ref, st_ref, dh_ref,
             dq_ref, dk_ref, dv_ref, dz_ref, dbeta_ref, dgc_ref, acc_ref, dstate):
        @pl.when(pl.program_id(0) == 0)
        def _():
            dstate[...] = jnp.zeros_like(dstate)
            acc_ref[...] = jnp.zeros_like(acc_ref)

        gc_all, beta_all = gc_ref[...], beta_ref[...]
        gct = _mmx(_eye(HEAD), gc_all, _NT)
        glast_all = gc_ref[CHUNK - 1:CHUNK, :]
        nw = nw_ref[...]
        ii, jj = _iota((CHUNK, CHUNK), 0), _iota((CHUNK, CHUNK), 1)
        last_row = _iota((CHUNK, 1), 0) == CHUNK - 1

        def group(h0, carry):
            dbeta_acc, dgc_acc, dgrow_acc, dnw_acc = carry
            heads = [h0 + s for s in range(GDN_GROUP)]
            mul, add, sub = (lambda x, y: x * y), (lambda x, y: x + y), (lambda x, y: x - y)
            rowsum = lambda x, y: jnp.sum(x * y, axis=1, keepdims=True)
            q, k = [q_ref[:, _lanes(_half(h))] for h in heads], [k_ref[:, _lanes(_half(h))] for h in heads]
            vv = [v_ref[:, _lanes(h)] for h in heads]
            bcol, gcol = [_col(beta_all, h) for h in heads], [_col(gc_all, h) for h in heads]
            s_in, dsn = [st_ref[0, h] for h in heads], [dstate[h] for h in heads]
            f = _gdn_heads_fwd(q, k, vv, bcol, gcol, [_row(gct, h) for h in heads],
                               [_col(glast_all, h) for h in heads], s_in)
            do = []
            for h in heads:
                oo, zz, dhh = o_ref[:, _lanes(h)], z_ref[:, _lanes(h)], dh_ref[:, _lanes(h)]
                rstd = lax.rsqrt(jnp.mean(oo * oo, axis=1, keepdims=True) + RMS_EPS)
                on, sz = oo * rstd, _silu(zz)
                dnw_acc = dnw_acc + jnp.sum(dhh * on * sz, axis=0, keepdims=True)
                dz_ref[:, _lanes(h)] = (dhh * on * nw * _dsilu(zz)).astype(dz_ref.dtype)
                don = dhh * nw * sz
                do.append(rstd * (don - on * jnp.mean(don * on, axis=1, keepdims=True)))
            decay, eg, inv = f["decay"], f["eg"], f["inv"]
            d_glast = _zip(lambda d, s, e: _total(d * s) * e, dsn, s_in, f["el"])
            dkt = _mms(f["vn"], dsn, _NT)
            dvn = _mms(f["kt"], dsn, _NN)
            dqd = _mms(do, s_in, _NT)
            ds_prev = _zip(lambda d, e, y: d * e + y, dsn, f["el"], _mms(f["qd"], do, _TN))
            datt = _mms(do, f["vn"], _NT)
            dvn = _zip(add, dvn, _mms(f["att"], do, _TN))
            dqk = _zip(mul, datt, decay)
            dq = _zip(lambda x, e, y: x * e + y, dqd, eg, _mms(dqk, k, _NN))
            dk = _mms(dqk, q, _TN)
            ddecay = _zip(mul, datt, f["qk"])
            dgcol = _zip(rowsum, dqd, f["qd"])
            dw = [-x for x in _mms(dvn, s_in, _NT)]
            ds_prev = _zip(sub, ds_prev, _mms(f["w"], dvn, _TN))
            drv, drk = _mms(inv, dvn, _TN), _mms(inv, dw, _TN)
            da = [jnp.where(ii > jj, -x, 0.0) for x in
                  _mms(_zip(_side_by_side, drv, drk), _zip(_side_by_side, f["u"], f["w"]), _NT)]
            dp = _zip(mul, da, decay)
            ddecay = _zip(lambda x, y, z_: x + y * z_, ddecay, da, f["p"])
            dkb = _zip(lambda x, y, e: x + y * e, _mms(dp, k, _NN), drk, eg)
            dk = _zip(add, dk, _mms(dp, f["kb"], _TN))
            dbeta = _zip(add, _zip(rowsum, drv, vv), _zip(rowsum, dkb, k))
            dgcol = _zip(add, dgcol, _zip(rowsum, drk, f["rk"]))
            dk = _zip(lambda x, y, b_, z_, e: x + y * b_ + z_ * e, dk, dkb, bcol, dkt, f["ekt"])
            tail = _zip(mul, dkt, f["kt"])
            d_glast = _zip(lambda x, y: x + _total(y), d_glast, tail)
            e_ = _zip(mul, ddecay, decay)
            dgcol = _zip(lambda x, t_, e, gl: x - jnp.sum(t_, axis=1, keepdims=True) + jnp.sum(e, axis=1, keepdims=True)
                         + jnp.where(last_row, gl, 0.0), dgcol, tail, e_, d_glast)
            for i_ in range(0, len(heads), 2):
                dq_ref[:, _lanes(_half(heads[i_]))] = dq[i_] + dq[i_ + 1]
                dk_ref[:, _lanes(_half(heads[i_]))] = dk[i_] + dk[i_ + 1]
            for i_, h in enumerate(heads):
                dstate[h] = ds_prev[i_]
                dv_ref[:, _lanes(h)] = drv[i_] * bcol[i_]
                dbeta_acc = _put_col(dbeta_acc, dbeta[i_], h)
                dgc_acc = _put_col(dgc_acc, dgcol[i_], h)
                dgrow_acc = _put_row(dgrow_acc, -jnp.sum(e_[i_], axis=0, keepdims=True), h)
            return dbeta_acc, dgc_acc, dgrow_acc, dnw_acc

        zero = jnp.zeros((CHUNK, HEAD), F32)
        dbeta_acc, dgc_acc, dgrow_acc, dnw_acc = _head_groups(
            group, (zero, zero, jnp.zeros((HEAD, CHUNK), F32), jnp.zeros((1, HEAD), F32)))
        dbeta_ref[...] = dbeta_acc
        dgc_ref[...] = dgc_acc + _mmx(_eye(CHUNK), dgrow_acc, _NT)
        acc_ref[...] += _put_sub(jnp.zeros((8, HEAD), F32), dnw_acc, 0)

    rows = lambda w: pl.BlockSpec((CHUNK, w), lambda i: (nc - 1 - i, 0))
    act = lambda w: jax.ShapeDtypeStruct((t, w), F32)
    return _call(body, name, (nc,),
                 [rows(GDN_QK_DIM), rows(GDN_QK_DIM), rows(D_INNER), rows(HEAD), rows(HEAD), rows(D_INNER),
                  _const(1, HEAD), rows(D_INNER),
                  pl.BlockSpec((1, GDN_V_HEADS, HEAD, HEAD), lambda i: (nc - 1 - i, 0, 0, 0)), rows(D_INNER)],
                 [rows(GDN_QK_DIM), rows(GDN_QK_DIM), rows(D_INNER), rows(D_INNER), rows(HEAD), rows(HEAD), _const(8, HEAD)],
                 [act(GDN_QK_DIM), act(GDN_QK_DIM), act(D_INNER), jax.ShapeDtypeStruct((t, D_INNER), MM_DTYPE), act(HEAD),
                  act(HEAD), jax.ShapeDtypeStruct((8, HEAD), F32)],
                 [pltpu.VMEM((GDN_V_HEADS, HEAD, HEAD), F32)], ("arbitrary",)
                 )(qn, kn, v, beta, gc, z, norm_w, o, states, dh)


def _sc_gate_fwd(bg, cv, z, name):
    t, w = bg.shape

    def body(b_ref, c_ref, z_ref, o_ref):
        o_ref[...] = (b_ref[...] * c_ref[...] * _silu(z_ref[...])).astype(o_ref.dtype)

    return _call(body, name, (t // EW_TB,), [_rows(w)] * 3, _rows(w), jax.ShapeDtypeStruct((t, w), MM_DTYPE),
                 semantics=("parallel",))(bg, cv, z)


def _sc_gate_bwd(dh, bg, cv, z, name):
    t, w = bg.shape

    def body(d_ref, b_ref, c_ref, z_ref, db_ref, dc_ref, dz_ref):
        d, b, c, zz = d_ref[...], b_ref[...], c_ref[...], z_ref[...]
        sz = _silu(zz)
        db_ref[...] = (d * c * sz).astype(db_ref.dtype)
        dc_ref[...] = d * b * sz
        dz_ref[...] = (d * b * c * _dsilu(zz)).astype(dz_ref.dtype)

    act, act_mm = jax.ShapeDtypeStruct((t, w), F32), jax.ShapeDtypeStruct((t, w), MM_DTYPE)
    return _call(body, name, (t // EW_TB,), [_rows(w)] * 4, [_rows(w)] * 3, [act_mm, act, act_mm],
                 semantics=("parallel",))(dh, bg, cv, z)


XBC_B = D_INNER
XBC_C = D_INNER + SSD_GROUPS * SSD_STATE


def _ssd_scalars(dtp, dt_bias, a_log):
    dt = _softplus(dtp + dt_bias)
    a = -jnp.exp(a_log)
    da = dt * a
    ac = _mmx(_chunk_mask(CHUNK, False), da, _NN)
    act = _mmx(_eye(HEAD), ac, _NT)
    aclast = jnp.sum(jnp.where(_iota(ac.shape, 0) == CHUNK - 1, ac, 0.0), axis=0, keepdims=True)
    return dt, a, da, ac, act, aclast


def _ssd_pairs_fwd(x2, bg, cg, cb, dt, ac, act, aclast, s2):
    ii, jj = _iota((CHUNK, CHUNK), 0), _iota((CHUNK, CHUNK), 1)
    half = _iota((CHUNK, HEAD), 1) < 64
    causal = ii >= jj
    pairs = range(len(x2))
    mul = lambda x, y: x * y
    pick = lambda a, b: jnp.where(half, a, b)
    aca, acb = [_col(ac, 2 * p) for p in pairs], [_col(ac, 2 * p + 1) for p in pairs]
    la, lb = [_col(aclast, 2 * p) for p in pairs], [_col(aclast, 2 * p + 1) for p in pairs]
    dt2 = [pick(_col(dt, 2 * p), _col(dt, 2 * p + 1)) for p in pairs]
    xdt = _zip(mul, x2, dt2)
    sega = [jnp.exp(jnp.where(causal, aca[p] - _row(act, 2 * p), NEG_BIG)) for p in pairs]
    segb = [jnp.exp(jnp.where(causal, acb[p] - _row(act, 2 * p + 1), NEG_BIG)) for p in pairs]
    ma, mb = _zip(mul, sega, cb), _zip(mul, segb, cb)
    ydiag = _zip(pick, _mms(ma, xdt, _NN), _mms(mb, xdt, _NN))
    cdec = _zip(lambda a, b: pick(jnp.exp(a), jnp.exp(b)), aca, acb)
    cs = _mms(cg, s2, _NT)
    tail = _zip(lambda l1, a, l2, b: pick(jnp.exp(l1 - a), jnp.exp(l2 - b)), la, aca, lb, acb)
    zt = _zip(mul, xdt, tail)
    ea, eb = [jnp.exp(x) for x in la], [jnp.exp(x) for x in lb]
    tot = _zip(lambda a, b: jnp.where(_iota((HEAD, 1), 0) < 64, a, b), ea, eb)
    s_out = _zip(lambda s, t_, y: s * t_ + y, s2, tot, _mms(zt, bg, _TN))
    return dict(half=half, dt2=dt2, xdt=xdt, sega=sega, segb=segb, ma=ma, mb=mb, ydiag=ydiag, cdec=cdec, cs=cs,
                tail=tail, zt=zt, ea=ea, eb=eb, tot=tot, s_out=s_out)


def _ssd_group_inputs(cx_ref):
    cxb = [cx_ref[:, XBC_B + g * SSD_STATE: XBC_B + (g + 1) * SSD_STATE] for g in range(SSD_GROUPS)]
    cxc = [cx_ref[:, XBC_C + g * SSD_STATE: XBC_C + (g + 1) * SSD_STATE] for g in range(SSD_GROUPS)]
    bg, cg = [_silu(x) for x in cxb], [_silu(x) for x in cxc]
    return cxb, cxc, bg, cg, _mms(cg, bg, _NT)


def _per_pair(group_list):
    return [group_list[p // (SSD_PAIRS // SSD_GROUPS)] for p in range(SSD_PAIRS)]


def _ssd_chunk_fwd(cx, dtp, z, dt_bias, a_log, dskip, norm_w, name):
    t = cx.shape[0]
    nc = t // CHUNK
    gw = D_INNER // SSD_GROUPS

    def body(cx_ref, dtp_ref, z_ref, db_ref, al_ref, sk_ref, nw_ref, y_ref, h_ref, st_ref, state):
        @pl.when(pl.program_id(0) == 0)
        def _():
            state[...] = jnp.zeros_like(state)

        st_ref[0] = state[...]
        dt, _, _, ac, act, aclast = _ssd_scalars(dtp_ref[...], db_ref[...], al_ref[...])
        _, _, bg, cg, cb = _ssd_group_inputs(cx_ref)
        x2 = [_silu(cx_ref[:, _lanes(p)]) for p in range(SSD_PAIRS)]
        f = _ssd_pairs_fwd(x2, _per_pair(bg), _per_pair(cg), _per_pair(cb), dt, ac, act, aclast,
                           [state[p] for p in range(SSD_PAIRS)])
        for p in range(SSD_PAIRS):
            state[p] = f["s_out"][p]
            y_ref[:, _lanes(p)] = f["ydiag"][p] + f["cs"][p] * f["cdec"][p] + sk_ref[:, _lanes(p)] * x2[p]
        for g in range(SSD_GROUPS):
            sl = slice(g * gw, (g + 1) * gw)
            yg = y_ref[:, sl] * _silu(z_ref[:, sl])
            rstd = lax.rsqrt(jnp.mean(yg * yg, axis=1, keepdims=True) + RMS_EPS)
            h_ref[:, sl] = (yg * rstd * nw_ref[:, sl]).astype(h_ref.dtype)

    rows = lambda w: pl.BlockSpec((CHUNK, w), lambda i: (i, 0))
    act_ = lambda w: jax.ShapeDtypeStruct((t, w), F32)
    return _call(body, name, (nc,),
                 [rows(SSD_CONV_DIM), rows(HEAD), rows(D_INNER), _const(1, HEAD), _const(1, HEAD),
                  _const(1, D_INNER), _const(1, D_INNER)],
                 [rows(D_INNER), rows(D_INNER), pl.BlockSpec((1, SSD_PAIRS, HEAD, SSD_STATE), lambda i: (i, 0, 0, 0))],
                 [act_(D_INNER), jax.ShapeDtypeStruct((t, D_INNER), MM_DTYPE),
                  jax.ShapeDtypeStruct((nc, SSD_PAIRS, HEAD, SSD_STATE), F32)],
                 [pltpu.VMEM((SSD_PAIRS, HEAD, SSD_STATE), F32)], ("arbitrary",)
                 )(cx, dtp, z, dt_bias, a_log, dskip, norm_w)


def _ssd_chunk_bwd(cx, dtp, z, dt_bias, a_log, dskip, norm_w, y, states, dh, name):
    t = cx.shape[0]
    nc = t // CHUNK
    gw = D_INNER // SSD_GROUPS

    def body(cx_ref, dtp_ref, z_ref, db_ref, al_ref, sk_ref, nw_ref, y_ref, st_ref, dh_ref,
             dcx_ref, ddtp_ref, dz_ref, wide_ref, acc_ref, dstate, dy_s):
        @pl.when(pl.program_id(0) == 0)
        def _():
            dstate[...] = jnp.zeros_like(dstate)
            wide_ref[...] = jnp.zeros_like(wide_ref)
            acc_ref[...] = jnp.zeros_like(acc_ref)

        dtp = dtp_ref[...]
        dt, a, da, ac, act, aclast = _ssd_scalars(dtp, db_ref[...], al_ref[...])
        ii, jj = _iota((CHUNK, CHUNK), 0), _iota((CHUNK, CHUNK), 1)
        last_row = _iota((CHUNK, 1), 0) == CHUNK - 1
        top = _iota((HEAD, SSD_STATE), 0) < 64
        for g in range(SSD_GROUPS):
            sl = slice(g * gw, (g + 1) * gw)
            yy, zz, dhh, nw = y_ref[:, sl], z_ref[:, sl], dh_ref[:, sl], nw_ref[:, sl]
            sz = _silu(zz)
            yg = yy * sz
            rstd = lax.rsqrt(jnp.mean(yg * yg, axis=1, keepdims=True) + RMS_EPS)
            n = yg * rstd
            dn = dhh * nw
            dyg = rstd * (dn - n * jnp.mean(dn * n, axis=1, keepdims=True))
            dy_s[:, sl] = dyg * sz
            dz_ref[:, sl] = (dyg * yy * _dsilu(zz)).astype(dz_ref.dtype)
            wide_ref[0:1, sl] += jnp.sum(dhh * n, axis=0, keepdims=True)

        pairs = range(SSD_PAIRS)
        mul, add, sub = (lambda x, y: x * y), (lambda x, y: x + y), (lambda x, y: x - y)
        rowsum = lambda x: jnp.sum(x, axis=1, keepdims=True)
        cxb, cxc, bg, cg, cb = _ssd_group_inputs(cx_ref)
        bgp, cgp = _per_pair(bg), _per_pair(cg)
        cxx = [cx_ref[:, _lanes(p)] for p in pairs]
        x2 = [_silu(x) for x in cxx]
        s2, dsn = [st_ref[0, p] for p in pairs], [dstate[p] for p in pairs]
        dy2 = [dy_s[:, _lanes(p)] for p in pairs]
        f = _ssd_pairs_fwd(x2, bgp, cgp, _per_pair(cb), dt, ac, act, aclast, s2)
        half = f["half"]
        lo = lambda x: jnp.where(half, x, 0.0)
        dx2 = [dy2[p] * sk_ref[:, _lanes(p)] for p in pairs]
        for p in pairs:
            wide_ref[1:2, _lanes(p)] += jnp.sum(dy2[p] * x2[p], axis=0, keepdims=True)
        gg = _zip(mul, dy2, f["cdec"])
        dc_p = _mms(gg, s2, _NN)
        ds_prev = _zip(lambda d, t_, y: d * t_ + y, dsn, f["tot"], _mms(gg, cgp, _TN))
        t1 = _zip(lambda d, c, e: d * c * e, dy2, f["cs"], f["cdec"])
        dac_a = [rowsum(lo(x)) for x in t1]
        dac_b = _zip(lambda x, a_: rowsum(x) - a_, t1, dac_a)
        dya = [lo(x) for x in dy2]
        dma, dmb = _mms(dya, f["xdt"], _NT), _mms(_zip(sub, dy2, dya), f["xdt"], _NT)
        dxdt = _zip(lambda a_, b_: jnp.where(half, a_, b_), _mms(f["ma"], dy2, _TN), _mms(f["mb"], dy2, _TN))
        dcb_p = _zip(lambda a_, sa, b_, sb: a_ * sa + b_ * sb, dma, f["sega"], dmb, f["segb"])
        ea_, eb_ = _zip(mul, dma, f["ma"]), _zip(mul, dmb, f["mb"])
        dac_a = _zip(lambda x, e: x + rowsum(e), dac_a, ea_)
        dac_b = _zip(lambda x, e: x + rowsum(e), dac_b, eb_)
        dzt = _mms(bgp, dsn, _NT)
        db_p = _mms(f["zt"], dsn, _NN)
        dxdt = _zip(lambda x, d, t_: x + d * t_, dxdt, dzt, f["tail"])
        t2 = _zip(mul, dzt, f["zt"])
        t2a = [rowsum(lo(x)) for x in t2]
        t2b = _zip(lambda x, a_: rowsum(x) - a_, t2, t2a)
        t3 = _zip(mul, dsn, s2)
        t3a = [_total(x[:64]) for x in t3]
        dla = _zip(lambda x, y, e: _total(x) + y * e, t2a, t3a, f["ea"])
        dlb = _zip(lambda x, y, e: _total(x) + _total(y[64:]) * e, t2b, t3, f["eb"])
        dac_a = _zip(lambda x, y, l: x - y + jnp.where(last_row, l, 0.0), dac_a, t2a, dla)
        dac_b = _zip(lambda x, y, l: x - y + jnp.where(last_row, l, 0.0), dac_b, t2b, dlb)
        dx2 = _zip(lambda x, d, t_: x + d * t_, dx2, dxdt, f["dt2"])
        t4 = _zip(mul, dxdt, x2)
        t4a = [rowsum(lo(x)) for x in t4]
        t4b = _zip(lambda x, a_: rowsum(x) - a_, t4, t4a)
        zero = jnp.zeros((CHUNK, HEAD), F32)
        ddt_acc, dac_acc, drow_acc = zero, zero, jnp.zeros((HEAD, CHUNK), F32)
        for p in pairs:
            dcx_ref[:, _lanes(p)] = dx2[p] * _dsilu(cxx[p])
            dstate[p] = ds_prev[p]
            ddt_acc = _put_col(_put_col(ddt_acc, t4a[p], 2 * p), t4b[p], 2 * p + 1)
            dac_acc = _put_col(_put_col(dac_acc, dac_a[p], 2 * p), dac_b[p], 2 * p + 1)
            drow_acc = _put_row(_put_row(drow_acc, -jnp.sum(ea_[p], axis=0, keepdims=True), 2 * p),
                                -jnp.sum(eb_[p], axis=0, keepdims=True), 2 * p + 1)
        per = SSD_PAIRS // SSD_GROUPS
        gsum = lambda xs: [functools.reduce(add, xs[g * per:(g + 1) * per]) for g in range(SSD_GROUPS)]
        dcb = gsum(dcb_p)
        dc = _zip(add, gsum(dc_p), _mms(dcb, bg, _NN))
        db = _zip(add, gsum(db_p), _mms(dcb, cg, _TN))
        for g in range(SSD_GROUPS):
            dcx_ref[:, XBC_B + g * SSD_STATE: XBC_B + (g + 1) * SSD_STATE] = db[g] * _dsilu(cxb[g])
            dcx_ref[:, XBC_C + g * SSD_STATE: XBC_C + (g + 1) * SSD_STATE] = dc[g] * _dsilu(cxc[g])
        dac = dac_acc + _mmx(_eye(CHUNK), drow_acc, _NT)
        dda = _mmx(_chunk_mask(CHUNK, True), dac, _NN)
        ddt = ddt_acc + dda * a
        ddtp = ddt * _sig(dtp + db_ref[...])
        ddtp_ref[...] = ddtp.astype(ddtp_ref.dtype)
        acc = _put_sub(jnp.zeros((8, HEAD), F32), jnp.sum(dda * da, axis=0, keepdims=True), 0)
        acc_ref[...] += _put_sub(acc, jnp.sum(ddtp, axis=0, keepdims=True), 1)

    rows = lambda w: pl.BlockSpec((CHUNK, w), lambda i: (nc - 1 - i, 0))
    act_ = lambda w: jax.ShapeDtypeStruct((t, w), F32)
    return _call(body, name, (nc,),
                 [rows(SSD_CONV_DIM), rows(HEAD), rows(D_INNER), _const(1, HEAD), _const(1, HEAD),
                  _const(1, D_INNER), _const(1, D_INNER), rows(D_INNER),
                  pl.BlockSpec((1, SSD_PAIRS, HEAD, SSD_STATE), lambda i: (nc - 1 - i, 0, 0, 0)), rows(D_INNER)],
                 [rows(SSD_CONV_DIM), rows(HEAD), rows(D_INNER), _const(8, D_INNER), _const(8, HEAD)],
                 [act_(SSD_CONV_DIM), jax.ShapeDtypeStruct((t, HEAD), MM_DTYPE), jax.ShapeDtypeStruct((t, D_INNER), MM_DTYPE),
                  jax.ShapeDtypeStruct((8, D_INNER), F32),
                  jax.ShapeDtypeStruct((8, HEAD), F32)],
                 [pltpu.VMEM((SSD_PAIRS, HEAD, SSD_STATE), F32), pltpu.VMEM((CHUNK, D_INNER), F32)], ("arbitrary",)
                 )(cx, dtp, z, dt_bias, a_log, dskip, norm_w, y, states, dh)


LN_TB = 512


def _ln_stats(x, y):
    u = ALPHA * x + y
    mu = jnp.mean(u, axis=1, keepdims=True)
    cen = u - mu
    rstd = lax.rsqrt(jnp.mean(cen * cen, axis=1, keepdims=True) + LN_EPS)
    return cen * rstd


def _ln_fwd(x, y, g, b, name):
    t, d = x.shape

    def body(x_ref, y_ref, g_ref, b_ref, o_ref, omm_ref):
        out = _ln_stats(x_ref[...], y_ref[...]) * g_ref[...] + b_ref[...]
        o_ref[...] = out
        omm_ref[...] = out.astype(omm_ref.dtype)

    return _call(body, name, (t // LN_TB,), [_rows(d, LN_TB), _rows(d, LN_TB), _const(1, d), _const(1, d)],
                 [_rows(d, LN_TB)] * 2, [jax.ShapeDtypeStruct((t, d), F32), jax.ShapeDtypeStruct((t, d), MM_DTYPE)],
                 semantics=("parallel",))(x, y, g, b)


def _ln_bwd(dout, x, y, g, name):
    t, d = x.shape

    def body(d_ref, x_ref, y_ref, g_ref, du_ref, dumm_ref, acc_ref):
        u = ALPHA * x_ref[...] + y_ref[...]
        mu = jnp.mean(u, axis=1, keepdims=True)
        cen = u - mu
        rstd = lax.rsqrt(jnp.mean(cen * cen, axis=1, keepdims=True) + LN_EPS)
        xh = cen * rstd
        do = d_ref[...]
        dxh = do * g_ref[...]
        du = rstd * (dxh - jnp.mean(dxh, axis=1, keepdims=True) - xh * jnp.mean(dxh * xh, axis=1, keepdims=True))
        du_ref[...] = du
        dumm_ref[...] = du.astype(dumm_ref.dtype)
        acc = _put_sub(jnp.zeros((8, d), F32), jnp.sum(do * xh, axis=0, keepdims=True), 0)
        acc = _put_sub(acc, jnp.sum(do, axis=0, keepdims=True), 1)

        @pl.when(pl.program_id(0) == 0)
        def _():
            acc_ref[...] = jnp.zeros_like(acc_ref)

        acc_ref[...] += acc

    return _call(body, name, (t // LN_TB,), [_rows(d, LN_TB)] * 3 + [_const(1, d)],
                 [_rows(d, LN_TB), _rows(d, LN_TB), _const(8, d)],
                 [jax.ShapeDtypeStruct((t, d), F32), jax.ShapeDtypeStruct((t, d), MM_DTYPE),
                  jax.ShapeDtypeStruct((8, d), F32)],
                 semantics=("arbitrary",))(dout, x, y, g)


def _loss_head(out, target, name):
    t, d = out.shape

    def body(o_ref, t_ref, d_ref, acc_ref):
        err = o_ref[...] - t_ref[...]
        d_ref[...] = err * (1.0 / d)

        @pl.when(pl.program_id(0) == 0)
        def _():
            acc_ref[...] = jnp.zeros_like(acc_ref)

        acc_ref[...] += _put_sub(jnp.zeros((8, d), F32), jnp.sum(err * err, axis=0, keepdims=True), 0)

    return _call(body, name, (t // LN_TB,), [_rows(d, LN_TB)] * 2, [_rows(d, LN_TB), _const(8, d)],
                 [jax.ShapeDtypeStruct((t, d), F32), jax.ShapeDtypeStruct((8, d), F32)],
                 semantics=("arbitrary",))(out, target)


def _adamw(w, gslots, m, v, name):
    r, c = w.shape
    rb = _tile_rows(r)
    c1 = 1.0 - ADAM_B1 ** ADAM_STEP
    c2 = 1.0 - ADAM_B2 ** ADAM_STEP

    def body(w_ref, g_ref, m_ref, v_ref, go_ref, d_ref, mo_ref, vo_ref):
        g = g_ref[0].astype(F32)
        for s in range(1, N_DEV):
            g = g + g_ref[s].astype(F32)
        mn = ADAM_B1 * m_ref[...] + (1.0 - ADAM_B1) * g
        vn = ADAM_B2 * v_ref[...] + (1.0 - ADAM_B2) * (g * g)
        go_ref[...] = g
        mo_ref[...] = mn
        vo_ref[...] = vn
        d_ref[...] = -ADAM_LR * ((mn / c1) / (jnp.sqrt(vn / c2) + ADAM_EPS) + ADAM_WD * w_ref[...])

    blk = pl.BlockSpec((rb, c), lambda i: (i, 0))
    sds = jax.ShapeDtypeStruct((r, c), F32)
    return _call(body, name, (r // rb,), [blk, pl.BlockSpec((N_DEV, rb, c), lambda i: (0, i, 0)), blk, blk],
                 [blk] * 4, [sds] * 4, semantics=("parallel",))(w, gslots, m, v)


def _tile_rows(r):
    for rb in (256, 128, 64, 32, 16, 8):
        if r % rb == 0:
            return rb
    return r


def _pack(arrs, lead=0):
    flats = []
    for a in arrs:
        f = a.reshape(a.shape[:lead] + (-1,)).astype(F32)
        flats.append(jnp.pad(f, [(0, 0)] * lead + [(0, (-f.shape[-1]) % 128)]))
    v = jnp.concatenate(flats, axis=-1)
    v = jnp.pad(v, [(0, 0)] * lead + [(0, (-v.shape[-1]) % 1024)])
    return v.reshape(v.shape[:lead] + (-1, 128))


def _unpack(buf, shapes, lead=0):
    flat = buf.reshape(buf.shape[:lead] + (-1,))
    outs, off = [], 0
    for s in shapes:
        n = math.prod(s)
        outs.append(flat[..., off:off + n].reshape(buf.shape[:lead] + tuple(s)))
        off += n + (-n) % 128
    return outs


def _cols_gathered(g):
    n, l, r, c = g.shape
    return g.transpose(1, 2, 0, 3).reshape(l, r, n * c)


def _cols_to_slabs(full):
    l, r, c = full.shape
    return full.reshape(l, r, N_DEV, c // N_DEV).transpose(2, 0, 1, 3)


def _rows_gathered(g):
    n, l, r, c = g.shape
    return g.transpose(1, 0, 2, 3).reshape(l, n * r, c)


def _rows_to_slabs(full):
    l, r, c = full.shape
    return full.reshape(l, N_DEV, r // N_DEV, c).transpose(1, 0, 2, 3)


def _pad_cols(w, at, width):
    return jnp.pad(w, ((0, 0), (at, width - at - w.shape[1])))


def _pad_lanes(v, width=HEAD):
    return jnp.pad(v.reshape(1, -1), ((0, 0), (0, width - v.size)))


def _taps8(w, bias=None):
    rows = [w] if bias is None else [w, bias.reshape(1, -1)]
    w8 = jnp.concatenate(rows, axis=0)
    return jnp.pad(w8, ((0, 8 - w8.shape[0]), (0, 0)))


class _Carrier:
    def __init__(self):
        self.jobs, self.got = {}, {}

    def put(self, matmul_name, key, src, slabs):
        self.jobs[matmul_name] = (key, src, slabs)

    def matmul(self, a, b, mode, name, **kw):
        job = self.jobs.pop(name, None)
        if job is None:
            return _matmul(a, b, mode, name, **kw)
        key, src, slabs = job
        out, self.got[key] = _matmul(a, b, mode, name, carry=(src, slabs), **kw)
        return out


def _gdn_forward(x, p, tag, mm):
    pq = mm(x, p["w_qkv"], "nn", tag + "_in_qkv")
    z = mm(x, p["w_z"], "nn", tag + "_in_z")
    ba = mm(x, p["w_ba"], "nn", tag + "_in_ba")
    c = _conv_fwd(pq, p["conv8"], 4, tag + "_conv")
    qn, kn, v, beta, gc = _gdn_ew_fwd(c, ba, p["a_log"], p["dt_bias"], tag + "_ew")
    o, h, states = _gdn_chunk_fwd(qn, kn, v, beta, gc, z, p["norm_w"], tag + "_chunk")
    y = mm(h, p["w_out"], "nn", tag + "_out")
    return y, dict(pq=pq, z=z, ba=ba, c=c, qn=qn, kn=kn, v=v, beta=beta, gc=gc, o=o, h=h, states=states)


def _gdn_backward(x, du, du_mm, p, s, tag, mm, ship):
    dh = mm(du_mm, p["w_out"], "nt", tag + "_bwd_dh")
    g_out = mm(s["h"], du_mm, "tn", tag + "_bwd_wout")
    ship("w_out", g_out)
    dq, dk, dv, dz, dbeta, dgc, nacc = _gdn_chunk_bwd(
        s["qn"], s["kn"], s["v"], s["beta"], s["gc"], s["z"], p["norm_w"], s["o"], s["states"], dh, tag + "_bwd_chunk")
    dc, dba, sacc = _gdn_ew_bwd(s["c"], s["ba"], p["a_log"], p["dt_bias"], dq, dk, dv, dbeta, dgc, tag + "_bwd_ew")
    dpq, dconv = _conv_bwd(dc, s["pq"], p["conv8"], 4, tag + "_bwd_conv")
    g_qkv = mm(x, dpq, "tn", tag + "_bwd_w_qkv")
    g_z = mm(x, dz, "tn", tag + "_bwd_w_z")
    g_ba = mm(x, dba, "tn", tag + "_bwd_w_ba")
    g_in = jnp.concatenate([g_qkv, g_z, g_ba[:, :GDN_V_HEADS], g_ba[:, HEAD:HEAD + GDN_V_HEADS]], axis=1)
    ship("w_in", g_in)
    dx = mm(dpq, p["w_qkv"], "nt", tag + "_bwd_dx_qkv", add=du, add_scale=ALPHA)
    dx = mm(dz, p["w_z"], "nt", tag + "_bwd_dx_z", add=dx)
    dx = mm(dba, p["w_ba"], "nt", tag + "_bwd_dx_ba", add=dx)
    grads = dict(w_in=g_in, w_out=g_out, conv_w=dconv[:4], a_log=sacc[0, :GDN_V_HEADS], dt_bias=sacc[1, :GDN_V_HEADS],
                 norm_w=nacc[0])
    return dx, grads


def _sc_forward(x, p, tag, mm):
    hh = mm(x, p["w_h"], "nn", tag + "_in_h")
    bg = mm(x, p["w_b"], "nn", tag + "_in_b")
    cg = mm(x, p["w_c"], "nn", tag + "_in_c")
    z = mm(x, p["w_z"], "nn", tag + "_in_z")
    cv = _conv_fwd(cg, p["conv8"], 3, tag + "_conv", u2=hh)
    h = _sc_gate_fwd(bg, cv, z, tag + "_gate")
    y = mm(h, p["w_out"], "nn", tag + "_out")
    return y, dict(hh=hh, bg=bg, cg=cg, z=z, cv=cv, h=h)


def _sc_backward(x, du, du_mm, p, s, tag, mm, ship):
    dh = mm(du_mm, p["w_out"], "nt", tag + "_bwd_dh")
    g_out = mm(s["h"], du_mm, "tn", tag + "_bwd_wout")
    ship("w_out", g_out)
    dbg, dcv, dz = _sc_gate_bwd(dh, s["bg"], s["cv"], s["z"], tag + "_bwd_gate")
    dcg, dhh, dconv = _conv_bwd(dcv, s["cg"], p["conv8"], 3, tag + "_bwd_conv", u2=s["hh"])
    g_in = jnp.concatenate([mm(x, d, "tn", tag + "_bwd_w_" + n)
                            for n, d in (("h", dhh), ("b", dbg), ("c", dcg), ("z", dz))], axis=1)
    ship("w_in", g_in)
    dx = mm(dhh, p["w_h"], "nt", tag + "_bwd_dx_h", add=du, add_scale=ALPHA)
    dx = mm(dbg, p["w_b"], "nt", tag + "_bwd_dx_b", add=dx)
    dx = mm(dcg, p["w_c"], "nt", tag + "_bwd_dx_c", add=dx)
    dx = mm(dz, p["w_z"], "nt", tag + "_bwd_dx_z", add=dx)
    return dx, dict(w_in=g_in, w_out=g_out, conv_w=dconv[:3])


def _ssd_forward(x, p, tag, mm):
    z = mm(x, p["w_z"], "nn", tag + "_in_z")
    xbc = mm(x, p["w_xbc"], "nn", tag + "_in_xbc")
    dtp = mm(x, p["w_dt"], "nn", tag + "_in_dt")
    cx = _conv_fwd(xbc, p["conv8"], 4, tag + "_conv", bias=True)
    y, h, states = _ssd_chunk_fwd(cx, dtp, z, p["dt_bias"], p["a_log"], p["dskip"], p["norm_w"], tag + "_chunk")
    out = mm(h, p["w_out"], "nn", tag + "_out")
    return out, dict(z=z, xbc=xbc, dtp=dtp, cx=cx, y=y, h=h, states=states)


def _ssd_backward(x, du, du_mm, p, s, tag, mm, ship):
    dh = mm(du_mm, p["w_out"], "nt", tag + "_bwd_dh")
    g_out = mm(s["h"], du_mm, "tn", tag + "_bwd_wout")
    ship("w_out", g_out)
    dcx, ddtp, dz, wide, acc = _ssd_chunk_bwd(s["cx"], s["dtp"], s["z"], p["dt_bias"], p["a_log"], p["dskip"],
                                              p["norm_w"], s["y"], s["states"], dh, tag + "_bwd_chunk")
    dxbc, dconv = _conv_bwd(dcx, s["xbc"], p["conv8"], 4, tag + "_bwd_conv")
    g_dt = mm(x, ddtp, "tn", tag + "_bwd_w_dt")
    g_in = jnp.concatenate([mm(x, dz, "tn", tag + "_bwd_w_z"), mm(x, dxbc, "tn", tag + "_bwd_w_xbc"),
                            g_dt[:, :32]], axis=1)
    ship("w_in", g_in)
    dx = mm(dz, p["w_z"], "nt", tag + "_bwd_dx_z", add=du, add_scale=ALPHA)
    dx = mm(dxbc, p["w_xbc"], "nt", tag + "_bwd_dx_xbc", add=dx)
    dx = mm(ddtp, p["w_dt"], "nt", tag + "_bwd_dx_dt", add=dx)
    grads = dict(w_in=g_in, w_out=g_out, conv_w=dconv[:4], conv_b=dconv[4], a_log=acc[0, :32], dt_bias=acc[1, :32],
                 d_skip=jnp.sum(wide[1].reshape(32, 64), axis=1), norm_w=wide[0])
    return dx, grads


_WEIGHTS = ['gdn_w_in', 'gdn_conv_w', 'gdn_a_log', 'gdn_dt_bias', 'gdn_norm_w', 'gdn_w_out', 'sc_w_in', 'sc_conv_w',
            'sc_w_out', 'ssd_w_in', 'ssd_conv_w', 'ssd_conv_b', 'ssd_a_log', 'ssd_dt_bias', 'ssd_d_skip',
            'ssd_norm_w', 'ssd_w_out', 'ln_g', 'ln_b']
_BIG = {'gdn_w_in': 'cols', 'gdn_w_out': 'rows', 'sc_w_in': 'cols', 'sc_w_out': 'rows', 'ssd_w_in': 'cols',
        'ssd_w_out': 'rows'}
_SMALL_SHARDED = ['gdn_conv_w', 'sc_conv_w', 'ssd_conv_w', 'ssd_conv_b', 'ssd_norm_w']
_SMALL = [n for n in _WEIGHTS if n not in _BIG]


def kernel(x, gdn_w_in, gdn_conv_w, gdn_a_log, gdn_dt_bias, gdn_norm_w, gdn_w_out, sc_w_in, sc_conv_w, sc_w_out, ssd_w_in, ssd_conv_w, ssd_conv_b, ssd_a_log, ssd_dt_bias, ssd_d_skip, ssd_norm_w, ssd_w_out, ln_g, ln_b, loss_target, m_gdn_w_in, m_gdn_conv_w, m_gdn_a_log, m_gdn_dt_bias, m_gdn_norm_w, m_gdn_w_out, m_sc_w_in, m_sc_conv_w, m_sc_w_out, m_ssd_w_in, m_ssd_conv_w, m_ssd_conv_b, m_ssd_a_log, m_ssd_dt_bias, m_ssd_d_skip, m_ssd_norm_w, m_ssd_w_out, m_ln_g, m_ln_b, v_gdn_w_in, v_gdn_conv_w, v_gdn_a_log, v_gdn_dt_bias, v_gdn_norm_w, v_gdn_w_out, v_sc_w_in, v_sc_conv_w, v_sc_w_out, v_ssd_w_in, v_ssd_conv_w, v_ssd_conv_b, v_ssd_a_log, v_ssd_dt_bias, v_ssd_d_skip, v_ssd_norm_w, v_ssd_w_out, v_ln_g, v_ln_b):
    args = locals()
    wts = {n: args[n] for n in _WEIGHTS}
    mom = {n: args["m_" + n] for n in _WEIGHTS}
    vel = {n: args["v_" + n] for n in _WEIGHTS}
    me = 4 * lax.axis_index("x") + 2 * lax.axis_index("y") + lax.axis_index("c")
    x0, target = x[0], loss_target[0]

    car = _Carrier()
    shard = lambda n, j: wts[n][j:j + 1].astype(MM_DTYPE)
    gathered_w = lambda n, j: (_cols_gathered if _BIG[n] == "cols" else _rows_gathered)(car.got[n, j])[0]

    for n in ('gdn_w_in', 'gdn_w_out'):
        car.got[n, 0] = _exchange(shard(n, 0), "gather_%s0" % n, slabs=False)
    riders = {0: [("l0_gdn_in_qkv", 'sc_w_in', 0), ("l0_gdn_in_z", 'sc_w_out', 0), ("l0_gdn_out", 'ssd_w_out', 0)],
              1: [("l1_sc_in_h", 'ssd_w_in', 0)],
              2: [("l2_ssd_in_xbc", 'gdn_w_in', 1), ("l2_ssd_in_z", 'gdn_w_out', 1)]}
    full = {}
    small_shapes = [wts[n].shape for n in _SMALL_SHARDED]
    gathered = _exchange(_pack([wts[n] for n in _SMALL_SHARDED]), "gather_small", slabs=False)
    for n, g in zip(_SMALL_SHARDED, _unpack(gathered, small_shapes, lead=1)):
        full[n] = jnp.moveaxis(g, 0, -2).reshape(g.shape[1:-1] + (N_DEV * g.shape[-1],))
    for n in _SMALL:
        full.setdefault(n, wts[n])

    def gdn_params(j):
        w = gathered_w('gdn_w_in', j)
        return dict(w_qkv=w[:, :GDN_CONV_DIM], w_z=w[:, GDN_CONV_DIM:GDN_CONV_DIM + D_INNER],
                    w_ba=jnp.concatenate([_pad_cols(w[:, 6144:6160], 0, HEAD), _pad_cols(w[:, 6160:6176], 0, HEAD)], 1),
                    conv8=_taps8(full['gdn_conv_w'][j]), a_log=_pad_lanes(full['gdn_a_log'][j]),
                    dt_bias=_pad_lanes(full['gdn_dt_bias'][j]), norm_w=full['gdn_norm_w'][j].reshape(1, HEAD),
                    w_out=gathered_w('gdn_w_out', j))

    def sc_params():
        w = gathered_w('sc_w_in', 0)
        return dict(w_h=w[:, :2048], w_b=w[:, 2048:4096], w_c=w[:, 4096:6144], w_z=w[:, 6144:],
                    conv8=_taps8(full['sc_conv_w'][0]), w_out=gathered_w('sc_w_out', 0))

    def ssd_params():
        w = gathered_w('ssd_w_in', 0)
        return dict(w_z=w[:, :D_INNER], w_xbc=w[:, D_INNER:D_INNER + SSD_CONV_DIM],
                    w_dt=_pad_cols(w[:, D_INNER + SSD_CONV_DIM:], 0, HEAD),
                    conv8=_taps8(full['ssd_conv_w'][0], full['ssd_conv_b'][0]), a_log=_pad_lanes(full['ssd_a_log'][0]),
                    dt_bias=_pad_lanes(full['ssd_dt_bias'][0]),
                    dskip=jnp.repeat(full['ssd_d_skip'][0], 64).reshape(1, D_INNER),
                    norm_w=full['ssd_norm_w'][0].reshape(1, D_INNER), w_out=gathered_w('ssd_w_out', 0))

    layers = [("gdn", _gdn_forward, _gdn_backward, lambda: gdn_params(0)), ("sc", _sc_forward, _sc_backward, sc_params),
              ("ssd", _ssd_forward, _ssd_backward, ssd_params), ("gdn", _gdn_forward, _gdn_backward, lambda: gdn_params(1))]

    acts, acts_mm, ys, saved, params = [x0], [x0.astype(MM_DTYPE)], [], [], []
    for i, (kind, fwd, _, make_params) in enumerate(layers):
        params.append(make_params())
        for matmul_name, n, j in riders.get(i, ()):
            car.put(matmul_name, (n, j), shard(n, j), False)
        y, s = fwd(acts_mm[-1], params[i], "l%d_%s" % (i, kind), car.matmul)
        out, out_mm = _ln_fwd(acts[-1], y, full['ln_g'][i].reshape(1, -1), full['ln_b'][i].reshape(1, -1), "l%d_ln" % i)
        acts.append(out)
        acts_mm.append(out_mm)
        ys.append(y)
        saved.append(s)
    dact, loss_acc = _loss_head(acts[-1], target, "loss_head")
    loss = lax.psum(0.5 / D_MODEL * jnp.sum(loss_acc[0]), ("x", "y", "c"))

    grad_riders = {3: dict(w_out="l3_gdn_bwd_w_qkv", w_in="l3_gdn_bwd_dx_qkv"),
                   2: dict(w_out="l2_ssd_bwd_w_z", w_in="l2_ssd_bwd_dx_xbc"),
                   1: dict(w_out="l1_sc_bwd_w_h", w_in="l0_gdn_bwd_w_qkv"),
                   0: dict(w_out="l0_gdn_bwd_w_z", w_in="l0_gdn_bwd_dx_qkv")}

    def shipper(i):
        def ship(key, g):
            slabs = _cols_to_slabs(g[None]) if key == 'w_in' else _rows_to_slabs(g[None])
            car.put(grad_riders[i][key], ('grad', i, key), slabs.astype(MM_DTYPE), True)
        return ship

    lg = [None] * DEPTH
    d_ln_g, d_ln_b = [None] * DEPTH, [None] * DEPTH
    for i in reversed(range(DEPTH)):
        kind, _, bwd, _ = layers[i]
        du, du_mm, acc = _ln_bwd(dact, acts[i], ys[i], full['ln_g'][i].reshape(1, -1), "l%d_ln_bwd" % i)
        d_ln_g[i], d_ln_b[i] = acc[0], acc[1]
        dact, lg[i] = bwd(acts_mm[i], du, du_mm, params[i], saved[i], "l%d_%s" % (i, kind), car.matmul, shipper(i))
    assert not car.jobs, car.jobs
    grad_x = dact[None]

    stack = lambda k: jnp.stack([lg[0][k], lg[3][k]])
    local = {
        'gdn_conv_w': stack('conv_w'), 'gdn_a_log': stack('a_log'),
        'gdn_dt_bias': stack('dt_bias'), 'gdn_norm_w': stack('norm_w'),
        'sc_conv_w': lg[1]['conv_w'][None],
        'ssd_conv_w': lg[2]['conv_w'][None], 'ssd_conv_b': lg[2]['conv_b'][None],
        'ssd_a_log': lg[2]['a_log'][None], 'ssd_dt_bias': lg[2]['dt_bias'][None], 'ssd_d_skip': lg[2]['d_skip'][None],
        'ssd_norm_w': lg[2]['norm_w'][None],
        'ln_g': jnp.stack(d_ln_g), 'ln_b': jnp.stack(d_ln_b)}

    out = {}
    layers_of = {'gdn': (0, 3), 'sc': (1,), 'ssd': (2,)}
    for n in _BIG:
        kind, key = n.split('_', 1)
        recv = jnp.concatenate([car.got['grad', i, key] for i in layers_of[kind]], axis=1)
        shp = wts[n].shape
        r, c = shp[0] * shp[1], shp[2]
        res = _adamw(wts[n].reshape(r, c), recv.reshape(N_DEV, r, c), mom[n].reshape(r, c), vel[n].reshape(r, c),
                     "adamw_" + n)
        out[n] = [a.reshape(shp) for a in res]
    full_shapes = [local[n].shape for n in _SMALL]
    gathered = _exchange(_pack([local[n] for n in _SMALL]), "gather_small_grads", slabs=False)
    gs = []
    for n, g in zip(_SMALL, _unpack(gathered, full_shapes, lead=1)):
        if n in _SMALL_SHARDED:
            width = wts[n].shape[-1]
            g = lax.dynamic_slice_in_dim(g, me * width, width, axis=g.ndim - 1)
        gs.append(g)
    shapes = [wts[n].shape for n in _SMALL]
    res = _adamw(_pack([wts[n] for n in _SMALL]), _pack(gs, lead=1), _pack([mom[n] for n in _SMALL]),
                 _pack([vel[n] for n in _SMALL]), "adamw_small")
    for k, n in enumerate(_SMALL):
        out[n] = [_unpack(a, shapes)[k] for a in res]

    return (loss, grad_x, *[out[n][0] for n in _WEIGHTS], *[out[n][1] for n in _WEIGHTS],
            *[out[n][2] for n in _WEIGHTS], *[out[n][3] for n in _WEIGHTS])
```

```python
import functools
import math

import jax
import jax.numpy as jnp
from jax import lax
from jax.experimental import pallas as pl
from jax.experimental.pallas import tpu as pltpu

F32 = jnp.float32
MM_DTYPE = jnp.bfloat16

N_DEV = 8
D_MODEL = 1024
D_INNER = 2048
CHUNK = 64
HEAD = 128
GDN_V_HEADS = 16
GDN_GROUP = 16
GDN_QK_HEADS = 8
GDN_QK_DIM = 1024
GDN_CONV_DIM = 4096
SSD_PAIRS = 16
SSD_GROUPS = 4
SSD_STATE = 128
SSD_CONV_DIM = 3072
DEPTH = 4
ALPHA = (2 * DEPTH) ** 0.25
RMS_EPS = 1e-6
LN_EPS = 1e-5
L2_EPS = 1e-6
ADAM_LR, ADAM_B1, ADAM_B2, ADAM_EPS, ADAM_WD, ADAM_STEP = 0.001, 0.9, 0.999, 1e-08, 0.01, 10

VMEM_LIMIT_BYTES = 48 * 1024 * 1024
NEG_BIG = -1e30

_NN = (((1,), (0,)), ((), ()))
_NT = (((1,), (1,)), ((), ()))
_TN = (((0,), (0,)), ((), ()))


def _mm(a, b, dims):
    return lax.dot_general(a.astype(MM_DTYPE), b.astype(MM_DTYPE), dims, preferred_element_type=F32)


def _mmx(a, b, dims):
    return lax.dot_general(a, b, dims, precision=lax.Precision.HIGHEST, preferred_element_type=F32)


def _iota(shape, dim):
    return lax.broadcasted_iota(jnp.int32, shape, dim)


def _eye(n):
    return (_iota((n, n), 0) == _iota((n, n), 1)).astype(F32)


def _sig(x):
    return jax.nn.sigmoid(x)


def _silu(x):
    return x * _sig(x)


def _dsilu(x):
    s = _sig(x)
    return s * (1.0 + x * (1.0 - s))


def _softplus(x):
    return jnp.maximum(x, 0.0) + jnp.log(1.0 + jnp.exp(-jnp.abs(x)))


def _col(x, h):
    return jnp.sum(jnp.where(_iota(x.shape, 1) == h, x, 0.0), axis=1, keepdims=True)


def _row(x, h):
    return jnp.sum(jnp.where(_iota(x.shape, 0) == h, x, 0.0), axis=0, keepdims=True)


def _put_col(acc, col, h):
    return jnp.where(_iota(acc.shape, 1) == h, col, acc)


def _put_row(acc, row, h):
    return jnp.where(_iota(acc.shape, 0) == h, row, acc)


def _put_sub(acc, row, j):
    return acc + jnp.where(_iota(acc.shape, 0) == j, row, 0.0)


def _lanes(h):
    return pl.ds(h * HEAD, HEAD) if isinstance(h, int) else pl.ds(pl.multiple_of(h * HEAD, HEAD), HEAD)


def _total(x):
    return jnp.sum(jnp.sum(x, axis=0, keepdims=True), axis=1, keepdims=True)


def _call(body, name, grid, in_specs, out_specs, out_shape, scratch_shapes=(), semantics=None):
    return pl.pallas_call(
        body, name=name, grid=grid, in_specs=in_specs, out_specs=out_specs, out_shape=out_shape,
        scratch_shapes=list(scratch_shapes),
        compiler_params=pltpu.CompilerParams(dimension_semantics=semantics, vmem_limit_bytes=VMEM_LIMIT_BYTES))


def _tile(n, pref):
    if n <= pref:
        return n
    t = pref
    while n % t:
        t -= 128
    return t


def _exchange_copies(src_ref, out_ref, send_sems, recv_sems, local_sem, slabs):
    x, y, c = lax.axis_index("x"), lax.axis_index("y"), lax.axis_index("c")
    me = 4 * x + 2 * y + c
    mine = src_ref.at[me] if slabs else src_ref
    local = pltpu.make_async_copy(mine, out_ref.at[me], local_sem)
    sends, recvs = [], []
    for r in range(1, N_DEV):
        px = 1 - x if r & 4 else x
        py = 1 - y if r & 2 else y
        pc = 1 - c if r & 1 else c
        peer = 4 * px + 2 * py + pc
        kw = dict(send_sem=send_sems.at[r - 1], recv_sem=recv_sems.at[r - 1], device_id=(px, py, pc),
                  device_id_type=pl.DeviceIdType.MESH)
        sends.append(pltpu.make_async_remote_copy(src_ref=src_ref.at[peer] if slabs else src_ref,
                                                  dst_ref=out_ref.at[me], **kw))
        recvs.append(pltpu.make_async_remote_copy(src_ref=mine, dst_ref=out_ref.at[peer], **kw))
    return local, sends, recvs


def _gather_copies(src_ref, out_ref, send_sems, recv_sems, local_sem):
    x, y, c = lax.axis_index("x"), lax.axis_index("y"), lax.axis_index("c")
    chips = [(1 - x, y), (x, 1 - y), (1 - x, 1 - y)]
    sibling = (x, y, 1 - c)

    def slot(px, py, pc):
        return out_ref.at[4 * px + 2 * py + pc]

    def copy(k, src, dst, to):
        return pltpu.make_async_remote_copy(src_ref=src, dst_ref=dst, send_sem=send_sems.at[k], recv_sem=recv_sems.at[k],
                                            device_id=to, device_id_type=pl.DeviceIdType.MESH)

    mine = slot(x, y, c)
    local = pltpu.make_async_copy(src_ref, mine, local_sem)
    first = [copy(0, src_ref, mine, sibling)] + [copy(1 + j, src_ref, mine, (*chip, c)) for j, chip in enumerate(chips)]
    landed = [copy(1 + j, src_ref, slot(*chip, c), sibling) for j, chip in enumerate(chips)]
    passed = [copy(4 + j, slot(*chip, c), slot(*chip, c), sibling) for j, chip in enumerate(chips)]
    from_sibling = [copy(0, src_ref, slot(x, y, 1 - c), sibling)] + \
        [copy(4 + j, src_ref, slot(*chip, 1 - c), sibling) for j, chip in enumerate(chips)]
    return local, first, landed, passed, from_sibling


def _exchange_start(*refs, slabs):
    if not slabs:
        local, first = _gather_copies(*refs)[:2]
        local.start()
        for cp in first:
            cp.start()
        return
    local, sends, _ = _exchange_copies(*refs, slabs=slabs)
    local.start()
    for cp in sends:
        cp.start()


def _exchange_wait(*refs, slabs):
    if not slabs:
        local, first, landed, passed, from_sibling = _gather_copies(*refs)
        for arrived, onward in zip(landed, passed):
            arrived.wait_recv()
            onward.start()
        for cp in from_sibling:
            cp.wait_recv()
        for cp in first + passed:
            cp.wait_send()
        local.wait()
        return
    local, sends, recvs = _exchange_copies(*refs, slabs=slabs)
    for cp in recvs:
        cp.wait_recv()
    for cp in sends:
        cp.wait_send()
    local.wait()


def _exchange_sems():
    return [pltpu.SemaphoreType.DMA((N_DEV - 1,)), pltpu.SemaphoreType.DMA((N_DEV - 1,)), pltpu.SemaphoreType.DMA(())]


def _exchange_shape(src, slabs):
    return jax.ShapeDtypeStruct((N_DEV,) + tuple(src.shape[1:] if slabs else src.shape), src.dtype)


def _exchange(src, name, slabs):
    def body(*refs):
        _exchange_start(*refs, slabs=slabs)
        _exchange_wait(*refs, slabs=slabs)

    return pl.pallas_call(
        body, name=name,
        in_specs=[pl.BlockSpec(memory_space=pl.ANY)], out_specs=pl.BlockSpec(memory_space=pl.ANY),
        out_shape=_exchange_shape(src, slabs), scratch_shapes=_exchange_sems(),
    )(src)


MM_TM, MM_TN, MM_TK = 1024, 1024, 1024


def _matmul(a, b, mode, name, add=None, add_scale=1.0, carry=None):
    if mode == "nn":
        (m, k), (_, n) = a.shape, b.shape
    elif mode == "nt":
        (m, k), (n, _) = a.shape, b.shape
    else:
        (k, m), (_, n) = a.shape, b.shape
    tk = _tile(k, 2 * MM_TK)
    tm, tn = _tile(m, 2 * MM_TM if mode == "nn" and add is None and tk <= MM_TK else MM_TM), _tile(n, MM_TN)
    nk = k // tk
    grid = (m // tm, n // tn, nk)
    dims = {"nn": _NN, "nt": _NT, "tn": _TN}[mode]
    n_in = 2 + (add is not None)

    def body(*refs):
        a_ref, b_ref, o_ref = refs[0], refs[1], refs[n_in + (carry is not None)]
        if carry is not None:
            ex = (refs[n_in], refs[n_in + 2]) + tuple(refs[n_in + 3:])
            step = (pl.program_id(0) * grid[1] + pl.program_id(1)) * grid[2] + pl.program_id(2)

            @pl.when(step == 0)
            def _():
                _exchange_start(*ex, slabs=carry[1])

        part = _mm(a_ref[...], b_ref[...], dims)
        first = part if add is None else part + add_scale * refs[2][...]
        if nk == 1:
            o_ref[...] = first
        else:
            @pl.when(pl.program_id(2) == 0)
            def _():
                o_ref[...] = first

            @pl.when(pl.program_id(2) > 0)
            def _():
                o_ref[...] += part

        if carry is not None:
            @pl.when(step == grid[0] * grid[1] * grid[2] - 1)
            def _():
                _exchange_wait(*ex, slabs=carry[1])

    if mode == "nn":
        specs = [pl.BlockSpec((tm, tk), lambda i, j, q: (i, q)), pl.BlockSpec((tk, tn), lambda i, j, q: (q, j))]
    elif mode == "nt":
        specs = [pl.BlockSpec((tm, tk), lambda i, j, q: (i, q)), pl.BlockSpec((tn, tk), lambda i, j, q: (j, q))]
    else:
        specs = [pl.BlockSpec((tk, tm), lambda i, j, q: (q, i)), pl.BlockSpec((tk, tn), lambda i, j, q: (q, j))]
    out_spec = pl.BlockSpec((tm, tn), lambda i, j, q: (i, j))
    args = [a, b]
    if add is not None:
        specs.append(out_spec)
        args.append(add)
    out_shape = jax.ShapeDtypeStruct((m, n), F32)
    if carry is None:
        return _call(body, name, grid, specs, out_spec, out_shape, semantics=("parallel", "parallel", "arbitrary"))(*args)
    hbm = pl.BlockSpec(memory_space=pl.ANY)
    return _call(body, name, grid, specs + [hbm], [out_spec, hbm], [out_shape, _exchange_shape(*carry)],
                 _exchange_sems(), ("arbitrary", "arbitrary", "arbitrary"))(*args, carry[0])


CONV_TB = 512
CONV_CB = 1024
HALO = 8


def _conv_specs(t, cb_n):
    tb = min(CONV_TB, t)
    nb = tb // HALO
    blk = pl.BlockSpec((tb, cb_n), lambda c, i: (i, c))
    prev = pl.BlockSpec((HALO, cb_n), lambda c, i: (jnp.maximum(i * nb - 1, 0), c))
    nxt = pl.BlockSpec((HALO, cb_n), lambda c, i: (jnp.minimum((i + 1) * nb, t // HALO - 1), c))
    w = pl.BlockSpec((8, cb_n), lambda c, i: (0, c))
    return blk, prev, nxt, w


def _shift_down(ext, s, tb):
    return (pltpu.roll(ext, s, 0) if s else ext)[HALO:HALO + tb]


def _shift_up(ext, s, tb):
    n = ext.shape[0]
    return (pltpu.roll(ext, n - s, 0) if s else ext)[0:tb]


def _conv_fwd(u, w8, ktaps, name, u2=None, bias=False):
    t, ch = u.shape
    cb_n = min(CONV_CB, ch)
    tb = min(CONV_TB, t)
    two = u2 is not None

    def body(*refs):
        if two:
            u_ref, up_ref, v_ref, vp_ref, w_ref, o_ref = refs
        else:
            u_ref, up_ref, w_ref, o_ref = refs
        first = pl.program_id(1) == 0
        blk, halo = u_ref[...], up_ref[...]
        if two:
            blk, halo = blk * v_ref[...], halo * vp_ref[...]
        ext = jnp.concatenate([jnp.where(first, 0.0, halo), blk], axis=0)
        acc = jnp.zeros((tb, cb_n), F32)
        for j in range(ktaps):
            acc = acc + w_ref[j:j + 1, :] * _shift_down(ext, ktaps - 1 - j, tb)
        if bias:
            acc = acc + w_ref[ktaps:ktaps + 1, :]
        o_ref[...] = acc

    blk, prev, _, wspec = _conv_specs(t, cb_n)
    specs, args = [blk, prev], [u, u]
    if two:
        specs += [blk, prev]
        args += [u2, u2]
    specs.append(wspec)
    args.append(w8)
    return _call(body, name, (ch // cb_n, t // tb), specs, blk, jax.ShapeDtypeStruct((t, ch), F32),
                 semantics=("parallel", "parallel"))(*args)


def _conv_bwd(dc, u, w8, ktaps, name, u2=None):
    t, ch = u.shape
    cb_n = min(CONV_CB, ch)
    tb = min(CONV_TB, t)
    two = u2 is not None

    def body(*refs):
        if two:
            dc_ref, dn_ref, u_ref, v_ref, w_ref, du_ref, dv_ref, dw_ref = refs
        else:
            dc_ref, dn_ref, u_ref, w_ref, du_ref, dw_ref = refs
        i = pl.program_id(1)
        d = dc_ref[...]
        dext = jnp.concatenate([d, jnp.where(i == t // tb - 1, 0.0, dn_ref[...])], axis=0)
        blk = u_ref[...] * v_ref[...] if two else u_ref[...]
        du = jnp.zeros((tb, cb_n), F32)
        dw = jnp.zeros((8, cb_n), F32)
        for j in range(ktaps):
            ahead = _shift_up(dext, ktaps - 1 - j, tb)
            du = du + w_ref[j:j + 1, :] * ahead
            dw = _put_sub(dw, jnp.sum(ahead * blk, axis=0, keepdims=True), j)
        dw = _put_sub(dw, jnp.sum(d, axis=0, keepdims=True), ktaps)
        if two:
            du_ref[...] = (du * v_ref[...]).astype(du_ref.dtype)
            dv_ref[...] = (du * u_ref[...]).astype(dv_ref.dtype)
        else:
            du_ref[...] = du.astype(du_ref.dtype)

        @pl.when(i == 0)
        def _():
            dw_ref[...] = jnp.zeros_like(dw_ref)

        dw_ref[...] += dw

    blk, _, nxt, wspec = _conv_specs(t, cb_n)
    specs, args = [blk, nxt, blk], [dc, dc, u]
    if two:
        specs.append(blk)
        args.append(u2)
    specs.append(wspec)
    args.append(w8)
    act = jax.ShapeDtypeStruct((t, ch), MM_DTYPE)
    outs = ([blk, blk, wspec], [act, act, jax.ShapeDtypeStruct((8, ch), F32)]) if two else \
        ([blk, wspec], [act, jax.ShapeDtypeStruct((8, ch), F32)])
    return _call(body, name, (ch // cb_n, t // tb), specs, outs[0], outs[1],
                 semantics=("parallel", "arbitrary"))(*args)


EW_TB = 256


def _chunk_mask(n, upper):
    i, j = _iota((n, n), 0), _iota((n, n), 1)
    same = jnp.right_shift(i, 6) == jnp.right_shift(j, 6)
    return (same & ((j >= i) if upper else (i >= j))).astype(F32)


def _rows(width, tb=EW_TB):
    return pl.BlockSpec((tb, width), lambda i: (i, 0))


def _const(rows, width):
    return pl.BlockSpec((rows, width), lambda i: (0, 0))


def _gdn_ew_fwd(c, ba, a_log, dt_bias, name):
    t = c.shape[0]
    tb = EW_TB

    def body(c_ref, ba_ref, al_ref, db_ref, q_ref, k_ref, v_ref, beta_ref, gc_ref):
        for h in range(GDN_QK_HEADS):
            for base, ref, scale in ((0, q_ref, HEAD ** -0.5), (GDN_QK_DIM, k_ref, 1.0)):
                s = _silu(c_ref[:, base + h * HEAD: base + (h + 1) * HEAD])
                r = lax.rsqrt(jnp.sum(s * s, axis=1, keepdims=True) + L2_EPS)
                ref[:, h * HEAD:(h + 1) * HEAD] = s * (r * scale)
        v_ref[...] = _silu(c_ref[:, 2 * GDN_QK_DIM:])
        beta_ref[...] = _sig(ba_ref[:, :HEAD])
        g = -jnp.exp(al_ref[...]) * _softplus(ba_ref[:, HEAD:] + db_ref[...])
        gc_ref[...] = _mmx(_chunk_mask(tb, False), g, _NN)

    act = lambda w: jax.ShapeDtypeStruct((t, w), F32)
    return _call(body, name, (t // tb,),
                 [_rows(GDN_CONV_DIM), _rows(2 * HEAD), _const(1, HEAD), _const(1, HEAD)],
                 [_rows(GDN_QK_DIM), _rows(GDN_QK_DIM), _rows(D_INNER), _rows(HEAD), _rows(HEAD)],
                 [act(GDN_QK_DIM), act(GDN_QK_DIM), act(D_INNER), act(HEAD), act(HEAD)],
                 semantics=("parallel",))(c, ba, a_log, dt_bias)


def _gdn_ew_bwd(c, ba, a_log, dt_bias, dqh, dkh, dv, dbeta, dgc, name):
    t = c.shape[0]
    tb = EW_TB

    def body(c_ref, ba_ref, al_ref, db_ref, dq_ref, dk_ref, dv_ref, dbeta_ref, dgc_ref, dc_ref, dba_ref, acc_ref):
        for h in range(GDN_QK_HEADS):
            for base, ref, scale in ((0, dq_ref, HEAD ** -0.5), (GDN_QK_DIM, dk_ref, 1.0)):
                cq = c_ref[:, base + h * HEAD: base + (h + 1) * HEAD]
                s = _silu(cq)
                r = lax.rsqrt(jnp.sum(s * s, axis=1, keepdims=True) + L2_EPS)
                dn = ref[:, h * HEAD:(h + 1) * HEAD] * scale
                ds = r * dn - s * (r * r * r) * jnp.sum(dn * s, axis=1, keepdims=True)
                dc_ref[:, base + h * HEAD: base + (h + 1) * HEAD] = ds * _dsilu(cq)
        dc_ref[:, 2 * GDN_QK_DIM:] = dv_ref[...] * _dsilu(c_ref[:, 2 * GDN_QK_DIM:])
        beta = _sig(ba_ref[:, :HEAD])
        dba_ref[:, :HEAD] = (dbeta_ref[...] * beta * (1.0 - beta)).astype(dba_ref.dtype)
        pre = ba_ref[:, HEAD:] + db_ref[...]
        ea = jnp.exp(al_ref[...])
        g = -ea * _softplus(pre)
        dg = _mmx(_chunk_mask(tb, True), dgc_ref[...], _NN)
        da_raw = dg * (-ea) * _sig(pre)
        dba_ref[:, HEAD:] = da_raw.astype(dba_ref.dtype)
        acc = jnp.zeros((8, HEAD), F32)
        acc = _put_sub(acc, jnp.sum(dg * g, axis=0, keepdims=True), 0)
        acc = _put_sub(acc, jnp.sum(da_raw, axis=0, keepdims=True), 1)

        @pl.when(pl.program_id(0) == 0)
        def _():
            acc_ref[...] = jnp.zeros_like(acc_ref)

        acc_ref[...] += acc

    act = lambda w: jax.ShapeDtypeStruct((t, w), F32)
    return _call(body, name, (t // tb,),
                 [_rows(GDN_CONV_DIM), _rows(2 * HEAD), _const(1, HEAD), _const(1, HEAD),
                  _rows(GDN_QK_DIM), _rows(GDN_QK_DIM), _rows(D_INNER), _rows(HEAD), _rows(HEAD)],
                 [_rows(GDN_CONV_DIM), _rows(2 * HEAD), _const(8, HEAD)],
                 [act(GDN_CONV_DIM), jax.ShapeDtypeStruct((t, 2 * HEAD), MM_DTYPE), jax.ShapeDtypeStruct((8, HEAD), F32)],
                 semantics=("arbitrary",))(c, ba, a_log, dt_bias, dqh, dkh, dv, dbeta, dgc)


def _zip(fn, *lists):
    return [fn(*xs) for xs in zip(*lists)]


def _mms(xs, ys, dims):
    return [_mm(x, y, dims) for x, y in zip(xs, ys)]


def _side_by_side(a, b):
    return jnp.concatenate([a, b], axis=1)


def _interleave(*gens):
    results, live = [None] * len(gens), list(range(len(gens)))
    while live:
        for i in list(live):
            try:
                next(gens[i])
            except StopIteration as stop:
                results[i] = stop.value
                live.remove(i)
    return results


def _gdn_local_stages(q, k, v, bcol, gcol, grow, glast):
    ii, jj = _iota((CHUNK, CHUNK), 0), _iota((CHUNK, CHUNK), 1)
    eye = _eye(CHUNK)
    mul = lambda x, y: x * y
    eg = [jnp.exp(g) for g in gcol]
    decay = _zip(lambda gc, gr: jnp.exp(jnp.where(ii >= jj, gc - gr, NEG_BIG)), gcol, grow)
    kb = _zip(mul, k, bcol)
    p, qk = _mms(kb, k, _NT), _mms(q, k, _NT)
    yield
    a = _zip(lambda x, d: jnp.where(ii > jj, x * d, 0.0), p, decay)
    inv, pw = [eye - x for x in a], a
    for _ in range(5):
        pw = _mms(pw, pw, _NN)
        yield
        inv = _zip(lambda x, y: x + y, inv, _mms(inv, pw, _NN))
        yield
    rv, rk = _zip(mul, v, bcol), _zip(mul, kb, eg)
    uw = _mms(inv, _zip(_side_by_side, rv, rk), _NN)
    u, w = [x[:, :HEAD] for x in uw], [x[:, HEAD:] for x in uw]
    yield
    att = _zip(mul, qk, decay)
    qd = _zip(mul, q, eg)
    ekt = _zip(lambda gl, gc: jnp.exp(gl - gc), glast, gcol)
    kt = _zip(mul, k, ekt)
    el = [jnp.exp(g) for g in glast]
    return dict(eg=eg, decay=decay, kb=kb, p=p, inv=inv, rv=rv, rk=rk, u=u, w=w, qk=qk, att=att, qd=qd, ekt=ekt, kt=kt,
                el=el)


def _gdn_state_stages(u, w, att, qd, kt, el, s_in):
    ws, qs = _mms(w, s_in, _NN), _mms(qd, s_in, _NN)
    yield
    vn = _zip(lambda x, y: x - y, u, ws)
    av, kv = _mms(att, vn, _NN), _mms(kt, vn, _TN)
    yield
    out = _zip(lambda x, y: x + y, qs, av)
    s_out = _zip(lambda s, e, y: s * e + y, s_in, el, kv)
    return dict(vn=vn, out=out, s_out=s_out)


def _gdn_heads_fwd(q, k, v, bcol, gcol, grow, glast, s_in):
    f, = _interleave(_gdn_local_stages(q, k, v, bcol, gcol, grow, glast))
    g, = _interleave(_gdn_state_stages(f["u"], f["w"], f["att"], f["qd"], f["kt"], f["el"], s_in))
    return {**f, **g}


def _head_groups(group, init):
    if GDN_GROUP == GDN_V_HEADS:
        return group(0, init)
    return lax.fori_loop(0, GDN_V_HEADS // GDN_GROUP, lambda gi, c: group(GDN_GROUP * gi, c), init)


def _half(h):
    return h // 2 if isinstance(h, int) else jnp.right_shift(h, 1)


def _gdn_chunk_fwd(qn, kn, v, beta, gc, z, norm_w, name):
    t = qn.shape[0]
    nc = t // CHUNK

    def body(q_ref, k_ref, v_ref, beta_ref, gc_ref, z_ref, nw_ref, o_ref, h_ref, st_ref, state):
        @pl.when(pl.program_id(0) == 0)
        def _():
            state[...] = jnp.zeros_like(state)

        st_ref[0] = state[...]
        gc_all, beta_all = gc_ref[...], beta_ref[...]
        gct = _mmx(_eye(HEAD), gc_all, _NT)
        glast_all = gc_ref[CHUNK - 1:CHUNK, :]
        nw = nw_ref[...]

        def group(h0, carry):
            heads = [h0 + s for s in range(GDN_GROUP)]
            f = _gdn_heads_fwd([q_ref[:, _lanes(_half(h))] for h in heads], [k_ref[:, _lanes(_half(h))] for h in heads],
                               [v_ref[:, _lanes(h)] for h in heads], [_col(beta_all, h) for h in heads],
                               [_col(gc_all, h) for h in heads], [_row(gct, h) for h in heads],
                               [_col(glast_all, h) for h in heads], [state[h] for h in heads])
            for h, s_out, o in zip(heads, f["s_out"], f["out"]):
                state[h] = s_out
                o_ref[:, _lanes(h)] = o
                rstd = lax.rsqrt(jnp.mean(o * o, axis=1, keepdims=True) + RMS_EPS)
                h_ref[:, _lanes(h)] = (o * rstd * nw * _silu(z_ref[:, _lanes(h)])).astype(h_ref.dtype)
            return carry

        _head_groups(group, 0)

    rows = lambda w: pl.BlockSpec((CHUNK, w), lambda i: (i, 0))
    act = lambda w: jax.ShapeDtypeStruct((t, w), F32)
    return _call(body, name, (nc,),
                 [rows(GDN_QK_DIM), rows(GDN_QK_DIM), rows(D_INNER), rows(HEAD), rows(HEAD), rows(D_INNER),
                  _const(1, HEAD)],
                 [rows(D_INNER), rows(D_INNER), pl.BlockSpec((1, GDN_V_HEADS, HEAD, HEAD), lambda i: (i, 0, 0, 0))],
                 [act(D_INNER), jax.ShapeDtypeStruct((t, D_INNER), MM_DTYPE),
                  jax.ShapeDtypeStruct((nc, GDN_V_HEADS, HEAD, HEAD), F32)],
                 [pltpu.VMEM((GDN_V_HEADS, HEAD, HEAD), F32)], ("arbitrary",))(qn, kn, v, beta, gc, z, norm_w)


def _gdn_chunk_bwd(qn, kn, v, beta, gc, z, norm_w, o, states, dh, name):
    t = qn.shape[0]
    nc = t // CHUNK

    def body(q_ref, k_ref, v_ref, beta_ref, gc_ref, z_ref, nw_ref, o_ref, st_ref, dh_ref,
             dq_ref, dk_ref, dv_ref, dz_ref, dbeta_ref, dgc_ref, acc_ref, dstate):
        @pl.when(pl.program_id(0) == 0)
        def _():
            dstate[...] = jnp.zeros_like(dstate)
            acc_ref[...] = jnp.zeros_like(acc_ref)

        gc_all, beta_all = gc_ref[...], beta_ref[...]
        gct = _mmx(_eye(HEAD), gc_all, _NT)
        glast_all = gc_ref[CHUNK - 1:CHUNK, :]
        nw = nw_ref[...]
        ii, jj = _iota((CHUNK, CHUNK), 0), _iota((CHUNK, CHUNK), 1)
        last_row = _iota((CHUNK, 1), 0) == CHUNK - 1

        def group(h0, carry):
            dbeta_acc, dgc_acc, dgrow_acc, dnw_acc = carry
            heads = [h0 + s for s in range(GDN_GROUP)]
            mul, add, sub = (lambda x, y: x * y), (lambda x, y: x + y), (lambda x, y: x - y)
            rowsum = lambda x, y: jnp.sum(x * y, axis=1, keepdims=True)
            q, k = [q_ref[:, _lanes(_half(h))] for h in heads], [k_ref[:, _lanes(_half(h))] for h in heads]
            vv = [v_ref[:, _lanes(h)] for h in heads]
            bcol, gcol = [_col(beta_all, h) for h in heads], [_col(gc_all, h) for h in heads]
            s_in, dsn = [st_ref[0, h] for h in heads], [dstate[h] for h in heads]
            f = _gdn_heads_fwd(q, k, vv, bcol, gcol, [_row(gct, h) for h in heads],
                               [_col(glast_all, h) for h in heads], s_in)
            do = []
            for h in heads:
                oo, zz, dhh = o_ref[:, _lanes(h)], z_ref[:, _lanes(h)], dh_ref[:, _lanes(h)]
                rstd = lax.rsqrt(jnp.mean(oo * oo, axis=1, keepdims=True) + RMS_EPS)
                on, sz = oo * rstd, _silu(zz)
                dnw_acc = dnw_acc + jnp.sum(dhh * on * sz, axis=0, keepdims=True)
                dz_ref[:, _lanes(h)] = (dhh * on * nw * _dsilu(zz)).astype(dz_ref.dtype)
                don = dhh * nw * sz
                do.append(rstd * (don - on * jnp.mean(don * on, axis=1, keepdims=True)))
            decay, eg, inv = f["decay"], f["eg"], f["inv"]
            d_glast = _zip(lambda d, s, e: _total(d * s) * e, dsn, s_in, f["el"])
            dkt = _mms(f["vn"], dsn, _NT)
            dvn = _mms(f["kt"], dsn, _NN)
            dqd = _mms(do, s_in, _NT)
            ds_prev = _zip(lambda d, e, y: d * e + y, dsn, f["el"], _mms(f["qd"], do, _TN))
            datt = _mms(do, f["vn"], _NT)
            dvn = _zip(add, dvn, _mms(f["att"], do, _TN))
            dqk = _zip(mul, datt, decay)
            dq = _zip(lambda x, e, y: x * e + y, dqd, eg, _mms(dqk, k, _NN))
            dk = _mms(dqk, q, _TN)
            ddecay = _zip(mul, datt, f["qk"])
            dgcol = _zip(rowsum, dqd, f["qd"])
            dw = [-x for x in _mms(dvn, s_in, _NT)]
            ds_prev = _zip(sub, ds_prev, _mms(f["w"], dvn, _TN))
            drv, drk = _mms(inv, dvn, _TN), _mms(inv, dw, _TN)
            da = [jnp.where(ii > jj, -x, 0.0) for x in
                  _mms(_zip(_side_by_side, drv, drk), _zip(_side_by_side, f["u"], f["w"]), _NT)]
            dp = _zip(mul, da, decay)
            ddecay = _zip(lambda x, y, z_: x + y * z_, ddecay, da, f["p"])
            dkb = _zip(lambda x, y, e: x + y * e, _mms(dp, k, _NN), drk, eg)
            dk = _zip(add, dk, _mms(dp, f["kb"], _TN))
            dbeta = _zip(add, _zip(rowsum, drv, vv), _zip(rowsum, dkb, k))
            dgcol = _zip(add, dgcol, _zip(rowsum, drk, f["rk"]))
            dk = _zip(lambda x, y, b_, z_, e: x + y * b_ + z_ * e, dk, dkb, bcol, dkt, f["ekt"])
            tail = _zip(mul, dkt, f["kt"])
            d_glast = _zip(lambda x, y: x + _total(y), d_glast, tail)
            e_ = _zip(mul, ddecay, decay)
            dgcol = _zip(lambda x, t_, e, gl: x - jnp.sum(t_, axis=1, keepdims=True) + jnp.sum(e, axis=1, keepdims=True)
                         + jnp.where(last_row, gl, 0.0), dgcol, tail, e_, d_glast)
            for i_ in range(0, len(heads), 2):
                dq_ref[:, _lanes(_half(heads[i_]))] = dq[i_] + dq[i_ + 1]
                dk_ref[:, _lanes(_half(heads[i_]))] = dk[i_] + dk[i_ + 1]
            for i_, h in enumerate(heads):
                dstate[h] = ds_prev[i_]
                dv_ref[:, _lanes(h)] = drv[i_] * bcol[i_]
                dbeta_acc = _put_col(dbeta_acc, dbeta[i_], h)
                dgc_acc = _put_col(dgc_acc, dgcol[i_], h)
                dgrow_acc = _put_row(dgrow_acc, -jnp.sum(e_[i_], axis=0, keepdims=True), h)
            return dbeta_acc, dgc_acc, dgrow_acc, dnw_acc

        zero = jnp.zeros((CHUNK, HEAD), F32)
        dbeta_acc, dgc_acc, dgrow_acc, dnw_acc = _head_groups(
            group, (zero, zero, jnp.zeros((HEAD, CHUNK), F32), jnp.zeros((1, HEAD), F32)))
        dbeta_ref[...] = dbeta_acc
        dgc_ref[...] = dgc_acc + _mmx(_eye(CHUNK), dgrow_acc, _NT)
        acc_ref[...] += _put_sub(jnp.zeros((8, HEAD), F32), dnw_acc, 0)

    rows = lambda w: pl.BlockSpec((CHUNK, w), lambda i: (nc - 1 - i, 0))
    act = lambda w: jax.ShapeDtypeStruct((t, w), F32)
    return _call(body, name, (nc,),
                 [rows(GDN_QK_DIM), rows(GDN_QK_DIM), rows(D_INNER), rows(HEAD), rows(HEAD), rows(D_INNER),
                  _const(1, HEAD), rows(D_INNER),
                  pl.BlockSpec((1, GDN_V_HEADS, HEAD, HEAD), lambda i: (nc - 1 - i, 0, 0, 0)), rows(D_INNER)],
                 [rows(GDN_QK_DIM), rows(GDN_QK_DIM), rows(D_INNER), rows(D_INNER), rows(HEAD), rows(HEAD), _const(8, HEAD)],
                 [act(GDN_QK_DIM), act(GDN_QK_DIM), act(D_INNER), jax.ShapeDtypeStruct((t, D_INNER), MM_DTYPE), act(HEAD),
                  act(HEAD), jax.ShapeDtypeStruct((8, HEAD), F32)],
                 [pltpu.VMEM((GDN_V_HEADS, HEAD, HEAD), F32)], ("arbitrary",)
                 )(qn, kn, v, beta, gc, z, norm_w, o, states, dh)


def _sc_gate_fwd(bg, cv, z, name):
    t, w = bg.shape

    def body(b_ref, c_ref, z_ref, o_ref):
        o_ref[...] = (b_ref[...] * c_ref[...] * _silu(z_ref[...])).astype(o_ref.dtype)

    return _call(body, name, (t // EW_TB,), [_rows(w)] * 3, _rows(w), jax.ShapeDtypeStruct((t, w), MM_DTYPE),
                 semantics=("parallel",))(bg, cv, z)


def _sc_gate_bwd(dh, bg, cv, z, name):
    t, w = bg.shape

    def body(d_ref, b_ref, c_ref, z_ref, db_ref, dc_ref, dz_ref):
        d, b, c, zz = d_ref[...], b_ref[...], c_ref[...], z_ref[...]
        sz = _silu(zz)
        db_ref[...] = (d * c * sz).astype(db_ref.dtype)
        dc_ref[...] = d * b * sz
        dz_ref[...] = (d * b * c * _dsilu(zz)).astype(dz_ref.dtype)

    act, act_mm = jax.ShapeDtypeStruct((t, w), F32), jax.ShapeDtypeStruct((t, w), MM_DTYPE)
    return _call(body, name, (t // EW_TB,), [_rows(w)] * 4, [_rows(w)] * 3, [act_mm, act, act_mm],
                 semantics=("parallel",))(dh, bg, cv, z)


XBC_B = D_INNER
XBC_C = D_INNER + SSD_GROUPS * SSD_STATE


def _ssd_scalars(dtp, dt_bias, a_log):
    dt = _softplus(dtp + dt_bias)
    a = -jnp.exp(a_log)
    da = dt * a
    ac = _mmx(_chunk_mask(CHUNK, False), da, _NN)
    act = _mmx(_eye(HEAD), ac, _NT)
    aclast = jnp.sum(jnp.where(_iota(ac.shape, 0) == CHUNK - 1, ac, 0.0), axis=0, keepdims=True)
    return dt, a, da, ac, act, aclast


def _ssd_pairs_fwd(x2, bg, cg, cb, dt, ac, act, aclast, s2):
    ii, jj = _iota((CHUNK, CHUNK), 0), _iota((CHUNK, CHUNK), 1)
    half = _iota((CHUNK, HEAD), 1) < 64
    causal = ii >= jj
    pairs = range(len(x2))
    mul = lambda x, y: x * y
    pick = lambda a, b: jnp.where(half, a, b)
    aca, acb = [_col(ac, 2 * p) for p in pairs], [_col(ac, 2 * p + 1) for p in pairs]
    la, lb = [_col(aclast, 2 * p) for p in pairs], [_col(aclast, 2 * p + 1) for p in pairs]
    dt2 = [pick(_col(dt, 2 * p), _col(dt, 2 * p + 1)) for p in pairs]
    xdt = _zip(mul, x2, dt2)
    sega = [jnp.exp(jnp.where(causal, aca[p] - _row(act, 2 * p), NEG_BIG)) for p in pairs]
    segb = [jnp.exp(jnp.where(causal, acb[p] - _row(act, 2 * p + 1), NEG_BIG)) for p in pairs]
    ma, mb = _zip(mul, sega, cb), _zip(mul, segb, cb)
    ydiag = _zip(pick, _mms(ma, xdt, _NN), _mms(mb, xdt, _NN))
    cdec = _zip(lambda a, b: pick(jnp.exp(a), jnp.exp(b)), aca, acb)
    cs = _mms(cg, s2, _NT)
    tail = _zip(lambda l1, a, l2, b: pick(jnp.exp(l1 - a), jnp.exp(l2 - b)), la, aca, lb, acb)
    zt = _zip(mul, xdt, tail)
    ea, eb = [jnp.exp(x) for x in la], [jnp.exp(x) for x in lb]
    tot = _zip(lambda a, b: jnp.where(_iota((HEAD, 1), 0) < 64, a, b), ea, eb)
    s_out = _zip(lambda s, t_, y: s * t_ + y, s2, tot, _mms(zt, bg, _TN))
    return dict(half=half, dt2=dt2, xdt=xdt, sega=sega, segb=segb, ma=ma, mb=mb, ydiag=ydiag, cdec=cdec, cs=cs,
                tail=tail, zt=zt, ea=ea, eb=eb, tot=tot, s_out=s_out)


def _ssd_group_inputs(cx_ref):
    cxb = [cx_ref[:, XBC_B + g * SSD_STATE: XBC_B + (g + 1) * SSD_STATE] for g in range(SSD_GROUPS)]
    cxc = [cx_ref[:, XBC_C + g * SSD_STATE: XBC_C + (g + 1) * SSD_STATE] for g in range(SSD_GROUPS)]
    bg, cg = [_silu(x) for x in cxb], [_silu(x) for x in cxc]
    return cxb, cxc, bg, cg, _mms(cg, bg, _NT)


def _per_pair(group_list):
    return [group_list[p // (SSD_PAIRS // SSD_GROUPS)] for p in range(SSD_PAIRS)]


def _ssd_chunk_fwd(cx, dtp, z, dt_bias, a_log, dskip, norm_w, name):
    t = cx.shape[0]
    nc = t // CHUNK
    gw = D_INNER // SSD_GROUPS

    def body(cx_ref, dtp_ref, z_ref, db_ref, al_ref, sk_ref, nw_ref, y_ref, h_ref, st_ref, state):
        @pl.when(pl.program_id(0) == 0)
        def _():
            state[...] = jnp.zeros_like(state)

        st_ref[0] = state[...]
        dt, _, _, ac, act, aclast = _ssd_scalars(dtp_ref[...], db_ref[...], al_ref[...])
        _, _, bg, cg, cb = _ssd_group_inputs(cx_ref)
        x2 = [_silu(cx_ref[:, _lanes(p)]) for p in range(SSD_PAIRS)]
        f = _ssd_pairs_fwd(x2, _per_pair(bg), _per_pair(cg), _per_pair(cb), dt, ac, act, aclast,
                           [state[p] for p in range(SSD_PAIRS)])
        for p in range(SSD_PAIRS):
            state[p] = f["s_out"][p]
            y_ref[:, _lanes(p)] = f["ydiag"][p] + f["cs"][p] * f["cdec"][p] + sk_ref[:, _lanes(p)] * x2[p]
        for g in range(SSD_GROUPS):
            sl = slice(g * gw, (g + 1) * gw)
            yg = y_ref[:, sl] * _silu(z_ref[:, sl])
            rstd = lax.rsqrt(jnp.mean(yg * yg, axis=1, keepdims=True) + RMS_EPS)
            h_ref[:, sl] = (yg * rstd * nw_ref[:, sl]).astype(h_ref.dtype)

    rows = lambda w: pl.BlockSpec((CHUNK, w), lambda i: (i, 0))
    act_ = lambda w: jax.ShapeDtypeStruct((t, w), F32)
    return _call(body, name, (nc,),
                 [rows(SSD_CONV_DIM), rows(HEAD), rows(D_INNER), _const(1, HEAD), _const(1, HEAD),
                  _const(1, D_INNER), _const(1, D_INNER)],
                 [rows(D_INNER), rows(D_INNER), pl.BlockSpec((1, SSD_PAIRS, HEAD, SSD_STATE), lambda i: (i, 0, 0, 0))],
                 [act_(D_INNER), jax.ShapeDtypeStruct((t, D_INNER), MM_DTYPE),
                  jax.ShapeDtypeStruct((nc, SSD_PAIRS, HEAD, SSD_STATE), F32)],
                 [pltpu.VMEM((SSD_PAIRS, HEAD, SSD_STATE), F32)], ("arbitrary",)
                 )(cx, dtp, z, dt_bias, a_log, dskip, norm_w)


def _ssd_chunk_bwd(cx, dtp, z, dt_bias, a_log, dskip, norm_w, y, states, dh, name):
    t = cx.shape[0]
    nc = t // CHUNK
    gw = D_INNER // SSD_GROUPS

    def body(cx_ref, dtp_ref, z_ref, db_ref, al_ref, sk_ref, nw_ref, y_ref, st_ref, dh_ref,
             dcx_ref, ddtp_ref, dz_ref, wide_ref, acc_ref, dstate, dy_s):
        @pl.when(pl.program_id(0) == 0)
        def _():
            dstate[...] = jnp.zeros_like(dstate)
            wide_ref[...] = jnp.zeros_like(wide_ref)
            acc_ref[...] = jnp.zeros_like(acc_ref)

        dtp = dtp_ref[...]
        dt, a, da, ac, act, aclast = _ssd_scalars(dtp, db_ref[...], al_ref[...])
        ii, jj = _iota((CHUNK, CHUNK), 0), _iota((CHUNK, CHUNK), 1)
        last_row = _iota((CHUNK, 1), 0) == CHUNK - 1
        top = _iota((HEAD, SSD_STATE), 0) < 64
        for g in range(SSD_GROUPS):
            sl = slice(g * gw, (g + 1) * gw)
            yy, zz, dhh, nw = y_ref[:, sl], z_ref[:, sl], dh_ref[:, sl], nw_ref[:, sl]
            sz = _silu(zz)
            yg = yy * sz
            rstd = lax.rsqrt(jnp.mean(yg * yg, axis=1, keepdims=True) + RMS_EPS)
            n = yg * rstd
            dn = dhh * nw
            dyg = rstd * (dn - n * jnp.mean(dn * n, axis=1, keepdims=True))
            dy_s[:, sl] = dyg * sz
            dz_ref[:, sl] = (dyg * yy * _dsilu(zz)).astype(dz_ref.dtype)
            wide_ref[0:1, sl] += jnp.sum(dhh * n, axis=0, keepdims=True)

        pairs = range(SSD_PAIRS)
        mul, add, sub = (lambda x, y: x * y), (lambda x, y: x + y), (lambda x, y: x - y)
        rowsum = lambda x: jnp.sum(x, axis=1, keepdims=True)
        cxb, cxc, bg, cg, cb = _ssd_group_inputs(cx_ref)
        bgp, cgp = _per_pair(bg), _per_pair(cg)
        cxx = [cx_ref[:, _lanes(p)] for p in pairs]
        x2 = [_silu(x) for x in cxx]
        s2, dsn = [st_ref[0, p] for p in pairs], [dstate[p] for p in pairs]
        dy2 = [dy_s[:, _lanes(p)] for p in pairs]
        f = _ssd_pairs_fwd(x2, bgp, cgp, _per_pair(cb), dt, ac, act, aclast, s2)
        half = f["half"]
        lo = lambda x: jnp.where(half, x, 0.0)
        dx2 = [dy2[p] * sk_ref[:, _lanes(p)] for p in pairs]
        for p in pairs:
            wide_ref[1:2, _lanes(p)] += jnp.sum(dy2[p] * x2[p], axis=0, keepdims=True)
        gg = _zip(mul, dy2, f["cdec"])
        dc_p = _mms(gg, s2, _NN)
        ds_prev = _zip(lambda d, t_, y: d * t_ + y, dsn, f["tot"], _mms(gg, cgp, _TN))
        t1 = _zip(lambda d, c, e: d * c * e, dy2, f["cs"], f["cdec"])
        dac_a = [rowsum(lo(x)) for x in t1]
        dac_b = _zip(lambda x, a_: rowsum(x) - a_, t1, dac_a)
        dya = [lo(x) for x in dy2]
        dma, dmb = _mms(dya, f["xdt"], _NT), _mms(_zip(sub, dy2, dya), f["xdt"], _NT)
        dxdt = _zip(lambda a_, b_: jnp.where(half, a_, b_), _mms(f["ma"], dy2, _TN), _mms(f["mb"], dy2, _TN))
        dcb_p = _zip(lambda a_, sa, b_, sb: a_ * sa + b_ * sb, dma, f["sega"], dmb, f["segb"])
        ea_, eb_ = _zip(mul, dma, f["ma"]), _zip(mul, dmb, f["mb"])
        dac_a = _zip(lambda x, e: x + rowsum(e), dac_a, ea_)
        dac_b = _zip(lambda x, e: x + rowsum(e), dac_b, eb_)
        dzt = _mms(bgp, dsn, _NT)
        db_p = _mms(f["zt"], dsn, _NN)
        dxdt = _zip(lambda x, d, t_: x + d * t_, dxdt, dzt, f["tail"])
        t2 = _zip(mul, dzt, f["zt"])
        t2a = [rowsum(lo(x)) for x in t2]
        t2b = _zip(lambda x, a_: rowsum(x) - a_, t2, t2a)
        t3 = _zip(mul, dsn, s2)
        t3a = [_total(x[:64]) for x in t3]
        dla = _zip(lambda x, y, e: _total(x) + y * e, t2a, t3a, f["ea"])
        dlb = _zip(lambda x, y, e: _total(x) + _total(y[64:]) * e, t2b, t3, f["eb"])
        dac_a = _zip(lambda x, y, l: x - y + jnp.where(last_row, l, 0.0), dac_a, t2a, dla)
        dac_b = _zip(lambda x, y, l: x - y + jnp.where(last_row, l, 0.0), dac_b, t2b, dlb)
        dx2 = _zip(lambda x, d, t_: x + d * t_, dx2, dxdt, f["dt2"])
        t4 = _zip(mul, dxdt, x2)
        t4a = [rowsum(lo(x)) for x in t4]
        t4b = _zip(lambda x, a_: rowsum(x) - a_, t4, t4a)
        zero = jnp.zeros((CHUNK, HEAD), F32)
        ddt_acc, dac_acc, drow_acc = zero, zero, jnp.zeros((HEAD, CHUNK), F32)
        for p in pairs:
            dcx_ref[:, _lanes(p)] = dx2[p] * _dsilu(cxx[p])
            dstate[p] = ds_prev[p]
            ddt_acc = _put_col(_put_col(ddt_acc, t4a[p], 2 * p), t4b[p], 2 * p + 1)
            dac_acc = _put_col(_put_col(dac_acc, dac_a[p], 2 * p), dac_b[p], 2 * p + 1)
            drow_acc = _put_row(_put_row(drow_acc, -jnp.sum(ea_[p], axis=0, keepdims=True), 2 * p),
                                -jnp.sum(eb_[p], axis=0, keepdims=True), 2 * p + 1)
        per = SSD_PAIRS // SSD_GROUPS
        gsum = lambda xs: [functools.reduce(add, xs[g * per:(g + 1) * per]) for g in range(SSD_GROUPS)]
        dcb = gsum(dcb_p)
        dc = _zip(add, gsum(dc_p), _mms(dcb, bg, _NN))
        db = _zip(add, gsum(db_p), _mms(dcb, cg, _TN))
        for g in range(SSD_GROUPS):
            dcx_ref[:, XBC_B + g * SSD_STATE: XBC_B + (g + 1) * SSD_STATE] = db[g] * _dsilu(cxb[g])
            dcx_ref[:, XBC_C + g * SSD_STATE: XBC_C + (g + 1) * SSD_STATE] = dc[g] * _dsilu(cxc[g])
        dac = dac_acc + _mmx(_eye(CHUNK), drow_acc, _NT)
        dda = _mmx(_chunk_mask(CHUNK, True), dac, _NN)
        ddt = ddt_acc + dda * a
        ddtp = ddt * _sig(dtp + db_ref[...])
        ddtp_ref[...] = ddtp.astype(ddtp_ref.dtype)
        acc = _put_sub(jnp.zeros((8, HEAD), F32), jnp.sum(dda * da, axis=0, keepdims=True), 0)
        acc_ref[...] += _put_sub(acc, jnp.sum(ddtp, axis=0, keepdims=True), 1)

    rows = lambda w: pl.BlockSpec((CHUNK, w), lambda i: (nc - 1 - i, 0))
    act_ = lambda w: jax.ShapeDtypeStruct((t, w), F32)
    return _call(body, name, (nc,),
                 [rows(SSD_CONV_DIM), rows(HEAD), rows(D_INNER), _const(1, HEAD), _const(1, HEAD),
                  _const(1, D_INNER), _const(1, D_INNER), rows(D_INNER),
                  pl.BlockSpec((1, SSD_PAIRS, HEAD, SSD_STATE), lambda i: (nc - 1 - i, 0, 0, 0)), rows(D_INNER)],
                 [rows(SSD_CONV_DIM), rows(HEAD), rows(D_INNER), _const(8, D_INNER), _const(8, HEAD)],
                 [act_(SSD_CONV_DIM), jax.ShapeDtypeStruct((t, HEAD), MM_DTYPE), jax.ShapeDtypeStruct((t, D_INNER), MM_DTYPE),
                  jax.ShapeDtypeStruct((8, D_INNER), F32),
                  jax.ShapeDtypeStruct((8, HEAD), F32)],
                 [pltpu.VMEM((SSD_PAIRS, HEAD, SSD_STATE), F32), pltpu.VMEM((CHUNK, D_INNER), F32)], ("arbitrary",)
                 )(cx, dtp, z, dt_bias, a_log, dskip, norm_w, y, states, dh)


LN_TB = 512


def _ln_stats(x, y):
    u = ALPHA * x + y
    mu = jnp.mean(u, axis=1, keepdims=True)
    cen = u - mu
    rstd = lax.rsqrt(jnp.mean(cen * cen, axis=1, keepdims=True) + LN_EPS)
    return cen * rstd


def _ln_fwd(x, y, g, b, name):
    t, d = x.shape

    def body(x_ref, y_ref, g_ref, b_ref, o_ref, omm_ref):
        out = _ln_stats(x_ref[...], y_ref[...]) * g_ref[...] + b_ref[...]
        o_ref[...] = out
        omm_ref[...] = out.astype(omm_ref.dtype)

    return _call(body, name, (t // LN_TB,), [_rows(d, LN_TB), _rows(d, LN_TB), _const(1, d), _const(1, d)],
                 [_rows(d, LN_TB)] * 2, [jax.ShapeDtypeStruct((t, d), F32), jax.ShapeDtypeStruct((t, d), MM_DTYPE)],
                 semantics=("parallel",))(x, y, g, b)


def _ln_bwd(dout, x, y, g, name):
    t, d = x.shape

    def body(d_ref, x_ref, y_ref, g_ref, du_ref, dumm_ref, acc_ref):
        u = ALPHA * x_ref[...] + y_ref[...]
        mu = jnp.mean(u, axis=1, keepdims=True)
        cen = u - mu
        rstd = lax.rsqrt(jnp.mean(cen * cen, axis=1, keepdims=True) + LN_EPS)
        xh = cen * rstd
        do = d_ref[...]
        dxh = do * g_ref[...]
        du = rstd * (dxh - jnp.mean(dxh, axis=1, keepdims=True) - xh * jnp.mean(dxh * xh, axis=1, keepdims=True))
        du_ref[...] = du
        dumm_ref[...] = du.astype(dumm_ref.dtype)
        acc = _put_sub(jnp.zeros((8, d), F32), jnp.sum(do * xh, axis=0, keepdims=True), 0)
        acc = _put_sub(acc, jnp.sum(do, axis=0, keepdims=True), 1)

        @pl.when(pl.program_id(0) == 0)
        def _():
            acc_ref[...] = jnp.zeros_like(acc_ref)

        acc_ref[...] += acc

    return _call(body, name, (t // LN_TB,), [_rows(d, LN_TB)] * 3 + [_const(1, d)],
                 [_rows(d, LN_TB), _rows(d, LN_TB), _const(8, d)],
                 [jax.ShapeDtypeStruct((t, d), F32), jax.ShapeDtypeStruct((t, d), MM_DTYPE),
                  jax.ShapeDtypeStruct((8, d), F32)],
                 semantics=("arbitrary",))(dout, x, y, g)


def _loss_head(out, target, name):
    t, d = out.shape

    def body(o_ref, t_ref, d_ref, acc_ref):
        err = o_ref[...] - t_ref[...]
        d_ref[...] = err * (1.0 / d)

        @pl.when(pl.program_id(0) == 0)
        def _():
            acc_ref[...] = jnp.zeros_like(acc_ref)

        acc_ref[...] += _put_sub(jnp.zeros((8, d), F32), jnp.sum(err * err, axis=0, keepdims=True), 0)

    return _call(body, name, (t // LN_TB,), [_rows(d, LN_TB)] * 2, [_rows(d, LN_TB), _const(8, d)],
                 [jax.ShapeDtypeStruct((t, d), F32), jax.ShapeDtypeStruct((8, d), F32)],
                 semantics=("arbitrary",))(out, target)


def _adamw(w, gslots, m, v, name):
    r, c = w.shape
    rb = _tile_rows(r)
    c1 = 1.0 - ADAM_B1 ** ADAM_STEP
    c2 = 1.0 - ADAM_B2 ** ADAM_STEP

    def body(w_ref, g_ref, m_ref, v_ref, go_ref, d_ref, mo_ref, vo_ref):
        g = g_ref[0].astype(F32)
        for s in range(1, N_DEV):
            g = g + g_ref[s].astype(F32)
        mn = ADAM_B1 * m_ref[...] + (1.0 - ADAM_B1) * g
        vn = ADAM_B2 * v_ref[...] + (1.0 - ADAM_B2) * (g * g)
        go_ref[...] = g
        mo_ref[...] = mn
        vo_ref[...] = vn
        d_ref[...] = -ADAM_LR * ((mn / c1) / (jnp.sqrt(vn / c2) + ADAM_EPS) + ADAM_WD * w_ref[...])

    blk = pl.BlockSpec((rb, c), lambda i: (i, 0))
    sds = jax.ShapeDtypeStruct((r, c), F32)
    return _call(body, name, (r // rb,), [blk, pl.BlockSpec((N_DEV, rb, c), lambda i: (0, i, 0)), blk, blk],
                 [blk] * 4, [sds] * 4, semantics=("parallel",))(w, gslots, m, v)


def _tile_rows(r):
    for rb in (256, 128, 64, 32, 16, 8):
        if r % rb == 0:
            return rb
    return r


def _pack(arrs, lead=0):
    flats = []
    for a in arrs:
        f = a.reshape(a.shape[:lead] + (-1,)).astype(F32)
        flats.append(jnp.pad(f, [(0, 0)] * lead + [(0, (-f.shape[-1]) % 128)]))
    v = jnp.concatenate(flats, axis=-1)
    v = jnp.pad(v, [(0, 0)] * lead + [(0, (-v.shape[-1]) % 1024)])
    return v.reshape(v.shape[:lead] + (-1, 128))


def _unpack(buf, shapes, lead=0):
    flat = buf.reshape(buf.shape[:lead] + (-1,))
    outs, off = [], 0
    for s in shapes:
        n = math.prod(s)
        outs.append(flat[..., off:off + n].reshape(buf.shape[:lead] + tuple(s)))
        off += n + (-n) % 128
    return outs


def _cols_gathered(g):
    n, l, r, c = g.shape
    return g.transpose(1, 2, 0, 3).reshape(l, r, n * c)


def _cols_to_slabs(full):
    l, r, c = full.shape
    return full.reshape(l, r, N_DEV, c // N_DEV).transpose(2, 0, 1, 3)


def _rows_gathered(g):
    n, l, r, c = g.shape
    return g.transpose(1, 0, 2, 3).reshape(l, n * r, c)


def _rows_to_slabs(full):
    l, r, c = full.shape
    return full.reshape(l, N_DEV, r // N_DEV, c).transpose(1, 0, 2, 3)


def _pad_cols(w, at, width):
    return jnp.pad(w, ((0, 0), (at, width - at - w.shape[1])))


def _pad_lanes(v, width=HEAD):
    return jnp.pad(v.reshape(1, -1), ((0, 0), (0, width - v.size)))


def _taps8(w, bias=None):
    rows = [w] if bias is None else [w, bias.reshape(1, -1)]
    w8 = jnp.concatenate(rows, axis=0)
    return jnp.pad(w8, ((0, 8 - w8.shape[0]), (0, 0)))


class _Carrier:
    def __init__(self):
        self.jobs, self.got = {}, {}

    def put(self, matmul_name, key, src, slabs):
        self.jobs[matmul_name] = (key, src, slabs)

    def matmul(self, a, b, mode, name, **kw):
        job = self.jobs.pop(name, None)
        if job is None:
            return _matmul(a, b, mode, name, **kw)
        key, src, slabs = job
        out, self.got[key] = _matmul(a, b, mode, name, carry=(src, slabs), **kw)
        return out


def _gdn_forward(x, p, tag, mm):
    pq = mm(x, p["w_qkv"], "nn", tag + "_in_qkv")
    z = mm(x, p["w_z"], "nn", tag + "_in_z")
    ba = mm(x, p["w_ba"], "nn", tag + "_in_ba")
    c = _conv_fwd(pq, p["conv8"], 4, tag + "_conv")
    qn, kn, v, beta, gc = _gdn_ew_fwd(c, ba, p["a_log"], p["dt_bias"], tag + "_ew")
    o, h, states = _gdn_chunk_fwd(qn, kn, v, beta, gc, z, p["norm_w"], tag + "_chunk")
    y = mm(h, p["w_out"], "nn", tag + "_out")
    return y, dict(pq=pq, z=z, ba=ba, c=c, qn=qn, kn=kn, v=v, beta=beta, gc=gc, o=o, h=h, states=states)


def _gdn_backward(x, du, du_mm, p, s, tag, mm, ship):
    dh = mm(du_mm, p["w_out"], "nt", tag + "_bwd_dh")
    g_out = mm(s["h"], du_mm, "tn", tag + "_bwd_wout")
    ship("w_out", g_out)
    dq, dk, dv, dz, dbeta, dgc, nacc = _gdn_chunk_bwd(
        s["qn"], s["kn"], s["v"], s["beta"], s["gc"], s["z"], p["norm_w"], s["o"], s["states"], dh, tag + "_bwd_chunk")
    dc, dba, sacc = _gdn_ew_bwd(s["c"], s["ba"], p["a_log"], p["dt_bias"], dq, dk, dv, dbeta, dgc, tag + "_bwd_ew")
    dpq, dconv = _conv_bwd(dc, s["pq"], p["conv8"], 4, tag + "_bwd_conv")
    g_qkv = mm(x, dpq, "tn", tag + "_bwd_w_qkv")
    g_z = mm(x, dz, "tn", tag + "_bwd_w_z")
    g_ba = mm(x, dba, "tn", tag + "_bwd_w_ba")
    g_in = jnp.concatenate([g_qkv, g_z, g_ba[:, :GDN_V_HEADS], g_ba[:, HEAD:HEAD + GDN_V_HEADS]], axis=1)
    ship("w_in", g_in)
    dx = mm(dpq, p["w_qkv"], "nt", tag + "_bwd_dx_qkv", add=du, add_scale=ALPHA)
    dx = mm(dz, p["w_z"], "nt", tag + "_bwd_dx_z", add=dx)
    dx = mm(dba, p["w_ba"], "nt", tag + "_bwd_dx_ba", add=dx)
    grads = dict(w_in=g_in, w_out=g_out, conv_w=dconv[:4], a_log=sacc[0, :GDN_V_HEADS], dt_bias=sacc[1, :GDN_V_HEADS],
                 norm_w=nacc[0])
    return dx, grads


def _sc_forward(x, p, tag, mm):
    hh = mm(x, p["w_h"], "nn", tag + "_in_h")
    bg = mm(x, p["w_b"], "nn", tag + "_in_b")
    cg = mm(x, p["w_c"], "nn", tag + "_in_c")
    z = mm(x, p["w_z"], "nn", tag + "_in_z")
    cv = _conv_fwd(cg, p["conv8"], 3, tag + "_conv", u2=hh)
    h = _sc_gate_fwd(bg, cv, z, tag + "_gate")
    y = mm(h, p["w_out"], "nn", tag + "_out")
    return y, dict(hh=hh, bg=bg, cg=cg, z=z, cv=cv, h=h)


def _sc_backward(x, du, du_mm, p, s, tag, mm, ship):
    dh = mm(du_mm, p["w_out"], "nt", tag + "_bwd_dh")
    g_out = mm(s["h"], du_mm, "tn", tag + "_bwd_wout")
    ship("w_out", g_out)
    dbg, dcv, dz = _sc_gate_bwd(dh, s["bg"], s["cv"], s["z"], tag + "_bwd_gate")
    dcg, dhh, dconv = _conv_bwd(dcv, s["cg"], p["conv8"], 3, tag + "_bwd_conv", u2=s["hh"])
    g_in = jnp.concatenate([mm(x, d, "tn", tag + "_bwd_w_" + n)
                            for n, d in (("h", dhh), ("b", dbg), ("c", dcg), ("z", dz))], axis=1)
    ship("w_in", g_in)
    dx = mm(dhh, p["w_h"], "nt", tag + "_bwd_dx_h", add=du, add_scale=ALPHA)
    dx = mm(dbg, p["w_b"], "nt", tag + "_bwd_dx_b", add=dx)
    dx = mm(dcg, p["w_c"], "nt", tag + "_bwd_dx_c", add=dx)
    dx = mm(dz, p["w_z"], "nt", tag + "_bwd_dx_z", add=dx)
    return dx, dict(w_in=g_in, w_out=g_out, conv_w=dconv[:3])


def _ssd_forward(x, p, tag, mm):
    z = mm(x, p["w_z"], "nn", tag + "_in_z")
    xbc = mm(x, p["w_xbc"], "nn", tag + "_in_xbc")
    dtp = mm(x, p["w_dt"], "nn", tag + "_in_dt")
    cx = _conv_fwd(xbc, p["conv8"], 4, tag + "_conv", bias=True)
    y, h, states = _ssd_chunk_fwd(cx, dtp, z, p["dt_bias"], p["a_log"], p["dskip"], p["norm_w"], tag + "_chunk")
    out = mm(h, p["w_out"], "nn", tag + "_out")
    return out, dict(z=z, xbc=xbc, dtp=dtp, cx=cx, y=y, h=h, states=states)


def _ssd_backward(x, du, du_mm, p, s, tag, mm, ship):
    dh = mm(du_mm, p["w_out"], "nt", tag + "_bwd_dh")
    g_out = mm(s["h"], du_mm, "tn", tag + "_bwd_wout")
    ship("w_out", g_out)
    dcx, ddtp, dz, wide, acc = _ssd_chunk_bwd(s["cx"], s["dtp"], s["z"], p["dt_bias"], p["a_log"], p["dskip"],
                                              p["norm_w"], s["y"], s["states"], dh, tag + "_bwd_chunk")
    dxbc, dconv = _conv_bwd(dcx, s["xbc"], p["conv8"], 4, tag + "_bwd_conv")
    g_dt = mm(x, ddtp, "tn", tag + "_bwd_w_dt")
    g_in = jnp.concatenate([mm(x, dz, "tn", tag + "_bwd_w_z"), mm(x, dxbc, "tn", tag + "_bwd_w_xbc"),
                            g_dt[:, :32]], axis=1)
    ship("w_in", g_in)
    dx = mm(dz, p["w_z"], "nt", tag + "_bwd_dx_z", add=du, add_scale=ALPHA)
    dx = mm(dxbc, p["w_xbc"], "nt", tag + "_bwd_dx_xbc", add=dx)
    dx = mm(ddtp, p["w_dt"], "nt", tag + "_bwd_dx_dt", add=dx)
    grads = dict(w_in=g_in, w_out=g_out, conv_w=dconv[:4], conv_b=dconv[4], a_log=acc[0, :32], dt_bias=acc[1, :32],
                 d_skip=jnp.sum(wide[1].reshape(32, 64), axis=1), norm_w=wide[0])
    return dx, grads


_WEIGHTS = ['gdn_w_in', 'gdn_conv_w', 'gdn_a_log', 'gdn_dt_bias', 'gdn_norm_w', 'gdn_w_out', 'sc_w_in', 'sc_conv_w',
            'sc_w_out', 'ssd_w_in', 'ssd_conv_w', 'ssd_conv_b', 'ssd_a_log', 'ssd_dt_bias', 'ssd_d_skip',
            'ssd_norm_w', 'ssd_w_out', 'ln_g', 'ln_b']
_BIG = {'gdn_w_in': 'cols', 'gdn_w_out': 'rows', 'sc_w_in': 'cols', 'sc_w_out': 'rows', 'ssd_w_in': 'cols',
        'ssd_w_out': 'rows'}
_SMALL_SHARDED = ['gdn_conv_w', 'sc_conv_w', 'ssd_conv_w', 'ssd_conv_b', 'ssd_norm_w']
_SMALL = [n for n in _WEIGHTS if n not in _BIG]


def kernel(x, gdn_w_in, gdn_conv_w, gdn_a_log, gdn_dt_bias, gdn_norm_w, gdn_w_out, sc_w_in, sc_conv_w, sc_w_out, ssd_w_in, ssd_conv_w, ssd_conv_b, ssd_a_log, ssd_dt_bias, ssd_d_skip, ssd_norm_w, ssd_w_out, ln_g, ln_b, loss_target, m_gdn_w_in, m_gdn_conv_w, m_gdn_a_log, m_gdn_dt_bias, m_gdn_norm_w, m_gdn_w_out, m_sc_w_in, m_sc_conv_w, m_sc_w_out, m_ssd_w_in, m_ssd_conv_w, m_ssd_conv_b, m_ssd_a_log, m_ssd_dt_bias, m_ssd_d_skip, m_ssd_norm_w, m_ssd_w_out, m_ln_g, m_ln_b, v_gdn_w_in, v_gdn_conv_w, v_gdn_a_log, v_gdn_dt_bias, v_gdn_norm_w, v_gdn_w_out, v_sc_w_in, v_sc_conv_w, v_sc_w_out, v_ssd_w_in, v_ssd_conv_w, v_ssd_conv_b, v_ssd_a_log, v_ssd_dt_bias, v_ssd_d_skip, v_ssd_norm_w, v_ssd_w_out, v_ln_g, v_ln_b):
    args = locals()
    wts = {n: args[n] for n in _WEIGHTS}
    mom = {n: args["m_" + n] for n in _WEIGHTS}
    vel = {n: args["v_" + n] for n in _WEIGHTS}
    me = 4 * lax.axis_index("x") + 2 * lax.axis_index("y") + lax.axis_index("c")
    x0, target = x[0], loss_target[0]

    car = _Carrier()
    shard = lambda n, j: wts[n][j:j + 1].astype(MM_DTYPE)
    gathered_w = lambda n, j: (_cols_gathered if _BIG[n] == "cols" else _rows_gathered)(car.got[n, j])[0]

    for n in ('gdn_w_in', 'gdn_w_out'):
        car.got[n, 0] = _exchange(shard(n, 0), "gather_%s0" % n, slabs=False)
    riders = {0: [("l0_gdn_in_qkv", 'sc_w_in', 0), ("l0_gdn_in_z", 'sc_w_out', 0), ("l0_gdn_out", 'ssd_w_out', 0)],
              1: [("l1_sc_in_h", 'ssd_w_in', 0)],
              2: [("l2_ssd_in_xbc", 'gdn_w_in', 1), ("l2_ssd_in_z", 'gdn_w_out', 1)]}
    full = {}
    small_shapes = [wts[n].shape for n in _SMALL_SHARDED]
    gathered = _exchange(_pack([wts[n] for n in _SMALL_SHARDED]), "gather_small", slabs=False)
    for n, g in zip(_SMALL_SHARDED, _unpack(gathered, small_shapes, lead=1)):
        full[n] = jnp.moveaxis(g, 0, -2).reshape(g.shape[1:-1] + (N_DEV * g.shape[-1],))
    for n in _SMALL:
        full.setdefault(n, wts[n])

    def gdn_params(j):
        w = gathered_w('gdn_w_in', j)
        return dict(w_qkv=w[:, :GDN_CONV_DIM], w_z=w[:, GDN_CONV_DIM:GDN_CONV_DIM + D_INNER],
                    w_ba=jnp.concatenate([_pad_cols(w[:, 6144:6160], 0, HEAD), _pad_cols(w[:, 6160:6176], 0, HEAD)], 1),
                    conv8=_taps8(full['gdn_conv_w'][j]), a_log=_pad_lanes(full['gdn_a_log'][j]),
                    dt_bias=_pad_lanes(full['gdn_dt_bias'][j]), norm_w=full['gdn_norm_w'][j].reshape(1, HEAD),
                    w_out=gathered_w('gdn_w_out', j))

    def sc_params():
        w = gathered_w('sc_w_in', 0)
        return dict(w_h=w[:, :2048], w_b=w[:, 2048:4096], w_c=w[:, 4096:6144], w_z=w[:, 6144:],
                    conv8=_taps8(full['sc_conv_w'][0]), w_out=gathered_w('sc_w_out', 0))

    def ssd_params():
        w = gathered_w('ssd_w_in', 0)
        return dict(w_z=w[:, :D_INNER], w_xbc=w[:, D_INNER:D_INNER + SSD_CONV_DIM],
                    w_dt=_pad_cols(w[:, D_INNER + SSD_CONV_DIM:], 0, HEAD),
                    conv8=_taps8(full['ssd_conv_w'][0], full['ssd_conv_b'][0]), a_log=_pad_lanes(full['ssd_a_log'][0]),
                    dt_bias=_pad_lanes(full['ssd_dt_bias'][0]),
                    dskip=jnp.repeat(full['ssd_d_skip'][0], 64).reshape(1, D_INNER),
                    norm_w=full['ssd_norm_w'][0].reshape(1, D_INNER), w_out=gathered_w('ssd_w_out', 0))

    layers = [("gdn", _gdn_forward, _gdn_backward, lambda: gdn_params(0)), ("sc", _sc_forward, _sc_backward, sc_params),
              ("ssd", _ssd_forward, _ssd_backward, ssd_params), ("gdn", _gdn_forward, _gdn_backward, lambda: gdn_params(1))]

    acts, acts_mm, ys, saved, params = [x0], [x0.astype(MM_DTYPE)], [], [], []
    for i, (kind, fwd, _, make_params) in enumerate(layers):
        params.append(make_params())
        for matmul_name, n, j in riders.get(i, ()):
            car.put(matmul_name, (n, j), shard(n, j), False)
        y, s = fwd(acts_mm[-1], params[i], "l%d_%s" % (i, kind), car.matmul)
        out, out_mm = _ln_fwd(acts[-1], y, full['ln_g'][i].reshape(1, -1), full['ln_b'][i].reshape(1, -1), "l%d_ln" % i)
        acts.append(out)
        acts_mm.append(out_mm)
        ys.append(y)
        saved.append(s)
    dact, loss_acc = _loss_head(acts[-1], target, "loss_head")
    loss = lax.psum(0.5 / D_MODEL * jnp.sum(loss_acc[0]), ("x", "y", "c"))

    grad_riders = {3: dict(w_out="l3_gdn_bwd_w_qkv", w_in="l3_gdn_bwd_dx_qkv"),
                   2: dict(w_out="l2_ssd_bwd_w_z", w_in="l2_ssd_bwd_dx_xbc"),
                   1: dict(w_out="l1_sc_bwd_w_h", w_in="l0_gdn_bwd_w_qkv"),
                   0: dict(w_out="l0_gdn_bwd_w_z", w_in="l0_gdn_bwd_dx_qkv")}

    def shipper(i):
        def ship(key, g):
            slabs = _cols_to_slabs(g[None]) if key == 'w_in' else _rows_to_slabs(g[None])
            car.put(grad_riders[i][key], ('grad', i, key), slabs.astype(MM_DTYPE), True)
        return ship

    lg = [None] * DEPTH
    d_ln_g, d_ln_b = [None] * DEPTH, [None] * DEPTH
    for i in reversed(range(DEPTH)):
        kind, _, bwd, _ = layers[i]
        du, du_mm, acc = _ln_bwd(dact, acts[i], ys[i], full['ln_g'][i].reshape(1, -1), "l%d_ln_bwd" % i)
        d_ln_g[i], d_ln_b[i] = acc[0], acc[1]
        dact, lg[i] = bwd(acts_mm[i], du, du_mm, params[i], saved[i], "l%d_%s" % (i, kind), car.matmul, shipper(i))
    assert not car.jobs, car.jobs
    grad_x = dact[None]

    stack = lambda k: jnp.stack([lg[0][k], lg[3][k]])
    local = {
        'gdn_conv_w': stack('conv_w'), 'gdn_a_log': stack('a_log'),
        'gdn_dt_bias': stack('dt_bias'), 'gdn_norm_w': stack('norm_w'),
        'sc_conv_w': lg[1]['conv_w'][None],
        'ssd_conv_w': lg[2]['conv_w'][None], 'ssd_conv_b': lg[2]['conv_b'][None],
        'ssd_a_log': lg[2]['a_log'][None], 'ssd_dt_bias': lg[2]['dt_bias'][None], 'ssd_d_skip': lg[2]['d_skip'][None],
        'ssd_norm_w': lg[2]['norm_w'][None],
        'ln_g': jnp.stack(d_ln_g), 'ln_b': jnp.stack(d_ln_b)}

    out = {}
    layers_of = {'gdn': (0, 3), 'sc': (1,), 'ssd': (2,)}
    for n in _BIG:
        kind, key = n.split('_', 1)
        recv = jnp.concatenate([car.got['grad', i, key] for i in layers_of[kind]], axis=1)
        shp = wts[n].shape
        r, c = shp[0] * shp[1], shp[2]
        res = _adamw(wts[n].reshape(r, c), recv.reshape(N_DEV, r, c), mom[n].reshape(r, c), vel[n].reshape(r, c),
                     "adamw_" + n)
        out[n] = [a.reshape(shp) for a in res]
    full_shapes = [local[n].shape for n in _SMALL]
    gathered = _exchange(_pack([local[n] for n in _SMALL]), "gather_small_grads", slabs=False)
    gs = []
    for n, g in zip(_SMALL, _unpack(gathered, full_shapes, lead=1)):
        if n in _SMALL_SHARDED:
            width = wts[n].shape[-1]
            g = lax.dynamic_slice_in_dim(g, me * width, width, axis=g.ndim - 1)
        gs.append(g)
    shapes = [wts[n].shape for n in _SMALL]
    res = _adamw(_pack([wts[n] for n in _SMALL]), _pack(gs, lead=1), _pack([mom[n] for n in _SMALL]),
                 _pack([vel[n] for n in _SMALL]), "adamw_small")
    for k, n in enumerate(_SMALL):
        out[n] = [_unpack(a, shapes)[k] for a in res]

    return (loss, grad_x, *[out[n][0] for n in _WEIGHTS], *[out[n][1] for n in _WEIGHTS],
            *[out[n][2] for n in _WEIGHTS], *[out[n][3] for n in _WEIGHTS])
```

```python
import functools
import math

import jax
import jax.numpy as jnp
from jax import lax
from jax.experimental import pallas as pl
from jax.experimental.pallas import tpu as pltpu

F32 = jnp.float32
MM_DTYPE = jnp.bfloat16

N_DEV = 8
D_MODEL = 1024
D_INNER = 2048
CHUNK = 64
HEAD = 128
GDN_V_HEADS = 16
GDN_GROUP = 16
GDN_QK_HEADS = 8
GDN_QK_DIM = 1024
GDN_CONV_DIM = 4096
SSD_PAIRS = 16
SSD_GROUPS = 4
SSD_STATE = 128
SSD_CONV_DIM = 3072
DEPTH = 4
ALPHA = (2 * DEPTH) ** 0.25
RMS_EPS = 1e-6
LN_EPS = 1e-5
L2_EPS = 1e-6
ADAM_LR, ADAM_B1, ADAM_B2, ADAM_EPS, ADAM_WD, ADAM_STEP = 0.001, 0.9, 0.999, 1e-08, 0.01, 10

VMEM_LIMIT_BYTES = 48 * 1024 * 1024
NEG_BIG = -1e30

_NN = (((1,), (0,)), ((), ()))
_NT = (((1,), (1,)), ((), ()))
_TN = (((0,), (0,)), ((), ()))


def _mm(a, b, dims):
    return lax.dot_general(a.astype(MM_DTYPE), b.astype(MM_DTYPE), dims, preferred_element_type=F32)


def _mmx(a, b, dims):
    return lax.dot_general(a, b, dims, precision=lax.Precision.HIGHEST, preferred_element_type=F32)


def _iota(shape, dim):
    return lax.broadcasted_iota(jnp.int32, shape, dim)


def _eye(n):
    return (_iota((n, n), 0) == _iota((n, n), 1)).astype(F32)


def _sig(x):
    return jax.nn.sigmoid(x)


def _silu(x):
    return x * _sig(x)


def _dsilu(x):
    s = _sig(x)
    return s * (1.0 + x * (1.0 - s))


def _softplus(x):
    return jnp.maximum(x, 0.0) + jnp.log(1.0 + jnp.exp(-jnp.abs(x)))


def _col(x, h):
    return jnp.sum(jnp.where(_iota(x.shape, 1) == h, x, 0.0), axis=1, keepdims=True)


def _row(x, h):
    return jnp.sum(jnp.where(_iota(x.shape, 0) == h, x, 0.0), axis=0, keepdims=True)


def _put_col(acc, col, h):
    return jnp.where(_iota(acc.shape, 1) == h, col, acc)


def _put_row(acc, row, h):
    return jnp.where(_iota(acc.shape, 0) == h, row, acc)


def _put_sub(acc, row, j):
    return acc + jnp.where(_iota(acc.shape, 0) == j, row, 0.0)


def _lanes(h):
    return pl.ds(h * HEAD, HEAD) if isinstance(h, int) else pl.ds(pl.multiple_of(h * HEAD, HEAD), HEAD)


def _total(x):
    return jnp.sum(jnp.sum(x, axis=0, keepdims=True), axis=1, keepdims=True)


def _call(body, name, grid, in_specs, out_specs, out_shape, scratch_shapes=(), semantics=None):
    return pl.pallas_call(
        body, name=name, grid=grid, in_specs=in_specs, out_specs=out_specs, out_shape=out_shape,
        scratch_shapes=list(scratch_shapes),
        compiler_params=pltpu.CompilerParams(dimension_semantics=semantics, vmem_limit_bytes=VMEM_LIMIT_BYTES))


def _tile(n, pref):
    if n <= pref:
        return n
    t = pref
    while n % t:
        t -= 128
    return t


def _exchange_copies(src_ref, out_ref, send_sems, recv_sems, local_sem, slabs):
    x, y, c = lax.axis_index("x"), lax.axis_index("y"), lax.axis_index("c")
    me = 4 * x + 2 * y + c
    mine = src_ref.at[me] if slabs else src_ref
    local = pltpu.make_async_copy(mine, out_ref.at[me], local_sem)
    sends, recvs = [], []
    for r in range(1, N_DEV):
        px = 1 - x if r & 4 else x
        py = 1 - y if r & 2 else y
        pc = 1 - c if r & 1 else c
        peer = 4 * px + 2 * py + pc
        kw = dict(send_sem=send_sems.at[r - 1], recv_sem=recv_sems.at[r - 1], device_id=(px, py, pc),
                  device_id_type=pl.DeviceIdType.MESH)
        sends.append(pltpu.make_async_remote_copy(src_ref=src_ref.at[peer] if slabs else src_ref,
                                                  dst_ref=out_ref.at[me], **kw))
        recvs.append(pltpu.make_async_remote_copy(src_ref=mine, dst_ref=out_ref.at[peer], **kw))
    return local, sends, recvs


def _gather_copies(src_ref, out_ref, send_sems, recv_sems, local_sem):
    x, y, c = lax.axis_index("x"), lax.axis_index("y"), lax.axis_index("c")
    chips = [(1 - x, y), (x, 1 - y), (1 - x, 1 - y)]
    sibling = (x, y, 1 - c)

    def slot(px, py, pc):
        return out_ref.at[4 * px + 2 * py + pc]

    def copy(k, src, dst, to):
        return pltpu.make_async_remote_copy(src_ref=src, dst_ref=dst, send_sem=send_sems.at[k], recv_sem=recv_sems.at[k],
                                            device_id=to, device_id_type=pl.DeviceIdType.MESH)

    mine = slot(x, y, c)
    local = pltpu.make_async_copy(src_ref, mine, local_sem)
    first = [copy(0, src_ref, mine, sibling)] + [copy(1 + j, src_ref, mine, (*chip, c)) for j, chip in enumerate(chips)]
    landed = [copy(1 + j, src_ref, slot(*chip, c), sibling) for j, chip in enumerate(chips)]
    passed = [copy(4 + j, slot(*chip, c), slot(*chip, c), sibling) for j, chip in enumerate(chips)]
    from_sibling = [copy(0, src_ref, slot(x, y, 1 - c), sibling)] + \
        [copy(4 + j, src_ref, slot(*chip, 1 - c), sibling) for j, chip in enumerate(chips)]
    return local, first, landed, passed, from_sibling


def _exchange_start(*refs, slabs):
    if not slabs:
        local, first = _gather_copies(*refs)[:2]
        local.start()
        for cp in first:
            cp.start()
        return
    local, sends, _ = _exchange_copies(*refs, slabs=slabs)
    local.start()
    for cp in sends:
        cp.start()


def _exchange_wait(*refs, slabs):
    if not slabs:
        local, first, landed, passed, from_sibling = _gather_copies(*refs)
        for arrived, onward in zip(landed, passed):
            arrived.wait_recv()
            onward.start()
        for cp in from_sibling:
            cp.wait_recv()
        for cp in first + passed:
            cp.wait_send()
        local.wait()
        return
    local, sends, recvs = _exchange_copies(*refs, slabs=slabs)
    for cp in recvs:
        cp.wait_recv()
    for cp in sends:
        cp.wait_send()
    local.wait()


def _exchange_sems():
    return [pltpu.SemaphoreType.DMA((N_DEV - 1,)), pltpu.SemaphoreType.DMA((N_DEV - 1,)), pltpu.SemaphoreType.DMA(())]


def _exchange_shape(src, slabs):
    return jax.ShapeDtypeStruct((N_DEV,) + tuple(src.shape[1:] if slabs else src.shape), src.dtype)


def _exchange(src, name, slabs):
    def body(*refs):
        _exchange_start(*refs, slabs=slabs)
        _exchange_wait(*refs, slabs=slabs)

    return pl.pallas_call(
        body, name=name,
        in_specs=[pl.BlockSpec(memory_space=pl.ANY)], out_specs=pl.BlockSpec(memory_space=pl.ANY),
        out_shape=_exchange_shape(src, slabs), scratch_shapes=_exchange_sems(),
    )(src)


MM_TM, MM_TN, MM_TK = 1024, 1024, 1024


def _matmul(a, b, mode, name, add=None, add_scale=1.0, carry=None):
    if mode == "nn":
        (m, k), (_, n) = a.shape, b.shape
    elif mode == "nt":
        (m, k), (n, _) = a.shape, b.shape
    else:
        (k, m), (_, n) = a.shape, b.shape
    tk = _tile(k, 2 * MM_TK)
    tm, tn = _tile(m, 2 * MM_TM if mode == "nn" and add is None and tk <= MM_TK else MM_TM), _tile(n, MM_TN)
    nk = k // tk
    grid = (m // tm, n // tn, nk)
    dims = {"nn": _NN, "nt": _NT, "tn": _TN}[mode]
    n_in = 2 + (add is not None)

    def body(*refs):
        a_ref, b_ref, o_ref = refs[0], refs[1], refs[n_in + (carry is not None)]
        if carry is not None:
            ex = (refs[n_in], refs[n_in + 2]) + tuple(refs[n_in + 3:])
            step = (pl.program_id(0) * grid[1] + pl.program_id(1)) * grid[2] + pl.program_id(2)

            @pl.when(step == 0)
            def _():
                _exchange_start(*ex, slabs=carry[1])

        part = _mm(a_ref[...], b_ref[...], dims)
        first = part if add is None else part + add_scale * refs[2][...]
        if nk == 1:
            o_ref[...] = first
        else:
            @pl.when(pl.program_id(2) == 0)
            def _():
                o_ref[...] = first

            @pl.when(pl.program_id(2) > 0)
            def _():
                o_ref[...] += part

        if carry is not None:
            @pl.when(step == grid[0] * grid[1] * grid[2] - 1)
            def _():
                _exchange_wait(*ex, slabs=carry[1])

    if mode == "nn":
        specs = [pl.BlockSpec((tm, tk), lambda i, j, q: (i, q)), pl.BlockSpec((tk, tn), lambda i, j, q: (q, j))]
    elif mode == "nt":
        specs = [pl.BlockSpec((tm, tk), lambda i, j, q: (i, q)), pl.BlockSpec((tn, tk), lambda i, j, q: (j, q))]
    else:
        specs = [pl.BlockSpec((tk, tm), lambda i, j, q: (q, i)), pl.BlockSpec((tk, tn), lambda i, j, q: (q, j))]
    out_spec = pl.BlockSpec((tm, tn), lambda i, j, q: (i, j))
    args = [a, b]
    if add is not None:
        specs.append(out_spec)
        args.append(add)
    out_shape = jax.ShapeDtypeStruct((m, n), F32)
    if carry is None:
        return _call(body, name, grid, specs, out_spec, out_shape, semantics=("parallel", "parallel", "arbitrary"))(*args)
    hbm = pl.BlockSpec(memory_space=pl.ANY)
    return _call(body, name, grid, specs + [hbm], [out_spec, hbm], [out_shape, _exchange_shape(*carry)],
                 _exchange_sems(), ("arbitrary", "arbitrary", "arbitrary"))(*args, carry[0])


CONV_TB = 512
CONV_CB = 1024
HALO = 8


def _conv_specs(t, cb_n, tb):
    nb = tb // HALO
    blk = pl.BlockSpec((tb, cb_n), lambda c, i: (i, c))
    prev = pl.BlockSpec((HALO, cb_n), lambda c, i: (jnp.maximum(i * nb - 1, 0), c))
    nxt = pl.BlockSpec((HALO, cb_n), lambda c, i: (jnp.minimum((i + 1) * nb, t // HALO - 1), c))
    w = pl.BlockSpec((8, cb_n), lambda c, i: (0, c))
    return blk, prev, nxt, w


def _shift_down(ext, s, tb):
    return (pltpu.roll(ext, s, 0) if s else ext)[HALO:HALO + tb]


def _shift_up(ext, s, tb):
    n = ext.shape[0]
    return (pltpu.roll(ext, n - s, 0) if s else ext)[0:tb]


def _conv_fwd(u, w8, ktaps, name, u2=None, bias=False):
    t, ch = u.shape
    cb_n = min(CONV_CB, ch)
    tb = min(2 * CONV_TB, t)
    two = u2 is not None

    def body(*refs):
        if two:
            u_ref, up_ref, v_ref, vp_ref, w_ref, o_ref = refs
        else:
            u_ref, up_ref, w_ref, o_ref = refs
        first = pl.program_id(1) == 0
        blk, halo = u_ref[...], up_ref[...]
        if two:
            blk, halo = blk * v_ref[...], halo * vp_ref[...]
        ext = jnp.concatenate([jnp.where(first, 0.0, halo), blk], axis=0)
        acc = jnp.zeros((tb, cb_n), F32)
        for j in range(ktaps):
            acc = acc + w_ref[j:j + 1, :] * _shift_down(ext, ktaps - 1 - j, tb)
        if bias:
            acc = acc + w_ref[ktaps:ktaps + 1, :]
        o_ref[...] = acc

    blk, prev, _, wspec = _conv_specs(t, cb_n, tb)
    specs, args = [blk, prev], [u, u]
    if two:
        specs += [blk, prev]
        args += [u2, u2]
    specs.append(wspec)
    args.append(w8)
    return _call(body, name, (ch // cb_n, t // tb), specs, blk, jax.ShapeDtypeStruct((t, ch), F32),
                 semantics=("parallel", "parallel"))(*args)


def _conv_bwd(dc, u, w8, ktaps, name, u2=None):
    t, ch = u.shape
    cb_n = min(CONV_CB, ch)
    two = u2 is not None
    tb = min(CONV_TB, t)

    def body(*refs):
        if two:
            dc_ref, dn_ref, u_ref, v_ref, w_ref, du_ref, dv_ref, dw_ref = refs
        else:
            dc_ref, dn_ref, u_ref, w_ref, du_ref, dw_ref = refs
        i = pl.program_id(1)
        d = dc_ref[...]
        dext = jnp.concatenate([d, jnp.where(i == t // tb - 1, 0.0, dn_ref[...])], axis=0)
        blk = u_ref[...] * v_ref[...] if two else u_ref[...]
        du = jnp.zeros((tb, cb_n), F32)
        dw = jnp.zeros((8, cb_n), F32)
        for j in range(ktaps):
            ahead = _shift_up(dext, ktaps - 1 - j, tb)
            du = du + w_ref[j:j + 1, :] * ahead
            dw = _put_sub(dw, jnp.sum(ahead * blk, axis=0, keepdims=True), j)
        dw = _put_sub(dw, jnp.sum(d, axis=0, keepdims=True), ktaps)
        if two:
            du_ref[...] = (du * v_ref[...]).astype(du_ref.dtype)
            dv_ref[...] = (du * u_ref[...]).astype(dv_ref.dtype)
        else:
            du_ref[...] = du.astype(du_ref.dtype)

        @pl.when(i == 0)
        def _():
            dw_ref[...] = jnp.zeros_like(dw_ref)

        dw_ref[...] += dw

    blk, _, nxt, wspec = _conv_specs(t, cb_n, tb)
    specs, args = [blk, nxt, blk], [dc, dc, u]
    if two:
        specs.append(blk)
        args.append(u2)
    specs.append(wspec)
    args.append(w8)
    act = jax.ShapeDtypeStruct((t, ch), MM_DTYPE)
    outs = ([blk, blk, wspec], [act, act, jax.ShapeDtypeStruct((8, ch), F32)]) if two else \
        ([blk, wspec], [act, jax.ShapeDtypeStruct((8, ch), F32)])
    return _call(body, name, (ch // cb_n, t // tb), specs, outs[0], outs[1],
                 semantics=("parallel", "arbitrary"))(*args)


EW_TB = 256


def _chunk_mask(n, upper):
    i, j = _iota((n, n), 0), _iota((n, n), 1)
    same = jnp.right_shift(i, 6) == jnp.right_shift(j, 6)
    return (same & ((j >= i) if upper else (i >= j))).astype(F32)


def _rows(width, tb=EW_TB):
    return pl.BlockSpec((tb, width), lambda i: (i, 0))


def _const(rows, width):
    return pl.BlockSpec((rows, width), lambda i: (0, 0))


def _gdn_ew_fwd(c, ba, a_log, dt_bias, name):
    t = c.shape[0]
    tb = EW_TB

    def body(c_ref, ba_ref, al_ref, db_ref, q_ref, k_ref, v_ref, beta_ref, gc_ref):
        for h in range(GDN_QK_HEADS):
            for base, ref, scale in ((0, q_ref, HEAD ** -0.5), (GDN_QK_DIM, k_ref, 1.0)):
                s = _silu(c_ref[:, base + h * HEAD: base + (h + 1) * HEAD])
                r = lax.rsqrt(jnp.sum(s * s, axis=1, keepdims=True) + L2_EPS)
                ref[:, h * HEAD:(h + 1) * HEAD] = s * (r * scale)
        v_ref[...] = _silu(c_ref[:, 2 * GDN_QK_DIM:])
        beta_ref[...] = _sig(ba_ref[:, :HEAD])
        g = -jnp.exp(al_ref[...]) * _softplus(ba_ref[:, HEAD:] + db_ref[...])
        gc_ref[...] = _mmx(_chunk_mask(tb, False), g, _NN)

    act = lambda w: jax.ShapeDtypeStruct((t, w), F32)
    return _call(body, name, (t // tb,),
                 [_rows(GDN_CONV_DIM), _rows(2 * HEAD), _const(1, HEAD), _const(1, HEAD)],
                 [_rows(GDN_QK_DIM), _rows(GDN_QK_DIM), _rows(D_INNER), _rows(HEAD), _rows(HEAD)],
                 [act(GDN_QK_DIM), act(GDN_QK_DIM), act(D_INNER), act(HEAD), act(HEAD)],
                 semantics=("parallel",))(c, ba, a_log, dt_bias)


def _gdn_ew_bwd(c, ba, a_log, dt_bias, dqh, dkh, dv, dbeta, dgc, name):
    t = c.shape[0]
    tb = EW_TB

    def body(c_ref, ba_ref, al_ref, db_ref, dq_ref, dk_ref, dv_ref, dbeta_ref, dgc_ref, dc_ref, dba_ref, acc_ref):
        for h in range(GDN_QK_HEADS):
            for base, ref, scale in ((0, dq_ref, HEAD ** -0.5), (GDN_QK_DIM, dk_ref, 1.0)):
                cq = c_ref[:, base + h * HEAD: base + (h + 1) * HEAD]
                s = _silu(cq)
                r = lax.rsqrt(jnp.sum(s * s, axis=1, keepdims=True) + L2_EPS)
                dn = ref[:, h * HEAD:(h + 1) * HEAD] * scale
                ds = r * dn - s * (r * r * r) * jnp.sum(dn * s, axis=1, keepdims=True)
                dc_ref[:, base + h * HEAD: base + (h + 1) * HEAD] = ds * _dsilu(cq)
        dc_ref[:, 2 * GDN_QK_DIM:] = dv_ref[...] * _dsilu(c_ref[:, 2 * GDN_QK_DIM:])
        beta = _sig(ba_ref[:, :HEAD])
        dba_ref[:, :HEAD] = (dbeta_ref[...] * beta * (1.0 - beta)).astype(dba_ref.dtype)
        pre = ba_ref[:, HEAD:] + db_ref[...]
        ea = jnp.exp(al_ref[...])
        g = -ea * _softplus(pre)
        dg = _mmx(_chunk_mask(tb, True), dgc_ref[...], _NN)
        da_raw = dg * (-ea) * _sig(pre)
        dba_ref[:, HEAD:] = da_raw.astype(dba_ref.dtype)
        acc = jnp.zeros((8, HEAD), F32)
        acc = _put_sub(acc, jnp.sum(dg * g, axis=0, keepdims=True), 0)
        acc = _put_sub(acc, jnp.sum(da_raw, axis=0, keepdims=True), 1)

        @pl.when(pl.program_id(0) == 0)
        def _():
            acc_ref[...] = jnp.zeros_like(acc_ref)

        acc_ref[...] += acc

    act = lambda w: jax.ShapeDtypeStruct((t, w), F32)
    return _call(body, name, (t // tb,),
                 [_rows(GDN_CONV_DIM), _rows(2 * HEAD), _const(1, HEAD), _const(1, HEAD),
                  _rows(GDN_QK_DIM), _rows(GDN_QK_DIM), _rows(D_INNER), _rows(HEAD), _rows(HEAD)],
                 [_rows(GDN_CONV_DIM), _rows(2 * HEAD), _const(8, HEAD)],
                 [act(GDN_CONV_DIM), jax.ShapeDtypeStruct((t, 2 * HEAD), MM_DTYPE), jax.ShapeDtypeStruct((8, HEAD), F32)],
                 semantics=("arbitrary",))(c, ba, a_log, dt_bias, dqh, dkh, dv, dbeta, dgc)


def _zip(fn, *lists):
    return [fn(*xs) for xs in zip(*lists)]


def _mms(xs, ys, dims):
    return [_mm(x, y, dims) for x, y in zip(xs, ys)]


def _side_by_side(a, b):
    return jnp.concatenate([a, b], axis=1)


def _interleave(*gens):
    results, live = [None] * len(gens), list(range(len(gens)))
    while live:
        for i in list(live):
            try:
                next(gens[i])
            except StopIteration as stop:
                results[i] = stop.value
                live.remove(i)
    return results


def _gdn_local_stages(q, k, v, bcol, gcol, grow, glast):
    ii, jj = _iota((CHUNK, CHUNK), 0), _iota((CHUNK, CHUNK), 1)
    eye = _eye(CHUNK)
    mul = lambda x, y: x * y
    eg = [jnp.exp(g) for g in gcol]
    decay = _zip(lambda gc, gr: jnp.exp(jnp.where(ii >= jj, gc - gr, NEG_BIG)), gcol, grow)
    kb = _zip(mul, k, bcol)
    p, qk = _mms(kb, k, _NT), _mms(q, k, _NT)
    yield
    a = _zip(lambda x, d: jnp.where(ii > jj, x * d, 0.0), p, decay)
    inv, pw = [eye - x for x in a], a
    for _ in range(5):
        pw = _mms(pw, pw, _NN)
        yield
        inv = _zip(lambda x, y: x + y, inv, _mms(inv, pw, _NN))
        yield
    rv, rk = _zip(mul, v, bcol), _zip(mul, kb, eg)
    uw = _mms(inv, _zip(_side_by_side, rv, rk), _NN)
    u, w = [x[:, :HEAD] for x in uw], [x[:, HEAD:] for x in uw]
    yield
    att = _zip(mul, qk, decay)
    qd = _zip(mul, q, eg)
    ekt = _zip(lambda gl, gc: jnp.exp(gl - gc), glast, gcol)
    kt = _zip(mul, k, ekt)
    el = [jnp.exp(g) for g in glast]
    return dict(eg=eg, decay=decay, kb=kb, p=p, inv=inv, rv=rv, rk=rk, u=u, w=w, qk=qk, att=att, qd=qd, ekt=ekt, kt=kt,
                el=el)


def _gdn_state_stages(u, w, att, qd, kt, el, s_in):
    ws, qs = _mms(w, s_in, _NN), _mms(qd, s_in, _NN)
    yield
    vn = _zip(lambda x, y: x - y, u, ws)
    av, kv = _mms(att, vn, _NN), _mms(kt, vn, _TN)
    yield
    out = _zip(lambda x, y: x + y, qs, av)
    s_out = _zip(lambda s, e, y: s * e + y, s_in, el, kv)
    return dict(vn=vn, out=out, s_out=s_out)


def _gdn_heads_fwd(q, k, v, bcol, gcol, grow, glast, s_in):
    f, = _interleave(_gdn_local_stages(q, k, v, bcol, gcol, grow, glast))
    g, = _interleave(_gdn_state_stages(f["u"], f["w"], f["att"], f["qd"], f["kt"], f["el"], s_in))
    return {**f, **g}


def _head_groups(group, init):
    if GDN_GROUP == GDN_V_HEADS:
        return group(0, init)
    return lax.fori_loop(0, GDN_V_HEADS // GDN_GROUP, lambda gi, c: group(GDN_GROUP * gi, c), init)


def _half(h):
    return h // 2 if isinstance(h, int) else jnp.right_shift(h, 1)


def _gdn_chunk_fwd(qn, kn, v, beta, gc, z, norm_w, name):
    t = qn.shape[0]
    nc = t // CHUNK

    def body(q_ref, k_ref, v_ref, beta_ref, gc_ref, z_ref, nw_ref, o_ref, h_ref, st_ref, state):
        @pl.when(pl.program_id(0) == 0)
        def _():
            state[...] = jnp.zeros_like(state)

        st_ref[0] = state[...]
        gc_all, beta_all = gc_ref[...], beta_ref[...]
        gct = _mmx(_eye(HEAD), gc_all, _NT)
        glast_all = gc_ref[CHUNK - 1:CHUNK, :]
        nw = nw_ref[...]

        def group(h0, carry):
            heads = [h0 + s for s in range(GDN_GROUP)]
            f = _gdn_heads_fwd([q_ref[:, _lanes(_half(h))] for h in heads], [k_ref[:, _lanes(_half(h))] for h in heads],
                               [v_ref[:, _lanes(h)] for h in heads], [_col(beta_all, h) for h in heads],
                               [_col(gc_all, h) for h in heads], [_row(gct, h) for h in heads],
                               [_col(glast_all, h) for h in heads], [state[h] for h in heads])
            for h, s_out, o in zip(heads, f["s_out"], f["out"]):
                state[h] = s_out
                o_ref[:, _lanes(h)] = o
                rstd = lax.rsqrt(jnp.mean(o * o, axis=1, keepdims=True) + RMS_EPS)
                h_ref[:, _lanes(h)] = (o * rstd * nw * _silu(z_ref[:, _lanes(h)])).astype(h_ref.dtype)
            return carry

        _head_groups(group, 0)

    rows = lambda w: pl.BlockSpec((CHUNK, w), lambda i: (i, 0))
    act = lambda w: jax.ShapeDtypeStruct((t, w), F32)
    return _call(body, name, (nc,),
                 [rows(GDN_QK_DIM), rows(GDN_QK_DIM), rows(D_INNER), rows(HEAD), rows(HEAD), rows(D_INNER),
                  _const(1, HEAD)],
                 [rows(D_INNER), rows(D_INNER), pl.BlockSpec((1, GDN_V_HEADS, HEAD, HEAD), lambda i: (i, 0, 0, 0))],
                 [act(D_INNER), jax.ShapeDtypeStruct((t, D_INNER), MM_DTYPE),
                  jax.ShapeDtypeStruct((nc, GDN_V_HEADS, HEAD, HEAD), F32)],
                 [pltpu.VMEM((GDN_V_HEADS, HEAD, HEAD), F32)], ("arbitrary",))(qn, kn, v, beta, gc, z, norm_w)


def _gdn_chunk_bwd(qn, kn, v, beta, gc, z, norm_w, o, states, dh, name):
    t = qn.shape[0]
    nc = t // CHUNK

    def body(q_ref, k_ref, v_ref, beta_ref, gc_ref, z_ref, nw_ref, o_ref, st_ref, dh_ref,
             dq_ref, dk_ref, dv_ref, dz_ref, dbeta_ref, dgc_ref, acc_ref, dstate):
        @pl.when(pl.program_id(0) == 0)
        def _():
            dstate[...] = jnp.zeros_like(dstate)
            acc_ref[...] = jnp.zeros_like(acc_ref)

        gc_all, beta_all = gc_ref[...], beta_ref[...]
        gct = _mmx(_eye(HEAD), gc_all, _NT)
        glast_all = gc_ref[CHUNK - 1:CHUNK, :]
        nw = nw_ref[...]
        ii, jj = _iota((CHUNK, CHUNK), 0), _iota((CHUNK, CHUNK), 1)
        last_row = _iota((CHUNK, 1), 0) == CHUNK - 1

        def group(h0, carry):
            dbeta_acc, dgc_acc, dgrow_acc, dnw_acc = carry
            heads = [h0 + s for s in range(GDN_GROUP)]
            mul, add, sub = (lambda x, y: x * y), (lambda x, y: x + y), (lambda x, y: x - y)
            rowsum = lambda x, y: jnp.sum(x * y, axis=1, keepdims=True)
            q, k = [q_ref[:, _lanes(_half(h))] for h in heads], [k_ref[:, _lanes(_half(h))] for h in heads]
            vv = [v_ref[:, _lanes(h)] for h in heads]
            bcol, gcol = [_col(beta_all, h) for h in heads], [_col(gc_all, h) for h in heads]
            s_in, dsn = [st_ref[0, h] for h in heads], [dstate[h] for h in heads]
            f = _gdn_heads_fwd(q, k, vv, bcol, gcol, [_row(gct, h) for h in heads],
                               [_col(glast_all, h) for h in heads], s_in)
            do = []
            for h in heads:
                oo, zz, dhh = o_ref[:, _lanes(h)], z_ref[:, _lanes(h)], dh_ref[:, _lanes(h)]
                rstd = lax.rsqrt(jnp.mean(oo * oo, axis=1, keepdims=True) + RMS_EPS)
                on, sz = oo * rstd, _silu(zz)
                dnw_acc = dnw_acc + jnp.sum(dhh * on * sz, axis=0, keepdims=True)
                dz_ref[:, _lanes(h)] = (dhh * on * nw * _dsilu(zz)).astype(dz_ref.dtype)
                don = dhh * nw * sz
                do.append(rstd * (don - on * jnp.mean(don * on, axis=1, keepdims=True)))
            decay, eg, inv = f["decay"], f["eg"], f["inv"]
            d_glast = _zip(lambda d, s, e: _total(d * s) * e, dsn, s_in, f["el"])
            dkt = _mms(f["vn"], dsn, _NT)
            dvn = _mms(f["kt"], dsn, _NN)
            dqd = _mms(do, s_in, _NT)
            ds_prev = _zip(lambda d, e, y: d * e + y, dsn, f["el"], _mms(f["qd"], do, _TN))
            datt = _mms(do, f["vn"], _NT)
            dvn = _zip(add, dvn, _mms(f["att"], do, _TN))
            dqk = _zip(mul, datt, decay)
            dq = _zip(lambda x, e, y: x * e + y, dqd, eg, _mms(dqk, k, _NN))
            dk = _mms(dqk, q, _TN)
            ddecay = _zip(mul, datt, f["qk"])
            dgcol = _zip(rowsum, dqd, f["qd"])
            dw = [-x for x in _mms(dvn, s_in, _NT)]
            ds_prev = _zip(sub, ds_prev, _mms(f["w"], dvn, _TN))
            drv, drk = _mms(inv, dvn, _TN), _mms(inv, dw, _TN)
            da = [jnp.where(ii > jj, -x, 0.0) for x in
                  _mms(_zip(_side_by_side, drv, drk), _zip(_side_by_side, f["u"], f["w"]), _NT)]
            dp = _zip(mul, da, decay)
            ddecay = _zip(lambda x, y, z_: x + y * z_, ddecay, da, f["p"])
            dkb = _zip(lambda x, y, e: x + y * e, _mms(dp, k, _NN), drk, eg)
            dk = _zip(add, dk, _mms(dp, f["kb"], _TN))
            dbeta = _zip(add, _zip(rowsum, drv, vv), _zip(rowsum, dkb, k))
            dgcol = _zip(add, dgcol, _zip(rowsum, drk, f["rk"]))
            dk = _zip(lambda x, y, b_, z_, e: x + y * b_ + z_ * e, dk, dkb, bcol, dkt, f["ekt"])
            tail = _zip(mul, dkt, f["kt"])
            d_glast = _zip(lambda x, y: x + _total(y), d_glast, tail)
            e_ = _zip(mul, ddecay, decay)
            dgcol = _zip(lambda x, t_, e, gl: x - jnp.sum(t_, axis=1, keepdims=True) + jnp.sum(e, axis=1, keepdims=True)
                         + jnp.where(last_row, gl, 0.0), dgcol, tail, e_, d_glast)
            for i_ in range(0, len(heads), 2):
                dq_ref[:, _lanes(_half(heads[i_]))] = dq[i_] + dq[i_ + 1]
                dk_ref[:, _lanes(_half(heads[i_]))] = dk[i_] + dk[i_ + 1]
            for i_, h in enumerate(heads):
                dstate[h] = ds_prev[i_]
                dv_ref[:, _lanes(h)] = drv[i_] * bcol[i_]
                dbeta_acc = _put_col(dbeta_acc, dbeta[i_], h)
                dgc_acc = _put_col(dgc_acc, dgcol[i_], h)
                dgrow_acc = _put_row(dgrow_acc, -jnp.sum(e_[i_], axis=0, keepdims=True), h)
            return dbeta_acc, dgc_acc, dgrow_acc, dnw_acc

        zero = jnp.zeros((CHUNK, HEAD), F32)
        dbeta_acc, dgc_acc, dgrow_acc, dnw_acc = _head_groups(
            group, (zero, zero, jnp.zeros((HEAD, CHUNK), F32), jnp.zeros((1, HEAD), F32)))
        dbeta_ref[...] = dbeta_acc
        dgc_ref[...] = dgc_acc + _mmx(_eye(CHUNK), dgrow_acc, _NT)
        acc_ref[...] += _put_sub(jnp.zeros((8, HEAD), F32), dnw_acc, 0)

    rows = lambda w: pl.BlockSpec((CHUNK, w), lambda i: (nc - 1 - i, 0))
    act = lambda w: jax.ShapeDtypeStruct((t, w), F32)
    return _call(body, name, (nc,),
                 [rows(GDN_QK_DIM), rows(GDN_QK_DIM), rows(D_INNER), rows(HEAD), rows(HEAD), rows(D_INNER),
                  _const(1, HEAD), rows(D_INNER),
                  pl.BlockSpec((1, GDN_V_HEADS, HEAD, HEAD), lambda i: (nc - 1 - i, 0, 0, 0)), rows(D_INNER)],
                 [rows(GDN_QK_DIM), rows(GDN_QK_DIM), rows(D_INNER), rows(D_INNER), rows(HEAD), rows(HEAD), _const(8, HEAD)],
                 [act(GDN_QK_DIM), act(GDN_QK_DIM), act(D_INNER), jax.ShapeDtypeStruct((t, D_INNER), MM_DTYPE), act(HEAD),
                  act(HEAD), jax.ShapeDtypeStruct((8, HEAD), F32)],
                 [pltpu.VMEM((GDN_V_HEADS, HEAD, HEAD), F32)], ("arbitrary",)
                 )(qn, kn, v, beta, gc, z, norm_w, o, states, dh)


def _sc_gate_fwd(bg, cv, z, name):
    t, w = bg.shape

    def body(b_ref, c_ref, z_ref, o_ref):
        o_ref[...] = (b_ref[...] * c_ref[...] * _silu(z_ref[...])).astype(o_ref.dtype)

    return _call(body, name, (t // EW_TB,), [_rows(w)] * 3, _rows(w), jax.ShapeDtypeStruct((t, w), MM_DTYPE),
                 semantics=("parallel",))(bg, cv, z)


def _sc_gate_bwd(dh, bg, cv, z, name):
    t, w = bg.shape

    def body(d_ref, b_ref, c_ref, z_ref, db_ref, dc_ref, dz_ref):
        d, b, c, zz = d_ref[...], b_ref[...], c_ref[...], z_ref[...]
        sz = _silu(zz)
        db_ref[...] = (d * c * sz).astype(db_ref.dtype)
        dc_ref[...] = d * b * sz
        dz_ref[...] = (d * b * c * _dsilu(zz)).astype(dz_ref.dtype)

    act, act_mm = jax.ShapeDtypeStruct((t, w), F32), jax.ShapeDtypeStruct((t, w), MM_DTYPE)
    return _call(body, name, (t // EW_TB,), [_rows(w)] * 4, [_rows(w)] * 3, [act_mm, act, act_mm],
                 semantics=("parallel",))(dh, bg, cv, z)


XBC_B = D_INNER
XBC_C = D_INNER + SSD_GROUPS * SSD_STATE


def _ssd_scalars(dtp, dt_bias, a_log):
    dt = _softplus(dtp + dt_bias)
    a = -jnp.exp(a_log)
    da = dt * a
    ac = _mmx(_chunk_mask(CHUNK, False), da, _NN)
    act = _mmx(_eye(HEAD), ac, _NT)
    aclast = jnp.sum(jnp.where(_iota(ac.shape, 0) == CHUNK - 1, ac, 0.0), axis=0, keepdims=True)
    return dt, a, da, ac, act, aclast


def _ssd_pairs_fwd(x2, bg, cg, cb, dt, ac, act, aclast, s2):
    ii, jj = _iota((CHUNK, CHUNK), 0), _iota((CHUNK, CHUNK), 1)
    half = _iota((CHUNK, HEAD), 1) < 64
    causal = ii >= jj
    pairs = range(len(x2))
    mul = lambda x, y: x * y
    pick = lambda a, b: jnp.where(half, a, b)
    aca, acb = [_col(ac, 2 * p) for p in pairs], [_col(ac, 2 * p + 1) for p in pairs]
    la, lb = [_col(aclast, 2 * p) for p in pairs], [_col(aclast, 2 * p + 1) for p in pairs]
    dt2 = [pick(_col(dt, 2 * p), _col(dt, 2 * p + 1)) for p in pairs]
    xdt = _zip(mul, x2, dt2)
    sega = [jnp.exp(jnp.where(causal, aca[p] - _row(act, 2 * p), NEG_BIG)) for p in pairs]
    segb = [jnp.exp(jnp.where(causal, acb[p] - _row(act, 2 * p + 1), NEG_BIG)) for p in pairs]
    ma, mb = _zip(mul, sega, cb), _zip(mul, segb, cb)
    ydiag = _zip(pick, _mms(ma, xdt, _NN), _mms(mb, xdt, _NN))
    cdec = _zip(lambda a, b: pick(jnp.exp(a), jnp.exp(b)), aca, acb)
    cs = _mms(cg, s2, _NT)
    tail = _zip(lambda l1, a, l2, b: pick(jnp.exp(l1 - a), jnp.exp(l2 - b)), la, aca, lb, acb)
    zt = _zip(mul, xdt, tail)
    ea, eb = [jnp.exp(x) for x in la], [jnp.exp(x) for x in lb]
    tot = _zip(lambda a, b: jnp.where(_iota((HEAD, 1), 0) < 64, a, b), ea, eb)
    s_out = _zip(lambda s, t_, y: s * t_ + y, s2, tot, _mms(zt, bg, _TN))
    return dict(half=half, dt2=dt2, xdt=xdt, sega=sega, segb=segb, ma=ma, mb=mb, ydiag=ydiag, cdec=cdec, cs=cs,
                tail=tail, zt=zt, ea=ea, eb=eb, tot=tot, s_out=s_out)


def _ssd_group_inputs(cx_ref):
    cxb = [cx_ref[:, XBC_B + g * SSD_STATE: XBC_B + (g + 1) * SSD_STATE] for g in range(SSD_GROUPS)]
    cxc = [cx_ref[:, XBC_C + g * SSD_STATE: XBC_C + (g + 1) * SSD_STATE] for g in range(SSD_GROUPS)]
    bg, cg = [_silu(x) for x in cxb], [_silu(x) for x in cxc]
    return cxb, cxc, bg, cg, _mms(cg, bg, _NT)


def _per_pair(group_list):
    return [group_list[p // (SSD_PAIRS // SSD_GROUPS)] for p in range(SSD_PAIRS)]


def _ssd_chunk_fwd(cx, dtp, z, dt_bias, a_log, dskip, norm_w, name):
    t = cx.shape[0]
    nc = t // CHUNK
    gw = D_INNER // SSD_GROUPS

    def body(cx_ref, dtp_ref, z_ref, db_ref, al_ref, sk_ref, nw_ref, y_ref, h_ref, st_ref, state):
        @pl.when(pl.program_id(0) == 0)
        def _():
            state[...] = jnp.zeros_like(state)

        st_ref[0] = state[...]
        dt, _, _, ac, act, aclast = _ssd_scalars(dtp_ref[...], db_ref[...], al_ref[...])
        _, _, bg, cg, cb = _ssd_group_inputs(cx_ref)
        x2 = [_silu(cx_ref[:, _lanes(p)]) for p in range(SSD_PAIRS)]
        f = _ssd_pairs_fwd(x2, _per_pair(bg), _per_pair(cg), _per_pair(cb), dt, ac, act, aclast,
                           [state[p] for p in range(SSD_PAIRS)])
        for p in range(SSD_PAIRS):
            state[p] = f["s_out"][p]
            y_ref[:, _lanes(p)] = f["ydiag"][p] + f["cs"][p] * f["cdec"][p] + sk_ref[:, _lanes(p)] * x2[p]
        for g in range(SSD_GROUPS):
            sl = slice(g * gw, (g + 1) * gw)
            yg = y_ref[:, sl] * _silu(z_ref[:, sl])
            rstd = lax.rsqrt(jnp.mean(yg * yg, axis=1, keepdims=True) + RMS_EPS)
            h_ref[:, sl] = (yg * rstd * nw_ref[:, sl]).astype(h_ref.dtype)

    rows = lambda w: pl.BlockSpec((CHUNK, w), lambda i: (i, 0))
    act_ = lambda w: jax.ShapeDtypeStruct((t, w), F32)
    return _call(body, name, (nc,),
                 [rows(SSD_CONV_DIM), rows(HEAD), rows(D_INNER), _const(1, HEAD), _const(1, HEAD),
                  _const(1, D_INNER), _const(1, D_INNER)],
                 [rows(D_INNER), rows(D_INNER), pl.BlockSpec((1, SSD_PAIRS, HEAD, SSD_STATE), lambda i: (i, 0, 0, 0))],
                 [act_(D_INNER), jax.ShapeDtypeStruct((t, D_INNER), MM_DTYPE),
                  jax.ShapeDtypeStruct((nc, SSD_PAIRS, HEAD, SSD_STATE), F32)],
                 [pltpu.VMEM((SSD_PAIRS, HEAD, SSD_STATE), F32)], ("arbitrary",)
                 )(cx, dtp, z, dt_bias, a_log, dskip, norm_w)


def _ssd_chunk_bwd(cx, dtp, z, dt_bias, a_log, dskip, norm_w, y, states, dh, name):
    t = cx.shape[0]
    nc = t // CHUNK
    gw = D_INNER // SSD_GROUPS

    def body(cx_ref, dtp_ref, z_ref, db_ref, al_ref, sk_ref, nw_ref, y_ref, st_ref, dh_ref,
             dcx_ref, ddtp_ref, dz_ref, wide_ref, acc_ref, dstate, dy_s):
        @pl.when(pl.program_id(0) == 0)
        def _():
            dstate[...] = jnp.zeros_like(dstate)
            wide_ref[...] = jnp.zeros_like(wide_ref)
            acc_ref[...] = jnp.zeros_like(acc_ref)

        dtp = dtp_ref[...]
        dt, a, da, ac, act, aclast = _ssd_scalars(dtp, db_ref[...], al_ref[...])
        ii, jj = _iota((CHUNK, CHUNK), 0), _iota((CHUNK, CHUNK), 1)
        last_row = _iota((CHUNK, 1), 0) == CHUNK - 1
        for g in range(SSD_GROUPS):
            sl = slice(g * gw, (g + 1) * gw)
            yy, zz, dhh, nw = y_ref[:, sl], z_ref[:, sl], dh_ref[:, sl], nw_ref[:, sl]
            sz = _silu(zz)
            yg = yy * sz
            rstd = lax.rsqrt(jnp.mean(yg * yg, axis=1, keepdims=True) + RMS_EPS)
            n = yg * rstd
            dn = dhh * nw
            dyg = rstd * (dn - n * jnp.mean(dn * n, axis=1, keepdims=True))
            dy_s[:, sl] = dyg * sz
            dz_ref[:, sl] = (dyg * yy * _dsilu(zz)).astype(dz_ref.dtype)
            wide_ref[0:1, sl] += jnp.sum(dhh * n, axis=0, keepdims=True)

        pairs = range(SSD_PAIRS)
        mul, add, sub = (lambda x, y: x * y), (lambda x, y: x + y), (lambda x, y: x - y)
        rowsum = lambda x: jnp.sum(x, axis=1, keepdims=True)
        cxb, cxc, bg, cg, cb = _ssd_group_inputs(cx_ref)
        bgp, cgp = _per_pair(bg), _per_pair(cg)
        cxx = [cx_ref[:, _lanes(p)] for p in pairs]
        x2 = [_silu(x) for x in cxx]
        s2, dsn = [st_ref[0, p] for p in pairs], [dstate[p] for p in pairs]
        dy2 = [dy_s[:, _lanes(p)] for p in pairs]
        f = _ssd_pairs_fwd(x2, bgp, cgp, _per_pair(cb), dt, ac, act, aclast, s2)
        half = f["half"]
        lo = lambda x: jnp.where(half, x, 0.0)
        dx2 = [dy2[p] * sk_ref[:, _lanes(p)] for p in pairs]
        for p in pairs:
            wide_ref[1:2, _lanes(p)] += jnp.sum(dy2[p] * x2[p], axis=0, keepdims=True)
        gg = _zip(mul, dy2, f["cdec"])
        dc_p = _mms(gg, s2, _NN)
        ds_prev = _zip(lambda d, t_, y: d * t_ + y, dsn, f["tot"], _mms(gg, cgp, _TN))
        t1 = _zip(lambda d, c, e: d * c * e, dy2, f["cs"], f["cdec"])
        dac_a = [rowsum(lo(x)) for x in t1]
        dac_b = _zip(lambda x, a_: rowsum(x) - a_, t1, dac_a)
        dya = [lo(x) for x in dy2]
        dma, dmb = _mms(dya, f["xdt"], _NT), _mms(_zip(sub, dy2, dya), f["xdt"], _NT)
        dxdt = _zip(lambda a_, b_: jnp.where(half, a_, b_), _mms(f["ma"], dy2, _TN), _mms(f["mb"], dy2, _TN))
        dcb_p = _zip(lambda a_, sa, b_, sb: a_ * sa + b_ * sb, dma, f["sega"], dmb, f["segb"])
        ea_, eb_ = _zip(mul, dma, f["ma"]), _zip(mul, dmb, f["mb"])
        dac_a = _zip(lambda x, e: x + rowsum(e), dac_a, ea_)
        dac_b = _zip(lambda x, e: x + rowsum(e), dac_b, eb_)
        dzt = _mms(bgp, dsn, _NT)
        db_p = _mms(f["zt"], dsn, _NN)
        dxdt = _zip(lambda x, d, t_: x + d * t_, dxdt, dzt, f["tail"])
        t2 = _zip(mul, dzt, f["zt"])
        t2a = [rowsum(lo(x)) for x in t2]
        t2b = _zip(lambda x, a_: rowsum(x) - a_, t2, t2a)
        t3 = _zip(mul, dsn, s2)
        t3a = [_total(x[:64]) for x in t3]
        dla = _zip(lambda x, y, e: _total(x) + y * e, t2a, t3a, f["ea"])
        dlb = _zip(lambda x, y, e: _total(x) + _total(y[64:]) * e, t2b, t3, f["eb"])
        dac_a = _zip(lambda x, y, l: x - y + jnp.where(last_row, l, 0.0), dac_a, t2a, dla)
        dac_b = _zip(lambda x, y, l: x - y + jnp.where(last_row, l, 0.0), dac_b, t2b, dlb)
        dx2 = _zip(lambda x, d, t_: x + d * t_, dx2, dxdt, f["dt2"])
        t4 = _zip(mul, dxdt, x2)
        t4a = [rowsum(lo(x)) for x in t4]
        t4b = _zip(lambda x, a_: rowsum(x) - a_, t4, t4a)
        zero = jnp.zeros((CHUNK, HEAD), F32)
        ddt_acc, dac_acc, drow_acc = zero, zero, jnp.zeros((HEAD, CHUNK), F32)
        for p in pairs:
            dcx_ref[:, _lanes(p)] = dx2[p] * _dsilu(cxx[p])
            dstate[p] = ds_prev[p]
            ddt_acc = _put_col(_put_col(ddt_acc, t4a[p], 2 * p), t4b[p], 2 * p + 1)
            dac_acc = _put_col(_put_col(dac_acc, dac_a[p], 2 * p), dac_b[p], 2 * p + 1)
            drow_acc = _put_row(_put_row(drow_acc, -jnp.sum(ea_[p], axis=0, keepdims=True), 2 * p),
                                -jnp.sum(eb_[p], axis=0, keepdims=True), 2 * p + 1)
        per = SSD_PAIRS // SSD_GROUPS
        gsum = lambda xs: [functools.reduce(add, xs[g * per:(g + 1) * per]) for g in range(SSD_GROUPS)]
        dcb = gsum(dcb_p)
        dc = _zip(add, gsum(dc_p), _mms(dcb, bg, _NN))
        db = _zip(add, gsum(db_p), _mms(dcb, cg, _TN))
        for g in range(SSD_GROUPS):
            dcx_ref[:, XBC_B + g * SSD_STATE: XBC_B + (g + 1) * SSD_STATE] = db[g] * _dsilu(cxb[g])
            dcx_ref[:, XBC_C + g * SSD_STATE: XBC_C + (g + 1) * SSD_STATE] = dc[g] * _dsilu(cxc[g])
        dac = dac_acc + _mmx(_eye(CHUNK), drow_acc, _NT)
        dda = _mmx(_chunk_mask(CHUNK, True), dac, _NN)
        ddt = ddt_acc + dda * a
        ddtp = ddt * _sig(dtp + db_ref[...])
        ddtp_ref[...] = ddtp.astype(ddtp_ref.dtype)
        acc = _put_sub(jnp.zeros((8, HEAD), F32), jnp.sum(dda * da, axis=0, keepdims=True), 0)
        acc_ref[...] += _put_sub(acc, jnp.sum(ddtp, axis=0, keepdims=True), 1)

    rows = lambda w: pl.BlockSpec((CHUNK, w), lambda i: (nc - 1 - i, 0))
    act_ = lambda w: jax.ShapeDtypeStruct((t, w), F32)
    return _call(body, name, (nc,),
                 [rows(SSD_CONV_DIM), rows(HEAD), rows(D_INNER), _const(1, HEAD), _const(1, HEAD),
                  _const(1, D_INNER), _const(1, D_INNER), rows(D_INNER),
                  pl.BlockSpec((1, SSD_PAIRS, HEAD, SSD_STATE), lambda i: (nc - 1 - i, 0, 0, 0)), rows(D_INNER)],
                 [rows(SSD_CONV_DIM), rows(HEAD), rows(D_INNER), _const(8, D_INNER), _const(8, HEAD)],
                 [act_(SSD_CONV_DIM), jax.ShapeDtypeStruct((t, HEAD), MM_DTYPE), jax.ShapeDtypeStruct((t, D_INNER), MM_DTYPE),
                  jax.ShapeDtypeStruct((8, D_INNER), F32),
                  jax.ShapeDtypeStruct((8, HEAD), F32)],
                 [pltpu.VMEM((SSD_PAIRS, HEAD, SSD_STATE), F32), pltpu.VMEM((CHUNK, D_INNER), F32)], ("arbitrary",)
                 )(cx, dtp, z, dt_bias, a_log, dskip, norm_w, y, states, dh)


LN_TB = 512


def _ln_stats(x, y):
    u = ALPHA * x + y
    mu = jnp.mean(u, axis=1, keepdims=True)
    cen = u - mu
    rstd = lax.rsqrt(jnp.mean(cen * cen, axis=1, keepdims=True) + LN_EPS)
    return cen * rstd


def _ln_fwd(x, y, g, b, name):
    t, d = x.shape

    def body(x_ref, y_ref, g_ref, b_ref, o_ref, omm_ref):
        out = _ln_stats(x_ref[...], y_ref[...]) * g_ref[...] + b_ref[...]
        o_ref[...] = out
        omm_ref[...] = out.astype(omm_ref.dtype)

    return _call(body, name, (t // LN_TB,), [_rows(d, LN_TB), _rows(d, LN_TB), _const(1, d), _const(1, d)],
                 [_rows(d, LN_TB)] * 2, [jax.ShapeDtypeStruct((t, d), F32), jax.ShapeDtypeStruct((t, d), MM_DTYPE)],
                 semantics=("parallel",))(x, y, g, b)


def _ln_loss(x, y, g, b, target, name):
    t, d = x.shape

    def body(x_ref, y_ref, g_ref, b_ref, t_ref, d_ref, acc_ref):
        err = _ln_stats(x_ref[...], y_ref[...]) * g_ref[...] + b_ref[...] - t_ref[...]
        d_ref[...] = err * (1.0 / d)

        @pl.when(pl.program_id(0) == 0)
        def _():
            acc_ref[...] = jnp.zeros_like(acc_ref)

        acc_ref[...] += _put_sub(jnp.zeros((8, d), F32), jnp.sum(err * err, axis=0, keepdims=True), 0)

    return _call(body, name, (t // LN_TB,),
                 [_rows(d, LN_TB), _rows(d, LN_TB), _const(1, d), _const(1, d), _rows(d, LN_TB)],
                 [_rows(d, LN_TB), _const(8, d)],
                 [jax.ShapeDtypeStruct((t, d), F32), jax.ShapeDtypeStruct((8, d), F32)],
                 semantics=("arbitrary",))(x, y, g, b, target)


def _ln_bwd(dout, x, y, g, name):
    t, d = x.shape

    def body(d_ref, x_ref, y_ref, g_ref, du_ref, dumm_ref, acc_ref):
        u = ALPHA * x_ref[...] + y_ref[...]
        mu = jnp.mean(u, axis=1, keepdims=True)
        cen = u - mu
        rstd = lax.rsqrt(jnp.mean(cen * cen, axis=1, keepdims=True) + LN_EPS)
        xh = cen * rstd
        do = d_ref[...]
        dxh = do * g_ref[...]
        du = rstd * (dxh - jnp.mean(dxh, axis=1, keepdims=True) - xh * jnp.mean(dxh * xh, axis=1, keepdims=True))
        du_ref[...] = du
        dumm_ref[...] = du.astype(dumm_ref.dtype)
        acc = _put_sub(jnp.zeros((8, d), F32), jnp.sum(do * xh, axis=0, keepdims=True), 0)
        acc = _put_sub(acc, jnp.sum(do, axis=0, keepdims=True), 1)

        @pl.when(pl.program_id(0) == 0)
        def _():
            acc_ref[...] = jnp.zeros_like(acc_ref)

        acc_ref[...] += acc

    return _call(body, name, (t // LN_TB,), [_rows(d, LN_TB)] * 3 + [_const(1, d)],
                 [_rows(d, LN_TB), _rows(d, LN_TB), _const(8, d)],
                 [jax.ShapeDtypeStruct((t, d), F32), jax.ShapeDtypeStruct((t, d), MM_DTYPE),
                  jax.ShapeDtypeStruct((8, d), F32)],
                 semantics=("arbitrary",))(dout, x, y, g)


def _adamw(w, gslots, m, v, name):
    r, c = w.shape
    rb = _tile_rows(r)
    c1 = 1.0 - ADAM_B1 ** ADAM_STEP
    c2 = 1.0 - ADAM_B2 ** ADAM_STEP

    def body(w_ref, g_ref, m_ref, v_ref, go_ref, d_ref, mo_ref, vo_ref):
        g = g_ref[0].astype(F32)
        for s in range(1, N_DEV):
            g = g + g_ref[s].astype(F32)
        mn = ADAM_B1 * m_ref[...] + (1.0 - ADAM_B1) * g
        vn = ADAM_B2 * v_ref[...] + (1.0 - ADAM_B2) * (g * g)
        go_ref[...] = g
        mo_ref[...] = mn
        vo_ref[...] = vn
        d_ref[...] = -ADAM_LR * ((mn / c1) / (jnp.sqrt(vn / c2) + ADAM_EPS) + ADAM_WD * w_ref[...])

    blk = pl.BlockSpec((rb, c), lambda i: (i, 0))
    sds = jax.ShapeDtypeStruct((r, c), F32)
    return _call(body, name, (r // rb,), [blk, pl.BlockSpec((N_DEV, rb, c), lambda i: (0, i, 0)), blk, blk],
                 [blk] * 4, [sds] * 4, semantics=("parallel",))(w, gslots, m, v)


def _tile_rows(r):
    for rb in (256, 128, 64, 32, 16, 8):
        if r % rb == 0:
            return rb
    return r


def _pack(arrs, lead=0):
    flats = []
    for a in arrs:
        f = a.reshape(a.shape[:lead] + (-1,)).astype(F32)
        flats.append(jnp.pad(f, [(0, 0)] * lead + [(0, (-f.shape[-1]) % 128)]))
    v = jnp.concatenate(flats, axis=-1)
    v = jnp.pad(v, [(0, 0)] * lead + [(0, (-v.shape[-1]) % 1024)])
    return v.reshape(v.shape[:lead] + (-1, 128))


def _unpack(buf, shapes, lead=0):
    flat = buf.reshape(buf.shape[:lead] + (-1,))
    outs, off = [], 0
    for s in shapes:
        n = math.prod(s)
        outs.append(flat[..., off:off + n].reshape(buf.shape[:lead] + tuple(s)))
        off += n + (-n) % 128
    return outs


def _cols_gathered(g):
    n, l, r, c = g.shape
    return g.transpose(1, 2, 0, 3).reshape(l, r, n * c)


def _cols_to_slabs(full):
    l, r, c = full.shape
    return full.reshape(l, r, N_DEV, c // N_DEV).transpose(2, 0, 1, 3)


def _rows_gathered(g):
    n, l, r, c = g.shape
    return g.transpose(1, 0, 2, 3).reshape(l, n * r, c)


def _rows_to_slabs(full):
    l, r, c = full.shape
    return full.reshape(l, N_DEV, r // N_DEV, c).transpose(1, 0, 2, 3)


def _pad_cols(w, at, width):
    return jnp.pad(w, ((0, 0), (at, width - at - w.shape[1])))


def _pad_lanes(v, width=HEAD):
    return jnp.pad(v.reshape(1, -1), ((0, 0), (0, width - v.size)))


def _taps8(w, bias=None):
    rows = [w] if bias is None else [w, bias.reshape(1, -1)]
    w8 = jnp.concatenate(rows, axis=0)
    return jnp.pad(w8, ((0, 8 - w8.shape[0]), (0, 0)))


class _Carrier:
    def __init__(self):
        self.jobs, self.got = {}, {}

    def put(self, matmul_name, key, src, slabs):
        self.jobs[matmul_name] = (key, src, slabs)

    def matmul(self, a, b, mode, name, **kw):
        job = self.jobs.pop(name, None)
        if job is None:
            return _matmul(a, b, mode, name, **kw)
        key, src, slabs = job
        out, self.got[key] = _matmul(a, b, mode, name, carry=(src, slabs), **kw)
        return out


def _gdn_forward(x, p, tag, mm):
    pq = mm(x, p["w_qkv"], "nn", tag + "_in_qkv")
    z = mm(x, p["w_z"], "nn", tag + "_in_z")
    ba = mm(x, p["w_ba"], "nn", tag + "_in_ba")
    c = _conv_fwd(pq, p["conv8"], 4, tag + "_conv")
    qn, kn, v, beta, gc = _gdn_ew_fwd(c, ba, p["a_log"], p["dt_bias"], tag + "_ew")
    o, h, states = _gdn_chunk_fwd(qn, kn, v, beta, gc, z, p["norm_w"], tag + "_chunk")
    y = mm(h, p["w_out"], "nn", tag + "_out")
    return y, dict(pq=pq, z=z, ba=ba, c=c, qn=qn, kn=kn, v=v, beta=beta, gc=gc, o=o, h=h, states=states)


def _gdn_backward(x, du, du_mm, p, s, tag, mm, ship):
    dh = mm(du_mm, p["w_out"], "nt", tag + "_bwd_dh")
    g_out = mm(s["h"], du_mm, "tn", tag + "_bwd_wout")
    ship("w_out", g_out)
    dq, dk, dv, dz, dbeta, dgc, nacc = _gdn_chunk_bwd(
        s["qn"], s["kn"], s["v"], s["beta"], s["gc"], s["z"], p["norm_w"], s["o"], s["states"], dh, tag + "_bwd_chunk")
    dc, dba, sacc = _gdn_ew_bwd(s["c"], s["ba"], p["a_log"], p["dt_bias"], dq, dk, dv, dbeta, dgc, tag + "_bwd_ew")
    dpq, dconv = _conv_bwd(dc, s["pq"], p["conv8"], 4, tag + "_bwd_conv")
    g_qkv = mm(x, dpq, "tn", tag + "_bwd_w_qkv")
    g_z = mm(x, dz, "tn", tag + "_bwd_w_z")
    g_ba = mm(x, dba, "tn", tag + "_bwd_w_ba")
    g_in = jnp.concatenate([g_qkv, g_z, g_ba[:, :GDN_V_HEADS], g_ba[:, HEAD:HEAD + GDN_V_HEADS]], axis=1)
    ship("w_in", g_in)
    dx = mm(dpq, p["w_qkv"], "nt", tag + "_bwd_dx_qkv", add=du, add_scale=ALPHA)
    dx = mm(dz, p["w_z"], "nt", tag + "_bwd_dx_z", add=dx)
    dx = mm(dba, p["w_ba"], "nt", tag + "_bwd_dx_ba", add=dx)
    grads = dict(w_in=g_in, w_out=g_out, conv_w=dconv[:4], a_log=sacc[0, :GDN_V_HEADS], dt_bias=sacc[1, :GDN_V_HEADS],
                 norm_w=nacc[0])
    return dx, grads


def _sc_forward(x, p, tag, mm):
    hh = mm(x, p["w_h"], "nn", tag + "_in_h")
    bg = mm(x, p["w_b"], "nn", tag + "_in_b")
    cg = mm(x, p["w_c"], "nn", tag + "_in_c")
    z = mm(x, p["w_z"], "nn", tag + "_in_z")
    cv = _conv_fwd(cg, p["conv8"], 3, tag + "_conv", u2=hh)
    h = _sc_gate_fwd(bg, cv, z, tag + "_gate")
    y = mm(h, p["w_out"], "nn", tag + "_out")
    return y, dict(hh=hh, bg=bg, cg=cg, z=z, cv=cv, h=h)


def _sc_backward(x, du, du_mm, p, s, tag, mm, ship):
    dh = mm(du_mm, p["w_out"], "nt", tag + "_bwd_dh")
    g_out = mm(s["h"], du_mm, "tn", tag + "_bwd_wout")
    ship("w_out", g_out)
    dbg, dcv, dz = _sc_gate_bwd(dh, s["bg"], s["cv"], s["z"], tag + "_bwd_gate")
    dcg, dhh, dconv = _conv_bwd(dcv, s["cg"], p["conv8"], 3, tag + "_bwd_conv", u2=s["hh"])
    g_in = jnp.concatenate([mm(x, d, "tn", tag + "_bwd_w_" + n)
                            for n, d in (("h", dhh), ("b", dbg), ("c", dcg), ("z", dz))], axis=1)
    ship("w_in", g_in)
    dx = mm(dhh, p["w_h"], "nt", tag + "_bwd_dx_h", add=du, add_scale=ALPHA)
    dx = mm(dbg, p["w_b"], "nt", tag + "_bwd_dx_b", add=dx)
    dx = mm(dcg, p["w_c"], "nt", tag + "_bwd_dx_c", add=dx)
    dx = mm(dz, p["w_z"], "nt", tag + "_bwd_dx_z", add=dx)
    return dx, dict(w_in=g_in, w_out=g_out, conv_w=dconv[:3])


def _ssd_forward(x, p, tag, mm):
    z = mm(x, p["w_z"], "nn", tag + "_in_z")
    xbc = mm(x, p["w_xbc"], "nn", tag + "_in_xbc")
    dtp = mm(x, p["w_dt"], "nn", tag + "_in_dt")
    cx = _conv_fwd(xbc, p["conv8"], 4, tag + "_conv", bias=True)
    y, h, states = _ssd_chunk_fwd(cx, dtp, z, p["dt_bias"], p["a_log"], p["dskip"], p["norm_w"], tag + "_chunk")
    out = mm(h, p["w_out"], "nn", tag + "_out")
    return out, dict(z=z, xbc=xbc, dtp=dtp, cx=cx, y=y, h=h, states=states)


def _ssd_backward(x, du, du_mm, p, s, tag, mm, ship):
    dh = mm(du_mm, p["w_out"], "nt", tag + "_bwd_dh")
    g_out = mm(s["h"], du_mm, "tn", tag + "_bwd_wout")
    ship("w_out", g_out)
    dcx, ddtp, dz, wide, acc = _ssd_chunk_bwd(s["cx"], s["dtp"], s["z"], p["dt_bias"], p["a_log"], p["dskip"],
                                              p["norm_w"], s["y"], s["states"], dh, tag + "_bwd_chunk")
    dxbc, dconv = _conv_bwd(dcx, s["xbc"], p["conv8"], 4, tag + "_bwd_conv")
    g_dt = mm(x, ddtp, "tn", tag + "_bwd_w_dt")
    g_in = jnp.concatenate([mm(x, dz, "tn", tag + "_bwd_w_z"), mm(x, dxbc, "tn", tag + "_bwd_w_xbc"),
                            g_dt[:, :32]], axis=1)
    ship("w_in", g_in)
    dx = mm(dz, p["w_z"], "nt", tag + "_bwd_dx_z", add=du, add_scale=ALPHA)
    dx = mm(dxbc, p["w_xbc"], "nt", tag + "_bwd_dx_xbc", add=dx)
    dx = mm(ddtp, p["w_dt"], "nt", tag + "_bwd_dx_dt", add=dx)
    grads = dict(w_in=g_in, w_out=g_out, conv_w=dconv[:4], conv_b=dconv[4], a_log=acc[0, :32], dt_bias=acc[1, :32],
                 d_skip=jnp.sum(wide[1].reshape(32, 64), axis=1), norm_w=wide[0])
    return dx, grads


_WEIGHTS = ['gdn_w_in', 'gdn_conv_w', 'gdn_a_log', 'gdn_dt_bias', 'gdn_norm_w', 'gdn_w_out', 'sc_w_in', 'sc_conv_w',
            'sc_w_out', 'ssd_w_in', 'ssd_conv_w', 'ssd_conv_b', 'ssd_a_log', 'ssd_dt_bias', 'ssd_d_skip',
            'ssd_norm_w', 'ssd_w_out', 'ln_g', 'ln_b']
_BIG = {'gdn_w_in': 'cols', 'gdn_w_out': 'rows', 'sc_w_in': 'cols', 'sc_w_out': 'rows', 'ssd_w_in': 'cols',
        'ssd_w_out': 'rows'}
_SMALL_SHARDED = ['gdn_conv_w', 'sc_conv_w', 'ssd_conv_w', 'ssd_conv_b', 'ssd_norm_w']
_SMALL = [n for n in _WEIGHTS if n not in _BIG]


def kernel(x, gdn_w_in, gdn_conv_w, gdn_a_log, gdn_dt_bias, gdn_norm_w, gdn_w_out, sc_w_in, sc_conv_w, sc_w_out, ssd_w_in, ssd_conv_w, ssd_conv_b, ssd_a_log, ssd_dt_bias, ssd_d_skip, ssd_norm_w, ssd_w_out, ln_g, ln_b, loss_target, m_gdn_w_in, m_gdn_conv_w, m_gdn_a_log, m_gdn_dt_bias, m_gdn_norm_w, m_gdn_w_out, m_sc_w_in, m_sc_conv_w, m_sc_w_out, m_ssd_w_in, m_ssd_conv_w, m_ssd_conv_b, m_ssd_a_log, m_ssd_dt_bias, m_ssd_d_skip, m_ssd_norm_w, m_ssd_w_out, m_ln_g, m_ln_b, v_gdn_w_in, v_gdn_conv_w, v_gdn_a_log, v_gdn_dt_bias, v_gdn_norm_w, v_gdn_w_out, v_sc_w_in, v_sc_conv_w, v_sc_w_out, v_ssd_w_in, v_ssd_conv_w, v_ssd_conv_b, v_ssd_a_log, v_ssd_dt_bias, v_ssd_d_skip, v_ssd_norm_w, v_ssd_w_out, v_ln_g, v_ln_b):
    args = locals()
    wts = {n: args[n] for n in _WEIGHTS}
    mom = {n: args["m_" + n] for n in _WEIGHTS}
    vel = {n: args["v_" + n] for n in _WEIGHTS}
    me = 4 * lax.axis_index("x") + 2 * lax.axis_index("y") + lax.axis_index("c")
    x0, target = x[0], loss_target[0]

    car = _Carrier()
    shard = lambda n, j: wts[n][j:j + 1].astype(MM_DTYPE)
    gathered_w = lambda n, j: (_cols_gathered if _BIG[n] == "cols" else _rows_gathered)(car.got[n, j])[0]

    for n in ('gdn_w_in', 'gdn_w_out'):
        car.got[n, 0] = _exchange(shard(n, 0), "gather_%s0" % n, slabs=False)
    riders = {0: [("l0_gdn_in_qkv", 'sc_w_in', 0), ("l0_gdn_in_z", 'sc_w_out', 0), ("l0_gdn_out", 'ssd_w_out', 0)],
              1: [("l1_sc_in_h", 'ssd_w_in', 0)],
              2: [("l2_ssd_in_xbc", 'gdn_w_in', 1), ("l2_ssd_in_z", 'gdn_w_out', 1)]}
    full = {}
    small_shapes = [wts[n].shape for n in _SMALL_SHARDED]
    gathered = _exchange(_pack([wts[n] for n in _SMALL_SHARDED]), "gather_small", slabs=False)
    for n, g in zip(_SMALL_SHARDED, _unpack(gathered, small_shapes, lead=1)):
        full[n] = jnp.moveaxis(g, 0, -2).reshape(g.shape[1:-1] + (N_DEV * g.shape[-1],))
    for n in _SMALL:
        full.setdefault(n, wts[n])

    def gdn_params(j):
        w = gathered_w('gdn_w_in', j)
        return dict(w_qkv=w[:, :GDN_CONV_DIM], w_z=w[:, GDN_CONV_DIM:GDN_CONV_DIM + D_INNER],
                    w_ba=jnp.concatenate([_pad_cols(w[:, 6144:6160], 0, HEAD), _pad_cols(w[:, 6160:6176], 0, HEAD)], 1),
                    conv8=_taps8(full['gdn_conv_w'][j]), a_log=_pad_lanes(full['gdn_a_log'][j]),
                    dt_bias=_pad_lanes(full['gdn_dt_bias'][j]), norm_w=full['gdn_norm_w'][j].reshape(1, HEAD),
                    w_out=gathered_w('gdn_w_out', j))

    def sc_params():
        w = gathered_w('sc_w_in', 0)
        return dict(w_h=w[:, :2048], w_b=w[:, 2048:4096], w_c=w[:, 4096:6144], w_z=w[:, 6144:],
                    conv8=_taps8(full['sc_conv_w'][0]), w_out=gathered_w('sc_w_out', 0))

    def ssd_params():
        w = gathered_w('ssd_w_in', 0)
        return dict(w_z=w[:, :D_INNER], w_xbc=w[:, D_INNER:D_INNER + SSD_CONV_DIM],
                    w_dt=_pad_cols(w[:, D_INNER + SSD_CONV_DIM:], 0, HEAD),
                    conv8=_taps8(full['ssd_conv_w'][0], full['ssd_conv_b'][0]), a_log=_pad_lanes(full['ssd_a_log'][0]),
                    dt_bias=_pad_lanes(full['ssd_dt_bias'][0]),
                    dskip=jnp.repeat(full['ssd_d_skip'][0], 64).reshape(1, D_INNER),
                    norm_w=full['ssd_norm_w'][0].reshape(1, D_INNER), w_out=gathered_w('ssd_w_out', 0))

    layers = [("gdn", _gdn_forward, _gdn_backward, lambda: gdn_params(0)), ("sc", _sc_forward, _sc_backward, sc_params),
              ("ssd", _ssd_forward, _ssd_backward, ssd_params), ("gdn", _gdn_forward, _gdn_backward, lambda: gdn_params(1))]

    acts, acts_mm, ys, saved, params = [x0], [x0.astype(MM_DTYPE)], [], [], []
    for i, (kind, fwd, _, make_params) in enumerate(layers):
        params.append(make_params())
        for matmul_name, n, j in riders.get(i, ()):
            car.put(matmul_name, (n, j), shard(n, j), False)
        y, s = fwd(acts_mm[-1], params[i], "l%d_%s" % (i, kind), car.matmul)
        ys.append(y)
        saved.append(s)
        gain, bias = full['ln_g'][i].reshape(1, -1), full['ln_b'][i].reshape(1, -1)
        if i < DEPTH - 1:
            out, out_mm = _ln_fwd(acts[-1], y, gain, bias, "l%d_ln" % i)
            acts.append(out)
            acts_mm.append(out_mm)
        else:
            dact, loss_acc = _ln_loss(acts[-1], y, gain, bias, target, "l%d_ln_loss" % i)
    loss = lax.psum(0.5 / D_MODEL * jnp.sum(loss_acc[0]), ("x", "y", "c"))

    grad_riders = {3: dict(w_out="l3_gdn_bwd_w_qkv", w_in="l3_gdn_bwd_dx_qkv"),
                   2: dict(w_out="l2_ssd_bwd_w_z", w_in="l2_ssd_bwd_dx_xbc"),
                   1: dict(w_out="l1_sc_bwd_w_h", w_in="l0_gdn_bwd_w_qkv"),
                   0: dict(w_out="l0_gdn_bwd_w_z", w_in="l0_gdn_bwd_dx_qkv")}

    def shipper(i):
        def ship(key, g):
            slabs = _cols_to_slabs(g[None]) if key == 'w_in' else _rows_to_slabs(g[None])
            car.put(grad_riders[i][key], ('grad', i, key), slabs.astype(MM_DTYPE), True)
        return ship

    lg = [None] * DEPTH
    d_ln_g, d_ln_b = [None] * DEPTH, [None] * DEPTH
    for i in reversed(range(DEPTH)):
        kind, _, bwd, _ = layers[i]
        du, du_mm, acc = _ln_bwd(dact, acts[i], ys[i], full['ln_g'][i].reshape(1, -1), "l%d_ln_bwd" % i)
        d_ln_g[i], d_ln_b[i] = acc[0], acc[1]
        dact, lg[i] = bwd(acts_mm[i], du, du_mm, params[i], saved[i], "l%d_%s" % (i, kind), car.matmul, shipper(i))
    assert not car.jobs, car.jobs
    grad_x = dact[None]

    stack = lambda k: jnp.stack([lg[0][k], lg[3][k]])
    local = {
        'gdn_conv_w': stack('conv_w'), 'gdn_a_log': stack('a_log'),
        'gdn_dt_bias': stack('dt_bias'), 'gdn_norm_w': stack('norm_w'),
        'sc_conv_w': lg[1]['conv_w'][None],
        'ssd_conv_w': lg[2]['conv_w'][None], 'ssd_conv_b': lg[2]['conv_b'][None],
        'ssd_a_log': lg[2]['a_log'][None], 'ssd_dt_bias': lg[2]['dt_bias'][None], 'ssd_d_skip': lg[2]['d_skip'][None],
        'ssd_norm_w': lg[2]['norm_w'][None],
        'ln_g': jnp.stack(d_ln_g), 'ln_b': jnp.stack(d_ln_b)}

    out = {}
    layers_of = {'gdn': (0, 3), 'sc': (1,), 'ssd': (2,)}
    for n in _BIG:
        kind, key = n.split('_', 1)
        recv = jnp.concatenate([car.got['grad', i, key] for i in layers_of[kind]], axis=1)
        shp = wts[n].shape
        r, c = shp[0] * shp[1], shp[2]
        res = _adamw(wts[n].reshape(r, c), recv.reshape(N_DEV, r, c), mom[n].reshape(r, c), vel[n].reshape(r, c),
                     "adamw_" + n)
        out[n] = [a.reshape(shp) for a in res]
    full_shapes = [local[n].shape for n in _SMALL]
    gathered = _exchange(_pack([local[n] for n in _SMALL]), "gather_small_grads", slabs=False)
    gs = []
    for n, g in zip(_SMALL, _unpack(gathered, full_shapes, lead=1)):
        if n in _SMALL_SHARDED:
            width = wts[n].shape[-1]
            g = lax.dynamic_slice_in_dim(g, me * width, width, axis=g.ndim - 1)
        gs.append(g)
    shapes = [wts[n].shape for n in _SMALL]
    res = _adamw(_pack([wts[n] for n in _SMALL]), _pack(gs, lead=1), _pack([mom[n] for n in _SMALL]),
                 _pack([vel[n] for n in _SMALL]), "adamw_small")
    for k, n in enumerate(_SMALL):
        out[n] = [_unpack(a, shapes)[k] for a in res]

    return (loss, grad_x, *[out[n][0] for n in _WEIGHTS], *[out[n][1] for n in _WEIGHTS],
            *[out[n][2] for n in _WEIGHTS], *[out[n][3] for n in _WEIGHTS])
```

```python
import functools
import math

import jax
import jax.numpy as jnp
from jax import lax
from jax.experimental import pallas as pl
from jax.experimental.pallas import tpu as pltpu

F32 = jnp.float32
MM_DTYPE = jnp.bfloat16

N_DEV = 8
D_MODEL = 1024
D_INNER = 2048
CHUNK = 64
HEAD = 128
GDN_V_HEADS = 16
GDN_GROUP = 16
GDN_QK_HEADS = 8
GDN_QK_DIM = 1024
GDN_CONV_DIM = 4096
SSD_PAIRS = 16
SSD_GROUPS = 4
SSD_STATE = 128
SSD_CONV_DIM = 3072
DEPTH = 4
ALPHA = (2 * DEPTH) ** 0.25
RMS_EPS = 1e-6
LN_EPS = 1e-5
L2_EPS = 1e-6
ADAM_LR, ADAM_B1, ADAM_B2, ADAM_EPS, ADAM_WD, ADAM_STEP = 0.001, 0.9, 0.999, 1e-08, 0.01, 10

VMEM_LIMIT_BYTES = 48 * 1024 * 1024
NEG_BIG = -1e30

_NN = (((1,), (0,)), ((), ()))
_NT = (((1,), (1,)), ((), ()))
_TN = (((0,), (0,)), ((), ()))


def _mm(a, b, dims):
    return lax.dot_general(a.astype(MM_DTYPE), b.astype(MM_DTYPE), dims, preferred_element_type=F32)


def _mmx(a, b, dims):
    return lax.dot_general(a, b, dims, precision=lax.Precision.HIGHEST, preferred_element_type=F32)


def _iota(shape, dim):
    return lax.broadcasted_iota(jnp.int32, shape, dim)


def _eye(n):
    return (_iota((n, n), 0) == _iota((n, n), 1)).astype(F32)


def _sig(x):
    return jax.nn.sigmoid(x)


def _silu(x):
    return x * _sig(x)


def _dsilu(x):
    s = _sig(x)
    return s * (1.0 + x * (1.0 - s))


def _softplus(x):
    return jnp.maximum(x, 0.0) + jnp.log(1.0 + jnp.exp(-jnp.abs(x)))


def _col(x, h):
    return jnp.sum(jnp.where(_iota(x.shape, 1) == h, x, 0.0), axis=1, keepdims=True)


def _row(x, h):
    return jnp.sum(jnp.where(_iota(x.shape, 0) == h, x, 0.0), axis=0, keepdims=True)


def _put_col(acc, col, h):
    return jnp.where(_iota(acc.shape, 1) == h, col, acc)


def _put_row(acc, row, h):
    return jnp.where(_iota(acc.shape, 0) == h, row, acc)


def _put_sub(acc, row, j):
    return acc + jnp.where(_iota(acc.shape, 0) == j, row, 0.0)


def _lanes(h):
    return pl.ds(h * HEAD, HEAD) if isinstance(h, int) else pl.ds(pl.multiple_of(h * HEAD, HEAD), HEAD)


def _total(x):
    return jnp.sum(jnp.sum(x, axis=0, keepdims=True), axis=1, keepdims=True)


def _call(body, name, grid, in_specs, out_specs, out_shape, scratch_shapes=(), semantics=None):
    return pl.pallas_call(
        body, name=name, grid=grid, in_specs=in_specs, out_specs=out_specs, out_shape=out_shape,
        scratch_shapes=list(scratch_shapes),
        compiler_params=pltpu.CompilerParams(dimension_semantics=semantics, vmem_limit_bytes=VMEM_LIMIT_BYTES))


def _tile(n, pref):
    if n <= pref:
        return n
    t = pref
    while n % t:
        t -= 128
    return t


def _exchange_copies(src_ref, out_ref, send_sems, recv_sems, local_sem, slabs):
    x, y, c = lax.axis_index("x"), lax.axis_index("y"), lax.axis_index("c")
    me = 4 * x + 2 * y + c
    mine = src_ref.at[me] if slabs else src_ref
    local = pltpu.make_async_copy(mine, out_ref.at[me], local_sem)
    sends, recvs = [], []
    for r in range(1, N_DEV):
        px = 1 - x if r & 4 else x
        py = 1 - y if r & 2 else y
        pc = 1 - c if r & 1 else c
        peer = 4 * px + 2 * py + pc
        kw = dict(send_sem=send_sems.at[r - 1], recv_sem=recv_sems.at[r - 1], device_id=(px, py, pc),
                  device_id_type=pl.DeviceIdType.MESH)
        sends.append(pltpu.make_async_remote_copy(src_ref=src_ref.at[peer] if slabs else src_ref,
                                                  dst_ref=out_ref.at[me], **kw))
        recvs.append(pltpu.make_async_remote_copy(src_ref=mine, dst_ref=out_ref.at[peer], **kw))
    return local, sends, recvs


def _gather_copies(src_ref, out_ref, send_sems, recv_sems, local_sem):
    x, y, c = lax.axis_index("x"), lax.axis_index("y"), lax.axis_index("c")
    chips = [(1 - x, y), (x, 1 - y), (1 - x, 1 - y)]
    sibling = (x, y, 1 - c)

    def slot(px, py, pc):
        return out_ref.at[4 * px + 2 * py + pc]

    def copy(k, src, dst, to):
        return pltpu.make_async_remote_copy(src_ref=src, dst_ref=dst, send_sem=send_sems.at[k], recv_sem=recv_sems.at[k],
                                            device_id=to, device_id_type=pl.DeviceIdType.MESH)

    mine = slot(x, y, c)
    local = pltpu.make_async_copy(src_ref, mine, local_sem)
    first = [copy(0, src_ref, mine, sibling)] + [copy(1 + j, src_ref, mine, (*chip, c)) for j, chip in enumerate(chips)]
    landed = [copy(1 + j, src_ref, slot(*chip, c), sibling) for j, chip in enumerate(chips)]
    passed = [copy(4 + j, slot(*chip, c), slot(*chip, c), sibling) for j, chip in enumerate(chips)]
    from_sibling = [copy(0, src_ref, slot(x, y, 1 - c), sibling)] + \
        [copy(4 + j, src_ref, slot(*chip, 1 - c), sibling) for j, chip in enumerate(chips)]
    return local, first, landed, passed, from_sibling


def _exchange_start(*refs, slabs):
    if not slabs:
        local, first = _gather_copies(*refs)[:2]
        local.start()
        for cp in first:
            cp.start()
        return
    local, sends, _ = _exchange_copies(*refs, slabs=slabs)
    local.start()
    for cp in sends:
        cp.start()


def _exchange_wait(*refs, slabs):
    if not slabs:
        local, first, landed, passed, from_sibling = _gather_copies(*refs)
        for arrived, onward in zip(landed, passed):
            arrived.wait_recv()
            onward.start()
        for cp in from_sibling:
            cp.wait_recv()
        for cp in first + passed:
            cp.wait_send()
        local.wait()
        return
    local, sends, recvs = _exchange_copies(*refs, slabs=slabs)
    for cp in recvs:
        cp.wait_recv()
    for cp in sends:
        cp.wait_send()
    local.wait()


def _exchange_sems():
    return [pltpu.SemaphoreType.DMA((N_DEV - 1,)), pltpu.SemaphoreType.DMA((N_DEV - 1,)), pltpu.SemaphoreType.DMA(())]


def _exchange_shape(src, slabs):
    return jax.ShapeDtypeStruct((N_DEV,) + tuple(src.shape[1:] if slabs else src.shape), src.dtype)


def _exchange(src, name, slabs):
    def body(*refs):
        _exchange_start(*refs, slabs=slabs)
        _exchange_wait(*refs, slabs=slabs)

    return pl.pallas_call(
        body, name=name,
        in_specs=[pl.BlockSpec(memory_space=pl.ANY)], out_specs=pl.BlockSpec(memory_space=pl.ANY),
        out_shape=_exchange_shape(src, slabs), scratch_shapes=_exchange_sems(),
    )(src)


MM_TM, MM_TN, MM_TK = 1024, 1024, 1024


def _matmul(a, b, mode, name, add=None, add_scale=1.0, carry=None):
    if mode == "nn":
        (m, k), (_, n) = a.shape, b.shape
    elif mode == "nt":
        (m, k), (n, _) = a.shape, b.shape
    else:
        (k, m), (_, n) = a.shape, b.shape
    tk = _tile(k, 2 * MM_TK)
    tm, tn = _tile(m, 2 * MM_TM if mode == "nn" and add is None and tk <= MM_TK else MM_TM), _tile(n, MM_TN)
    nk = k // tk
    grid = (m // tm, n // tn, nk)
    dims = {"nn": _NN, "nt": _NT, "tn": _TN}[mode]
    n_in = 2 + (add is not None)

    def body(*refs):
        a_ref, b_ref, o_ref = refs[0], refs[1], refs[n_in + (carry is not None)]
        if carry is not None:
            ex = (refs[n_in], refs[n_in + 2]) + tuple(refs[n_in + 3:])
            step = (pl.program_id(0) * grid[1] + pl.program_id(1)) * grid[2] + pl.program_id(2)

            @pl.when(step == 0)
            def _():
                _exchange_start(*ex, slabs=carry[1])

        part = _mm(a_ref[...], b_ref[...], dims)
        first = part if add is None else part + add_scale * refs[2][...]
        if nk == 1:
            o_ref[...] = first
        else:
            @pl.when(pl.program_id(2) == 0)
            def _():
                o_ref[...] = first

            @pl.when(pl.program_id(2) > 0)
            def _():
                o_ref[...] += part

        if carry is not None:
            @pl.when(step == grid[0] * grid[1] * grid[2] - 1)
            def _():
                _exchange_wait(*ex, slabs=carry[1])

    if mode == "nn":
        specs = [pl.BlockSpec((tm, tk), lambda i, j, q: (i, q)), pl.BlockSpec((tk, tn), lambda i, j, q: (q, j))]
    elif mode == "nt":
        specs = [pl.BlockSpec((tm, tk), lambda i, j, q: (i, q)), pl.BlockSpec((tn, tk), lambda i, j, q: (j, q))]
    else:
        specs = [pl.BlockSpec((tk, tm), lambda i, j, q: (q, i)), pl.BlockSpec((tk, tn), lambda i, j, q: (q, j))]
    out_spec = pl.BlockSpec((tm, tn), lambda i, j, q: (i, j))
    args = [a, b]
    if add is not None:
        specs.append(out_spec)
        args.append(add)
    out_shape = jax.ShapeDtypeStruct((m, n), F32)
    if carry is None:
        return _call(body, name, grid, specs, out_spec, out_shape, semantics=("parallel", "parallel", "arbitrary"))(*args)
    hbm = pl.BlockSpec(memory_space=pl.ANY)
    return _call(body, name, grid, specs + [hbm], [out_spec, hbm], [out_shape, _exchange_shape(*carry)],
                 _exchange_sems(), ("arbitrary", "arbitrary", "arbitrary"))(*args, carry[0])


CONV_TB = 512
CONV_CB = 1024
HALO = 8


def _conv_specs(t, cb_n, tb):
    nb = tb // HALO
    blk = pl.BlockSpec((tb, cb_n), lambda c, i: (i, c))
    prev = pl.BlockSpec((HALO, cb_n), lambda c, i: (jnp.maximum(i * nb - 1, 0), c))
    nxt = pl.BlockSpec((HALO, cb_n), lambda c, i: (jnp.minimum((i + 1) * nb, t // HALO - 1), c))
    w = pl.BlockSpec((8, cb_n), lambda c, i: (0, c))
    return blk, prev, nxt, w


def _shift_down(ext, s, tb):
    return (pltpu.roll(ext, s, 0) if s else ext)[HALO:HALO + tb]


def _shift_up(ext, s, tb):
    n = ext.shape[0]
    return (pltpu.roll(ext, n - s, 0) if s else ext)[0:tb]


def _conv_fwd(u, w8, ktaps, name, u2=None, bias=False):
    t, ch = u.shape
    cb_n = min(CONV_CB, ch)
    tb = min(2 * CONV_TB, t)
    two = u2 is not None

    def body(*refs):
        if two:
            u_ref, up_ref, v_ref, vp_ref, w_ref, o_ref = refs
        else:
            u_ref, up_ref, w_ref, o_ref = refs
        first = pl.program_id(1) == 0
        blk, halo = u_ref[...], up_ref[...]
        if two:
            blk, halo = blk * v_ref[...], halo * vp_ref[...]
        ext = jnp.concatenate([jnp.where(first, 0.0, halo), blk], axis=0)
        acc = jnp.zeros((tb, cb_n), F32)
        for j in range(ktaps):
            acc = acc + w_ref[j:j + 1, :] * _shift_down(ext, ktaps - 1 - j, tb)
        if bias:
            acc = acc + w_ref[ktaps:ktaps + 1, :]
        o_ref[...] = acc

    blk, prev, _, wspec = _conv_specs(t, cb_n, tb)
    specs, args = [blk, prev], [u, u]
    if two:
        specs += [blk, prev]
        args += [u2, u2]
    specs.append(wspec)
    args.append(w8)
    return _call(body, name, (ch // cb_n, t // tb), specs, blk, jax.ShapeDtypeStruct((t, ch), F32),
                 semantics=("parallel", "parallel"))(*args)


def _conv_bwd(dc, u, w8, ktaps, name, u2=None):
    t, ch = u.shape
    cb_n = min(CONV_CB, ch)
    two = u2 is not None
    tb = min(CONV_TB if two else 2 * CONV_TB, t)

    def body(*refs):
        if two:
            dc_ref, dn_ref, u_ref, v_ref, w_ref, du_ref, dv_ref, dw_ref = refs
        else:
            dc_ref, dn_ref, u_ref, w_ref, du_ref, dw_ref = refs
        i = pl.program_id(1)
        d = dc_ref[...]
        dext = jnp.concatenate([d, jnp.where(i == t // tb - 1, 0.0, dn_ref[...])], axis=0)
        blk = u_ref[...] * v_ref[...] if two else u_ref[...]
        du = jnp.zeros((tb, cb_n), F32)
        dw = jnp.zeros((8, cb_n), F32)
        for j in range(ktaps):
            ahead = _shift_up(dext, ktaps - 1 - j, tb)
            du = du + w_ref[j:j + 1, :] * ahead
            dw = _put_sub(dw, jnp.sum(ahead * blk, axis=0, keepdims=True), j)
        dw = _put_sub(dw, jnp.sum(d, axis=0, keepdims=True), ktaps)
        if two:
            du_ref[...] = (du * v_ref[...]).astype(du_ref.dtype)
            dv_ref[...] = (du * u_ref[...]).astype(dv_ref.dtype)
        else:
            du_ref[...] = du.astype(du_ref.dtype)

        @pl.when(i == 0)
        def _():
            dw_ref[...] = jnp.zeros_like(dw_ref)

        dw_ref[...] += dw

    blk, _, nxt, wspec = _conv_specs(t, cb_n, tb)
    specs, args = [blk, nxt, blk], [dc, dc, u]
    if two:
        specs.append(blk)
        args.append(u2)
    specs.append(wspec)
    args.append(w8)
    act = jax.ShapeDtypeStruct((t, ch), MM_DTYPE)
    outs = ([blk, blk, wspec], [act, act, jax.ShapeDtypeStruct((8, ch), F32)]) if two else \
        ([blk, wspec], [act, jax.ShapeDtypeStruct((8, ch), F32)])
    return _call(body, name, (ch // cb_n, t // tb), specs, outs[0], outs[1],
                 semantics=("parallel", "arbitrary"))(*args)


EW_TB = 256


def _chunk_mask(n, upper):
    i, j = _iota((n, n), 0), _iota((n, n), 1)
    same = jnp.right_shift(i, 6) == jnp.right_shift(j, 6)
    return (same & ((j >= i) if upper else (i >= j))).astype(F32)


def _rows(width, tb=EW_TB):
    return pl.BlockSpec((tb, width), lambda i: (i, 0))


def _const(rows, width):
    return pl.BlockSpec((rows, width), lambda i: (0, 0))


def _gdn_ew_fwd(c, ba, a_log, dt_bias, name):
    t = c.shape[0]
    tb = min(2 * EW_TB, t)

    def body(c_ref, ba_ref, al_ref, db_ref, q_ref, k_ref, v_ref, beta_ref, gc_ref):
        for h in range(GDN_QK_HEADS):
            for base, ref, scale in ((0, q_ref, HEAD ** -0.5), (GDN_QK_DIM, k_ref, 1.0)):
                s = _silu(c_ref[:, base + h * HEAD: base + (h + 1) * HEAD])
                r = lax.rsqrt(jnp.sum(s * s, axis=1, keepdims=True) + L2_EPS)
                ref[:, h * HEAD:(h + 1) * HEAD] = s * (r * scale)
        v_ref[...] = _silu(c_ref[:, 2 * GDN_QK_DIM:])
        beta_ref[...] = _sig(ba_ref[:, :HEAD])
        g = -jnp.exp(al_ref[...]) * _softplus(ba_ref[:, HEAD:] + db_ref[...])
        gc_ref[...] = _mmx(_chunk_mask(tb, False), g, _NN)

    act = lambda w: jax.ShapeDtypeStruct((t, w), F32)
    return _call(body, name, (t // tb,),
                 [_rows(GDN_CONV_DIM, tb), _rows(2 * HEAD, tb), _const(1, HEAD), _const(1, HEAD)],
                 [_rows(GDN_QK_DIM, tb), _rows(GDN_QK_DIM, tb), _rows(D_INNER, tb), _rows(HEAD, tb), _rows(HEAD, tb)],
                 [act(GDN_QK_DIM), act(GDN_QK_DIM), act(D_INNER), act(HEAD), act(HEAD)],
                 semantics=("parallel",))(c, ba, a_log, dt_bias)


def _gdn_ew_bwd(c, ba, a_log, dt_bias, dqh, dkh, dv, dbeta, dgc, name):
    t = c.shape[0]
    tb = EW_TB

    def body(c_ref, ba_ref, al_ref, db_ref, dq_ref, dk_ref, dv_ref, dbeta_ref, dgc_ref, dc_ref, dba_ref, acc_ref):
        for h in range(GDN_QK_HEADS):
            for base, ref, scale in ((0, dq_ref, HEAD ** -0.5), (GDN_QK_DIM, dk_ref, 1.0)):
                cq = c_ref[:, base + h * HEAD: base + (h + 1) * HEAD]
                s = _silu(cq)
                r = lax.rsqrt(jnp.sum(s * s, axis=1, keepdims=True) + L2_EPS)
                dn = ref[:, h * HEAD:(h + 1) * HEAD] * scale
                ds = r * dn - s * (r * r * r) * jnp.sum(dn * s, axis=1, keepdims=True)
                dc_ref[:, base + h * HEAD: base + (h + 1) * HEAD] = ds * _dsilu(cq)
        dc_ref[:, 2 * GDN_QK_DIM:] = dv_ref[...] * _dsilu(c_ref[:, 2 * GDN_QK_DIM:])
        beta = _sig(ba_ref[:, :HEAD])
        dba_ref[:, :HEAD] = (dbeta_ref[...] * beta * (1.0 - beta)).astype(dba_ref.dtype)
        pre = ba_ref[:, HEAD:] + db_ref[...]
        ea = jnp.exp(al_ref[...])
        g = -ea * _softplus(pre)
        dg = _mmx(_chunk_mask(tb, True), dgc_ref[...], _NN)
        da_raw = dg * (-ea) * _sig(pre)
        dba_ref[:, HEAD:] = da_raw.astype(dba_ref.dtype)
        acc = jnp.zeros((8, HEAD), F32)
        acc = _put_sub(acc, jnp.sum(dg * g, axis=0, keepdims=True), 0)
        acc = _put_sub(acc, jnp.sum(da_raw, axis=0, keepdims=True), 1)

        @pl.when(pl.program_id(0) == 0)
        def _():
            acc_ref[...] = jnp.zeros_like(acc_ref)

        acc_ref[...] += acc

    act = lambda w: jax.ShapeDtypeStruct((t, w), F32)
    return _call(body, name, (t // tb,),
                 [_rows(GDN_CONV_DIM), _rows(2 * HEAD), _const(1, HEAD), _const(1, HEAD),
                  _rows(GDN_QK_DIM), _rows(GDN_QK_DIM), _rows(D_INNER), _rows(HEAD), _rows(HEAD)],
                 [_rows(GDN_CONV_DIM), _rows(2 * HEAD), _const(8, HEAD)],
                 [act(GDN_CONV_DIM), jax.ShapeDtypeStruct((t, 2 * HEAD), MM_DTYPE), jax.ShapeDtypeStruct((8, HEAD), F32)],
                 semantics=("arbitrary",))(c, ba, a_log, dt_bias, dqh, dkh, dv, dbeta, dgc)


def _zip(fn, *lists):
    return [fn(*xs) for xs in zip(*lists)]


def _mms(xs, ys, dims):
    return [_mm(x, y, dims) for x, y in zip(xs, ys)]


def _side_by_side(a, b):
    return jnp.concatenate([a, b], axis=1)


def _interleave(*gens):
    results, live = [None] * len(gens), list(range(len(gens)))
    while live:
        for i in list(live):
            try:
                next(gens[i])
            except StopIteration as stop:
                results[i] = stop.value
                live.remove(i)
    return results


def _gdn_local_stages(q, k, v, bcol, gcol, grow, glast):
    ii, jj = _iota((CHUNK, CHUNK), 0), _iota((CHUNK, CHUNK), 1)
    eye = _eye(CHUNK)
    mul = lambda x, y: x * y
    eg = [jnp.exp(g) for g in gcol]
    decay = _zip(lambda gc, gr: jnp.exp(jnp.where(ii >= jj, gc - gr, NEG_BIG)), gcol, grow)
    kb = _zip(mul, k, bcol)
    p, qk = _mms(kb, k, _NT), _mms(q, k, _NT)
    yield
    a = _zip(lambda x, d: jnp.where(ii > jj, x * d, 0.0), p, decay)
    inv, pw = [eye - x for x in a], a
    for _ in range(5):
        pw = _mms(pw, pw, _NN)
        yield
        inv = _zip(lambda x, y: x + y, inv, _mms(inv, pw, _NN))
        yield
    rv, rk = _zip(mul, v, bcol), _zip(mul, kb, eg)
    uw = _mms(inv, _zip(_side_by_side, rv, rk), _NN)
    u, w = [x[:, :HEAD] for x in uw], [x[:, HEAD:] for x in uw]
    yield
    att = _zip(mul, qk, decay)
    qd = _zip(mul, q, eg)
    ekt = _zip(lambda gl, gc: jnp.exp(gl - gc), glast, gcol)
    kt = _zip(mul, k, ekt)
    el = [jnp.exp(g) for g in glast]
    return dict(eg=eg, decay=decay, kb=kb, p=p, inv=inv, rv=rv, rk=rk, u=u, w=w, qk=qk, att=att, qd=qd, ekt=ekt, kt=kt,
                el=el)


def _gdn_state_stages(u, w, att, qd, kt, el, s_in):
    ws, qs = _mms(w, s_in, _NN), _mms(qd, s_in, _NN)
    yield
    vn = _zip(lambda x, y: x - y, u, ws)
    av, kv = _mms(att, vn, _NN), _mms(kt, vn, _TN)
    yield
    out = _zip(lambda x, y: x + y, qs, av)
    s_out = _zip(lambda s, e, y: s * e + y, s_in, el, kv)
    return dict(vn=vn, out=out, s_out=s_out)


def _gdn_heads_fwd(q, k, v, bcol, gcol, grow, glast, s_in):
    f, = _interleave(_gdn_local_stages(q, k, v, bcol, gcol, grow, glast))
    g, = _interleave(_gdn_state_stages(f["u"], f["w"], f["att"], f["qd"], f["kt"], f["el"], s_in))
    return {**f, **g}


def _head_groups(group, init):
    if GDN_GROUP == GDN_V_HEADS:
        return group(0, init)
    return lax.fori_loop(0, GDN_V_HEADS // GDN_GROUP, lambda gi, c: group(GDN_GROUP * gi, c), init)


def _half(h):
    return h // 2 if isinstance(h, int) else jnp.right_shift(h, 1)


def _gdn_chunk_fwd(qn, kn, v, beta, gc, z, norm_w, name):
    t = qn.shape[0]
    nc = t // CHUNK

    def body(q_ref, k_ref, v_ref, beta_ref, gc_ref, z_ref, nw_ref, o_ref, h_ref, st_ref, state):
        @pl.when(pl.program_id(0) == 0)
        def _():
            state[...] = jnp.zeros_like(state)

        st_ref[0] = state[...]
        gc_all, beta_all = gc_ref[...], beta_ref[...]
        gct = _mmx(_eye(HEAD), gc_all, _NT)
        glast_all = gc_ref[CHUNK - 1:CHUNK, :]
        nw = nw_ref[...]

        def group(h0, carry):
            heads = [h0 + s for s in range(GDN_GROUP)]
            f = _gdn_heads_fwd([q_ref[:, _lanes(_half(h))] for h in heads], [k_ref[:, _lanes(_half(h))] for h in heads],
                               [v_ref[:, _lanes(h)] for h in heads], [_col(beta_all, h) for h in heads],
                               [_col(gc_all, h) for h in heads], [_row(gct, h) for h in heads],
                               [_col(glast_all, h) for h in heads], [state[h] for h in heads])
            for h, s_out, o in zip(heads, f["s_out"], f["out"]):
                state[h] = s_out
                o_ref[:, _lanes(h)] = o
                rstd = lax.rsqrt(jnp.mean(o * o, axis=1, keepdims=True) + RMS_EPS)
                h_ref[:, _lanes(h)] = (o * rstd * nw * _silu(z_ref[:, _lanes(h)])).astype(h_ref.dtype)
            return carry

        _head_groups(group, 0)

    rows = lambda w: pl.BlockSpec((CHUNK, w), lambda i: (i, 0))
    act = lambda w: jax.ShapeDtypeStruct((t, w), F32)
    return _call(body, name, (nc,),
                 [rows(GDN_QK_DIM), rows(GDN_QK_DIM), rows(D_INNER), rows(HEAD), rows(HEAD), rows(D_INNER),
                  _const(1, HEAD)],
                 [rows(D_INNER), rows(D_INNER), pl.BlockSpec((1, GDN_V_HEADS, HEAD, HEAD), lambda i: (i, 0, 0, 0))],
                 [act(D_INNER), jax.ShapeDtypeStruct((t, D_INNER), MM_DTYPE),
                  jax.ShapeDtypeStruct((nc, GDN_V_HEADS, HEAD, HEAD), F32)],
                 [pltpu.VMEM((GDN_V_HEADS, HEAD, HEAD), F32)], ("arbitrary",))(qn, kn, v, beta, gc, z, norm_w)


def _gdn_chunk_bwd(qn, kn, v, beta, gc, z, norm_w, o, states, dh, name):
    t = qn.shape[0]
    nc = t // CHUNK

    def body(q_ref, k_ref, v_ref, beta_ref, gc_ref, z_ref, nw_ref, o_ref, st_ref, dh_ref,
             dq_ref, dk_ref, dv_ref, dz_ref, dbeta_ref, dgc_ref, acc_ref, dstate):
        @pl.when(pl.program_id(0) == 0)
        def _():
            dstate[...] = jnp.zeros_like(dstate)
            acc_ref[...] = jnp.zeros_like(acc_ref)

        gc_all, beta_all = gc_ref[...], beta_ref[...]
        gct = _mmx(_eye(HEAD), gc_all, _NT)
        glast_all = gc_ref[CHUNK - 1:CHUNK, :]
        nw = nw_ref[...]
        ii, jj = _iota((CHUNK, CHUNK), 0), _iota((CHUNK, CHUNK), 1)
        last_row = _iota((CHUNK, 1), 0) == CHUNK - 1

        def group(h0, carry):
            dbeta_acc, dgc_acc, dgrow_acc, dnw_acc = carry
            heads = [h0 + s for s in range(GDN_GROUP)]
            mul, add, sub = (lambda x, y: x * y), (lambda x, y: x + y), (lambda x, y: x - y)
            rowsum = lambda x, y: jnp.sum(x * y, axis=1, keepdims=True)
            q, k = [q_ref[:, _lanes(_half(h))] for h in heads], [k_ref[:, _lanes(_half(h))] for h in heads]
            vv = [v_ref[:, _lanes(h)] for h in heads]
            bcol, gcol = [_col(beta_all, h) for h in heads], [_col(gc_all, h) for h in heads]
            s_in, dsn = [st_ref[0, h] for h in heads], [dstate[h] for h in heads]
            f = _gdn_heads_fwd(q, k, vv, bcol, gcol, [_row(gct, h) for h in heads],
                               [_col(glast_all, h) for h in heads], s_in)
            do = []
            for h in heads:
                oo, zz, dhh = o_ref[:, _lanes(h)], z_ref[:, _lanes(h)], dh_ref[:, _lanes(h)]
                rstd = lax.rsqrt(jnp.mean(oo * oo, axis=1, keepdims=True) + RMS_EPS)
                on, sz = oo * rstd, _silu(zz)
                dnw_acc = dnw_acc + jnp.sum(dhh * on * sz, axis=0, keepdims=True)
                dz_ref[:, _lanes(h)] = (dhh * on * nw * _dsilu(zz)).astype(dz_ref.dtype)
                don = dhh * nw * sz
                do.append(rstd * (don - on * jnp.mean(don * on, axis=1, keepdims=True)))
            decay, eg, inv = f["decay"], f["eg"], f["inv"]
            d_glast = _zip(lambda d, s, e: _total(d * s) * e, dsn, s_in, f["el"])
            dkt = _mms(f["vn"], dsn, _NT)
            dvn = _mms(f["kt"], dsn, _NN)
            dqd = _mms(do, s_in, _NT)
            ds_prev = _zip(lambda d, e, y: d * e + y, dsn, f["el"], _mms(f["qd"], do, _TN))
            datt = _mms(do, f["vn"], _NT)
            dvn = _zip(add, dvn, _mms(f["att"], do, _TN))
            dqk = _zip(mul, datt, decay)
            dq = _zip(lambda x, e, y: x * e + y, dqd, eg, _mms(dqk, k, _NN))
            dk = _mms(dqk, q, _TN)
            ddecay = _zip(mul, datt, f["qk"])
            dgcol = _zip(rowsum, dqd, f["qd"])
            dw = [-x for x in _mms(dvn, s_in, _NT)]
            ds_prev = _zip(sub, ds_prev, _mms(f["w"], dvn, _TN))
            drv, drk = _mms(inv, dvn, _TN), _mms(inv, dw, _TN)
            da = [jnp.where(ii > jj, -x, 0.0) for x in
                  _mms(_zip(_side_by_side, drv, drk), _zip(_side_by_side, f["u"], f["w"]), _NT)]
            dp = _zip(mul, da, decay)
            ddecay = _zip(lambda x, y, z_: x + y * z_, ddecay, da, f["p"])
            dkb = _zip(lambda x, y, e: x + y * e, _mms(dp, k, _NN), drk, eg)
            dk = _zip(add, dk, _mms(dp, f["kb"], _TN))
            dbeta = _zip(add, _zip(rowsum, drv, vv), _zip(rowsum, dkb, k))
            dgcol = _zip(add, dgcol, _zip(rowsum, drk, f["rk"]))
            dk = _zip(lambda x, y, b_, z_, e: x + y * b_ + z_ * e, dk, dkb, bcol, dkt, f["ekt"])
            tail = _zip(mul, dkt, f["kt"])
            d_glast = _zip(lambda x, y: x + _total(y), d_glast, tail)
            e_ = _zip(mul, ddecay, decay)
            dgcol = _zip(lambda x, t_, e, gl: x - jnp.sum(t_, axis=1, keepdims=True) + jnp.sum(e, axis=1, keepdims=True)
                         + jnp.where(last_row, gl, 0.0), dgcol, tail, e_, d_glast)
            for i_ in range(0, len(heads), 2):
                dq_ref[:, _lanes(_half(heads[i_]))] = dq[i_] + dq[i_ + 1]
                dk_ref[:, _lanes(_half(heads[i_]))] = dk[i_] + dk[i_ + 1]
            for i_, h in enumerate(heads):
                dstate[h] = ds_prev[i_]
                dv_ref[:, _lanes(h)] = drv[i_] * bcol[i_]
                dbeta_acc = _put_col(dbeta_acc, dbeta[i_], h)
                dgc_acc = _put_col(dgc_acc, dgcol[i_], h)
                dgrow_acc = _put_row(dgrow_acc, -jnp.sum(e_[i_], axis=0, keepdims=True), h)
            return dbeta_acc, dgc_acc, dgrow_acc, dnw_acc

        zero = jnp.zeros((CHUNK, HEAD), F32)
        dbeta_acc, dgc_acc, dgrow_acc, dnw_acc = _head_groups(
            group, (zero, zero, jnp.zeros((HEAD, CHUNK), F32), jnp.zeros((1, HEAD), F32)))
        dbeta_ref[...] = dbeta_acc
        dgc_ref[...] = dgc_acc + _mmx(_eye(CHUNK), dgrow_acc, _NT)
        acc_ref[...] += _put_sub(jnp.zeros((8, HEAD), F32), dnw_acc, 0)

    rows = lambda w: pl.BlockSpec((CHUNK, w), lambda i: (nc - 1 - i, 0))
    act = lambda w: jax.ShapeDtypeStruct((t, w), F32)
    return _call(body, name, (nc,),
                 [rows(GDN_QK_DIM), rows(GDN_QK_DIM), rows(D_INNER), rows(HEAD), rows(HEAD), rows(D_INNER),
                  _const(1, HEAD), rows(D_INNER),
                  pl.BlockSpec((1, GDN_V_HEADS, HEAD, HEAD), lambda i: (nc - 1 - i, 0, 0, 0)), rows(D_INNER)],
                 [rows(GDN_QK_DIM), rows(GDN_QK_DIM), rows(D_INNER), rows(D_INNER), rows(HEAD), rows(HEAD), _const(8, HEAD)],
                 [act(GDN_QK_DIM), act(GDN_QK_DIM), act(D_INNER), jax.ShapeDtypeStruct((t, D_INNER), MM_DTYPE), act(HEAD),
                  act(HEAD), jax.ShapeDtypeStruct((8, HEAD), F32)],
                 [pltpu.VMEM((GDN_V_HEADS, HEAD, HEAD), F32)], ("arbitrary",)
                 )(qn, kn, v, beta, gc, z, norm_w, o, states, dh)


def _sc_gate_fwd(bg, cv, z, name):
    t, w = bg.shape

    def body(b_ref, c_ref, z_ref, o_ref):
        o_ref[...] = (b_ref[...] * c_ref[...] * _silu(z_ref[...])).astype(o_ref.dtype)

    tb = min(2 * EW_TB, t)
    return _call(body, name, (t // tb,), [_rows(w, tb)] * 3, _rows(w, tb), jax.ShapeDtypeStruct((t, w), MM_DTYPE),
                 semantics=("parallel",))(bg, cv, z)


def _sc_gate_bwd(dh, bg, cv, z, name):
    t, w = bg.shape

    def body(d_ref, b_ref, c_ref, z_ref, db_ref, dc_ref, dz_ref):
        d, b, c, zz = d_ref[...], b_ref[...], c_ref[...], z_ref[...]
        sz = _silu(zz)
        db_ref[...] = (d * c * sz).astype(db_ref.dtype)
        dc_ref[...] = d * b * sz
        dz_ref[...] = (d * b * c * _dsilu(zz)).astype(dz_ref.dtype)

    act, act_mm = jax.ShapeDtypeStruct((t, w), F32), jax.ShapeDtypeStruct((t, w), MM_DTYPE)
    return _call(body, name, (t // EW_TB,), [_rows(w)] * 4, [_rows(w)] * 3, [act_mm, act, act_mm],
                 semantics=("parallel",))(dh, bg, cv, z)


XBC_B = D_INNER
XBC_C = D_INNER + SSD_GROUPS * SSD_STATE


def _ssd_scalars(dtp, dt_bias, a_log):
    dt = _softplus(dtp + dt_bias)
    a = -jnp.exp(a_log)
    da = dt * a
    ac = _mmx(_chunk_mask(CHUNK, False), da, _NN)
    act = _mmx(_eye(HEAD), ac, _NT)
    aclast = jnp.sum(jnp.where(_iota(ac.shape, 0) == CHUNK - 1, ac, 0.0), axis=0, keepdims=True)
    return dt, a, da, ac, act, aclast


def _ssd_pairs_fwd(x2, bg, cg, cb, dt, ac, act, aclast, s2):
    ii, jj = _iota((CHUNK, CHUNK), 0), _iota((CHUNK, CHUNK), 1)
    half = _iota((CHUNK, HEAD), 1) < 64
    causal = ii >= jj
    pairs = range(len(x2))
    mul = lambda x, y: x * y
    pick = lambda a, b: jnp.where(half, a, b)
    aca, acb = [_col(ac, 2 * p) for p in pairs], [_col(ac, 2 * p + 1) for p in pairs]
    la, lb = [_col(aclast, 2 * p) for p in pairs], [_col(aclast, 2 * p + 1) for p in pairs]
    dt2 = [pick(_col(dt, 2 * p), _col(dt, 2 * p + 1)) for p in pairs]
    xdt = _zip(mul, x2, dt2)
    sega = [jnp.exp(jnp.where(causal, aca[p] - _row(act, 2 * p), NEG_BIG)) for p in pairs]
    segb = [jnp.exp(jnp.where(causal, acb[p] - _row(act, 2 * p + 1), NEG_BIG)) for p in pairs]
    ma, mb = _zip(mul, sega, cb), _zip(mul, segb, cb)
    ydiag = _zip(pick, _mms(ma, xdt, _NN), _mms(mb, xdt, _NN))
    cdec = _zip(lambda a, b: pick(jnp.exp(a), jnp.exp(b)), aca, acb)
    cs = _mms(cg, s2, _NT)
    tail = _zip(lambda l1, a, l2, b: pick(jnp.exp(l1 - a), jnp.exp(l2 - b)), la, aca, lb, acb)
    zt = _zip(mul, xdt, tail)
    ea, eb = [jnp.exp(x) for x in la], [jnp.exp(x) for x in lb]
    tot = _zip(lambda a, b: jnp.where(_iota((HEAD, 1), 0) < 64, a, b), ea, eb)
    s_out = _zip(lambda s, t_, y: s * t_ + y, s2, tot, _mms(zt, bg, _TN))
    return dict(half=half, dt2=dt2, xdt=xdt, sega=sega, segb=segb, ma=ma, mb=mb, ydiag=ydiag, cdec=cdec, cs=cs,
                tail=tail, zt=zt, ea=ea, eb=eb, tot=tot, s_out=s_out)


def _ssd_group_inputs(cx_ref):
    cxb = [cx_ref[:, XBC_B + g * SSD_STATE: XBC_B + (g + 1) * SSD_STATE] for g in range(SSD_GROUPS)]
    cxc = [cx_ref[:, XBC_C + g * SSD_STATE: XBC_C + (g + 1) * SSD_STATE] for g in range(SSD_GROUPS)]
    bg, cg = [_silu(x) for x in cxb], [_silu(x) for x in cxc]
    return cxb, cxc, bg, cg, _mms(cg, bg, _NT)


def _per_pair(group_list):
    return [group_list[p // (SSD_PAIRS // SSD_GROUPS)] for p in range(SSD_PAIRS)]


def _ssd_chunk_fwd(cx, dtp, z, dt_bias, a_log, dskip, norm_w, name):
    t = cx.shape[0]
    nc = t // CHUNK
    gw = D_INNER // SSD_GROUPS

    def body(cx_ref, dtp_ref, z_ref, db_ref, al_ref, sk_ref, nw_ref, y_ref, h_ref, st_ref, state):
        @pl.when(pl.program_id(0) == 0)
        def _():
            state[...] = jnp.zeros_like(state)

        st_ref[0] = state[...]
        dt, _, _, ac, act, aclast = _ssd_scalars(dtp_ref[...], db_ref[...], al_ref[...])
        _, _, bg, cg, cb = _ssd_group_inputs(cx_ref)
        x2 = [_silu(cx_ref[:, _lanes(p)]) for p in range(SSD_PAIRS)]
        f = _ssd_pairs_fwd(x2, _per_pair(bg), _per_pair(cg), _per_pair(cb), dt, ac, act, aclast,
                           [state[p] for p in range(SSD_PAIRS)])
        for p in range(SSD_PAIRS):
            state[p] = f["s_out"][p]
            y_ref[:, _lanes(p)] = f["ydiag"][p] + f["cs"][p] * f["cdec"][p] + sk_ref[:, _lanes(p)] * x2[p]
        for g in range(SSD_GROUPS):
            sl = slice(g * gw, (g + 1) * gw)
            yg = y_ref[:, sl] * _silu(z_ref[:, sl])
            rstd = lax.rsqrt(jnp.mean(yg * yg, axis=1, keepdims=True) + RMS_EPS)
            h_ref[:, sl] = (yg * rstd * nw_ref[:, sl]).astype(h_ref.dtype)

    rows = lambda w: pl.BlockSpec((CHUNK, w), lambda i: (i, 0))
    act_ = lambda w: jax.ShapeDtypeStruct((t, w), F32)
    return _call(body, name, (nc,),
                 [rows(SSD_CONV_DIM), rows(HEAD), rows(D_INNER), _const(1, HEAD), _const(1, HEAD),
                  _const(1, D_INNER), _const(1, D_INNER)],
                 [rows(D_INNER), rows(D_INNER), pl.BlockSpec((1, SSD_PAIRS, HEAD, SSD_STATE), lambda i: (i, 0, 0, 0))],
                 [act_(D_INNER), jax.ShapeDtypeStruct((t, D_INNER), MM_DTYPE),
                  jax.ShapeDtypeStruct((nc, SSD_PAIRS, HEAD, SSD_STATE), F32)],
                 [pltpu.VMEM((SSD_PAIRS, HEAD, SSD_STATE), F32)], ("arbitrary",)
                 )(cx, dtp, z, dt_bias, a_log, dskip, norm_w)


def _ssd_chunk_bwd(cx, dtp, z, dt_bias, a_log, dskip, norm_w, y, states, dh, name):
    t = cx.shape[0]
    nc = t // CHUNK
    gw = D_INNER // SSD_GROUPS

    def body(cx_ref, dtp_ref, z_ref, db_ref, al_ref, sk_ref, nw_ref, y_ref, st_ref, dh_ref,
             dcx_ref, ddtp_ref, dz_ref, wide_ref, acc_ref, dstate, dy_s):
        @pl.when(pl.program_id(0) == 0)
        def _():
            dstate[...] = jnp.zeros_like(dstate)
            wide_ref[...] = jnp.zeros_like(wide_ref)
            acc_ref[...] = jnp.zeros_like(acc_ref)

        dtp = dtp_ref[...]
        dt, a, da, ac, act, aclast = _ssd_scalars(dtp, db_ref[...], al_ref[...])
        ii, jj = _iota((CHUNK, CHUNK), 0), _iota((CHUNK, CHUNK), 1)
        last_row = _iota((CHUNK, 1), 0) == CHUNK - 1
        for g in range(SSD_GROUPS):
            sl = slice(g * gw, (g + 1) * gw)
            yy, zz, dhh, nw = y_ref[:, sl], z_ref[:, sl], dh_ref[:, sl], nw_ref[:, sl]
            sz = _silu(zz)
            yg = yy * sz
            rstd = lax.rsqrt(jnp.mean(yg * yg, axis=1, keepdims=True) + RMS_EPS)
            n = yg * rstd
            dn = dhh * nw
            dyg = rstd * (dn - n * jnp.mean(dn * n, axis=1, keepdims=True))
            dy_s[:, sl] = dyg * sz
            dz_ref[:, sl] = (dyg * yy * _dsilu(zz)).astype(dz_ref.dtype)
            wide_ref[0:1, sl] += jnp.sum(dhh * n, axis=0, keepdims=True)

        pairs = range(SSD_PAIRS)
        mul, add, sub = (lambda x, y: x * y), (lambda x, y: x + y), (lambda x, y: x - y)
        rowsum = lambda x: jnp.sum(x, axis=1, keepdims=True)
        cxb, cxc, bg, cg, cb = _ssd_group_inputs(cx_ref)
        bgp, cgp = _per_pair(bg), _per_pair(cg)
        cxx = [cx_ref[:, _lanes(p)] for p in pairs]
        x2 = [_silu(x) for x in cxx]
        s2, dsn = [st_ref[0, p] for p in pairs], [dstate[p] for p in pairs]
        dy2 = [dy_s[:, _lanes(p)] for p in pairs]
        f = _ssd_pairs_fwd(x2, bgp, cgp, _per_pair(cb), dt, ac, act, aclast, s2)
        half = f["half"]
        lo = lambda x: jnp.where(half, x, 0.0)
        dx2 = [dy2[p] * sk_ref[:, _lanes(p)] for p in pairs]
        for p in pairs:
            wide_ref[1:2, _lanes(p)] += jnp.sum(dy2[p] * x2[p], axis=0, keepdims=True)
        gg = _zip(mul, dy2, f["cdec"])
        dc_p = _mms(gg, s2, _NN)
        ds_prev = _zip(lambda d, t_, y: d * t_ + y, dsn, f["tot"], _mms(gg, cgp, _TN))
        t1 = _zip(lambda d, c, e: d * c * e, dy2, f["cs"], f["cdec"])
        dac_a = [rowsum(lo(x)) for x in t1]
        dac_b = _zip(lambda x, a_: rowsum(x) - a_, t1, dac_a)
        dya = [lo(x) for x in dy2]
        dma, dmb = _mms(dya, f["xdt"], _NT), _mms(_zip(sub, dy2, dya), f["xdt"], _NT)
        dxdt = _zip(lambda a_, b_: jnp.where(half, a_, b_), _mms(f["ma"], dy2, _TN), _mms(f["mb"], dy2, _TN))
        dcb_p = _zip(lambda a_, sa, b_, sb: a_ * sa + b_ * sb, dma, f["sega"], dmb, f["segb"])
        ea_, eb_ = _zip(mul, dma, f["ma"]), _zip(mul, dmb, f["mb"])
        dac_a = _zip(lambda x, e: x + rowsum(e), dac_a, ea_)
        dac_b = _zip(lambda x, e: x + rowsum(e), dac_b, eb_)
        dzt = _mms(bgp, dsn, _NT)
        db_p = _mms(f["zt"], dsn, _NN)
        dxdt = _zip(lambda x, d, t_: x + d * t_, dxdt, dzt, f["tail"])
        t2 = _zip(mul, dzt, f["zt"])
        t2a = [rowsum(lo(x)) for x in t2]
        t2b = _zip(lambda x, a_: rowsum(x) - a_, t2, t2a)
        t3 = _zip(mul, dsn, s2)
        t3a = [_total(x[:64]) for x in t3]
        dla = _zip(lambda x, y, e: _total(x) + y * e, t2a, t3a, f["ea"])
        dlb = _zip(lambda x, y, e: _total(x) + _total(y[64:]) * e, t2b, t3, f["eb"])
        dac_a = _zip(lambda x, y, l: x - y + jnp.where(last_row, l, 0.0), dac_a, t2a, dla)
        dac_b = _zip(lambda x, y, l: x - y + jnp.where(last_row, l, 0.0), dac_b, t2b, dlb)
        dx2 = _zip(lambda x, d, t_: x + d * t_, dx2, dxdt, f["dt2"])
        t4 = _zip(mul, dxdt, x2)
        t4a = [rowsum(lo(x)) for x in t4]
        t4b = _zip(lambda x, a_: rowsum(x) - a_, t4, t4a)
        zero = jnp.zeros((CHUNK, HEAD), F32)
        ddt_acc, dac_acc, drow_acc = zero, zero, jnp.zeros((HEAD, CHUNK), F32)
        for p in pairs:
            dcx_ref[:, _lanes(p)] = dx2[p] * _dsilu(cxx[p])
            dstate[p] = ds_prev[p]
            ddt_acc = _put_col(_put_col(ddt_acc, t4a[p], 2 * p), t4b[p], 2 * p + 1)
            dac_acc = _put_col(_put_col(dac_acc, dac_a[p], 2 * p), dac_b[p], 2 * p + 1)
            drow_acc = _put_row(_put_row(drow_acc, -jnp.sum(ea_[p], axis=0, keepdims=True), 2 * p),
                                -jnp.sum(eb_[p], axis=0, keepdims=True), 2 * p + 1)
        per = SSD_PAIRS // SSD_GROUPS
        gsum = lambda xs: [functools.reduce(add, xs[g * per:(g + 1) * per]) for g in range(SSD_GROUPS)]
        dcb = gsum(dcb_p)
        dc = _zip(add, gsum(dc_p), _mms(dcb, bg, _NN))
        db = _zip(add, gsum(db_p), _mms(dcb, cg, _TN))
        for g in range(SSD_GROUPS):
            dcx_ref[:, XBC_B + g * SSD_STATE: XBC_B + (g + 1) * SSD_STATE] = db[g] * _dsilu(cxb[g])
            dcx_ref[:, XBC_C + g * SSD_STATE: XBC_C + (g + 1) * SSD_STATE] = dc[g] * _dsilu(cxc[g])
        dac = dac_acc + _mmx(_eye(CHUNK), drow_acc, _NT)
        dda = _mmx(_chunk_mask(CHUNK, True), dac, _NN)
        ddt = ddt_acc + dda * a
        ddtp = ddt * _sig(dtp + db_ref[...])
        ddtp_ref[...] = ddtp.astype(ddtp_ref.dtype)
        acc = _put_sub(jnp.zeros((8, HEAD), F32), jnp.sum(dda * da, axis=0, keepdims=True), 0)
        acc_ref[...] += _put_sub(acc, jnp.sum(ddtp, axis=0, keepdims=True), 1)

    rows = lambda w: pl.BlockSpec((CHUNK, w), lambda i: (nc - 1 - i, 0))
    act_ = lambda w: jax.ShapeDtypeStruct((t, w), F32)
    return _call(body, name, (nc,),
                 [rows(SSD_CONV_DIM), rows(HEAD), rows(D_INNER), _const(1, HEAD), _const(1, HEAD),
                  _const(1, D_INNER), _const(1, D_INNER), rows(D_INNER),
                  pl.BlockSpec((1, SSD_PAIRS, HEAD, SSD_STATE), lambda i: (nc - 1 - i, 0, 0, 0)), rows(D_INNER)],
                 [rows(SSD_CONV_DIM), rows(HEAD), rows(D_INNER), _const(8, D_INNER), _const(8, HEAD)],
                 [act_(SSD_CONV_DIM), jax.ShapeDtypeStruct((t, HEAD), MM_DTYPE), jax.ShapeDtypeStruct((t, D_INNER), MM_DTYPE),
                  jax.ShapeDtypeStruct((8, D_INNER), F32),
                  jax.ShapeDtypeStruct((8, HEAD), F32)],
                 [pltpu.VMEM((SSD_PAIRS, HEAD, SSD_STATE), F32), pltpu.VMEM((CHUNK, D_INNER), F32)], ("arbitrary",)
                 )(cx, dtp, z, dt_bias, a_log, dskip, norm_w, y, states, dh)


LN_TB = 512


def _ln_stats(x, y):
    u = ALPHA * x + y
    mu = jnp.mean(u, axis=1, keepdims=True)
    cen = u - mu
    rstd = lax.rsqrt(jnp.mean(cen * cen, axis=1, keepdims=True) + LN_EPS)
    return cen * rstd


def _ln_fwd(x, y, g, b, name):
    t, d = x.shape

    def body(x_ref, y_ref, g_ref, b_ref, o_ref, omm_ref):
        out = _ln_stats(x_ref[...], y_ref[...]) * g_ref[...] + b_ref[...]
        o_ref[...] = out
        omm_ref[...] = out.astype(omm_ref.dtype)

    tb = min(2 * LN_TB, t)
    return _call(body, name, (t // tb,), [_rows(d, tb), _rows(d, tb), _const(1, d), _const(1, d)],
                 [_rows(d, tb)] * 2, [jax.ShapeDtypeStruct((t, d), F32), jax.ShapeDtypeStruct((t, d), MM_DTYPE)],
                 semantics=("parallel",))(x, y, g, b)


def _ln_loss(x, y, g, b, target, name):
    t, d = x.shape

    def body(x_ref, y_ref, g_ref, b_ref, t_ref, d_ref, acc_ref):
        err = _ln_stats(x_ref[...], y_ref[...]) * g_ref[...] + b_ref[...] - t_ref[...]
        d_ref[...] = err * (1.0 / d)

        @pl.when(pl.program_id(0) == 0)
        def _():
            acc_ref[...] = jnp.zeros_like(acc_ref)

        acc_ref[...] += _put_sub(jnp.zeros((8, d), F32), jnp.sum(err * err, axis=0, keepdims=True), 0)

    return _call(body, name, (t // LN_TB,),
                 [_rows(d, LN_TB), _rows(d, LN_TB), _const(1, d), _const(1, d), _rows(d, LN_TB)],
                 [_rows(d, LN_TB), _const(8, d)],
                 [jax.ShapeDtypeStruct((t, d), F32), jax.ShapeDtypeStruct((8, d), F32)],
                 semantics=("arbitrary",))(x, y, g, b, target)


def _ln_bwd(dout, x, y, g, name):
    t, d = x.shape

    def body(d_ref, x_ref, y_ref, g_ref, du_ref, dumm_ref, acc_ref):
        u = ALPHA * x_ref[...] + y_ref[...]
        mu = jnp.mean(u, axis=1, keepdims=True)
        cen = u - mu
        rstd = lax.rsqrt(jnp.mean(cen * cen, axis=1, keepdims=True) + LN_EPS)
        xh = cen * rstd
        do = d_ref[...]
        dxh = do * g_ref[...]
        du = rstd * (dxh - jnp.mean(dxh, axis=1, keepdims=True) - xh * jnp.mean(dxh * xh, axis=1, keepdims=True))
        du_ref[...] = du
        dumm_ref[...] = du.astype(dumm_ref.dtype)
        acc = _put_sub(jnp.zeros((8, d), F32), jnp.sum(do * xh, axis=0, keepdims=True), 0)
        acc = _put_sub(acc, jnp.sum(do, axis=0, keepdims=True), 1)

        @pl.when(pl.program_id(0) == 0)
        def _():
            acc_ref[...] = jnp.zeros_like(acc_ref)

        acc_ref[...] += acc

    return _call(body, name, (t // LN_TB,), [_rows(d, LN_TB)] * 3 + [_const(1, d)],
                 [_rows(d, LN_TB), _rows(d, LN_TB), _const(8, d)],
                 [jax.ShapeDtypeStruct((t, d), F32), jax.ShapeDtypeStruct((t, d), MM_DTYPE),
                  jax.ShapeDtypeStruct((8, d), F32)],
                 semantics=("arbitrary",))(dout, x, y, g)


def _adamw(w, gslots, m, v, name):
    r, c = w.shape
    rb = _tile_rows(r)
    c1 = 1.0 - ADAM_B1 ** ADAM_STEP
    c2 = 1.0 - ADAM_B2 ** ADAM_STEP

    def body(w_ref, g_ref, m_ref, v_ref, go_ref, d_ref, mo_ref, vo_ref):
        g = g_ref[0].astype(F32)
        for s in range(1, N_DEV):
            g = g + g_ref[s].astype(F32)
        mn = ADAM_B1 * m_ref[...] + (1.0 - ADAM_B1) * g
        vn = ADAM_B2 * v_ref[...] + (1.0 - ADAM_B2) * (g * g)
        go_ref[...] = g
        mo_ref[...] = mn
        vo_ref[...] = vn
        d_ref[...] = -ADAM_LR * ((mn / c1) / (jnp.sqrt(vn / c2) + ADAM_EPS) + ADAM_WD * w_ref[...])

    blk = pl.BlockSpec((rb, c), lambda i: (i, 0))
    sds = jax.ShapeDtypeStruct((r, c), F32)
    return _call(body, name, (r // rb,), [blk, pl.BlockSpec((N_DEV, rb, c), lambda i: (0, i, 0)), blk, blk],
                 [blk] * 4, [sds] * 4, semantics=("parallel",))(w, gslots, m, v)


def _tile_rows(r):
    for rb in (256, 128, 64, 32, 16, 8):
        if r % rb == 0:
            return rb
    return r


def _pack(arrs, lead=0):
    flats = []
    for a in arrs:
        f = a.reshape(a.shape[:lead] + (-1,)).astype(F32)
        flats.append(jnp.pad(f, [(0, 0)] * lead + [(0, (-f.shape[-1]) % 128)]))
    v = jnp.concatenate(flats, axis=-1)
    v = jnp.pad(v, [(0, 0)] * lead + [(0, (-v.shape[-1]) % 1024)])
    return v.reshape(v.shape[:lead] + (-1, 128))


def _unpack(buf, shapes, lead=0):
    flat = buf.reshape(buf.shape[:lead] + (-1,))
    outs, off = [], 0
    for s in shapes:
        n = math.prod(s)
        outs.append(flat[..., off:off + n].reshape(buf.shape[:lead] + tuple(s)))
        off += n + (-n) % 128
    return outs


def _cols_gathered(g):
    n, l, r, c = g.shape
    return g.transpose(1, 2, 0, 3).reshape(l, r, n * c)


def _cols_to_slabs(full):
    l, r, c = full.shape
    return full.reshape(l, r, N_DEV, c // N_DEV).transpose(2, 0, 1, 3)


def _rows_gathered(g):
    n, l, r, c = g.shape
    return g.transpose(1, 0, 2, 3).reshape(l, n * r, c)


def _rows_to_slabs(full):
    l, r, c = full.shape
    return full.reshape(l, N_DEV, r // N_DEV, c).transpose(1, 0, 2, 3)


def _pad_cols(w, at, width):
    return jnp.pad(w, ((0, 0), (at, width - at - w.shape[1])))


def _pad_lanes(v, width=HEAD):
    return jnp.pad(v.reshape(1, -1), ((0, 0), (0, width - v.size)))


def _taps8(w, bias=None):
    rows = [w] if bias is None else [w, bias.reshape(1, -1)]
    w8 = jnp.concatenate(rows, axis=0)
    return jnp.pad(w8, ((0, 8 - w8.shape[0]), (0, 0)))


class _Carrier:
    def __init__(self):
        self.jobs, self.got = {}, {}

    def put(self, matmul_name, key, src, slabs):
        self.jobs[matmul_name] = (key, src, slabs)

    def matmul(self, a, b, mode, name, **kw):
        job = self.jobs.pop(name, None)
        if job is None:
            return _matmul(a, b, mode, name, **kw)
        key, src, slabs = job
        out, self.got[key] = _matmul(a, b, mode, name, carry=(src, slabs), **kw)
        return out


def _gdn_forward(x, p, tag, mm):
    pq = mm(x, p["w_qkv"], "nn", tag + "_in_qkv")
    z = mm(x, p["w_z"], "nn", tag + "_in_z")
    ba = mm(x, p["w_ba"], "nn", tag + "_in_ba")
    c = _conv_fwd(pq, p["conv8"], 4, tag + "_conv")
    qn, kn, v, beta, gc = _gdn_ew_fwd(c, ba, p["a_log"], p["dt_bias"], tag + "_ew")
    o, h, states = _gdn_chunk_fwd(qn, kn, v, beta, gc, z, p["norm_w"], tag + "_chunk")
    y = mm(h, p["w_out"], "nn", tag + "_out")
    return y, dict(pq=pq, z=z, ba=ba, c=c, qn=qn, kn=kn, v=v, beta=beta, gc=gc, o=o, h=h, states=states)


def _gdn_backward(x, du, du_mm, p, s, tag, mm, ship):
    dh = mm(du_mm, p["w_out"], "nt", tag + "_bwd_dh")
    g_out = mm(s["h"], du_mm, "tn", tag + "_bwd_wout")
    ship("w_out", g_out)
    dq, dk, dv, dz, dbeta, dgc, nacc = _gdn_chunk_bwd(
        s["qn"], s["kn"], s["v"], s["beta"], s["gc"], s["z"], p["norm_w"], s["o"], s["states"], dh, tag + "_bwd_chunk")
    dc, dba, sacc = _gdn_ew_bwd(s["c"], s["ba"], p["a_log"], p["dt_bias"], dq, dk, dv, dbeta, dgc, tag + "_bwd_ew")
    dpq, dconv = _conv_bwd(dc, s["pq"], p["conv8"], 4, tag + "_bwd_conv")
    g_qkv = mm(x, dpq, "tn", tag + "_bwd_w_qkv")
    g_z = mm(x, dz, "tn", tag + "_bwd_w_z")
    g_ba = mm(x, dba, "tn", tag + "_bwd_w_ba")
    g_in = jnp.concatenate([g_qkv, g_z, g_ba[:, :GDN_V_HEADS], g_ba[:, HEAD:HEAD + GDN_V_HEADS]], axis=1)
    ship("w_in", g_in)
    dx = mm(dpq, p["w_qkv"], "nt", tag + "_bwd_dx_qkv", add=du, add_scale=ALPHA)
    dx = mm(dz, p["w_z"], "nt", tag + "_bwd_dx_z", add=dx)
    dx = mm(dba, p["w_ba"], "nt", tag + "_bwd_dx_ba", add=dx)
    grads = dict(w_in=g_in, w_out=g_out, conv_w=dconv[:4], a_log=sacc[0, :GDN_V_HEADS], dt_bias=sacc[1, :GDN_V_HEADS],
                 norm_w=nacc[0])
    return dx, grads


def _sc_forward(x, p, tag, mm):
    hh = mm(x, p["w_h"], "nn", tag + "_in_h")
    bg = mm(x, p["w_b"], "nn", tag + "_in_b")
    cg = mm(x, p["w_c"], "nn", tag + "_in_c")
    z = mm(x, p["w_z"], "nn", tag + "_in_z")
    cv = _conv_fwd(cg, p["conv8"], 3, tag + "_conv", u2=hh)
    h = _sc_gate_fwd(bg, cv, z, tag + "_gate")
    y = mm(h, p["w_out"], "nn", tag + "_out")
    return y, dict(hh=hh, bg=bg, cg=cg, z=z, cv=cv, h=h)


def _sc_backward(x, du, du_mm, p, s, tag, mm, ship):
    dh = mm(du_mm, p["w_out"], "nt", tag + "_bwd_dh")
    g_out = mm(s["h"], du_mm, "tn", tag + "_bwd_wout")
    ship("w_out", g_out)
    dbg, dcv, dz = _sc_gate_bwd(dh, s["bg"], s["cv"], s["z"], tag + "_bwd_gate")
    dcg, dhh, dconv = _conv_bwd(dcv, s["cg"], p["conv8"], 3, tag + "_bwd_conv", u2=s["hh"])
    g_in = jnp.concatenate([mm(x, d, "tn", tag + "_bwd_w_" + n)
                            for n, d in (("h", dhh), ("b", dbg), ("c", dcg), ("z", dz))], axis=1)
    ship("w_in", g_in)
    dx = mm(dhh, p["w_h"], "nt", tag + "_bwd_dx_h", add=du, add_scale=ALPHA)
    dx = mm(dbg, p["w_b"], "nt", tag + "_bwd_dx_b", add=dx)
    dx = mm(dcg, p["w_c"], "nt", tag + "_bwd_dx_c", add=dx)
    dx = mm(dz, p["w_z"], "nt", tag + "_bwd_dx_z", add=dx)
    return dx, dict(w_in=g_in, w_out=g_out, conv_w=dconv[:3])


def _ssd_forward(x, p, tag, mm):
    z = mm(x, p["w_z"], "nn", tag + "_in_z")
    xbc = mm(x, p["w_xbc"], "nn", tag + "_in_xbc")
    dtp = mm(x, p["w_dt"], "nn", tag + "_in_dt")
    cx = _conv_fwd(xbc, p["conv8"], 4, tag + "_conv", bias=True)
    y, h, states = _ssd_chunk_fwd(cx, dtp, z, p["dt_bias"], p["a_log"], p["dskip"], p["norm_w"], tag + "_chunk")
    out = mm(h, p["w_out"], "nn", tag + "_out")
    return out, dict(z=z, xbc=xbc, dtp=dtp, cx=cx, y=y, h=h, states=states)


def _ssd_backward(x, du, du_mm, p, s, tag, mm, ship):
    dh = mm(du_mm, p["w_out"], "nt", tag + "_bwd_dh")
    g_out = mm(s["h"], du_mm, "tn", tag + "_bwd_wout")
    ship("w_out", g_out)
    dcx, ddtp, dz, wide, acc = _ssd_chunk_bwd(s["cx"], s["dtp"], s["z"], p["dt_bias"], p["a_log"], p["dskip"],
                                              p["norm_w"], s["y"], s["states"], dh, tag + "_bwd_chunk")
    dxbc, dconv = _conv_bwd(dcx, s["xbc"], p["conv8"], 4, tag + "_bwd_conv")
    g_dt = mm(x, ddtp, "tn", tag + "_bwd_w_dt")
    g_in = jnp.concatenate([mm(x, dz, "tn", tag + "_bwd_w_z"), mm(x, dxbc, "tn", tag + "_bwd_w_xbc"),
                            g_dt[:, :32]], axis=1)
    ship("w_in", g_in)
    dx = mm(dz, p["w_z"], "nt", tag + "_bwd_dx_z", add=du, add_scale=ALPHA)
    dx = mm(dxbc, p["w_xbc"], "nt", tag + "_bwd_dx_xbc", add=dx)
    dx = mm(ddtp, p["w_dt"], "nt", tag + "_bwd_dx_dt", add=dx)
    grads = dict(w_in=g_in, w_out=g_out, conv_w=dconv[:4], conv_b=dconv[4], a_log=acc[0, :32], dt_bias=acc[1, :32],
                 d_skip=jnp.sum(wide[1].reshape(32, 64), axis=1), norm_w=wide[0])
    return dx, grads


_WEIGHTS = ['gdn_w_in', 'gdn_conv_w', 'gdn_a_log', 'gdn_dt_bias', 'gdn_norm_w', 'gdn_w_out', 'sc_w_in', 'sc_conv_w',
            'sc_w_out', 'ssd_w_in', 'ssd_conv_w', 'ssd_conv_b', 'ssd_a_log', 'ssd_dt_bias', 'ssd_d_skip',
            'ssd_norm_w', 'ssd_w_out', 'ln_g', 'ln_b']
_BIG = {'gdn_w_in': 'cols', 'gdn_w_out': 'rows', 'sc_w_in': 'cols', 'sc_w_out': 'rows', 'ssd_w_in': 'cols',
        'ssd_w_out': 'rows'}
_SMALL_SHARDED = ['gdn_conv_w', 'sc_conv_w', 'ssd_conv_w', 'ssd_conv_b', 'ssd_norm_w']
_SMALL = [n for n in _WEIGHTS if n not in _BIG]


def kernel(x, gdn_w_in, gdn_conv_w, gdn_a_log, gdn_dt_bias, gdn_norm_w, gdn_w_out, sc_w_in, sc_conv_w, sc_w_out, ssd_w_in, ssd_conv_w, ssd_conv_b, ssd_a_log, ssd_dt_bias, ssd_d_skip, ssd_norm_w, ssd_w_out, ln_g, ln_b, loss_target, m_gdn_w_in, m_gdn_conv_w, m_gdn_a_log, m_gdn_dt_bias, m_gdn_norm_w, m_gdn_w_out, m_sc_w_in, m_sc_conv_w, m_sc_w_out, m_ssd_w_in, m_ssd_conv_w, m_ssd_conv_b, m_ssd_a_log, m_ssd_dt_bias, m_ssd_d_skip, m_ssd_norm_w, m_ssd_w_out, m_ln_g, m_ln_b, v_gdn_w_in, v_gdn_conv_w, v_gdn_a_log, v_gdn_dt_bias, v_gdn_norm_w, v_gdn_w_out, v_sc_w_in, v_sc_conv_w, v_sc_w_out, v_ssd_w_in, v_ssd_conv_w, v_ssd_conv_b, v_ssd_a_log, v_ssd_dt_bias, v_ssd_d_skip, v_ssd_norm_w, v_ssd_w_out, v_ln_g, v_ln_b):
    args = locals()
    wts = {n: args[n] for n in _WEIGHTS}
    mom = {n: args["m_" + n] for n in _WEIGHTS}
    vel = {n: args["v_" + n] for n in _WEIGHTS}
    me = 4 * lax.axis_index("x") + 2 * lax.axis_index("y") + lax.axis_index("c")
    x0, target = x[0], loss_target[0]

    car = _Carrier()
    shard = lambda n, j: wts[n][j:j + 1].astype(MM_DTYPE)
    gathered_w = lambda n, j: (_cols_gathered if _BIG[n] == "cols" else _rows_gathered)(car.got[n, j])[0]

    for n in ('gdn_w_in', 'gdn_w_out'):
        car.got[n, 0] = _exchange(shard(n, 0), "gather_%s0" % n, slabs=False)
    riders = {0: [("l0_gdn_in_qkv", 'sc_w_in', 0), ("l0_gdn_in_z", 'sc_w_out', 0), ("l0_gdn_out", 'ssd_w_out', 0)],
              1: [("l1_sc_in_h", 'ssd_w_in', 0)],
              2: [("l2_ssd_in_xbc", 'gdn_w_in', 1), ("l2_ssd_in_z", 'gdn_w_out', 1)]}
    full = {}
    small_shapes = [wts[n].shape for n in _SMALL_SHARDED]
    gathered = _exchange(_pack([wts[n] for n in _SMALL_SHARDED]), "gather_small", slabs=False)
    for n, g in zip(_SMALL_SHARDED, _unpack(gathered, small_shapes, lead=1)):
        full[n] = jnp.moveaxis(g, 0, -2).reshape(g.shape[1:-1] + (N_DEV * g.shape[-1],))
    for n in _SMALL:
        full.setdefault(n, wts[n])

    def gdn_params(j):
        w = gathered_w('gdn_w_in', j)
        return dict(w_qkv=w[:, :GDN_CONV_DIM], w_z=w[:, GDN_CONV_DIM:GDN_CONV_DIM + D_INNER],
                    w_ba=jnp.concatenate([_pad_cols(w[:, 6144:6160], 0, HEAD), _pad_cols(w[:, 6160:6176], 0, HEAD)], 1),
                    conv8=_taps8(full['gdn_conv_w'][j]), a_log=_pad_lanes(full['gdn_a_log'][j]),
                    dt_bias=_pad_lanes(full['gdn_dt_bias'][j]), norm_w=full['gdn_norm_w'][j].reshape(1, HEAD),
                    w_out=gathered_w('gdn_w_out', j))

    def sc_params():
        w = gathered_w('sc_w_in', 0)
        return dict(w_h=w[:, :2048], w_b=w[:, 2048:4096], w_c=w[:, 4096:6144], w_z=w[:, 6144:],
                    conv8=_taps8(full['sc_conv_w'][0]), w_out=gathered_w('sc_w_out', 0))

    def ssd_params():
        w = gathered_w('ssd_w_in', 0)
        return dict(w_z=w[:, :D_INNER], w_xbc=w[:, D_INNER:D_INNER + SSD_CONV_DIM],
                    w_dt=_pad_cols(w[:, D_INNER + SSD_CONV_DIM:], 0, HEAD),
                    conv8=_taps8(full['ssd_conv_w'][0], full['ssd_conv_b'][0]), a_log=_pad_lanes(full['ssd_a_log'][0]),
                    dt_bias=_pad_lanes(full['ssd_dt_bias'][0]),
                    dskip=jnp.repeat(full['ssd_d_skip'][0], 64).reshape(1, D_INNER),
                    norm_w=full['ssd_norm_w'][0].reshape(1, D_INNER), w_out=gathered_w('ssd_w_out', 0))

    layers = [("gdn", _gdn_forward, _gdn_backward, lambda: gdn_params(0)), ("sc", _sc_forward, _sc_backward, sc_params),
              ("ssd", _ssd_forward, _ssd_backward, ssd_params), ("gdn", _gdn_forward, _gdn_backward, lambda: gdn_params(1))]

    acts, acts_mm, ys, saved, params = [x0], [x0.astype(MM_DTYPE)], [], [], []
    for i, (kind, fwd, _, make_params) in enumerate(layers):
        params.append(make_params())
        for matmul_name, n, j in riders.get(i, ()):
            car.put(matmul_name, (n, j), shard(n, j), False)
        y, s = fwd(acts_mm[-1], params[i], "l%d_%s" % (i, kind), car.matmul)
        ys.append(y)
        saved.append(s)
        gain, bias = full['ln_g'][i].reshape(1, -1), full['ln_b'][i].reshape(1, -1)
        if i < DEPTH - 1:
            out, out_mm = _ln_fwd(acts[-1], y, gain, bias, "l%d_ln" % i)
            acts.append(out)
            acts_mm.append(out_mm)
        else:
            dact, loss_acc = _ln_loss(acts[-1], y, gain, bias, target, "l%d_ln_loss" % i)
    loss = lax.psum(0.5 / D_MODEL * jnp.sum(loss_acc[0]), ("x", "y", "c"))

    grad_riders = {3: dict(w_out="l3_gdn_bwd_w_qkv", w_in="l3_gdn_bwd_dx_qkv"),
                   2: dict(w_out="l2_ssd_bwd_w_z", w_in="l2_ssd_bwd_dx_xbc"),
                   1: dict(w_out="l1_sc_bwd_w_h", w_in="l0_gdn_bwd_w_qkv"),
                   0: dict(w_out="l0_gdn_bwd_w_z", w_in="l0_gdn_bwd_dx_qkv")}

    def shipper(i):
        def ship(key, g):
            slabs = _cols_to_slabs(g[None]) if key == 'w_in' else _rows_to_slabs(g[None])
            car.put(grad_riders[i][key], ('grad', i, key), slabs.astype(MM_DTYPE), True)
        return ship

    lg = [None] * DEPTH
    d_ln_g, d_ln_b = [None] * DEPTH, [None] * DEPTH
    for i in reversed(range(DEPTH)):
        kind, _, bwd, _ = layers[i]
        du, du_mm, acc = _ln_bwd(dact, acts[i], ys[i], full['ln_g'][i].reshape(1, -1), "l%d_ln_bwd" % i)
        d_ln_g[i], d_ln_b[i] = acc[0], acc[1]
        dact, lg[i] = bwd(acts_mm[i], du, du_mm, params[i], saved[i], "l%d_%s" % (i, kind), car.matmul, shipper(i))
    assert not car.jobs, car.jobs
    grad_x = dact[None]

    stack = lambda k: jnp.stack([lg[0][k], lg[3][k]])
    local = {
        'gdn_conv_w': stack('conv_w'), 'gdn_a_log': stack('a_log'),
        'gdn_dt_bias': stack('dt_bias'), 'gdn_norm_w': stack('norm_w'),
        'sc_conv_w': lg[1]['conv_w'][None],
        'ssd_conv_w': lg[2]['conv_w'][None], 'ssd_conv_b': lg[2]['conv_b'][None],
        'ssd_a_log': lg[2]['a_log'][None], 'ssd_dt_bias': lg[2]['dt_bias'][None], 'ssd_d_skip': lg[2]['d_skip'][None],
        'ssd_norm_w': lg[2]['norm_w'][None],
        'ln_g': jnp.stack(d_ln_g), 'ln_b': jnp.stack(d_ln_b)}

    out = {}
    layers_of = {'gdn': (0, 3), 'sc': (1,), 'ssd': (2,)}
    for n in _BIG:
        kind, key = n.split('_', 1)
        recv = jnp.concatenate([car.got['grad', i, key] for i in layers_of[kind]], axis=1)
        shp = wts[n].shape
        r, c = shp[0] * shp[1], shp[2]
        res = _adamw(wts[n].reshape(r, c), recv.reshape(N_DEV, r, c), mom[n].reshape(r, c), vel[n].reshape(r, c),
                     "adamw_" + n)
        out[n] = [a.reshape(shp) for a in res]
    full_shapes = [local[n].shape for n in _SMALL]
    gathered = _exchange(_pack([local[n] for n in _SMALL]), "gather_small_grads", slabs=False)
    gs = []
    for n, g in zip(_SMALL, _unpack(gathered, full_shapes, lead=1)):
        if n in _SMALL_SHARDED:
            width = wts[n].shape[-1]
            g = lax.dynamic_slice_in_dim(g, me * width, width, axis=g.ndim - 1)
        gs.append(g)
    shapes = [wts[n].shape for n in _SMALL]
    res = _adamw(_pack([wts[n] for n in _SMALL]), _pack(gs, lead=1), _pack([mom[n] for n in _SMALL]),
                 _pack([vel[n] for n in _SMALL]), "adamw_small")
    for k, n in enumerate(_SMALL):
        out[n] = [_unpack(a, shapes)[k] for a in res]

    return (loss, grad_x, *[out[n][0] for n in _WEIGHTS], *[out[n][1] for n in _WEIGHTS],
            *[out[n][2] for n in _WEIGHTS], *[out[n][3] for n in _WEIGHTS])
```

```python
import functools
import math

import jax
import jax.numpy as jnp
from jax import lax
from jax.experimental import pallas as pl
from jax.experimental.pallas import tpu as pltpu

F32 = jnp.float32
MM_DTYPE = jnp.bfloat16

N_DEV = 8
D_MODEL = 1024
D_INNER = 2048
CHUNK = 64
HEAD = 128
GDN_V_HEADS = 16
GDN_GROUP = 16
GDN_QK_HEADS = 8
GDN_QK_DIM = 1024
GDN_CONV_DIM = 4096
SSD_PAIRS = 16
SSD_GROUPS = 4
SSD_STATE = 128
SSD_CONV_DIM = 3072
DEPTH = 4
ALPHA = (2 * DEPTH) ** 0.25
RMS_EPS = 1e-6
LN_EPS = 1e-5
L2_EPS = 1e-6
ADAM_LR, ADAM_B1, ADAM_B2, ADAM_EPS, ADAM_WD, ADAM_STEP = 0.001, 0.9, 0.999, 1e-08, 0.01, 10

VMEM_LIMIT_BYTES = 48 * 1024 * 1024
NEG_BIG = -1e30

_NN = (((1,), (0,)), ((), ()))
_NT = (((1,), (1,)), ((), ()))
_TN = (((0,), (0,)), ((), ()))


def _mm(a, b, dims):
    return lax.dot_general(a.astype(MM_DTYPE), b.astype(MM_DTYPE), dims, preferred_element_type=F32)


def _mmx(a, b, dims):
    return lax.dot_general(a, b, dims, precision=lax.Precision.HIGHEST, preferred_element_type=F32)


def _iota(shape, dim):
    return lax.broadcasted_iota(jnp.int32, shape, dim)


def _eye(n):
    return (_iota((n, n), 0) == _iota((n, n), 1)).astype(F32)


def _sig(x):
    return jax.nn.sigmoid(x)


def _silu(x):
    return x * _sig(x)


def _dsilu(x):
    s = _sig(x)
    return s * (1.0 + x * (1.0 - s))


def _softplus(x):
    return jnp.maximum(x, 0.0) + jnp.log(1.0 + jnp.exp(-jnp.abs(x)))


def _col(x, h):
    return jnp.sum(jnp.where(_iota(x.shape, 1) == h, x, 0.0), axis=1, keepdims=True)


def _row(x, h):
    return jnp.sum(jnp.where(_iota(x.shape, 0) == h, x, 0.0), axis=0, keepdims=True)


def _put_col(acc, col, h):
    return jnp.where(_iota(acc.shape, 1) == h, col, acc)


def _put_row(acc, row, h):
    return jnp.where(_iota(acc.shape, 0) == h, row, acc)


def _put_sub(acc, row, j):
    return acc + jnp.where(_iota(acc.shape, 0) == j, row, 0.0)


def _lanes(h):
    return pl.ds(h * HEAD, HEAD) if isinstance(h, int) else pl.ds(pl.multiple_of(h * HEAD, HEAD), HEAD)


def _total(x):
    return jnp.sum(jnp.sum(x, axis=0, keepdims=True), axis=1, keepdims=True)


def _call(body, name, grid, in_specs, out_specs, out_shape, scratch_shapes=(), semantics=None):
    return pl.pallas_call(
        body, name=name, grid=grid, in_specs=in_specs, out_specs=out_specs, out_shape=out_shape,
        scratch_shapes=list(scratch_shapes),
        compiler_params=pltpu.CompilerParams(dimension_semantics=semantics, vmem_limit_bytes=VMEM_LIMIT_BYTES))


def _tile(n, pref):
    if n <= pref:
        return n
    t = pref
    while n % t:
        t -= 128
    return t


def _exchange_copies(src_ref, out_ref, send_sems, recv_sems, local_sem, slabs):
    x, y, c = lax.axis_index("x"), lax.axis_index("y"), lax.axis_index("c")
    me = 4 * x + 2 * y + c
    mine = src_ref.at[me] if slabs else src_ref
    local = pltpu.make_async_copy(mine, out_ref.at[me], local_sem)
    sends, recvs = [], []
    for r in range(1, N_DEV):
        px = 1 - x if r & 4 else x
        py = 1 - y if r & 2 else y
        pc = 1 - c if r & 1 else c
        peer = 4 * px + 2 * py + pc
        kw = dict(send_sem=send_sems.at[r - 1], recv_sem=recv_sems.at[r - 1], device_id=(px, py, pc),
                  device_id_type=pl.DeviceIdType.MESH)
        sends.append(pltpu.make_async_remote_copy(src_ref=src_ref.at[peer] if slabs else src_ref,
                                                  dst_ref=out_ref.at[me], **kw))
        recvs.append(pltpu.make_async_remote_copy(src_ref=mine, dst_ref=out_ref.at[peer], **kw))
    return local, sends, recvs


def _gather_copies(src_ref, out_ref, send_sems, recv_sems, local_sem):
    x, y, c = lax.axis_index("x"), lax.axis_index("y"), lax.axis_index("c")
    chips = [(1 - x, y), (x, 1 - y), (1 - x, 1 - y)]
    sibling = (x, y, 1 - c)

    def slot(px, py, pc):
        return out_ref.at[4 * px + 2 * py + pc]

    def copy(k, src, dst, to):
        return pltpu.make_async_remote_copy(src_ref=src, dst_ref=dst, send_sem=send_sems.at[k], recv_sem=recv_sems.at[k],
                                            device_id=to, device_id_type=pl.DeviceIdType.MESH)

    mine = slot(x, y, c)
    local = pltpu.make_async_copy(src_ref, mine, local_sem)
    first = [copy(0, src_ref, mine, sibling)] + [copy(1 + j, src_ref, mine, (*chip, c)) for j, chip in enumerate(chips)]
    landed = [copy(1 + j, src_ref, slot(*chip, c), sibling) for j, chip in enumerate(chips)]
    passed = [copy(4 + j, slot(*chip, c), slot(*chip, c), sibling) for j, chip in enumerate(chips)]
    from_sibling = [copy(0, src_ref, slot(x, y, 1 - c), sibling)] + \
        [copy(4 + j, src_ref, slot(*chip, 1 - c), sibling) for j, chip in enumerate(chips)]
    return local, first, landed, passed, from_sibling


def _exchange_start(*refs, slabs):
    if not slabs:
        local, first = _gather_copies(*refs)[:2]
        local.start()
        for cp in first:
            cp.start()
        return
    local, sends, _ = _exchange_copies(*refs, slabs=slabs)
    local.start()
    for cp in sends:
        cp.start()


def _exchange_wait(*refs, slabs):
    if not slabs:
        local, first, landed, passed, from_sibling = _gather_copies(*refs)
        for arrived, onward in zip(landed, passed):
            arrived.wait_recv()
            onward.start()
        for cp in from_sibling:
            cp.wait_recv()
        for cp in first + passed:
            cp.wait_send()
        local.wait()
        return
    local, sends, recvs = _exchange_copies(*refs, slabs=slabs)
    for cp in recvs:
        cp.wait_recv()
    for cp in sends:
        cp.wait_send()
    local.wait()


def _exchange_sems():
    return [pltpu.SemaphoreType.DMA((N_DEV - 1,)), pltpu.SemaphoreType.DMA((N_DEV - 1,)), pltpu.SemaphoreType.DMA(())]


def _exchange_shape(src, slabs):
    return jax.ShapeDtypeStruct((N_DEV,) + tuple(src.shape[1:] if slabs else src.shape), src.dtype)


def _exchange(src, name, slabs):
    def body(*refs):
        _exchange_start(*refs, slabs=slabs)
        _exchange_wait(*refs, slabs=slabs)

    return pl.pallas_call(
        body, name=name,
        in_specs=[pl.BlockSpec(memory_space=pl.ANY)], out_specs=pl.BlockSpec(memory_space=pl.ANY),
        out_shape=_exchange_shape(src, slabs), scratch_shapes=_exchange_sems(),
    )(src)


MM_TM, MM_TN, MM_TK = 1024, 1024, 1024


def _matmul(a, b, mode, name, add=None, add_scale=1.0, carry=None):
    if mode == "nn":
        (m, k), (_, n) = a.shape, b.shape
    elif mode == "nt":
        (m, k), (n, _) = a.shape, b.shape
    else:
        (k, m), (_, n) = a.shape, b.shape
    tk = _tile(k, 2 * MM_TK)
    tm, tn = _tile(m, 2 * MM_TM if mode == "nn" and add is None and tk <= MM_TK else MM_TM), _tile(n, MM_TN)
    nk = k // tk
    grid = (m // tm, n // tn, nk)
    dims = {"nn": _NN, "nt": _NT, "tn": _TN}[mode]
    n_in = 2 + (add is not None)

    def body(*refs):
        a_ref, b_ref, o_ref = refs[0], refs[1], refs[n_in + (carry is not None)]
        if carry is not None:
            ex = (refs[n_in], refs[n_in + 2]) + tuple(refs[n_in + 3:])
            step = (pl.program_id(0) * grid[1] + pl.program_id(1)) * grid[2] + pl.program_id(2)

            @pl.when(step == 0)
            def _():
                _exchange_start(*ex, slabs=carry[1])

        part = _mm(a_ref[...], b_ref[...], dims)
        first = part if add is None else part + add_scale * refs[2][...]
        if nk == 1:
            o_ref[...] = first
        else:
            @pl.when(pl.program_id(2) == 0)
            def _():
                o_ref[...] = first

            @pl.when(pl.program_id(2) > 0)
            def _():
                o_ref[...] += part

        if carry is not None:
            @pl.when(step == grid[0] * grid[1] * grid[2] - 1)
            def _():
                _exchange_wait(*ex, slabs=carry[1])

    if mode == "nn":
        specs = [pl.BlockSpec((tm, tk), lambda i, j, q: (i, q)), pl.BlockSpec((tk, tn), lambda i, j, q: (q, j))]
    elif mode == "nt":
        specs = [pl.BlockSpec((tm, tk), lambda i, j, q: (i, q)), pl.BlockSpec((tn, tk), lambda i, j, q: (j, q))]
    else:
        specs = [pl.BlockSpec((tk, tm), lambda i, j, q: (q, i)), pl.BlockSpec((tk, tn), lambda i, j, q: (q, j))]
    out_spec = pl.BlockSpec((tm, tn), lambda i, j, q: (i, j))
    args = [a, b]
    if add is not None:
        specs.append(out_spec)
        args.append(add)
    out_shape = jax.ShapeDtypeStruct((m, n), F32)
    if carry is None:
        return _call(body, name, grid, specs, out_spec, out_shape, semantics=("parallel", "parallel", "arbitrary"))(*args)
    hbm = pl.BlockSpec(memory_space=pl.ANY)
    return _call(body, name, grid, specs + [hbm], [out_spec, hbm], [out_shape, _exchange_shape(*carry)],
                 _exchange_sems(), ("arbitrary", "arbitrary", "arbitrary"))(*args, carry[0])


CONV_TB = 512
CONV_CB = 1024
HALO = 8


def _conv_specs(t, cb_n, tb):
    nb = tb // HALO
    blk = pl.BlockSpec((tb, cb_n), lambda c, i: (i, c))
    prev = pl.BlockSpec((HALO, cb_n), lambda c, i: (jnp.maximum(i * nb - 1, 0), c))
    nxt = pl.BlockSpec((HALO, cb_n), lambda c, i: (jnp.minimum((i + 1) * nb, t // HALO - 1), c))
    w = pl.BlockSpec((8, cb_n), lambda c, i: (0, c))
    return blk, prev, nxt, w


def _shift_down(ext, s, tb):
    return (pltpu.roll(ext, s, 0) if s else ext)[HALO:HALO + tb]


def _shift_up(ext, s, tb):
    n = ext.shape[0]
    return (pltpu.roll(ext, n - s, 0) if s else ext)[0:tb]


def _conv_fwd(u, w8, ktaps, name, u2=None, bias=False):
    t, ch = u.shape
    cb_n = min(CONV_CB, ch)
    tb = min(2 * CONV_TB, t)
    two = u2 is not None

    def body(*refs):
        if two:
            u_ref, up_ref, v_ref, vp_ref, w_ref, o_ref = refs
        else:
            u_ref, up_ref, w_ref, o_ref = refs
        first = pl.program_id(1) == 0
        blk, halo = u_ref[...], up_ref[...]
        if two:
            blk, halo = blk * v_ref[...], halo * vp_ref[...]
        ext = jnp.concatenate([jnp.where(first, 0.0, halo), blk], axis=0)
        acc = jnp.zeros((tb, cb_n), F32)
        for j in range(ktaps):
            acc = acc + w_ref[j:j + 1, :] * _shift_down(ext, ktaps - 1 - j, tb)
        if bias:
            acc = acc + w_ref[ktaps:ktaps + 1, :]
        o_ref[...] = acc

    blk, prev, _, wspec = _conv_specs(t, cb_n, tb)
    specs, args = [blk, prev], [u, u]
    if two:
        specs += [blk, prev]
        args += [u2, u2]
    specs.append(wspec)
    args.append(w8)
    return _call(body, name, (ch // cb_n, t // tb), specs, blk, jax.ShapeDtypeStruct((t, ch), F32),
                 semantics=("parallel", "parallel"))(*args)


def _conv_bwd(dc, u, w8, ktaps, name, u2=None):
    t, ch = u.shape
    cb_n = min(CONV_CB, ch)
    two = u2 is not None
    tb = min(CONV_TB if two else 2 * CONV_TB, t)

    def body(*refs):
        if two:
            dc_ref, dn_ref, u_ref, v_ref, w_ref, du_ref, dv_ref, dw_ref = refs
        else:
            dc_ref, dn_ref, u_ref, w_ref, du_ref, dw_ref = refs
        i = pl.program_id(1)
        d = dc_ref[...]
        dext = jnp.concatenate([d, jnp.where(i == t // tb - 1, 0.0, dn_ref[...])], axis=0)
        blk = u_ref[...] * v_ref[...] if two else u_ref[...]
        du = jnp.zeros((tb, cb_n), F32)
        dw = jnp.zeros((8, cb_n), F32)
        for j in range(ktaps):
            ahead = _shift_up(dext, ktaps - 1 - j, tb)
            du = du + w_ref[j:j + 1, :] * ahead
            dw = _put_sub(dw, jnp.sum(ahead * blk, axis=0, keepdims=True), j)
        dw = _put_sub(dw, jnp.sum(d, axis=0, keepdims=True), ktaps)
        if two:
            du_ref[...] = (du * v_ref[...]).astype(du_ref.dtype)
            dv_ref[...] = (du * u_ref[...]).astype(dv_ref.dtype)
        else:
            du_ref[...] = du.astype(du_ref.dtype)

        @pl.when(i == 0)
        def _():
            dw_ref[...] = jnp.zeros_like(dw_ref)

        dw_ref[...] += dw

    blk, _, nxt, wspec = _conv_specs(t, cb_n, tb)
    specs, args = [blk, nxt, blk], [dc, dc, u]
    if two:
        specs.append(blk)
        args.append(u2)
    specs.append(wspec)
    args.append(w8)
    act = jax.ShapeDtypeStruct((t, ch), MM_DTYPE)
    outs = ([blk, blk, wspec], [act, act, jax.ShapeDtypeStruct((8, ch), F32)]) if two else \
        ([blk, wspec], [act, jax.ShapeDtypeStruct((8, ch), F32)])
    return _call(body, name, (ch // cb_n, t // tb), specs, outs[0], outs[1],
                 semantics=("parallel", "arbitrary"))(*args)


EW_TB = 256


def _chunk_mask(n, upper):
    i, j = _iota((n, n), 0), _iota((n, n), 1)
    same = jnp.right_shift(i, 6) == jnp.right_shift(j, 6)
    return (same & ((j >= i) if upper else (i >= j))).astype(F32)


def _rows(width, tb=EW_TB):
    return pl.BlockSpec((tb, width), lambda i: (i, 0))


def _const(rows, width):
    return pl.BlockSpec((rows, width), lambda i: (0, 0))


def _gdn_ew_fwd(c, ba, a_log, dt_bias, name):
    t = c.shape[0]
    tb = min(2 * EW_TB, t)

    def body(c_ref, ba_ref, al_ref, db_ref, q_ref, k_ref, v_ref, beta_ref, gc_ref):
        for h in range(GDN_QK_HEADS):
            for base, ref, scale in ((0, q_ref, HEAD ** -0.5), (GDN_QK_DIM, k_ref, 1.0)):
                s = _silu(c_ref[:, base + h * HEAD: base + (h + 1) * HEAD])
                r = lax.rsqrt(jnp.sum(s * s, axis=1, keepdims=True) + L2_EPS)
                ref[:, h * HEAD:(h + 1) * HEAD] = s * (r * scale)
        v_ref[...] = _silu(c_ref[:, 2 * GDN_QK_DIM:])
        beta_ref[...] = _sig(ba_ref[:, :HEAD])
        g = -jnp.exp(al_ref[...]) * _softplus(ba_ref[:, HEAD:] + db_ref[...])
        gc_ref[...] = _mmx(_chunk_mask(tb, False), g, _NN)

    act = lambda w: jax.ShapeDtypeStruct((t, w), F32)
    return _call(body, name, (t // tb,),
                 [_rows(GDN_CONV_DIM, tb), _rows(2 * HEAD, tb), _const(1, HEAD), _const(1, HEAD)],
                 [_rows(GDN_QK_DIM, tb), _rows(GDN_QK_DIM, tb), _rows(D_INNER, tb), _rows(HEAD, tb), _rows(HEAD, tb)],
                 [act(GDN_QK_DIM), act(GDN_QK_DIM), act(D_INNER), act(HEAD), act(HEAD)],
                 semantics=("parallel",))(c, ba, a_log, dt_bias)


def _gdn_ew_bwd(c, ba, a_log, dt_bias, dqh, dkh, dv, dbeta, dgc, name):
    t = c.shape[0]
    tb = EW_TB

    def body(c_ref, ba_ref, al_ref, db_ref, dq_ref, dk_ref, dv_ref, dbeta_ref, dgc_ref, dc_ref, dba_ref, acc_ref):
        for h in range(GDN_QK_HEADS):
            for base, ref, scale in ((0, dq_ref, HEAD ** -0.5), (GDN_QK_DIM, dk_ref, 1.0)):
                cq = c_ref[:, base + h * HEAD: base + (h + 1) * HEAD]
                s = _silu(cq)
                r = lax.rsqrt(jnp.sum(s * s, axis=1, keepdims=True) + L2_EPS)
                dn = ref[:, h * HEAD:(h + 1) * HEAD] * scale
                ds = r * dn - s * (r * r * r) * jnp.sum(dn * s, axis=1, keepdims=True)
                dc_ref[:, base + h * HEAD: base + (h + 1) * HEAD] = ds * _dsilu(cq)
        dc_ref[:, 2 * GDN_QK_DIM:] = dv_ref[...] * _dsilu(c_ref[:, 2 * GDN_QK_DIM:])
        beta = _sig(ba_ref[:, :HEAD])
        dba_ref[:, :HEAD] = (dbeta_ref[...] * beta * (1.0 - beta)).astype(dba_ref.dtype)
        pre = ba_ref[:, HEAD:] + db_ref[...]
        ea = jnp.exp(al_ref[...])
        g = -ea * _softplus(pre)
        dg = _mmx(_chunk_mask(tb, True), dgc_ref[...], _NN)
        da_raw = dg * (-ea) * _sig(pre)
        dba_ref[:, HEAD:] = da_raw.astype(dba_ref.dtype)
        acc = jnp.zeros((8, HEAD), F32)
        acc = _put_sub(acc, jnp.sum(dg * g, axis=0, keepdims=True), 0)
        acc = _put_sub(acc, jnp.sum(da_raw, axis=0, keepdims=True), 1)

        @pl.when(pl.program_id(0) == 0)
        def _():
            acc_ref[...] = jnp.zeros_like(acc_ref)

        acc_ref[...] += acc

    act = lambda w: jax.ShapeDtypeStruct((t, w), F32)
    return _call(body, name, (t // tb,),
                 [_rows(GDN_CONV_DIM), _rows(2 * HEAD), _const(1, HEAD), _const(1, HEAD),
                  _rows(GDN_QK_DIM), _rows(GDN_QK_DIM), _rows(D_INNER), _rows(HEAD), _rows(HEAD)],
                 [_rows(GDN_CONV_DIM), _rows(2 * HEAD), _const(8, HEAD)],
                 [act(GDN_CONV_DIM), jax.ShapeDtypeStruct((t, 2 * HEAD), MM_DTYPE), jax.ShapeDtypeStruct((8, HEAD), F32)],
                 semantics=("arbitrary",))(c, ba, a_log, dt_bias, dqh, dkh, dv, dbeta, dgc)


def _zip(fn, *lists):
    return [fn(*xs) for xs in zip(*lists)]


def _mms(xs, ys, dims):
    return [_mm(x, y, dims) for x, y in zip(xs, ys)]


def _side_by_side(a, b):
    return jnp.concatenate([a, b], axis=1)


def _interleave(*gens):
    results, live = [None] * len(gens), list(range(len(gens)))
    while live:
        for i in list(live):
            try:
                next(gens[i])
            except StopIteration as stop:
                results[i] = stop.value
                live.remove(i)
    return results


def _gdn_local_stages(q, k, v, bcol, gcol, grow, glast):
    ii, jj = _iota((CHUNK, CHUNK), 0), _iota((CHUNK, CHUNK), 1)
    eye = _eye(CHUNK)
    mul = lambda x, y: x * y
    eg = [jnp.exp(g) for g in gcol]
    decay = _zip(lambda gc, gr: jnp.exp(jnp.where(ii >= jj, gc - gr, NEG_BIG)), gcol, grow)
    kb = _zip(mul, k, bcol)
    p, qk = _mms(kb, k, _NT), _mms(q, k, _NT)
    yield
    a = _zip(lambda x, d: jnp.where(ii > jj, x * d, 0.0), p, decay)
    inv, pw = [eye - x for x in a], a
    for _ in range(5):
        pw = _mms(pw, pw, _NN)
        yield
        inv = _zip(lambda x, y: x + y, inv, _mms(inv, pw, _NN))
        yield
    rv, rk = _zip(mul, v, bcol), _zip(mul, kb, eg)
    uw = _mms(inv, _zip(_side_by_side, rv, rk), _NN)
    u, w = [x[:, :HEAD] for x in uw], [x[:, HEAD:] for x in uw]
    yield
    att = _zip(mul, qk, decay)
    qd = _zip(mul, q, eg)
    ekt = _zip(lambda gl, gc: jnp.exp(gl - gc), glast, gcol)
    kt = _zip(mul, k, ekt)
    el = [jnp.exp(g) for g in glast]
    return dict(eg=eg, decay=decay, kb=kb, p=p, inv=inv, rv=rv, rk=rk, u=u, w=w, qk=qk, att=att, qd=qd, ekt=ekt, kt=kt,
                el=el)


def _gdn_state_stages(u, w, att, qd, kt, el, s_in):
    ws, qs = _mms(w, s_in, _NN), _mms(qd, s_in, _NN)
    yield
    vn = _zip(lambda x, y: x - y, u, ws)
    av, kv = _mms(att, vn, _NN), _mms(kt, vn, _TN)
    yield
    out = _zip(lambda x, y: x + y, qs, av)
    s_out = _zip(lambda s, e, y: s * e + y, s_in, el, kv)
    return dict(vn=vn, out=out, s_out=s_out)


def _gdn_heads_fwd(q, k, v, bcol, gcol, grow, glast, s_in):
    f, = _interleave(_gdn_local_stages(q, k, v, bcol, gcol, grow, glast))
    g, = _interleave(_gdn_state_stages(f["u"], f["w"], f["att"], f["qd"], f["kt"], f["el"], s_in))
    return {**f, **g}


def _head_groups(group, init):
    if GDN_GROUP == GDN_V_HEADS:
        return group(0, init)
    return lax.fori_loop(0, GDN_V_HEADS // GDN_GROUP, lambda gi, c: group(GDN_GROUP * gi, c), init)


def _half(h):
    return h // 2 if isinstance(h, int) else jnp.right_shift(h, 1)


def _gdn_chunk_fwd(qn, kn, v, beta, gc, z, norm_w, name):
    t = qn.shape[0]
    nc = t // CHUNK

    def body(q_ref, k_ref, v_ref, beta_ref, gc_ref, z_ref, nw_ref, o_ref, h_ref, st_ref, state):
        @pl.when(pl.program_id(0) == 0)
        def _():
            state[...] = jnp.zeros_like(state)

        st_ref[0] = state[...]
        gc_all, beta_all = gc_ref[...], beta_ref[...]
        gct = _mmx(_eye(HEAD), gc_all, _NT)
        glast_all = gc_ref[CHUNK - 1:CHUNK, :]
        nw = nw_ref[...]

        def group(h0, carry):
            heads = [h0 + s for s in range(GDN_GROUP)]
            f = _gdn_heads_fwd([q_ref[:, _lanes(_half(h))] for h in heads], [k_ref[:, _lanes(_half(h))] for h in heads],
                               [v_ref[:, _lanes(h)] for h in heads], [_col(beta_all, h) for h in heads],
                               [_col(gc_all, h) for h in heads], [_row(gct, h) for h in heads],
                               [_col(glast_all, h) for h in heads], [state[h] for h in heads])
            for h, s_out, o in zip(heads, f["s_out"], f["out"]):
                state[h] = s_out
                o_ref[:, _lanes(h)] = o
                rstd = lax.rsqrt(jnp.mean(o * o, axis=1, keepdims=True) + RMS_EPS)
                h_ref[:, _lanes(h)] = (o * rstd * nw * _silu(z_ref[:, _lanes(h)])).astype(h_ref.dtype)
            return carry

        _head_groups(group, 0)

    rows = lambda w: pl.BlockSpec((CHUNK, w), lambda i: (i, 0))
    act = lambda w: jax.ShapeDtypeStruct((t, w), F32)
    return _call(body, name, (nc,),
                 [rows(GDN_QK_DIM), rows(GDN_QK_DIM), rows(D_INNER), rows(HEAD), rows(HEAD), rows(D_INNER),
                  _const(1, HEAD)],
                 [rows(D_INNER), rows(D_INNER), pl.BlockSpec((1, GDN_V_HEADS, HEAD, HEAD), lambda i: (i, 0, 0, 0))],
                 [act(D_INNER), jax.ShapeDtypeStruct((t, D_INNER), MM_DTYPE),
                  jax.ShapeDtypeStruct((nc, GDN_V_HEADS, HEAD, HEAD), F32)],
                 [pltpu.VMEM((GDN_V_HEADS, HEAD, HEAD), F32)], ("arbitrary",))(qn, kn, v, beta, gc, z, norm_w)


def _gdn_chunk_bwd(qn, kn, v, beta, gc, z, norm_w, o, states, dh, name):
    t = qn.shape[0]
    nc = t // CHUNK

    def body(q_ref, k_ref, v_ref, beta_ref, gc_ref, z_ref, nw_ref, o_ref, st_ref, dh_ref,
             dq_ref, dk_ref, dv_ref, dz_ref, dbeta_ref, dgc_ref, acc_ref, dstate):
        @pl.when(pl.program_id(0) == 0)
        def _():
            dstate[...] = jnp.zeros_like(dstate)
            acc_ref[...] = jnp.zeros_like(acc_ref)

        gc_all, beta_all = gc_ref[...], beta_ref[...]
        gct = _mmx(_eye(HEAD), gc_all, _NT)
        glast_all = gc_ref[CHUNK - 1:CHUNK, :]
        nw = nw_ref[...]
        ii, jj = _iota((CHUNK, CHUNK), 0), _iota((CHUNK, CHUNK), 1)
        last_row = _iota((CHUNK, 1), 0) == CHUNK - 1

        def group(h0, carry):
            dbeta_acc, dgc_acc, dgrow_acc, dnw_acc = carry
            heads = [h0 + s for s in range(GDN_GROUP)]
            mul, add, sub = (lambda x, y: x * y), (lambda x, y: x + y), (lambda x, y: x - y)
            rowsum = lambda x, y: jnp.sum(x * y, axis=1, keepdims=True)
            q, k = [q_ref[:, _lanes(_half(h))] for h in heads], [k_ref[:, _lanes(_half(h))] for h in heads]
            vv = [v_ref[:, _lanes(h)] for h in heads]
            bcol, gcol = [_col(beta_all, h) for h in heads], [_col(gc_all, h) for h in heads]
            s_in, dsn = [st_ref[0, h] for h in heads], [dstate[h] for h in heads]
            f = _gdn_heads_fwd(q, k, vv, bcol, gcol, [_row(gct, h) for h in heads],
                               [_col(glast_all, h) for h in heads], s_in)
            do = []
            for h in heads:
                oo, zz, dhh = o_ref[:, _lanes(h)], z_ref[:, _lanes(h)], dh_ref[:, _lanes(h)]
                rstd = lax.rsqrt(jnp.mean(oo * oo, axis=1, keepdims=True) + RMS_EPS)
                on, sz = oo * rstd, _silu(zz)
                dnw_acc = dnw_acc + jnp.sum(dhh * on * sz, axis=0, keepdims=True)
                dz_ref[:, _lanes(h)] = (dhh * on * nw * _dsilu(zz)).astype(dz_ref.dtype)
                don = dhh * nw * sz
                do.append(rstd * (don - on * jnp.mean(don * on, axis=1, keepdims=True)))
            decay, eg, inv = f["decay"], f["eg"], f["inv"]
            d_glast = _zip(lambda d, s, e: _total(d * s) * e, dsn, s_in, f["el"])
            dkt = _mms(f["vn"], dsn, _NT)
            dvn = _mms(f["kt"], dsn, _NN)
            dqd = _mms(do, s_in, _NT)
            ds_prev = _zip(lambda d, e, y: d * e + y, dsn, f["el"], _mms(f["qd"], do, _TN))
            datt = _mms(do, f["vn"], _NT)
            dvn = _zip(add, dvn, _mms(f["att"], do, _TN))
            dqk = _zip(mul, datt, decay)
            dq = _zip(lambda x, e, y: x * e + y, dqd, eg, _mms(dqk, k, _NN))
            dk = _mms(dqk, q, _TN)
            ddecay = _zip(mul, datt, f["qk"])
            dgcol = _zip(rowsum, dqd, f["qd"])
            dw = [-x for x in _mms(dvn, s_in, _NT)]
            ds_prev = _zip(sub, ds_prev, _mms(f["w"], dvn, _TN))
            drv, drk = _mms(inv, dvn, _TN), _mms(inv, dw, _TN)
            da = _zip(lambda x, y: jnp.where(ii > jj, -(x + y), 0.0), _mms(drv, f["u"], _NT), _mms(drk, f["w"], _NT))
            dp = _zip(mul, da, decay)
            ddecay = _zip(lambda x, y, z_: x + y * z_, ddecay, da, f["p"])
            dkb = _zip(lambda x, y, e: x + y * e, _mms(dp, k, _NN), drk, eg)
            dk = _zip(add, dk, _mms(dp, f["kb"], _TN))
            dbeta = _zip(add, _zip(rowsum, drv, vv), _zip(rowsum, dkb, k))
            dgcol = _zip(add, dgcol, _zip(rowsum, drk, f["rk"]))
            dk = _zip(lambda x, y, b_, z_, e: x + y * b_ + z_ * e, dk, dkb, bcol, dkt, f["ekt"])
            tail = _zip(mul, dkt, f["kt"])
            d_glast = _zip(lambda x, y: x + _total(y), d_glast, tail)
            e_ = _zip(mul, ddecay, decay)
            dgcol = _zip(lambda x, t_, e, gl: x - jnp.sum(t_, axis=1, keepdims=True) + jnp.sum(e, axis=1, keepdims=True)
                         + jnp.where(last_row, gl, 0.0), dgcol, tail, e_, d_glast)
            for i_ in range(0, len(heads), 2):
                dq_ref[:, _lanes(_half(heads[i_]))] = dq[i_] + dq[i_ + 1]
                dk_ref[:, _lanes(_half(heads[i_]))] = dk[i_] + dk[i_ + 1]
            for i_, h in enumerate(heads):
                dstate[h] = ds_prev[i_]
                dv_ref[:, _lanes(h)] = drv[i_] * bcol[i_]
                dbeta_acc = _put_col(dbeta_acc, dbeta[i_], h)
                dgc_acc = _put_col(dgc_acc, dgcol[i_], h)
                dgrow_acc = _put_row(dgrow_acc, -jnp.sum(e_[i_], axis=0, keepdims=True), h)
            return dbeta_acc, dgc_acc, dgrow_acc, dnw_acc

        zero = jnp.zeros((CHUNK, HEAD), F32)
        dbeta_acc, dgc_acc, dgrow_acc, dnw_acc = _head_groups(
            group, (zero, zero, jnp.zeros((HEAD, CHUNK), F32), jnp.zeros((1, HEAD), F32)))
        dbeta_ref[...] = dbeta_acc
        dgc_ref[...] = dgc_acc + _mmx(_eye(CHUNK), dgrow_acc, _NT)
        acc_ref[...] += _put_sub(jnp.zeros((8, HEAD), F32), dnw_acc, 0)

    rows = lambda w: pl.BlockSpec((CHUNK, w), lambda i: (nc - 1 - i, 0))
    act = lambda w: jax.ShapeDtypeStruct((t, w), F32)
    return _call(body, name, (nc,),
                 [rows(GDN_QK_DIM), rows(GDN_QK_DIM), rows(D_INNER), rows(HEAD), rows(HEAD), rows(D_INNER),
                  _const(1, HEAD), rows(D_INNER),
                  pl.BlockSpec((1, GDN_V_HEADS, HEAD, HEAD), lambda i: (nc - 1 - i, 0, 0, 0)), rows(D_INNER)],
                 [rows(GDN_QK_DIM), rows(GDN_QK_DIM), rows(D_INNER), rows(D_INNER), rows(HEAD), rows(HEAD), _const(8, HEAD)],
                 [act(GDN_QK_DIM), act(GDN_QK_DIM), act(D_INNER), jax.ShapeDtypeStruct((t, D_INNER), MM_DTYPE), act(HEAD),
                  act(HEAD), jax.ShapeDtypeStruct((8, HEAD), F32)],
                 [pltpu.VMEM((GDN_V_HEADS, HEAD, HEAD), F32)], ("arbitrary",)
                 )(qn, kn, v, beta, gc, z, norm_w, o, states, dh)


def _sc_gate_fwd(bg, cv, z, name):
    t, w = bg.shape

    def body(b_ref, c_ref, z_ref, o_ref):
        o_ref[...] = (b_ref[...] * c_ref[...] * _silu(z_ref[...])).astype(o_ref.dtype)

    tb = min(2 * EW_TB, t)
    return _call(body, name, (t // tb,), [_rows(w, tb)] * 3, _rows(w, tb), jax.ShapeDtypeStruct((t, w), MM_DTYPE),
                 semantics=("parallel",))(bg, cv, z)


def _sc_gate_bwd(dh, bg, cv, z, name):
    t, w = bg.shape

    def body(d_ref, b_ref, c_ref, z_ref, db_ref, dc_ref, dz_ref):
        d, b, c, zz = d_ref[...], b_ref[...], c_ref[...], z_ref[...]
        sz = _silu(zz)
        db_ref[...] = (d * c * sz).astype(db_ref.dtype)
        dc_ref[...] = d * b * sz
        dz_ref[...] = (d * b * c * _dsilu(zz)).astype(dz_ref.dtype)

    act, act_mm = jax.ShapeDtypeStruct((t, w), F32), jax.ShapeDtypeStruct((t, w), MM_DTYPE)
    return _call(body, name, (t // EW_TB,), [_rows(w)] * 4, [_rows(w)] * 3, [act_mm, act, act_mm],
                 semantics=("parallel",))(dh, bg, cv, z)


XBC_B = D_INNER
XBC_C = D_INNER + SSD_GROUPS * SSD_STATE


def _ssd_scalars(dtp, dt_bias, a_log):
    dt = _softplus(dtp + dt_bias)
    a = -jnp.exp(a_log)
    da = dt * a
    ac = _mmx(_chunk_mask(CHUNK, False), da, _NN)
    act = _mmx(_eye(HEAD), ac, _NT)
    aclast = jnp.sum(jnp.where(_iota(ac.shape, 0) == CHUNK - 1, ac, 0.0), axis=0, keepdims=True)
    return dt, a, da, ac, act, aclast


def _ssd_pairs_fwd(x2, bg, cg, cb, dt, ac, act, aclast, s2):
    ii, jj = _iota((CHUNK, CHUNK), 0), _iota((CHUNK, CHUNK), 1)
    half = _iota((CHUNK, HEAD), 1) < 64
    causal = ii >= jj
    pairs = range(len(x2))
    mul = lambda x, y: x * y
    pick = lambda a, b: jnp.where(half, a, b)
    aca, acb = [_col(ac, 2 * p) for p in pairs], [_col(ac, 2 * p + 1) for p in pairs]
    la, lb = [_col(aclast, 2 * p) for p in pairs], [_col(aclast, 2 * p + 1) for p in pairs]
    dt2 = [pick(_col(dt, 2 * p), _col(dt, 2 * p + 1)) for p in pairs]
    xdt = _zip(mul, x2, dt2)
    sega = [jnp.exp(jnp.where(causal, aca[p] - _row(act, 2 * p), NEG_BIG)) for p in pairs]
    segb = [jnp.exp(jnp.where(causal, acb[p] - _row(act, 2 * p + 1), NEG_BIG)) for p in pairs]
    ma, mb = _zip(mul, sega, cb), _zip(mul, segb, cb)
    ydiag = _zip(pick, _mms(ma, xdt, _NN), _mms(mb, xdt, _NN))
    cdec = _zip(lambda a, b: pick(jnp.exp(a), jnp.exp(b)), aca, acb)
    cs = _mms(cg, s2, _NT)
    tail = _zip(lambda l1, a, l2, b: pick(jnp.exp(l1 - a), jnp.exp(l2 - b)), la, aca, lb, acb)
    zt = _zip(mul, xdt, tail)
    ea, eb = [jnp.exp(x) for x in la], [jnp.exp(x) for x in lb]
    tot = _zip(lambda a, b: jnp.where(_iota((HEAD, 1), 0) < 64, a, b), ea, eb)
    s_out = _zip(lambda s, t_, y: s * t_ + y, s2, tot, _mms(zt, bg, _TN))
    return dict(half=half, dt2=dt2, xdt=xdt, sega=sega, segb=segb, ma=ma, mb=mb, ydiag=ydiag, cdec=cdec, cs=cs,
                tail=tail, zt=zt, ea=ea, eb=eb, tot=tot, s_out=s_out)


def _ssd_group_inputs(cx_ref):
    cxb = [cx_ref[:, XBC_B + g * SSD_STATE: XBC_B + (g + 1) * SSD_STATE] for g in range(SSD_GROUPS)]
    cxc = [cx_ref[:, XBC_C + g * SSD_STATE: XBC_C + (g + 1) * SSD_STATE] for g in range(SSD_GROUPS)]
    bg, cg = [_silu(x) for x in cxb], [_silu(x) for x in cxc]
    return cxb, cxc, bg, cg, _mms(cg, bg, _NT)


def _per_pair(group_list):
    return [group_list[p // (SSD_PAIRS // SSD_GROUPS)] for p in range(SSD_PAIRS)]


def _ssd_chunk_fwd(cx, dtp, z, dt_bias, a_log, dskip, norm_w, name):
    t = cx.shape[0]
    nc = t // CHUNK
    gw = D_INNER // SSD_GROUPS

    def body(cx_ref, dtp_ref, z_ref, db_ref, al_ref, sk_ref, nw_ref, y_ref, h_ref, st_ref, state):
        @pl.when(pl.program_id(0) == 0)
        def _():
            state[...] = jnp.zeros_like(state)

        st_ref[0] = state[...]
        dt, _, _, ac, act, aclast = _ssd_scalars(dtp_ref[...], db_ref[...], al_ref[...])
        _, _, bg, cg, cb = _ssd_group_inputs(cx_ref)
        x2 = [_silu(cx_ref[:, _lanes(p)]) for p in range(SSD_PAIRS)]
        f = _ssd_pairs_fwd(x2, _per_pair(bg), _per_pair(cg), _per_pair(cb), dt, ac, act, aclast,
                           [state[p] for p in range(SSD_PAIRS)])
        for p in range(SSD_PAIRS):
            state[p] = f["s_out"][p]
            y_ref[:, _lanes(p)] = f["ydiag"][p] + f["cs"][p] * f["cdec"][p] + sk_ref[:, _lanes(p)] * x2[p]
        for g in range(SSD_GROUPS):
            sl = slice(g * gw, (g + 1) * gw)
            yg = y_ref[:, sl] * _silu(z_ref[:, sl])
            rstd = lax.rsqrt(jnp.mean(yg * yg, axis=1, keepdims=True) + RMS_EPS)
            h_ref[:, sl] = (yg * rstd * nw_ref[:, sl]).astype(h_ref.dtype)

    rows = lambda w: pl.BlockSpec((CHUNK, w), lambda i: (i, 0))
    act_ = lambda w: jax.ShapeDtypeStruct((t, w), F32)
    return _call(body, name, (nc,),
                 [rows(SSD_CONV_DIM), rows(HEAD), rows(D_INNER), _const(1, HEAD), _const(1, HEAD),
                  _const(1, D_INNER), _const(1, D_INNER)],
                 [rows(D_INNER), rows(D_INNER), pl.BlockSpec((1, SSD_PAIRS, HEAD, SSD_STATE), lambda i: (i, 0, 0, 0))],
                 [act_(D_INNER), jax.ShapeDtypeStruct((t, D_INNER), MM_DTYPE),
                  jax.ShapeDtypeStruct((nc, SSD_PAIRS, HEAD, SSD_STATE), F32)],
                 [pltpu.VMEM((SSD_PAIRS, HEAD, SSD_STATE), F32)], ("arbitrary",)
                 )(cx, dtp, z, dt_bias, a_log, dskip, norm_w)


def _ssd_chunk_bwd(cx, dtp, z, dt_bias, a_log, dskip, norm_w, y, states, dh, name):
    t = cx.shape[0]
    nc = t // CHUNK
    gw = D_INNER // SSD_GROUPS

    def body(cx_ref, dtp_ref, z_ref, db_ref, al_ref, sk_ref, nw_ref, y_ref, st_ref, dh_ref,
             dcx_ref, ddtp_ref, dz_ref, wide_ref, acc_ref, dstate, dy_s):
        @pl.when(pl.program_id(0) == 0)
        def _():
            dstate[...] = jnp.zeros_like(dstate)
            wide_ref[...] = jnp.zeros_like(wide_ref)
            acc_ref[...] = jnp.zeros_like(acc_ref)

        dtp = dtp_ref[...]
        dt, a, da, ac, act, aclast = _ssd_scalars(dtp, db_ref[...], al_ref[...])
        ii, jj = _iota((CHUNK, CHUNK), 0), _iota((CHUNK, CHUNK), 1)
        last_row = _iota((CHUNK, 1), 0) == CHUNK - 1
        for g in range(SSD_GROUPS):
            sl = slice(g * gw, (g + 1) * gw)
            yy, zz, dhh, nw = y_ref[:, sl], z_ref[:, sl], dh_ref[:, sl], nw_ref[:, sl]
            sz = _silu(zz)
            yg = yy * sz
            rstd = lax.rsqrt(jnp.mean(yg * yg, axis=1, keepdims=True) + RMS_EPS)
            n = yg * rstd
            dn = dhh * nw
            dyg = rstd * (dn - n * jnp.mean(dn * n, axis=1, keepdims=True))
            dy_s[:, sl] = dyg * sz
            dz_ref[:, sl] = (dyg * yy * _dsilu(zz)).astype(dz_ref.dtype)
            wide_ref[0:1, sl] += jnp.sum(dhh * n, axis=0, keepdims=True)

        pairs = range(SSD_PAIRS)
        mul, add, sub = (lambda x, y: x * y), (lambda x, y: x + y), (lambda x, y: x - y)
        rowsum = lambda x: jnp.sum(x, axis=1, keepdims=True)
        cxb, cxc, bg, cg, cb = _ssd_group_inputs(cx_ref)
        bgp, cgp = _per_pair(bg), _per_pair(cg)
        cxx = [cx_ref[:, _lanes(p)] for p in pairs]
        x2 = [_silu(x) for x in cxx]
        s2, dsn = [st_ref[0, p] for p in pairs], [dstate[p] for p in pairs]
        dy2 = [dy_s[:, _lanes(p)] for p in pairs]
        f = _ssd_pairs_fwd(x2, bgp, cgp, _per_pair(cb), dt, ac, act, aclast, s2)
        half = f["half"]
        lo = lambda x: jnp.where(half, x, 0.0)
        dx2 = [dy2[p] * sk_ref[:, _lanes(p)] for p in pairs]
        for p in pairs:
            wide_ref[1:2, _lanes(p)] += jnp.sum(dy2[p] * x2[p], axis=0, keepdims=True)
        gg = _zip(mul, dy2, f["cdec"])
        dc_p = _mms(gg, s2, _NN)
        ds_prev = _zip(lambda d, t_, y: d * t_ + y, dsn, f["tot"], _mms(gg, cgp, _TN))
        t1 = _zip(lambda d, c, e: d * c * e, dy2, f["cs"], f["cdec"])
        dac_a = [rowsum(lo(x)) for x in t1]
        dac_b = _zip(lambda x, a_: rowsum(x) - a_, t1, dac_a)
        dya = [lo(x) for x in dy2]
        dma, dmb = _mms(dya, f["xdt"], _NT), _mms(_zip(sub, dy2, dya), f["xdt"], _NT)
        dxdt = _zip(lambda a_, b_: jnp.where(half, a_, b_), _mms(f["ma"], dy2, _TN), _mms(f["mb"], dy2, _TN))
        dcb_p = _zip(lambda a_, sa, b_, sb: a_ * sa + b_ * sb, dma, f["sega"], dmb, f["segb"])
        ea_, eb_ = _zip(mul, dma, f["ma"]), _zip(mul, dmb, f["mb"])
        dac_a = _zip(lambda x, e: x + rowsum(e), dac_a, ea_)
        dac_b = _zip(lambda x, e: x + rowsum(e), dac_b, eb_)
        dzt = _mms(bgp, dsn, _NT)
        db_p = _mms(f["zt"], dsn, _NN)
        dxdt = _zip(lambda x, d, t_: x + d * t_, dxdt, dzt, f["tail"])
        t2 = _zip(mul, dzt, f["zt"])
        t2a = [rowsum(lo(x)) for x in t2]
        t2b = _zip(lambda x, a_: rowsum(x) - a_, t2, t2a)
        t3 = _zip(mul, dsn, s2)
        t3a = [_total(x[:64]) for x in t3]
        dla = _zip(lambda x, y, e: _total(x) + y * e, t2a, t3a, f["ea"])
        dlb = _zip(lambda x, y, e: _total(x) + _total(y[64:]) * e, t2b, t3, f["eb"])
        dac_a = _zip(lambda x, y, l: x - y + jnp.where(last_row, l, 0.0), dac_a, t2a, dla)
        dac_b = _zip(lambda x, y, l: x - y + jnp.where(last_row, l, 0.0), dac_b, t2b, dlb)
        dx2 = _zip(lambda x, d, t_: x + d * t_, dx2, dxdt, f["dt2"])
        t4 = _zip(mul, dxdt, x2)
        t4a = [rowsum(lo(x)) for x in t4]
        t4b = _zip(lambda x, a_: rowsum(x) - a_, t4, t4a)
        zero = jnp.zeros((CHUNK, HEAD), F32)
        ddt_acc, dac_acc, drow_acc = zero, zero, jnp.zeros((HEAD, CHUNK), F32)
        for p in pairs:
            dcx_ref[:, _lanes(p)] = dx2[p] * _dsilu(cxx[p])
            dstate[p] = ds_prev[p]
            ddt_acc = _put_col(_put_col(ddt_acc, t4a[p], 2 * p), t4b[p], 2 * p + 1)
            dac_acc = _put_col(_put_col(dac_acc, dac_a[p], 2 * p), dac_b[p], 2 * p + 1)
            drow_acc = _put_row(_put_row(drow_acc, -jnp.sum(ea_[p], axis=0, keepdims=True), 2 * p),
                                -jnp.sum(eb_[p], axis=0, keepdims=True), 2 * p + 1)
        per = SSD_PAIRS // SSD_GROUPS
        gsum = lambda xs: [functools.reduce(add, xs[g * per:(g + 1) * per]) for g in range(SSD_GROUPS)]
        dcb = gsum(dcb_p)
        dc = _zip(add, gsum(dc_p), _mms(dcb, bg, _NN))
        db = _zip(add, gsum(db_p), _mms(dcb, cg, _TN))
        for g in range(SSD_GROUPS):
            dcx_ref[:, XBC_B + g * SSD_STATE: XBC_B + (g + 1) * SSD_STATE] = db[g] * _dsilu(cxb[g])
            dcx_ref[:, XBC_C + g * SSD_STATE: XBC_C + (g + 1) * SSD_STATE] = dc[g] * _dsilu(cxc[g])
        dac = dac_acc + _mmx(_eye(CHUNK), drow_acc, _NT)
        dda = _mmx(_chunk_mask(CHUNK, True), dac, _NN)
        ddt = ddt_acc + dda * a
        ddtp = ddt * _sig(dtp + db_ref[...])
        ddtp_ref[...] = ddtp.astype(ddtp_ref.dtype)
        acc = _put_sub(jnp.zeros((8, HEAD), F32), jnp.sum(dda * da, axis=0, keepdims=True), 0)
        acc_ref[...] += _put_sub(acc, jnp.sum(ddtp, axis=0, keepdims=True), 1)

    rows = lambda w: pl.BlockSpec((CHUNK, w), lambda i: (nc - 1 - i, 0))
    act_ = lambda w: jax.ShapeDtypeStruct((t, w), F32)
    return _call(body, name, (nc,),
                 [rows(SSD_CONV_DIM), rows(HEAD), rows(D_INNER), _const(1, HEAD), _const(1, HEAD),
                  _const(1, D_INNER), _const(1, D_INNER), rows(D_INNER),
                  pl.BlockSpec((1, SSD_PAIRS, HEAD, SSD_STATE), lambda i: (nc - 1 - i, 0, 0, 0)), rows(D_INNER)],
                 [rows(SSD_CONV_DIM), rows(HEAD), rows(D_INNER), _const(8, D_INNER), _const(8, HEAD)],
                 [act_(SSD_CONV_DIM), jax.ShapeDtypeStruct((t, HEAD), MM_DTYPE), jax.ShapeDtypeStruct((t, D_INNER), MM_DTYPE),
                  jax.ShapeDtypeStruct((8, D_INNER), F32),
                  jax.ShapeDtypeStruct((8, HEAD), F32)],
                 [pltpu.VMEM((SSD_PAIRS, HEAD, SSD_STATE), F32), pltpu.VMEM((CHUNK, D_INNER), F32)], ("arbitrary",)
                 )(cx, dtp, z, dt_bias, a_log, dskip, norm_w, y, states, dh)


LN_TB = 512


def _ln_stats(x, y):
    u = ALPHA * x + y
    mu = jnp.mean(u, axis=1, keepdims=True)
    cen = u - mu
    rstd = lax.rsqrt(jnp.mean(cen * cen, axis=1, keepdims=True) + LN_EPS)
    return cen * rstd


def _ln_fwd(x, y, g, b, name):
    t, d = x.shape

    def body(x_ref, y_ref, g_ref, b_ref, o_ref, omm_ref):
        out = _ln_stats(x_ref[...], y_ref[...]) * g_ref[...] + b_ref[...]
        o_ref[...] = out
        omm_ref[...] = out.astype(omm_ref.dtype)

    tb = min(2 * LN_TB, t)
    return _call(body, name, (t // tb,), [_rows(d, tb), _rows(d, tb), _const(1, d), _const(1, d)],
                 [_rows(d, tb)] * 2, [jax.ShapeDtypeStruct((t, d), F32), jax.ShapeDtypeStruct((t, d), MM_DTYPE)],
                 semantics=("parallel",))(x, y, g, b)


def _ln_loss(x, y, g, b, target, name):
    t, d = x.shape

    def body(x_ref, y_ref, g_ref, b_ref, t_ref, d_ref, acc_ref):
        err = _ln_stats(x_ref[...], y_ref[...]) * g_ref[...] + b_ref[...] - t_ref[...]
        d_ref[...] = err * (1.0 / d)

        @pl.when(pl.program_id(0) == 0)
        def _():
            acc_ref[...] = jnp.zeros_like(acc_ref)

        acc_ref[...] += _put_sub(jnp.zeros((8, d), F32), jnp.sum(err * err, axis=0, keepdims=True), 0)

    return _call(body, name, (t // LN_TB,),
                 [_rows(d, LN_TB), _rows(d, LN_TB), _const(1, d), _const(1, d), _rows(d, LN_TB)],
                 [_rows(d, LN_TB), _const(8, d)],
                 [jax.ShapeDtypeStruct((t, d), F32), jax.ShapeDtypeStruct((8, d), F32)],
                 semantics=("arbitrary",))(x, y, g, b, target)


def _ln_bwd(dout, x, y, g, name):
    t, d = x.shape

    def body(d_ref, x_ref, y_ref, g_ref, du_ref, dumm_ref, acc_ref):
        u = ALPHA * x_ref[...] + y_ref[...]
        mu = jnp.mean(u, axis=1, keepdims=True)
        cen = u - mu
        rstd = lax.rsqrt(jnp.mean(cen * cen, axis=1, keepdims=True) + LN_EPS)
        xh = cen * rstd
        do = d_ref[...]
        dxh = do * g_ref[...]
        du = rstd * (dxh - jnp.mean(dxh, axis=1, keepdims=True) - xh * jnp.mean(dxh * xh, axis=1, keepdims=True))
        du_ref[...] = du
        dumm_ref[...] = du.astype(dumm_ref.dtype)
        acc = _put_sub(jnp.zeros((8, d), F32), jnp.sum(do * xh, axis=0, keepdims=True), 0)
        acc = _put_sub(acc, jnp.sum(do, axis=0, keepdims=True), 1)

        @pl.when(pl.program_id(0) == 0)
        def _():
            acc_ref[...] = jnp.zeros_like(acc_ref)

        acc_ref[...] += acc

    return _call(body, name, (t // LN_TB,), [_rows(d, LN_TB)] * 3 + [_const(1, d)],
                 [_rows(d, LN_TB), _rows(d, LN_TB), _const(8, d)],
                 [jax.ShapeDtypeStruct((t, d), F32), jax.ShapeDtypeStruct((t, d), MM_DTYPE),
                  jax.ShapeDtypeStruct((8, d), F32)],
                 semantics=("arbitrary",))(dout, x, y, g)


def _adamw(w, gslots, m, v, name):
    r, c = w.shape
    rb = _tile_rows(r)
    c1 = 1.0 - ADAM_B1 ** ADAM_STEP
    c2 = 1.0 - ADAM_B2 ** ADAM_STEP

    def body(w_ref, g_ref, m_ref, v_ref, go_ref, d_ref, mo_ref, vo_ref):
        g = g_ref[0].astype(F32)
        for s in range(1, N_DEV):
            g = g + g_ref[s].astype(F32)
        mn = ADAM_B1 * m_ref[...] + (1.0 - ADAM_B1) * g
        vn = ADAM_B2 * v_ref[...] + (1.0 - ADAM_B2) * (g * g)
        go_ref[...] = g
        mo_ref[...] = mn
        vo_ref[...] = vn
        d_ref[...] = -ADAM_LR * ((mn / c1) / (jnp.sqrt(vn / c2) + ADAM_EPS) + ADAM_WD * w_ref[...])

    blk = pl.BlockSpec((rb, c), lambda i: (i, 0))
    sds = jax.ShapeDtypeStruct((r, c), F32)
    return _call(body, name, (r // rb,), [blk, pl.BlockSpec((N_DEV, rb, c), lambda i: (0, i, 0)), blk, blk],
                 [blk] * 4, [sds] * 4, semantics=("parallel",))(w, gslots, m, v)


def _tile_rows(r):
    for rb in (256, 128, 64, 32, 16, 8):
        if r % rb == 0:
            return rb
    return r


def _pack(arrs, lead=0):
    flats = []
    for a in arrs:
        f = a.reshape(a.shape[:lead] + (-1,)).astype(F32)
        flats.append(jnp.pad(f, [(0, 0)] * lead + [(0, (-f.shape[-1]) % 128)]))
    v = jnp.concatenate(flats, axis=-1)
    v = jnp.pad(v, [(0, 0)] * lead + [(0, (-v.shape[-1]) % 1024)])
    return v.reshape(v.shape[:lead] + (-1, 128))


def _unpack(buf, shapes, lead=0):
    flat = buf.reshape(buf.shape[:lead] + (-1,))
    outs, off = [], 0
    for s in shapes:
        n = math.prod(s)
        outs.append(flat[..., off:off + n].reshape(buf.shape[:lead] + tuple(s)))
        off += n + (-n) % 128
    return outs


def _cols_gathered(g):
    n, l, r, c = g.shape
    return g.transpose(1, 2, 0, 3).reshape(l, r, n * c)


def _cols_to_slabs(full):
    l, r, c = full.shape
    return full.reshape(l, r, N_DEV, c // N_DEV).transpose(2, 0, 1, 3)


def _rows_gathered(g):
    n, l, r, c = g.shape
    return g.transpose(1, 0, 2, 3).reshape(l, n * r, c)


def _rows_to_slabs(full):
    l, r, c = full.shape
    return full.reshape(l, N_DEV, r // N_DEV, c).transpose(1, 0, 2, 3)


def _pad_cols(w, at, width):
    return jnp.pad(w, ((0, 0), (at, width - at - w.shape[1])))


def _pad_lanes(v, width=HEAD):
    return jnp.pad(v.reshape(1, -1), ((0, 0), (0, width - v.size)))


def _taps8(w, bias=None):
    rows = [w] if bias is None else [w, bias.reshape(1, -1)]
    w8 = jnp.concatenate(rows, axis=0)
    return jnp.pad(w8, ((0, 8 - w8.shape[0]), (0, 0)))


class _Carrier:
    def __init__(self):
        self.jobs, self.got = {}, {}

    def put(self, matmul_name, key, src, slabs):
        self.jobs[matmul_name] = (key, src, slabs)

    def matmul(self, a, b, mode, name, **kw):
        job = self.jobs.pop(name, None)
        if job is None:
            return _matmul(a, b, mode, name, **kw)
        key, src, slabs = job
        out, self.got[key] = _matmul(a, b, mode, name, carry=(src, slabs), **kw)
        return out


def _gdn_forward(x, p, tag, mm):
    pq = mm(x, p["w_qkv"], "nn", tag + "_in_qkv")
    z = mm(x, p["w_z"], "nn", tag + "_in_z")
    ba = mm(x, p["w_ba"], "nn", tag + "_in_ba")
    c = _conv_fwd(pq, p["conv8"], 4, tag + "_conv")
    qn, kn, v, beta, gc = _gdn_ew_fwd(c, ba, p["a_log"], p["dt_bias"], tag + "_ew")
    o, h, states = _gdn_chunk_fwd(qn, kn, v, beta, gc, z, p["norm_w"], tag + "_chunk")
    y = mm(h, p["w_out"], "nn", tag + "_out")
    return y, dict(pq=pq, z=z, ba=ba, c=c, qn=qn, kn=kn, v=v, beta=beta, gc=gc, o=o, h=h, states=states)


def _gdn_backward(x, du, du_mm, p, s, tag, mm, ship):
    dh = mm(du_mm, p["w_out"], "nt", tag + "_bwd_dh")
    g_out = mm(s["h"], du_mm, "tn", tag + "_bwd_wout")
    ship("w_out", g_out)
    dq, dk, dv, dz, dbeta, dgc, nacc = _gdn_chunk_bwd(
        s["qn"], s["kn"], s["v"], s["beta"], s["gc"], s["z"], p["norm_w"], s["o"], s["states"], dh, tag + "_bwd_chunk")
    dc, dba, sacc = _gdn_ew_bwd(s["c"], s["ba"], p["a_log"], p["dt_bias"], dq, dk, dv, dbeta, dgc, tag + "_bwd_ew")
    dpq, dconv = _conv_bwd(dc, s["pq"], p["conv8"], 4, tag + "_bwd_conv")
    g_qkv = mm(x, dpq, "tn", tag + "_bwd_w_qkv")
    g_z = mm(x, dz, "tn", tag + "_bwd_w_z")
    g_ba = mm(x, dba, "tn", tag + "_bwd_w_ba")
    g_in = jnp.concatenate([g_qkv, g_z, g_ba[:, :GDN_V_HEADS], g_ba[:, HEAD:HEAD + GDN_V_HEADS]], axis=1)
    ship("w_in", g_in)
    dx = mm(dpq, p["w_qkv"], "nt", tag + "_bwd_dx_qkv", add=du, add_scale=ALPHA)
    dx = mm(dz, p["w_z"], "nt", tag + "_bwd_dx_z", add=dx)
    dx = mm(dba, p["w_ba"], "nt", tag + "_bwd_dx_ba", add=dx)
    grads = dict(w_in=g_in, w_out=g_out, conv_w=dconv[:4], a_log=sacc[0, :GDN_V_HEADS], dt_bias=sacc[1, :GDN_V_HEADS],
                 norm_w=nacc[0])
    return dx, grads


def _sc_forward(x, p, tag, mm):
    hh = mm(x, p["w_h"], "nn", tag + "_in_h")
    bg = mm(x, p["w_b"], "nn", tag + "_in_b")
    cg = mm(x, p["w_c"], "nn", tag + "_in_c")
    z = mm(x, p["w_z"], "nn", tag + "_in_z")
    cv = _conv_fwd(cg, p["conv8"], 3, tag + "_conv", u2=hh)
    h = _sc_gate_fwd(bg, cv, z, tag + "_gate")
    y = mm(h, p["w_out"], "nn", tag + "_out")
    return y, dict(hh=hh, bg=bg, cg=cg, z=z, cv=cv, h=h)


def _sc_backward(x, du, du_mm, p, s, tag, mm, ship):
    dh = mm(du_mm, p["w_out"], "nt", tag + "_bwd_dh")
    g_out = mm(s["h"], du_mm, "tn", tag + "_bwd_wout")
    ship("w_out", g_out)
    dbg, dcv, dz = _sc_gate_bwd(dh, s["bg"], s["cv"], s["z"], tag + "_bwd_gate")
    dcg, dhh, dconv = _conv_bwd(dcv, s["cg"], p["conv8"], 3, tag + "_bwd_conv", u2=s["hh"])
    g_in = jnp.concatenate([mm(x, d, "tn", tag + "_bwd_w_" + n)
                            for n, d in (("h", dhh), ("b", dbg), ("c", dcg), ("z", dz))], axis=1)
    ship("w_in", g_in)
    dx = mm(dhh, p["w_h"], "nt", tag + "_bwd_dx_h", add=du, add_scale=ALPHA)
    dx = mm(dbg, p["w_b"], "nt", tag + "_bwd_dx_b", add=dx)
    dx = mm(dcg, p["w_c"], "nt", tag + "_bwd_dx_c", add=dx)
    dx = mm(dz, p["w_z"], "nt", tag + "_bwd_dx_z", add=dx)
    return dx, dict(w_in=g_in, w_out=g_out, conv_w=dconv[:3])


def _ssd_forward(x, p, tag, mm):
    z = mm(x, p["w_z"], "nn", tag + "_in_z")
    xbc = mm(x, p["w_xbc"], "nn", tag + "_in_xbc")
    dtp = mm(x, p["w_dt"], "nn", tag + "_in_dt")
    cx = _conv_fwd(xbc, p["conv8"], 4, tag + "_conv", bias=True)
    y, h, states = _ssd_chunk_fwd(cx, dtp, z, p["dt_bias"], p["a_log"], p["dskip"], p["norm_w"], tag + "_chunk")
    out = mm(h, p["w_out"], "nn", tag + "_out")
    return out, dict(z=z, xbc=xbc, dtp=dtp, cx=cx, y=y, h=h, states=states)


def _ssd_backward(x, du, du_mm, p, s, tag, mm, ship):
    dh = mm(du_mm, p["w_out"], "nt", tag + "_bwd_dh")
    g_out = mm(s["h"], du_mm, "tn", tag + "_bwd_wout")
    ship("w_out", g_out)
    dcx, ddtp, dz, wide, acc = _ssd_chunk_bwd(s["cx"], s["dtp"], s["z"], p["dt_bias"], p["a_log"], p["dskip"],
                                              p["norm_w"], s["y"], s["states"], dh, tag + "_bwd_chunk")
    dxbc, dconv = _conv_bwd(dcx, s["xbc"], p["conv8"], 4, tag + "_bwd_conv")
    g_dt = mm(x, ddtp, "tn", tag + "_bwd_w_dt")
    g_in = jnp.concatenate([mm(x, dz, "tn", tag + "_bwd_w_z"), mm(x, dxbc, "tn", tag + "_bwd_w_xbc"),
                            g_dt[:, :32]], axis=1)
    ship("w_in", g_in)
    dx = mm(dz, p["w_z"], "nt", tag + "_bwd_dx_z", add=du, add_scale=ALPHA)
    dx = mm(dxbc, p["w_xbc"], "nt", tag + "_bwd_dx_xbc", add=dx)
    dx = mm(ddtp, p["w_dt"], "nt", tag + "_bwd_dx_dt", add=dx)
    grads = dict(w_in=g_in, w_out=g_out, conv_w=dconv[:4], conv_b=dconv[4], a_log=acc[0, :32], dt_bias=acc[1, :32],
                 d_skip=jnp.sum(wide[1].reshape(32, 64), axis=1), norm_w=wide[0])
    return dx, grads


_WEIGHTS = ['gdn_w_in', 'gdn_conv_w', 'gdn_a_log', 'gdn_dt_bias', 'gdn_norm_w', 'gdn_w_out', 'sc_w_in', 'sc_conv_w',
            'sc_w_out', 'ssd_w_in', 'ssd_conv_w', 'ssd_conv_b', 'ssd_a_log', 'ssd_dt_bias', 'ssd_d_skip',
            'ssd_norm_w', 'ssd_w_out', 'ln_g', 'ln_b']
_BIG = {'gdn_w_in': 'cols', 'gdn_w_out': 'rows', 'sc_w_in': 'cols', 'sc_w_out': 'rows', 'ssd_w_in': 'cols',
        'ssd_w_out': 'rows'}
_SMALL_SHARDED = ['gdn_conv_w', 'sc_conv_w', 'ssd_conv_w', 'ssd_conv_b', 'ssd_norm_w']
_SMALL = [n for n in _WEIGHTS if n not in _BIG]


def kernel(x, gdn_w_in, gdn_conv_w, gdn_a_log, gdn_dt_bias, gdn_norm_w, gdn_w_out, sc_w_in, sc_conv_w, sc_w_out, ssd_w_in, ssd_conv_w, ssd_conv_b, ssd_a_log, ssd_dt_bias, ssd_d_skip, ssd_norm_w, ssd_w_out, ln_g, ln_b, loss_target, m_gdn_w_in, m_gdn_conv_w, m_gdn_a_log, m_gdn_dt_bias, m_gdn_norm_w, m_gdn_w_out, m_sc_w_in, m_sc_conv_w, m_sc_w_out, m_ssd_w_in, m_ssd_conv_w, m_ssd_conv_b, m_ssd_a_log, m_ssd_dt_bias, m_ssd_d_skip, m_ssd_norm_w, m_ssd_w_out, m_ln_g, m_ln_b, v_gdn_w_in, v_gdn_conv_w, v_gdn_a_log, v_gdn_dt_bias, v_gdn_norm_w, v_gdn_w_out, v_sc_w_in, v_sc_conv_w, v_sc_w_out, v_ssd_w_in, v_ssd_conv_w, v_ssd_conv_b, v_ssd_a_log, v_ssd_dt_bias, v_ssd_d_skip, v_ssd_norm_w, v_ssd_w_out, v_ln_g, v_ln_b):
    args = locals()
    wts = {n: args[n] for n in _WEIGHTS}
    mom = {n: args["m_" + n] for n in _WEIGHTS}
    vel = {n: args["v_" + n] for n in _WEIGHTS}
    me = 4 * lax.axis_index("x") + 2 * lax.axis_index("y") + lax.axis_index("c")
    x0, target = x[0], loss_target[0]

    car = _Carrier()
    shard = lambda n, j: wts[n][j:j + 1].astype(MM_DTYPE)
    gathered_w = lambda n, j: (_cols_gathered if _BIG[n] == "cols" else _rows_gathered)(car.got[n, j])[0]

    for n in ('gdn_w_in', 'gdn_w_out'):
        car.got[n, 0] = _exchange(shard(n, 0), "gather_%s0" % n, slabs=False)
    riders = {0: [("l0_gdn_in_qkv", 'sc_w_in', 0), ("l0_gdn_in_z", 'sc_w_out', 0), ("l0_gdn_out", 'ssd_w_out', 0)],
              1: [("l1_sc_in_h", 'ssd_w_in', 0)],
              2: [("l2_ssd_in_xbc", 'gdn_w_in', 1), ("l2_ssd_in_z", 'gdn_w_out', 1)]}
    full = {}
    small_shapes = [wts[n].shape for n in _SMALL_SHARDED]
    gathered = _exchange(_pack([wts[n] for n in _SMALL_SHARDED]), "gather_small", slabs=False)
    for n, g in zip(_SMALL_SHARDED, _unpack(gathered, small_shapes, lead=1)):
        full[n] = jnp.moveaxis(g, 0, -2).reshape(g.shape[1:-1] + (N_DEV * g.shape[-1],))
    for n in _SMALL:
        full.setdefault(n, wts[n])

    def gdn_params(j):
        w = gathered_w('gdn_w_in', j)
        return dict(w_qkv=w[:, :GDN_CONV_DIM], w_z=w[:, GDN_CONV_DIM:GDN_CONV_DIM + D_INNER],
                    w_ba=jnp.concatenate([_pad_cols(w[:, 6144:6160], 0, HEAD), _pad_cols(w[:, 6160:6176], 0, HEAD)], 1),
                    conv8=_taps8(full['gdn_conv_w'][j]), a_log=_pad_lanes(full['gdn_a_log'][j]),
                    dt_bias=_pad_lanes(full['gdn_dt_bias'][j]), norm_w=full['gdn_norm_w'][j].reshape(1, HEAD),
                    w_out=gathered_w('gdn_w_out', j))

    def sc_params():
        w = gathered_w('sc_w_in', 0)
        return dict(w_h=w[:, :2048], w_b=w[:, 2048:4096], w_c=w[:, 4096:6144], w_z=w[:, 6144:],
                    conv8=_taps8(full['sc_conv_w'][0]), w_out=gathered_w('sc_w_out', 0))

    def ssd_params():
        w = gathered_w('ssd_w_in', 0)
        return dict(w_z=w[:, :D_INNER], w_xbc=w[:, D_INNER:D_INNER + SSD_CONV_DIM],
                    w_dt=_pad_cols(w[:, D_INNER + SSD_CONV_DIM:], 0, HEAD),
                    conv8=_taps8(full['ssd_conv_w'][0], full['ssd_conv_b'][0]), a_log=_pad_lanes(full['ssd_a_log'][0]),
                    dt_bias=_pad_lanes(full['ssd_dt_bias'][0]),
                    dskip=jnp.repeat(full['ssd_d_skip'][0], 64).reshape(1, D_INNER),
                    norm_w=full['ssd_norm_w'][0].reshape(1, D_INNER), w_out=gathered_w('ssd_w_out', 0))

    layers = [("gdn", _gdn_forward, _gdn_backward, lambda: gdn_params(0)), ("sc", _sc_forward, _sc_backward, sc_params),
              ("ssd", _ssd_forward, _ssd_backward, ssd_params), ("gdn", _gdn_forward, _gdn_backward, lambda: gdn_params(1))]

    acts, acts_mm, ys, saved, params = [x0], [x0.astype(MM_DTYPE)], [], [], []
    for i, (kind, fwd, _, make_params) in enumerate(layers):
        params.append(make_params())
        for matmul_name, n, j in riders.get(i, ()):
            car.put(matmul_name, (n, j), shard(n, j), False)
        y, s = fwd(acts_mm[-1], params[i], "l%d_%s" % (i, kind), car.matmul)
        ys.append(y)
        saved.append(s)
        gain, bias = full['ln_g'][i].reshape(1, -1), full['ln_b'][i].reshape(1, -1)
        if i < DEPTH - 1:
            out, out_mm = _ln_fwd(acts[-1], y, gain, bias, "l%d_ln" % i)
            acts.append(out)
            acts_mm.append(out_mm)
        else:
            dact, loss_acc = _ln_loss(acts[-1], y, gain, bias, target, "l%d_ln_loss" % i)
    loss = lax.psum(0.5 / D_MODEL * jnp.sum(loss_acc[0]), ("x", "y", "c"))

    grad_riders = {3: dict(w_out="l3_gdn_bwd_w_qkv", w_in="l3_gdn_bwd_dx_qkv"),
                   2: dict(w_out="l2_ssd_bwd_w_z", w_in="l2_ssd_bwd_dx_xbc"),
                   1: dict(w_out="l1_sc_bwd_w_h", w_in="l0_gdn_bwd_w_qkv"),
                   0: dict(w_out="l0_gdn_bwd_w_z", w_in="l0_gdn_bwd_dx_qkv")}

    def shipper(i):
        def ship(key, g):
            slabs = _cols_to_slabs(g[None]) if key == 'w_in' else _rows_to_slabs(g[None])
            car.put(grad_riders[i][key], ('grad', i, key), slabs.astype(MM_DTYPE), True)
        return ship

    lg = [None] * DEPTH
    d_ln_g, d_ln_b = [None] * DEPTH, [None] * DEPTH
    for i in reversed(range(DEPTH)):
        kind, _, bwd, _ = layers[i]
        du, du_mm, acc = _ln_bwd(dact, acts[i], ys[i], full['ln_g'][i].reshape(1, -1), "l%d_ln_bwd" % i)
        d_ln_g[i], d_ln_b[i] = acc[0], acc[1]
        dact, lg[i] = bwd(acts_mm[i], du, du_mm, params[i], saved[i], "l%d_%s" % (i, kind), car.matmul, shipper(i))
    assert not car.jobs, car.jobs
    grad_x = dact[None]

    stack = lambda k: jnp.stack([lg[0][k], lg[3][k]])
    local = {
        'gdn_conv_w': stack('conv_w'), 'gdn_a_log': stack('a_log'),
        'gdn_dt_bias': stack('dt_bias'), 'gdn_norm_w': stack('norm_w'),
        'sc_conv_w': lg[1]['conv_w'][None],
        'ssd_conv_w': lg[2]['conv_w'][None], 'ssd_conv_b': lg[2]['conv_b'][None],
        'ssd_a_log': lg[2]['a_log'][None], 'ssd_dt_bias': lg[2]['dt_bias'][None], 'ssd_d_skip': lg[2]['d_skip'][None],
        'ssd_norm_w': lg[2]['norm_w'][None],
        'ln_g': jnp.stack(d_ln_g), 'ln_b': jnp.stack(d_ln_b)}

    out = {}
    layers_of = {'gdn': (0, 3), 'sc': (1,), 'ssd': (2,)}
    for n in _BIG:
        kind, key = n.split('_', 1)
        recv = jnp.concatenate([car.got['grad', i, key] for i in layers_of[kind]], axis=1)
        shp = wts[n].shape
        r, c = shp[0] * shp[1], shp[2]
        res = _adamw(wts[n].reshape(r, c), recv.reshape(N_DEV, r, c), mom[n].reshape(r, c), vel[n].reshape(r, c),
                     "adamw_" + n)
        out[n] = [a.reshape(shp) for a in res]
    full_shapes = [local[n].shape for n in _SMALL]
    gathered = _exchange(_pack([local[n] for n in _SMALL]), "gather_small_grads", slabs=False)
    gs = []
    for n, g in zip(_SMALL, _unpack(gathered, full_shapes, lead=1)):
        if n in _SMALL_SHARDED:
            width = wts[n].shape[-1]
            g = lax.dynamic_slice_in_dim(g, me * width, width, axis=g.ndim - 1)
        gs.append(g)
    shapes = [wts[n].shape for n in _SMALL]
    res = _adamw(_pack([wts[n] for n in _SMALL]), _pack(gs, lead=1), _pack([mom[n] for n in _SMALL]),
                 _pack([vel[n] for n in _SMALL]), "adamw_small")
    for k, n in enumerate(_SMALL):
        out[n] = [_unpack(a, shapes)[k] for a in res]

    return (loss, grad_x, *[out[n][0] for n in _WEIGHTS], *[out[n][1] for n in _WEIGHTS],
            *[out[n][2] for n in _WEIGHTS], *[out[n][3] for n in _WEIGHTS])
```

```python
import functools
import math

import jax
import jax.numpy as jnp
from jax import lax
from jax.experimental import pallas as pl
from jax.experimental.pallas import tpu as pltpu

F32 = jnp.float32
MM_DTYPE = jnp.bfloat16

N_DEV = 8
D_MODEL = 1024
D_INNER = 2048
CHUNK = 64
HEAD = 128
GDN_V_HEADS = 16
GDN_GROUP = 16
GDN_QK_HEADS = 8
GDN_QK_DIM = 1024
GDN_CONV_DIM = 4096
SSD_PAIRS = 16
SSD_GROUPS = 4
SSD_STATE = 128
SSD_CONV_DIM = 3072
DEPTH = 4
ALPHA = (2 * DEPTH) ** 0.25
RMS_EPS = 1e-6
LN_EPS = 1e-5
L2_EPS = 1e-6
ADAM_LR, ADAM_B1, ADAM_B2, ADAM_EPS, ADAM_WD, ADAM_STEP = 0.001, 0.9, 0.999, 1e-08, 0.01, 10

VMEM_LIMIT_BYTES = 48 * 1024 * 1024
NEG_BIG = -1e30

_NN = (((1,), (0,)), ((), ()))
_NT = (((1,), (1,)), ((), ()))
_TN = (((0,), (0,)), ((), ()))


def _mm(a, b, dims):
    return lax.dot_general(a.astype(MM_DTYPE), b.astype(MM_DTYPE), dims, preferred_element_type=F32)


def _mmx(a, b, dims):
    return lax.dot_general(a, b, dims, precision=lax.Precision.HIGHEST, preferred_element_type=F32)


def _iota(shape, dim):
    return lax.broadcasted_iota(jnp.int32, shape, dim)


def _eye(n):
    return (_iota((n, n), 0) == _iota((n, n), 1)).astype(F32)


def _sig(x):
    return jax.nn.sigmoid(x)


def _silu(x):
    return x * _sig(x)


def _dsilu(x):
    s = _sig(x)
    return s * (1.0 + x * (1.0 - s))


def _softplus(x):
    return jnp.maximum(x, 0.0) + jnp.log(1.0 + jnp.exp(-jnp.abs(x)))


def _col(x, h):
    return jnp.sum(jnp.where(_iota(x.shape, 1) == h, x, 0.0), axis=1, keepdims=True)


def _row(x, h):
    return jnp.sum(jnp.where(_iota(x.shape, 0) == h, x, 0.0), axis=0, keepdims=True)


def _put_col(acc, col, h):
    return jnp.where(_iota(acc.shape, 1) == h, col, acc)


def _put_row(acc, row, h):
    return jnp.where(_iota(acc.shape, 0) == h, row, acc)


def _put_sub(acc, row, j):
    return acc + jnp.where(_iota(acc.shape, 0) == j, row, 0.0)


def _lanes(h):
    return pl.ds(h * HEAD, HEAD) if isinstance(h, int) else pl.ds(pl.multiple_of(h * HEAD, HEAD), HEAD)


def _total(x):
    return jnp.sum(jnp.sum(x, axis=0, keepdims=True), axis=1, keepdims=True)


def _call(body, name, grid, in_specs, out_specs, out_shape, scratch_shapes=(), semantics=None):
    return pl.pallas_call(
        body, name=name, grid=grid, in_specs=in_specs, out_specs=out_specs, out_shape=out_shape,
        scratch_shapes=list(scratch_shapes),
        compiler_params=pltpu.CompilerParams(dimension_semantics=semantics, vmem_limit_bytes=VMEM_LIMIT_BYTES))


def _tile(n, pref):
    if n <= pref:
        return n
    t = pref
    while n % t:
        t -= 128
    return t


def _exchange_copies(src_ref, out_ref, send_sems, recv_sems, local_sem, slabs):
    x, y, c = lax.axis_index("x"), lax.axis_index("y"), lax.axis_index("c")
    me = 4 * x + 2 * y + c
    mine = src_ref.at[me] if slabs else src_ref
    local = pltpu.make_async_copy(mine, out_ref.at[me], local_sem)
    sends, recvs = [], []
    for r in range(1, N_DEV):
        px = 1 - x if r & 4 else x
        py = 1 - y if r & 2 else y
        pc = 1 - c if r & 1 else c
        peer = 4 * px + 2 * py + pc
        kw = dict(send_sem=send_sems.at[r - 1], recv_sem=recv_sems.at[r - 1], device_id=(px, py, pc),
                  device_id_type=pl.DeviceIdType.MESH)
        sends.append(pltpu.make_async_remote_copy(src_ref=src_ref.at[peer] if slabs else src_ref,
                                                  dst_ref=out_ref.at[me], **kw))
        recvs.append(pltpu.make_async_remote_copy(src_ref=mine, dst_ref=out_ref.at[peer], **kw))
    return local, sends, recvs


def _gather_copies(src_ref, out_ref, send_sems, recv_sems, local_sem):
    x, y, c = lax.axis_index("x"), lax.axis_index("y"), lax.axis_index("c")
    chips = [(1 - x, y), (x, 1 - y), (1 - x, 1 - y)]
    sibling = (x, y, 1 - c)

    def slot(px, py, pc):
        return out_ref.at[4 * px + 2 * py + pc]

    def copy(k, src, dst, to):
        return pltpu.make_async_remote_copy(src_ref=src, dst_ref=dst, send_sem=send_sems.at[k], recv_sem=recv_sems.at[k],
                                            device_id=to, device_id_type=pl.DeviceIdType.MESH)

    mine = slot(x, y, c)
    local = pltpu.make_async_copy(src_ref, mine, local_sem)
    first = [copy(0, src_ref, mine, sibling)] + [copy(1 + j, src_ref, mine, (*chip, c)) for j, chip in enumerate(chips)]
    landed = [copy(1 + j, src_ref, slot(*chip, c), sibling) for j, chip in enumerate(chips)]
    passed = [copy(4 + j, slot(*chip, c), slot(*chip, c), sibling) for j, chip in enumerate(chips)]
    from_sibling = [copy(0, src_ref, slot(x, y, 1 - c), sibling)] + \
        [copy(4 + j, src_ref, slot(*chip, 1 - c), sibling) for j, chip in enumerate(chips)]
    return local, first, landed, passed, from_sibling


def _exchange_start(*refs, slabs):
    if not slabs:
        local, first = _gather_copies(*refs)[:2]
        local.start()
        for cp in first:
            cp.start()
        return
    local, sends, _ = _exchange_copies(*refs, slabs=slabs)
    local.start()
    for cp in sends:
        cp.start()


def _exchange_wait(*refs, slabs):
    if not slabs:
        local, first, landed, passed, from_sibling = _gather_copies(*refs)
        for arrived, onward in zip(landed, passed):
            arrived.wait_recv()
            onward.start()
        for cp in from_sibling:
            cp.wait_recv()
        for cp in first + passed:
            cp.wait_send()
        local.wait()
        return
    local, sends, recvs = _exchange_copies(*refs, slabs=slabs)
    for cp in recvs:
        cp.wait_recv()
    for cp in sends:
        cp.wait_send()
    local.wait()


def _exchange_sems():
    return [pltpu.SemaphoreType.DMA((N_DEV - 1,)), pltpu.SemaphoreType.DMA((N_DEV - 1,)), pltpu.SemaphoreType.DMA(())]


def _exchange_shape(src, slabs):
    return jax.ShapeDtypeStruct((N_DEV,) + tuple(src.shape[1:] if slabs else src.shape), src.dtype)


def _exchange(src, name, slabs):
    def body(*refs):
        _exchange_start(*refs, slabs=slabs)
        _exchange_wait(*refs, slabs=slabs)

    return pl.pallas_call(
        body, name=name,
        in_specs=[pl.BlockSpec(memory_space=pl.ANY)], out_specs=pl.BlockSpec(memory_space=pl.ANY),
        out_shape=_exchange_shape(src, slabs), scratch_shapes=_exchange_sems(),
    )(src)


MM_TM, MM_TN, MM_TK = 1024, 1024, 1024


def _matmul(a, b, mode, name, add=None, add_scale=1.0, carry=None):
    if mode == "nn":
        (m, k), (_, n) = a.shape, b.shape
    elif mode == "nt":
        (m, k), (n, _) = a.shape, b.shape
    else:
        (k, m), (_, n) = a.shape, b.shape
    tk = _tile(k, 2 * MM_TK)
    tm, tn = _tile(m, 2 * MM_TM if mode == "nn" and add is None and tk <= MM_TK else MM_TM), _tile(n, MM_TN)
    nk = k // tk
    grid = (m // tm, n // tn, nk)
    dims = {"nn": _NN, "nt": _NT, "tn": _TN}[mode]
    n_in = 2 + (add is not None)

    def body(*refs):
        a_ref, b_ref, o_ref = refs[0], refs[1], refs[n_in + (carry is not None)]
        if carry is not None:
            ex = (refs[n_in], refs[n_in + 2]) + tuple(refs[n_in + 3:])
            step = (pl.program_id(0) * grid[1] + pl.program_id(1)) * grid[2] + pl.program_id(2)

            @pl.when(step == 0)
            def _():
                _exchange_start(*ex, slabs=carry[1])

        part = _mm(a_ref[...], b_ref[...], dims)
        first = part if add is None else part + add_scale * refs[2][...]
        if nk == 1:
            o_ref[...] = first
        else:
            @pl.when(pl.program_id(2) == 0)
            def _():
                o_ref[...] = first

            @pl.when(pl.program_id(2) > 0)
            def _():
                o_ref[...] += part

        if carry is not None:
            @pl.when(step == grid[0] * grid[1] * grid[2] - 1)
            def _():
                _exchange_wait(*ex, slabs=carry[1])

    if mode == "nn":
        specs = [pl.BlockSpec((tm, tk), lambda i, j, q: (i, q)), pl.BlockSpec((tk, tn), lambda i, j, q: (q, j))]
    elif mode == "nt":
        specs = [pl.BlockSpec((tm, tk), lambda i, j, q: (i, q)), pl.BlockSpec((tn, tk), lambda i, j, q: (j, q))]
    else:
        specs = [pl.BlockSpec((tk, tm), lambda i, j, q: (q, i)), pl.BlockSpec((tk, tn), lambda i, j, q: (q, j))]
    out_spec = pl.BlockSpec((tm, tn), lambda i, j, q: (i, j))
    args = [a, b]
    if add is not None:
        specs.append(out_spec)
        args.append(add)
    out_shape = jax.ShapeDtypeStruct((m, n), F32)
    if carry is None:
        return _call(body, name, grid, specs, out_spec, out_shape, semantics=("parallel", "parallel", "arbitrary"))(*args)
    hbm = pl.BlockSpec(memory_space=pl.ANY)
    return _call(body, name, grid, specs + [hbm], [out_spec, hbm], [out_shape, _exchange_shape(*carry)],
                 _exchange_sems(), ("arbitrary", "arbitrary", "arbitrary"))(*args, carry[0])


CONV_TB = 512
CONV_CB = 1024
HALO = 8


def _conv_specs(t, cb_n, tb):
    nb = tb // HALO
    blk = pl.BlockSpec((tb, cb_n), lambda c, i: (i, c))
    prev = pl.BlockSpec((HALO, cb_n), lambda c, i: (jnp.maximum(i * nb - 1, 0), c))
    nxt = pl.BlockSpec((HALO, cb_n), lambda c, i: (jnp.minimum((i + 1) * nb, t // HALO - 1), c))
    w = pl.BlockSpec((8, cb_n), lambda c, i: (0, c))
    return blk, prev, nxt, w


def _shift_down(ext, s, tb):
    return (pltpu.roll(ext, s, 0) if s else ext)[HALO:HALO + tb]


def _shift_up(ext, s, tb):
    n = ext.shape[0]
    return (pltpu.roll(ext, n - s, 0) if s else ext)[0:tb]


def _conv_fwd(u, w8, ktaps, name, u2=None, bias=False):
    t, ch = u.shape
    cb_n = min(CONV_CB, ch)
    tb = min(2 * CONV_TB, t)
    two = u2 is not None

    def body(*refs):
        if two:
            u_ref, up_ref, v_ref, vp_ref, w_ref, o_ref = refs
        else:
            u_ref, up_ref, w_ref, o_ref = refs
        first = pl.program_id(1) == 0
        blk, halo = u_ref[...], up_ref[...]
        if two:
            blk, halo = blk * v_ref[...], halo * vp_ref[...]
        ext = jnp.concatenate([jnp.where(first, 0.0, halo), blk], axis=0)
        acc = jnp.zeros((tb, cb_n), F32)
        for j in range(ktaps):
            acc = acc + w_ref[j:j + 1, :] * _shift_down(ext, ktaps - 1 - j, tb)
        if bias:
            acc = acc + w_ref[ktaps:ktaps + 1, :]
        o_ref[...] = acc

    blk, prev, _, wspec = _conv_specs(t, cb_n, tb)
    specs, args = [blk, prev], [u, u]
    if two:
        specs += [blk, prev]
        args += [u2, u2]
    specs.append(wspec)
    args.append(w8)
    return _call(body, name, (ch // cb_n, t // tb), specs, blk, jax.ShapeDtypeStruct((t, ch), F32),
                 semantics=("parallel", "parallel"))(*args)


def _conv_bwd(dc, u, w8, ktaps, name, u2=None):
    t, ch = u.shape
    cb_n = min(CONV_CB, ch)
    two = u2 is not None
    tb = min(CONV_TB if two else 2 * CONV_TB, t)

    def body(*refs):
        if two:
            dc_ref, dn_ref, u_ref, v_ref, w_ref, du_ref, dv_ref, dw_ref = refs
        else:
            dc_ref, dn_ref, u_ref, w_ref, du_ref, dw_ref = refs
        i = pl.program_id(1)
        d = dc_ref[...]
        dext = jnp.concatenate([d, jnp.where(i == t // tb - 1, 0.0, dn_ref[...])], axis=0)
        blk = u_ref[...] * v_ref[...] if two else u_ref[...]
        du = jnp.zeros((tb, cb_n), F32)
        dw = jnp.zeros((8, cb_n), F32)
        for j in range(ktaps):
            ahead = _shift_up(dext, ktaps - 1 - j, tb)
            du = du + w_ref[j:j + 1, :] * ahead
            dw = _put_sub(dw, jnp.sum(ahead * blk, axis=0, keepdims=True), j)
        dw = _put_sub(dw, jnp.sum(d, axis=0, keepdims=True), ktaps)
        if two:
            du_ref[...] = (du * v_ref[...]).astype(du_ref.dtype)
            dv_ref[...] = (du * u_ref[...]).astype(dv_ref.dtype)
        else:
            du_ref[...] = du.astype(du_ref.dtype)

        @pl.when(i == 0)
        def _():
            dw_ref[...] = jnp.zeros_like(dw_ref)

        dw_ref[...] += dw

    blk, _, nxt, wspec = _conv_specs(t, cb_n, tb)
    specs, args = [blk, nxt, blk], [dc, dc, u]
    if two:
        specs.append(blk)
        args.append(u2)
    specs.append(wspec)
    args.append(w8)
    act = jax.ShapeDtypeStruct((t, ch), MM_DTYPE)
    outs = ([blk, blk, wspec], [act, act, jax.ShapeDtypeStruct((8, ch), F32)]) if two else \
        ([blk, wspec], [act, jax.ShapeDtypeStruct((8, ch), F32)])
    return _call(body, name, (ch // cb_n, t // tb), specs, outs[0], outs[1],
                 semantics=("parallel", "arbitrary"))(*args)


EW_TB = 256


def _chunk_mask(n, upper):
    i, j = _iota((n, n), 0), _iota((n, n), 1)
    same = jnp.right_shift(i, 6) == jnp.right_shift(j, 6)
    return (same & ((j >= i) if upper else (i >= j))).astype(F32)


def _rows(width, tb=EW_TB):
    return pl.BlockSpec((tb, width), lambda i: (i, 0))


def _const(rows, width):
    return pl.BlockSpec((rows, width), lambda i: (0, 0))


def _gdn_ew_fwd(c, ba, a_log, dt_bias, name):
    t = c.shape[0]
    tb = min(2 * EW_TB, t)

    def body(c_ref, ba_ref, al_ref, db_ref, q_ref, k_ref, v_ref, beta_ref, gc_ref):
        for h in range(GDN_QK_HEADS):
            for base, ref, scale in ((0, q_ref, HEAD ** -0.5), (GDN_QK_DIM, k_ref, 1.0)):
                s = _silu(c_ref[:, base + h * HEAD: base + (h + 1) * HEAD])
                r = lax.rsqrt(jnp.sum(s * s, axis=1, keepdims=True) + L2_EPS)
                ref[:, h * HEAD:(h + 1) * HEAD] = s * (r * scale)
        v_ref[...] = _silu(c_ref[:, 2 * GDN_QK_DIM:])
        beta_ref[...] = _sig(ba_ref[:, :HEAD])
        g = -jnp.exp(al_ref[...]) * _softplus(ba_ref[:, HEAD:] + db_ref[...])
        gc_ref[...] = _mmx(_chunk_mask(tb, False), g, _NN)

    act = lambda w: jax.ShapeDtypeStruct((t, w), F32)
    return _call(body, name, (t // tb,),
                 [_rows(GDN_CONV_DIM, tb), _rows(2 * HEAD, tb), _const(1, HEAD), _const(1, HEAD)],
                 [_rows(GDN_QK_DIM, tb), _rows(GDN_QK_DIM, tb), _rows(D_INNER, tb), _rows(HEAD, tb), _rows(HEAD, tb)],
                 [act(GDN_QK_DIM), act(GDN_QK_DIM), act(D_INNER), act(HEAD), act(HEAD)],
                 semantics=("parallel",))(c, ba, a_log, dt_bias)


def _gdn_ew_bwd(c, ba, a_log, dt_bias, dqh, dkh, dv, dbeta, dgc, name):
    t = c.shape[0]
    tb = EW_TB

    def body(c_ref, ba_ref, al_ref, db_ref, dq_ref, dk_ref, dv_ref, dbeta_ref, dgc_ref, dc_ref, dba_ref, acc_ref):
        for h in range(GDN_QK_HEADS):
            for base, ref, scale in ((0, dq_ref, HEAD ** -0.5), (GDN_QK_DIM, dk_ref, 1.0)):
                cq = c_ref[:, base + h * HEAD: base + (h + 1) * HEAD]
                s = _silu(cq)
                r = lax.rsqrt(jnp.sum(s * s, axis=1, keepdims=True) + L2_EPS)
                dn = ref[:, h * HEAD:(h + 1) * HEAD] * scale
                ds = r * dn - s * (r * r * r) * jnp.sum(dn * s, axis=1, keepdims=True)
                dc_ref[:, base + h * HEAD: base + (h + 1) * HEAD] = ds * _dsilu(cq)
        dc_ref[:, 2 * GDN_QK_DIM:] = dv_ref[...] * _dsilu(c_ref[:, 2 * GDN_QK_DIM:])
        beta = _sig(ba_ref[:, :HEAD])
        dba_ref[:, :HEAD] = (dbeta_ref[...] * beta * (1.0 - beta)).astype(dba_ref.dtype)
        pre = ba_ref[:, HEAD:] + db_ref[...]
        ea = jnp.exp(al_ref[...])
        g = -ea * _softplus(pre)
        dg = _mmx(_chunk_mask(tb, True), dgc_ref[...], _NN)
        da_raw = dg * (-ea) * _sig(pre)
        dba_ref[:, HEAD:] = da_raw.astype(dba_ref.dtype)
        acc = jnp.zeros((8, HEAD), F32)
        acc = _put_sub(acc, jnp.sum(dg * g, axis=0, keepdims=True), 0)
        acc = _put_sub(acc, jnp.sum(da_raw, axis=0, keepdims=True), 1)

        @pl.when(pl.program_id(0) == 0)
        def _():
            acc_ref[...] = jnp.zeros_like(acc_ref)

        acc_ref[...] += acc

    act = lambda w: jax.ShapeDtypeStruct((t, w), F32)
    return _call(body, name, (t // tb,),
                 [_rows(GDN_CONV_DIM), _rows(2 * HEAD), _const(1, HEAD), _const(1, HEAD),
                  _rows(GDN_QK_DIM), _rows(GDN_QK_DIM), _rows(D_INNER), _rows(HEAD), _rows(HEAD)],
                 [_rows(GDN_CONV_DIM), _rows(2 * HEAD), _const(8, HEAD)],
                 [act(GDN_CONV_DIM), jax.ShapeDtypeStruct((t, 2 * HEAD), MM_DTYPE), jax.ShapeDtypeStruct((8, HEAD), F32)],
                 semantics=("arbitrary",))(c, ba, a_log, dt_bias, dqh, dkh, dv, dbeta, dgc)


def _zip(fn, *lists):
    return [fn(*xs) for xs in zip(*lists)]


def _mms(xs, ys, dims):
    return [_mm(x, y, dims) for x, y in zip(xs, ys)]


def _cast(xs):
    seen = {}
    return [seen.setdefault(id(x), x.astype(MM_DTYPE)) for x in xs]


def _side_by_side(a, b):
    return jnp.concatenate([a, b], axis=1)


def _interleave(*gens):
    results, live = [None] * len(gens), list(range(len(gens)))
    while live:
        for i in list(live):
            try:
                next(gens[i])
            except StopIteration as stop:
                results[i] = stop.value
                live.remove(i)
    return results


def _gdn_local_stages(q, k, v, bcol, gcol, grow, glast):
    ii, jj = _iota((CHUNK, CHUNK), 0), _iota((CHUNK, CHUNK), 1)
    eye = _eye(CHUNK)
    mul = lambda x, y: x * y
    eg = [jnp.exp(g) for g in gcol]
    decay = _zip(lambda gc, gr: jnp.exp(jnp.where(ii >= jj, gc - gr, NEG_BIG)), gcol, grow)
    kb = _zip(mul, k, bcol)
    kc = _cast(k)
    p, qk = _mms(kb, kc, _NT), _mms(q, kc, _NT)
    yield
    a = _zip(lambda x, d: jnp.where(ii > jj, x * d, 0.0), p, decay)
    inv, pw = [eye - x for x in a], _cast(a)
    for _ in range(5):
        pw = _cast(_mms(pw, pw, _NN))
        yield
        inv = _zip(lambda x, y: x + y, inv, _mms(inv, pw, _NN))
        yield
    rv, rk = _zip(mul, v, bcol), _zip(mul, kb, eg)
    uw = _mms(inv, _zip(_side_by_side, rv, rk), _NN)
    u, w = [x[:, :HEAD] for x in uw], [x[:, HEAD:] for x in uw]
    yield
    att = _zip(mul, qk, decay)
    qd = _zip(mul, q, eg)
    ekt = _zip(lambda gl, gc: jnp.exp(gl - gc), glast, gcol)
    kt = _zip(mul, k, ekt)
    el = [jnp.exp(g) for g in glast]
    return dict(eg=eg, decay=decay, kb=kb, p=p, inv=inv, rv=rv, rk=rk, u=u, w=w, qk=qk, att=att, qd=qd, ekt=ekt, kt=kt,
                el=el)


def _gdn_state_stages(u, w, att, qd, kt, el, s_in):
    sc = _cast(s_in)
    ws, qs = _mms(w, sc, _NN), _mms(qd, sc, _NN)
    yield
    vn = _zip(lambda x, y: x - y, u, ws)
    vc = _cast(vn)
    av, kv = _mms(att, vc, _NN), _mms(kt, vc, _TN)
    yield
    out = _zip(lambda x, y: x + y, qs, av)
    s_out = _zip(lambda s, e, y: s * e + y, s_in, el, kv)
    return dict(vn=vn, out=out, s_out=s_out)


def _gdn_heads_fwd(q, k, v, bcol, gcol, grow, glast, s_in):
    f, = _interleave(_gdn_local_stages(q, k, v, bcol, gcol, grow, glast))
    g, = _interleave(_gdn_state_stages(f["u"], f["w"], f["att"], f["qd"], f["kt"], f["el"], s_in))
    return {**f, **g}


def _head_groups(group, init):
    if GDN_GROUP == GDN_V_HEADS:
        return group(0, init)
    return lax.fori_loop(0, GDN_V_HEADS // GDN_GROUP, lambda gi, c: group(GDN_GROUP * gi, c), init)


def _half(h):
    return h // 2 if isinstance(h, int) else jnp.right_shift(h, 1)


def _gdn_chunk_fwd(qn, kn, v, beta, gc, z, norm_w, name):
    t = qn.shape[0]
    nc = t // CHUNK

    def body(q_ref, k_ref, v_ref, beta_ref, gc_ref, z_ref, nw_ref, o_ref, h_ref, st_ref, state):
        @pl.when(pl.program_id(0) == 0)
        def _():
            state[...] = jnp.zeros_like(state)

        st_ref[0] = state[...]
        gc_all, beta_all = gc_ref[...], beta_ref[...]
        gct = _mmx(_eye(HEAD), gc_all, _NT)
        glast_all = gc_ref[CHUNK - 1:CHUNK, :]
        nw = nw_ref[...]

        def group(h0, carry):
            heads = [h0 + s for s in range(GDN_GROUP)]
            f = _gdn_heads_fwd([q_ref[:, _lanes(_half(h))] for h in heads], [k_ref[:, _lanes(_half(h))] for h in heads],
                               [v_ref[:, _lanes(h)] for h in heads], [_col(beta_all, h) for h in heads],
                               [_col(gc_all, h) for h in heads], [_row(gct, h) for h in heads],
                               [_col(glast_all, h) for h in heads], [state[h] for h in heads])
            for h, s_out, o in zip(heads, f["s_out"], f["out"]):
                state[h] = s_out
                o_ref[:, _lanes(h)] = o
                rstd = lax.rsqrt(jnp.mean(o * o, axis=1, keepdims=True) + RMS_EPS)
                h_ref[:, _lanes(h)] = (o * rstd * nw * _silu(z_ref[:, _lanes(h)])).astype(h_ref.dtype)
            return carry

        _head_groups(group, 0)

    rows = lambda w: pl.BlockSpec((CHUNK, w), lambda i: (i, 0))
    act = lambda w: jax.ShapeDtypeStruct((t, w), F32)
    return _call(body, name, (nc,),
                 [rows(GDN_QK_DIM), rows(GDN_QK_DIM), rows(D_INNER), rows(HEAD), rows(HEAD), rows(D_INNER),
                  _const(1, HEAD)],
                 [rows(D_INNER), rows(D_INNER), pl.BlockSpec((1, GDN_V_HEADS, HEAD, HEAD), lambda i: (i, 0, 0, 0))],
                 [act(D_INNER), jax.ShapeDtypeStruct((t, D_INNER), MM_DTYPE),
                  jax.ShapeDtypeStruct((nc, GDN_V_HEADS, HEAD, HEAD), F32)],
                 [pltpu.VMEM((GDN_V_HEADS, HEAD, HEAD), F32)], ("arbitrary",))(qn, kn, v, beta, gc, z, norm_w)


def _gdn_chunk_bwd(qn, kn, v, beta, gc, z, norm_w, o, states, dh, name):
    t = qn.shape[0]
    nc = t // CHUNK

    def body(q_ref, k_ref, v_ref, beta_ref, gc_ref, z_ref, nw_ref, o_ref, st_ref, dh_ref,
             dq_ref, dk_ref, dv_ref, dz_ref, dbeta_ref, dgc_ref, acc_ref, dstate):
        @pl.when(pl.program_id(0) == 0)
        def _():
            dstate[...] = jnp.zeros_like(dstate)
            acc_ref[...] = jnp.zeros_like(acc_ref)

        gc_all, beta_all = gc_ref[...], beta_ref[...]
        gct = _mmx(_eye(HEAD), gc_all, _NT)
        glast_all = gc_ref[CHUNK - 1:CHUNK, :]
        nw = nw_ref[...]
        ii, jj = _iota((CHUNK, CHUNK), 0), _iota((CHUNK, CHUNK), 1)
        last_row = _iota((CHUNK, 1), 0) == CHUNK - 1

        def group(h0, carry):
            dbeta_acc, dgc_acc, dgrow_acc, dnw_acc = carry
            heads = [h0 + s for s in range(GDN_GROUP)]
            mul, add, sub = (lambda x, y: x * y), (lambda x, y: x + y), (lambda x, y: x - y)
            rowsum = lambda x, y: jnp.sum(x * y, axis=1, keepdims=True)
            q, k = [q_ref[:, _lanes(_half(h))] for h in heads], [k_ref[:, _lanes(_half(h))] for h in heads]
            vv = [v_ref[:, _lanes(h)] for h in heads]
            bcol, gcol = [_col(beta_all, h) for h in heads], [_col(gc_all, h) for h in heads]
            s_in, dsn = [st_ref[0, h] for h in heads], [dstate[h] for h in heads]
            f = _gdn_heads_fwd(q, k, vv, bcol, gcol, [_row(gct, h) for h in heads],
                               [_col(glast_all, h) for h in heads], s_in)
            do = []
            for h in heads:
                oo, zz, dhh = o_ref[:, _lanes(h)], z_ref[:, _lanes(h)], dh_ref[:, _lanes(h)]
                rstd = lax.rsqrt(jnp.mean(oo * oo, axis=1, keepdims=True) + RMS_EPS)
                on, sz = oo * rstd, _silu(zz)
                dnw_acc = dnw_acc + jnp.sum(dhh * on * sz, axis=0, keepdims=True)
                dz_ref[:, _lanes(h)] = (dhh * on * nw * _dsilu(zz)).astype(dz_ref.dtype)
                don = dhh * nw * sz
                do.append(rstd * (don - on * jnp.mean(don * on, axis=1, keepdims=True)))
            decay, eg, inv = f["decay"], f["eg"], f["inv"]
            d_glast = _zip(lambda d, s, e: _total(d * s) * e, dsn, s_in, f["el"])
            dkt = _mms(f["vn"], dsn, _NT)
            dvn = _mms(f["kt"], dsn, _NN)
            dqd = _mms(do, s_in, _NT)
            ds_prev = _zip(lambda d, e, y: d * e + y, dsn, f["el"], _mms(f["qd"], do, _TN))
            datt = _mms(do, f["vn"], _NT)
            dvn = _zip(add, dvn, _mms(f["att"], do, _TN))
            dqk = _zip(mul, datt, decay)
            dq = _zip(lambda x, e, y: x * e + y, dqd, eg, _mms(dqk, k, _NN))
            dk = _mms(dqk, q, _TN)
            ddecay = _zip(mul, datt, f["qk"])
            dgcol = _zip(rowsum, dqd, f["qd"])
            dw = [-x for x in _mms(dvn, s_in, _NT)]
            ds_prev = _zip(sub, ds_prev, _mms(f["w"], dvn, _TN))
            drv, drk = _mms(inv, dvn, _TN), _mms(inv, dw, _TN)
            da = [jnp.where(ii > jj, -x, 0.0) for x in
                  _mms(_zip(_side_by_side, drv, drk), _zip(_side_by_side, f["u"], f["w"]), _NT)]
            dp = _zip(mul, da, decay)
            ddecay = _zip(lambda x, y, z_: x + y * z_, ddecay, da, f["p"])
            dkb = _zip(lambda x, y, e: x + y * e, _mms(dp, k, _NN), drk, eg)
            dk = _zip(add, dk, _mms(dp, f["kb"], _TN))
            dbeta = _zip(add, _zip(rowsum, drv, vv), _zip(rowsum, dkb, k))
            dgcol = _zip(add, dgcol, _zip(rowsum, drk, f["rk"]))
            dk = _zip(lambda x, y, b_, z_, e: x + y * b_ + z_ * e, dk, dkb, bcol, dkt, f["ekt"])
            tail = _zip(mul, dkt, f["kt"])
            d_glast = _zip(lambda x, y: x + _total(y), d_glast, tail)
            e_ = _zip(mul, ddecay, decay)
            dgcol = _zip(lambda x, t_, e, gl: x - jnp.sum(t_, axis=1, keepdims=True) + jnp.sum(e, axis=1, keepdims=True)
                         + jnp.where(last_row, gl, 0.0), dgcol, tail, e_, d_glast)
            for i_ in range(0, len(heads), 2):
                dq_ref[:, _lanes(_half(heads[i_]))] = dq[i_] + dq[i_ + 1]
                dk_ref[:, _lanes(_half(heads[i_]))] = dk[i_] + dk[i_ + 1]
            for i_, h in enumerate(heads):
                dstate[h] = ds_prev[i_]
                dv_ref[:, _lanes(h)] = drv[i_] * bcol[i_]
                dbeta_acc = _put_col(dbeta_acc, dbeta[i_], h)
                dgc_acc = _put_col(dgc_acc, dgcol[i_], h)
                dgrow_acc = _put_row(dgrow_acc, -jnp.sum(e_[i_], axis=0, keepdims=True), h)
            return dbeta_acc, dgc_acc, dgrow_acc, dnw_acc

        zero = jnp.zeros((CHUNK, HEAD), F32)
        dbeta_acc, dgc_acc, dgrow_acc, dnw_acc = _head_groups(
            group, (zero, zero, jnp.zeros((HEAD, CHUNK), F32), jnp.zeros((1, HEAD), F32)))
        dbeta_ref[...] = dbeta_acc
        dgc_ref[...] = dgc_acc + _mmx(_eye(CHUNK), dgrow_acc, _NT)
        acc_ref[...] += _put_sub(jnp.zeros((8, HEAD), F32), dnw_acc, 0)

    rows = lambda w: pl.BlockSpec((CHUNK, w), lambda i: (nc - 1 - i, 0))
    act = lambda w: jax.ShapeDtypeStruct((t, w), F32)
    return _call(body, name, (nc,),
                 [rows(GDN_QK_DIM), rows(GDN_QK_DIM), rows(D_INNER), rows(HEAD), rows(HEAD), rows(D_INNER),
                  _const(1, HEAD), rows(D_INNER),
                  pl.BlockSpec((1, GDN_V_HEADS, HEAD, HEAD), lambda i: (nc - 1 - i, 0, 0, 0)), rows(D_INNER)],
                 [rows(GDN_QK_DIM), rows(GDN_QK_DIM), rows(D_INNER), rows(D_INNER), rows(HEAD), rows(HEAD), _const(8, HEAD)],
                 [act(GDN_QK_DIM), act(GDN_QK_DIM), act(D_INNER), jax.ShapeDtypeStruct((t, D_INNER), MM_DTYPE), act(HEAD),
                  act(HEAD), jax.ShapeDtypeStruct((8, HEAD), F32)],
                 [pltpu.VMEM((GDN_V_HEADS, HEAD, HEAD), F32)], ("arbitrary",)
                 )(qn, kn, v, beta, gc, z, norm_w, o, states, dh)


def _sc_gate_fwd(bg, cv, z, name):
    t, w = bg.shape

    def body(b_ref, c_ref, z_ref, o_ref):
        o_ref[...] = (b_ref[...] * c_ref[...] * _silu(z_ref[...])).astype(o_ref.dtype)

    tb = min(2 * EW_TB, t)
    return _call(body, name, (t // tb,), [_rows(w, tb)] * 3, _rows(w, tb), jax.ShapeDtypeStruct((t, w), MM_DTYPE),
                 semantics=("parallel",))(bg, cv, z)


def _sc_gate_bwd(dh, bg, cv, z, name):
    t, w = bg.shape

    def body(d_ref, b_ref, c_ref, z_ref, db_ref, dc_ref, dz_ref):
        d, b, c, zz = d_ref[...], b_ref[...], c_ref[...], z_ref[...]
        sz = _silu(zz)
        db_ref[...] = (d * c * sz).astype(db_ref.dtype)
        dc_ref[...] = d * b * sz
        dz_ref[...] = (d * b * c * _dsilu(zz)).astype(dz_ref.dtype)

    act, act_mm = jax.ShapeDtypeStruct((t, w), F32), jax.ShapeDtypeStruct((t, w), MM_DTYPE)
    return _call(body, name, (t // EW_TB,), [_rows(w)] * 4, [_rows(w)] * 3, [act_mm, act, act_mm],
                 semantics=("parallel",))(dh, bg, cv, z)


XBC_B = D_INNER
XBC_C = D_INNER + SSD_GROUPS * SSD_STATE


def _ssd_scalars(dtp, dt_bias, a_log):
    dt = _softplus(dtp + dt_bias)
    a = -jnp.exp(a_log)
    da = dt * a
    ac = _mmx(_chunk_mask(CHUNK, False), da, _NN)
    act = _mmx(_eye(HEAD), ac, _NT)
    aclast = jnp.sum(jnp.where(_iota(ac.shape, 0) == CHUNK - 1, ac, 0.0), axis=0, keepdims=True)
    return dt, a, da, ac, act, aclast


def _ssd_pairs_fwd(x2, bg, cg, cb, dt, ac, act, aclast, s2):
    ii, jj = _iota((CHUNK, CHUNK), 0), _iota((CHUNK, CHUNK), 1)
    half = _iota((CHUNK, HEAD), 1) < 64
    causal = ii >= jj
    pairs = range(len(x2))
    mul = lambda x, y: x * y
    pick = lambda a, b: jnp.where(half, a, b)
    aca, acb = [_col(ac, 2 * p) for p in pairs], [_col(ac, 2 * p + 1) for p in pairs]
    la, lb = [_col(aclast, 2 * p) for p in pairs], [_col(aclast, 2 * p + 1) for p in pairs]
    dt2 = [pick(_col(dt, 2 * p), _col(dt, 2 * p + 1)) for p in pairs]
    xdt = _zip(mul, x2, dt2)
    sega = [jnp.exp(jnp.where(causal, aca[p] - _row(act, 2 * p), NEG_BIG)) for p in pairs]
    segb = [jnp.exp(jnp.where(causal, acb[p] - _row(act, 2 * p + 1), NEG_BIG)) for p in pairs]
    ma, mb = _zip(mul, sega, cb), _zip(mul, segb, cb)
    ydiag = _zip(pick, _mms(ma, xdt, _NN), _mms(mb, xdt, _NN))
    cdec = _zip(lambda a, b: pick(jnp.exp(a), jnp.exp(b)), aca, acb)
    cs = _mms(cg, s2, _NT)
    tail = _zip(lambda l1, a, l2, b: pick(jnp.exp(l1 - a), jnp.exp(l2 - b)), la, aca, lb, acb)
    zt = _zip(mul, xdt, tail)
    ea, eb = [jnp.exp(x) for x in la], [jnp.exp(x) for x in lb]
    tot = _zip(lambda a, b: jnp.where(_iota((HEAD, 1), 0) < 64, a, b), ea, eb)
    s_out = _zip(lambda s, t_, y: s * t_ + y, s2, tot, _mms(zt, bg, _TN))
    return dict(half=half, dt2=dt2, xdt=xdt, sega=sega, segb=segb, ma=ma, mb=mb, ydiag=ydiag, cdec=cdec, cs=cs,
                tail=tail, zt=zt, ea=ea, eb=eb, tot=tot, s_out=s_out)


def _ssd_group_inputs(cx_ref):
    cxb = [cx_ref[:, XBC_B + g * SSD_STATE: XBC_B + (g + 1) * SSD_STATE] for g in range(SSD_GROUPS)]
    cxc = [cx_ref[:, XBC_C + g * SSD_STATE: XBC_C + (g + 1) * SSD_STATE] for g in range(SSD_GROUPS)]
    bg, cg = [_silu(x) for x in cxb], [_silu(x) for x in cxc]
    return cxb, cxc, bg, cg, _mms(cg, bg, _NT)


def _per_pair(group_list):
    return [group_list[p // (SSD_PAIRS // SSD_GROUPS)] for p in range(SSD_PAIRS)]


def _ssd_chunk_fwd(cx, dtp, z, dt_bias, a_log, dskip, norm_w, name):
    t = cx.shape[0]
    nc = t // CHUNK
    gw = D_INNER // SSD_GROUPS

    def body(cx_ref, dtp_ref, z_ref, db_ref, al_ref, sk_ref, nw_ref, y_ref, h_ref, st_ref, state):
        @pl.when(pl.program_id(0) == 0)
        def _():
            state[...] = jnp.zeros_like(state)

        st_ref[0] = state[...]
        dt, _, _, ac, act, aclast = _ssd_scalars(dtp_ref[...], db_ref[...], al_ref[...])
        _, _, bg, cg, cb = _ssd_group_inputs(cx_ref)
        x2 = [_silu(cx_ref[:, _lanes(p)]) for p in range(SSD_PAIRS)]
        f = _ssd_pairs_fwd(x2, _per_pair(bg), _per_pair(cg), _per_pair(cb), dt, ac, act, aclast,
                           [state[p] for p in range(SSD_PAIRS)])
        for p in range(SSD_PAIRS):
            state[p] = f["s_out"][p]
            y_ref[:, _lanes(p)] = f["ydiag"][p] + f["cs"][p] * f["cdec"][p] + sk_ref[:, _lanes(p)] * x2[p]
        for g in range(SSD_GROUPS):
            sl = slice(g * gw, (g + 1) * gw)
            yg = y_ref[:, sl] * _silu(z_ref[:, sl])
            rstd = lax.rsqrt(jnp.mean(yg * yg, axis=1, keepdims=True) + RMS_EPS)
            h_ref[:, sl] = (yg * rstd * nw_ref[:, sl]).astype(h_ref.dtype)

    rows = lambda w: pl.BlockSpec((CHUNK, w), lambda i: (i, 0))
    act_ = lambda w: jax.ShapeDtypeStruct((t, w), F32)
    return _call(body, name, (nc,),
                 [rows(SSD_CONV_DIM), rows(HEAD), rows(D_INNER), _const(1, HEAD), _const(1, HEAD),
                  _const(1, D_INNER), _const(1, D_INNER)],
                 [rows(D_INNER), rows(D_INNER), pl.BlockSpec((1, SSD_PAIRS, HEAD, SSD_STATE), lambda i: (i, 0, 0, 0))],
                 [act_(D_INNER), jax.ShapeDtypeStruct((t, D_INNER), MM_DTYPE),
                  jax.ShapeDtypeStruct((nc, SSD_PAIRS, HEAD, SSD_STATE), F32)],
                 [pltpu.VMEM((SSD_PAIRS, HEAD, SSD_STATE), F32)], ("arbitrary",)
                 )(cx, dtp, z, dt_bias, a_log, dskip, norm_w)


def _ssd_chunk_bwd(cx, dtp, z, dt_bias, a_log, dskip, norm_w, y, states, dh, name):
    t = cx.shape[0]
    nc = t // CHUNK
    gw = D_INNER // SSD_GROUPS

    def body(cx_ref, dtp_ref, z_ref, db_ref, al_ref, sk_ref, nw_ref, y_ref, st_ref, dh_ref,
             dcx_ref, ddtp_ref, dz_ref, wide_ref, acc_ref, dstate, dy_s):
        @pl.when(pl.program_id(0) == 0)
        def _():
            dstate[...] = jnp.zeros_like(dstate)
            wide_ref[...] = jnp.zeros_like(wide_ref)
            acc_ref[...] = jnp.zeros_like(acc_ref)

        dtp = dtp_ref[...]
        dt, a, da, ac, act, aclast = _ssd_scalars(dtp, db_ref[...], al_ref[...])
        ii, jj = _iota((CHUNK, CHUNK), 0), _iota((CHUNK, CHUNK), 1)
        last_row = _iota((CHUNK, 1), 0) == CHUNK - 1
        for g in range(SSD_GROUPS):
            sl = slice(g * gw, (g + 1) * gw)
            yy, zz, dhh, nw = y_ref[:, sl], z_ref[:, sl], dh_ref[:, sl], nw_ref[:, sl]
            sz = _silu(zz)
            yg = yy * sz
            rstd = lax.rsqrt(jnp.mean(yg * yg, axis=1, keepdims=True) + RMS_EPS)
            n = yg * rstd
            dn = dhh * nw
            dyg = rstd * (dn - n * jnp.mean(dn * n, axis=1, keepdims=True))
            dy_s[:, sl] = dyg * sz
            dz_ref[:, sl] = (dyg * yy * _dsilu(zz)).astype(dz_ref.dtype)
            wide_ref[0:1, sl] += jnp.sum(dhh * n, axis=0, keepdims=True)

        pairs = range(SSD_PAIRS)
        mul, add, sub = (lambda x, y: x * y), (lambda x, y: x + y), (lambda x, y: x - y)
        rowsum = lambda x: jnp.sum(x, axis=1, keepdims=True)
        cxb, cxc, bg, cg, cb = _ssd_group_inputs(cx_ref)
        bgp, cgp = _per_pair(bg), _per_pair(cg)
        cxx = [cx_ref[:, _lanes(p)] for p in pairs]
        x2 = [_silu(x) for x in cxx]
        s2, dsn = [st_ref[0, p] for p in pairs], [dstate[p] for p in pairs]
        dy2 = [dy_s[:, _lanes(p)] for p in pairs]
        f = _ssd_pairs_fwd(x2, bgp, cgp, _per_pair(cb), dt, ac, act, aclast, s2)
        half = f["half"]
        lo = lambda x: jnp.where(half, x, 0.0)
        dx2 = [dy2[p] * sk_ref[:, _lanes(p)] for p in pairs]
        for p in pairs:
            wide_ref[1:2, _lanes(p)] += jnp.sum(dy2[p] * x2[p], axis=0, keepdims=True)
        gg = _zip(mul, dy2, f["cdec"])
        dc_p = _mms(gg, s2, _NN)
        ds_prev = _zip(lambda d, t_, y: d * t_ + y, dsn, f["tot"], _mms(gg, cgp, _TN))
        t1 = _zip(lambda d, c, e: d * c * e, dy2, f["cs"], f["cdec"])
        dac_a = [rowsum(lo(x)) for x in t1]
        dac_b = _zip(lambda x, a_: rowsum(x) - a_, t1, dac_a)
        dya = [lo(x) for x in dy2]
        dma, dmb = _mms(dya, f["xdt"], _NT), _mms(_zip(sub, dy2, dya), f["xdt"], _NT)
        dxdt = _zip(lambda a_, b_: jnp.where(half, a_, b_), _mms(f["ma"], dy2, _TN), _mms(f["mb"], dy2, _TN))
        dcb_p = _zip(lambda a_, sa, b_, sb: a_ * sa + b_ * sb, dma, f["sega"], dmb, f["segb"])
        ea_, eb_ = _zip(mul, dma, f["ma"]), _zip(mul, dmb, f["mb"])
        dac_a = _zip(lambda x, e: x + rowsum(e), dac_a, ea_)
        dac_b = _zip(lambda x, e: x + rowsum(e), dac_b, eb_)
        dzt = _mms(bgp, dsn, _NT)
        db_p = _mms(f["zt"], dsn, _NN)
        dxdt = _zip(lambda x, d, t_: x + d * t_, dxdt, dzt, f["tail"])
        t2 = _zip(mul, dzt, f["zt"])
        t2a = [rowsum(lo(x)) for x in t2]
        t2b = _zip(lambda x, a_: rowsum(x) - a_, t2, t2a)
        t3 = _zip(mul, dsn, s2)
        t3a = [_total(x[:64]) for x in t3]
        dla = _zip(lambda x, y, e: _total(x) + y * e, t2a, t3a, f["ea"])
        dlb = _zip(lambda x, y, e: _total(x) + _total(y[64:]) * e, t2b, t3, f["eb"])
        dac_a = _zip(lambda x, y, l: x - y + jnp.where(last_row, l, 0.0), dac_a, t2a, dla)
        dac_b = _zip(lambda x, y, l: x - y + jnp.where(last_row, l, 0.0), dac_b, t2b, dlb)
        dx2 = _zip(lambda x, d, t_: x + d * t_, dx2, dxdt, f["dt2"])
        t4 = _zip(mul, dxdt, x2)
        t4a = [rowsum(lo(x)) for x in t4]
        t4b = _zip(lambda x, a_: rowsum(x) - a_, t4, t4a)
        zero = jnp.zeros((CHUNK, HEAD), F32)
        ddt_acc, dac_acc, drow_acc = zero, zero, jnp.zeros((HEAD, CHUNK), F32)
        for p in pairs:
            dcx_ref[:, _lanes(p)] = dx2[p] * _dsilu(cxx[p])
            dstate[p] = ds_prev[p]
            ddt_acc = _put_col(_put_col(ddt_acc, t4a[p], 2 * p), t4b[p], 2 * p + 1)
            dac_acc = _put_col(_put_col(dac_acc, dac_a[p], 2 * p), dac_b[p], 2 * p + 1)
            drow_acc = _put_row(_put_row(drow_acc, -jnp.sum(ea_[p], axis=0, keepdims=True), 2 * p),
                                -jnp.sum(eb_[p], axis=0, keepdims=True), 2 * p + 1)
        per = SSD_PAIRS // SSD_GROUPS
        gsum = lambda xs: [functools.reduce(add, xs[g * per:(g + 1) * per]) for g in range(SSD_GROUPS)]
        dcb = gsum(dcb_p)
        dc = _zip(add, gsum(dc_p), _mms(dcb, bg, _NN))
        db = _zip(add, gsum(db_p), _mms(dcb, cg, _TN))
        for g in range(SSD_GROUPS):
            dcx_ref[:, XBC_B + g * SSD_STATE: XBC_B + (g + 1) * SSD_STATE] = db[g] * _dsilu(cxb[g])
            dcx_ref[:, XBC_C + g * SSD_STATE: XBC_C + (g + 1) * SSD_STATE] = dc[g] * _dsilu(cxc[g])
        dac = dac_acc + _mmx(_eye(CHUNK), drow_acc, _NT)
        dda = _mmx(_chunk_mask(CHUNK, True), dac, _NN)
        ddt = ddt_acc + dda * a
        ddtp = ddt * _sig(dtp + db_ref[...])
        ddtp_ref[...] = ddtp.astype(ddtp_ref.dtype)
        acc = _put_sub(jnp.zeros((8, HEAD), F32), jnp.sum(dda * da, axis=0, keepdims=True), 0)
        acc_ref[...] += _put_sub(acc, jnp.sum(ddtp, axis=0, keepdims=True), 1)

    rows = lambda w: pl.BlockSpec((CHUNK, w), lambda i: (nc - 1 - i, 0))
    act_ = lambda w: jax.ShapeDtypeStruct((t, w), F32)
    return _call(body, name, (nc,),
                 [rows(SSD_CONV_DIM), rows(HEAD), rows(D_INNER), _const(1, HEAD), _const(1, HEAD),
                  _const(1, D_INNER), _const(1, D_INNER), rows(D_INNER),
                  pl.BlockSpec((1, SSD_PAIRS, HEAD, SSD_STATE), lambda i: (nc - 1 - i, 0, 0, 0)), rows(D_INNER)],
                 [rows(SSD_CONV_DIM), rows(HEAD), rows(D_INNER), _const(8, D_INNER), _const(8, HEAD)],
                 [act_(SSD_CONV_DIM), jax.ShapeDtypeStruct((t, HEAD), MM_DTYPE), jax.ShapeDtypeStruct((t, D_INNER), MM_DTYPE),
                  jax.ShapeDtypeStruct((8, D_INNER), F32),
                  jax.ShapeDtypeStruct((8, HEAD), F32)],
                 [pltpu.VMEM((SSD_PAIRS, HEAD, SSD_STATE), F32), pltpu.VMEM((CHUNK, D_INNER), F32)], ("arbitrary",)
                 )(cx, dtp, z, dt_bias, a_log, dskip, norm_w, y, states, dh)


LN_TB = 512


def _ln_stats(x, y):
    u = ALPHA * x + y
    mu = jnp.mean(u, axis=1, keepdims=True)
    cen = u - mu
    rstd = lax.rsqrt(jnp.mean(cen * cen, axis=1, keepdims=True) + LN_EPS)
    return cen * rstd


def _ln_fwd(x, y, g, b, name):
    t, d = x.shape

    def body(x_ref, y_ref, g_ref, b_ref, o_ref, omm_ref):
        out = _ln_stats(x_ref[...], y_ref[...]) * g_ref[...] + b_ref[...]
        o_ref[...] = out
        omm_ref[...] = out.astype(omm_ref.dtype)

    tb = min(2 * LN_TB, t)
    return _call(body, name, (t // tb,), [_rows(d, tb), _rows(d, tb), _const(1, d), _const(1, d)],
                 [_rows(d, tb)] * 2, [jax.ShapeDtypeStruct((t, d), F32), jax.ShapeDtypeStruct((t, d), MM_DTYPE)],
                 semantics=("parallel",))(x, y, g, b)


def _ln_loss(x, y, g, b, target, name):
    t, d = x.shape

    def body(x_ref, y_ref, g_ref, b_ref, t_ref, d_ref, acc_ref):
        err = _ln_stats(x_ref[...], y_ref[...]) * g_ref[...] + b_ref[...] - t_ref[...]
        d_ref[...] = err * (1.0 / d)

        @pl.when(pl.program_id(0) == 0)
        def _():
            acc_ref[...] = jnp.zeros_like(acc_ref)

        acc_ref[...] += _put_sub(jnp.zeros((8, d), F32), jnp.sum(err * err, axis=0, keepdims=True), 0)

    return _call(body, name, (t // LN_TB,),
                 [_rows(d, LN_TB), _rows(d, LN_TB), _const(1, d), _const(1, d), _rows(d, LN_TB)],
                 [_rows(d, LN_TB), _const(8, d)],
                 [jax.ShapeDtypeStruct((t, d), F32), jax.ShapeDtypeStruct((8, d), F32)],
                 semantics=("arbitrary",))(x, y, g, b, target)


def _ln_bwd(dout, x, y, g, name):
    t, d = x.shape

    def body(d_ref, x_ref, y_ref, g_ref, du_ref, dumm_ref, acc_ref):
        u = ALPHA * x_ref[...] + y_ref[...]
        mu = jnp.mean(u, axis=1, keepdims=True)
        cen = u - mu
        rstd = lax.rsqrt(jnp.mean(cen * cen, axis=1, keepdims=True) + LN_EPS)
        xh = cen * rstd
        do = d_ref[...]
        dxh = do * g_ref[...]
        du = rstd * (dxh - jnp.mean(dxh, axis=1, keepdims=True) - xh * jnp.mean(dxh * xh, axis=1, keepdims=True))
        du_ref[...] = du
        dumm_ref[...] = du.astype(dumm_ref.dtype)
        acc = _put_sub(jnp.zeros((8, d), F32), jnp.sum(do * xh, axis=0, keepdims=True), 0)
        acc = _put_sub(acc, jnp.sum(do, axis=0, keepdims=True), 1)

        @pl.when(pl.program_id(0) == 0)
        def _():
            acc_ref[...] = jnp.zeros_like(acc_ref)

        acc_ref[...] += acc

    return _call(body, name, (t // LN_TB,), [_rows(d, LN_TB)] * 3 + [_const(1, d)],
                 [_rows(d, LN_TB), _rows(d, LN_TB), _const(8, d)],
                 [jax.ShapeDtypeStruct((t, d), F32), jax.ShapeDtypeStruct((t, d), MM_DTYPE),
                  jax.ShapeDtypeStruct((8, d), F32)],
                 semantics=("arbitrary",))(dout, x, y, g)


def _adamw(w, gslots, m, v, name):
    r, c = w.shape
    rb = _tile_rows(r)
    c1 = 1.0 - ADAM_B1 ** ADAM_STEP
    c2 = 1.0 - ADAM_B2 ** ADAM_STEP

    def body(w_ref, g_ref, m_ref, v_ref, go_ref, d_ref, mo_ref, vo_ref):
        g = g_ref[0].astype(F32)
        for s in range(1, N_DEV):
            g = g + g_ref[s].astype(F32)
        mn = ADAM_B1 * m_ref[...] + (1.0 - ADAM_B1) * g
        vn = ADAM_B2 * v_ref[...] + (1.0 - ADAM_B2) * (g * g)
        go_ref[...] = g
        mo_ref[...] = mn
        vo_ref[...] = vn
        d_ref[...] = -ADAM_LR * ((mn / c1) / (jnp.sqrt(vn / c2) + ADAM_EPS) + ADAM_WD * w_ref[...])

    blk = pl.BlockSpec((rb, c), lambda i: (i, 0))
    sds = jax.ShapeDtypeStruct((r, c), F32)
    return _call(body, name, (r // rb,), [blk, pl.BlockSpec((N_DEV, rb, c), lambda i: (0, i, 0)), blk, blk],
                 [blk] * 4, [sds] * 4, semantics=("parallel",))(w, gslots, m, v)


def _tile_rows(r):
    for rb in (256, 128, 64, 32, 16, 8):
        if r % rb == 0:
            return rb
    return r


def _pack(arrs, lead=0):
    flats = []
    for a in arrs:
        f = a.reshape(a.shape[:lead] + (-1,)).astype(F32)
        flats.append(jnp.pad(f, [(0, 0)] * lead + [(0, (-f.shape[-1]) % 128)]))
    v = jnp.concatenate(flats, axis=-1)
    v = jnp.pad(v, [(0, 0)] * lead + [(0, (-v.shape[-1]) % 1024)])
    return v.reshape(v.shape[:lead] + (-1, 128))


def _unpack(buf, shapes, lead=0):
    flat = buf.reshape(buf.shape[:lead] + (-1,))
    outs, off = [], 0
    for s in shapes:
        n = math.prod(s)
        outs.append(flat[..., off:off + n].reshape(buf.shape[:lead] + tuple(s)))
        off += n + (-n) % 128
    return outs


def _cols_gathered(g):
    n, l, r, c = g.shape
    return g.transpose(1, 2, 0, 3).reshape(l, r, n * c)


def _cols_to_slabs(full):
    l, r, c = full.shape
    return full.reshape(l, r, N_DEV, c // N_DEV).transpose(2, 0, 1, 3)


def _rows_gathered(g):
    n, l, r, c = g.shape
    return g.transpose(1, 0, 2, 3).reshape(l, n * r, c)


def _rows_to_slabs(full):
    l, r, c = full.shape
    return full.reshape(l, N_DEV, r // N_DEV, c).transpose(1, 0, 2, 3)


def _pad_cols(w, at, width):
    return jnp.pad(w, ((0, 0), (at, width - at - w.shape[1])))


def _pad_lanes(v, width=HEAD):
    return jnp.pad(v.reshape(1, -1), ((0, 0), (0, width - v.size)))


def _taps8(w, bias=None):
    rows = [w] if bias is None else [w, bias.reshape(1, -1)]
    w8 = jnp.concatenate(rows, axis=0)
    return jnp.pad(w8, ((0, 8 - w8.shape[0]), (0, 0)))


class _Carrier:
    def __init__(self):
        self.jobs, self.got = {}, {}

    def put(self, matmul_name, key, src, slabs):
        self.jobs[matmul_name] = (key, src, slabs)

    def matmul(self, a, b, mode, name, **kw):
        job = self.jobs.pop(name, None)
        if job is None:
            return _matmul(a, b, mode, name, **kw)
        key, src, slabs = job
        out, self.got[key] = _matmul(a, b, mode, name, carry=(src, slabs), **kw)
        return out


def _gdn_forward(x, p, tag, mm):
    pq = mm(x, p["w_qkv"], "nn", tag + "_in_qkv")
    z = mm(x, p["w_z"], "nn", tag + "_in_z")
    ba = mm(x, p["w_ba"], "nn", tag + "_in_ba")
    c = _conv_fwd(pq, p["conv8"], 4, tag + "_conv")
    qn, kn, v, beta, gc = _gdn_ew_fwd(c, ba, p["a_log"], p["dt_bias"], tag + "_ew")
    o, h, states = _gdn_chunk_fwd(qn, kn, v, beta, gc, z, p["norm_w"], tag + "_chunk")
    y = mm(h, p["w_out"], "nn", tag + "_out")
    return y, dict(pq=pq, z=z, ba=ba, c=c, qn=qn, kn=kn, v=v, beta=beta, gc=gc, o=o, h=h, states=states)


def _gdn_backward(x, du, du_mm, p, s, tag, mm, ship):
    dh = mm(du_mm, p["w_out"], "nt", tag + "_bwd_dh")
    g_out = mm(s["h"], du_mm, "tn", tag + "_bwd_wout")
    ship("w_out", g_out)
    dq, dk, dv, dz, dbeta, dgc, nacc = _gdn_chunk_bwd(
        s["qn"], s["kn"], s["v"], s["beta"], s["gc"], s["z"], p["norm_w"], s["o"], s["states"], dh, tag + "_bwd_chunk")
    dc, dba, sacc = _gdn_ew_bwd(s["c"], s["ba"], p["a_log"], p["dt_bias"], dq, dk, dv, dbeta, dgc, tag + "_bwd_ew")
    dpq, dconv = _conv_bwd(dc, s["pq"], p["conv8"], 4, tag + "_bwd_conv")
    g_qkv = mm(x, dpq, "tn", tag + "_bwd_w_qkv")
    g_z = mm(x, dz, "tn", tag + "_bwd_w_z")
    g_ba = mm(x, dba, "tn", tag + "_bwd_w_ba")
    g_in = jnp.concatenate([g_qkv, g_z, g_ba[:, :GDN_V_HEADS], g_ba[:, HEAD:HEAD + GDN_V_HEADS]], axis=1)
    ship("w_in", g_in)
    dx = mm(dpq, p["w_qkv"], "nt", tag + "_bwd_dx_qkv", add=du, add_scale=ALPHA)
    dx = mm(dz, p["w_z"], "nt", tag + "_bwd_dx_z", add=dx)
    dx = mm(dba, p["w_ba"], "nt", tag + "_bwd_dx_ba", add=dx)
    grads = dict(w_in=g_in, w_out=g_out, conv_w=dconv[:4], a_log=sacc[0, :GDN_V_HEADS], dt_bias=sacc[1, :GDN_V_HEADS],
                 norm_w=nacc[0])
    return dx, grads


def _sc_forward(x, p, tag, mm):
    hh = mm(x, p["w_h"], "nn", tag + "_in_h")
    bg = mm(x, p["w_b"], "nn", tag + "_in_b")
    cg = mm(x, p["w_c"], "nn", tag + "_in_c")
    z = mm(x, p["w_z"], "nn", tag + "_in_z")
    cv = _conv_fwd(cg, p["conv8"], 3, tag + "_conv", u2=hh)
    h = _sc_gate_fwd(bg, cv, z, tag + "_gate")
    y = mm(h, p["w_out"], "nn", tag + "_out")
    return y, dict(hh=hh, bg=bg, cg=cg, z=z, cv=cv, h=h)


def _sc_backward(x, du, du_mm, p, s, tag, mm, ship):
    dh = mm(du_mm, p["w_out"], "nt", tag + "_bwd_dh")
    g_out = mm(s["h"], du_mm, "tn", tag + "_bwd_wout")
    ship("w_out", g_out)
    dbg, dcv, dz = _sc_gate_bwd(dh, s["bg"], s["cv"], s["z"], tag + "_bwd_gate")
    dcg, dhh, dconv = _conv_bwd(dcv, s["cg"], p["conv8"], 3, tag + "_bwd_conv", u2=s["hh"])
    g_in = jnp.concatenate([mm(x, d, "tn", tag + "_bwd_w_" + n)
                            for n, d in (("h", dhh), ("b", dbg), ("c", dcg), ("z", dz))], axis=1)
    ship("w_in", g_in)
    dx = mm(dhh, p["w_h"], "nt", tag + "_bwd_dx_h", add=du, add_scale=ALPHA)
    dx = mm(dbg, p["w_b"], "nt", tag + "_bwd_dx_b", add=dx)
    dx = mm(dcg, p["w_c"], "nt", tag + "_bwd_dx_c", add=dx)
    dx = mm(dz, p["w_z"], "nt", tag + "_bwd_dx_z", add=dx)
    return dx, dict(w_in=g_in, w_out=g_out, conv_w=dconv[:3])


def _ssd_forward(x, p, tag, mm):
    z = mm(x, p["w_z"], "nn", tag + "_in_z")
    xbc = mm(x, p["w_xbc"], "nn", tag + "_in_xbc")
    dtp = mm(x, p["w_dt"], "nn", tag + "_in_dt")
    cx = _conv_fwd(xbc, p["conv8"], 4, tag + "_conv", bias=True)
    y, h, states = _ssd_chunk_fwd(cx, dtp, z, p["dt_bias"], p["a_log"], p["dskip"], p["norm_w"], tag + "_chunk")
    out = mm(h, p["w_out"], "nn", tag + "_out")
    return out, dict(z=z, xbc=xbc, dtp=dtp, cx=cx, y=y, h=h, states=states)


def _ssd_backward(x, du, du_mm, p, s, tag, mm, ship):
    dh = mm(du_mm, p["w_out"], "nt", tag + "_bwd_dh")
    g_out = mm(s["h"], du_mm, "tn", tag + "_bwd_wout")
    ship("w_out", g_out)
    dcx, ddtp, dz, wide, acc = _ssd_chunk_bwd(s["cx"], s["dtp"], s["z"], p["dt_bias"], p["a_log"], p["dskip"],
                                              p["norm_w"], s["y"], s["states"], dh, tag + "_bwd_chunk")
    dxbc, dconv = _conv_bwd(dcx, s["xbc"], p["conv8"], 4, tag + "_bwd_conv")
    g_dt = mm(x, ddtp, "tn", tag + "_bwd_w_dt")
    g_in = jnp.concatenate([mm(x, dz, "tn", tag + "_bwd_w_z"), mm(x, dxbc, "tn", tag + "_bwd_w_xbc"),
                            g_dt[:, :32]], axis=1)
    ship("w_in", g_in)
    dx = mm(dz, p["w_z"], "nt", tag + "_bwd_dx_z", add=du, add_scale=ALPHA)
    dx = mm(dxbc, p["w_xbc"], "nt", tag + "_bwd_dx_xbc", add=dx)
    dx = mm(ddtp, p["w_dt"], "nt", tag + "_bwd_dx_dt", add=dx)
    grads = dict(w_in=g_in, w_out=g_out, conv_w=dconv[:4], conv_b=dconv[4], a_log=acc[0, :32], dt_bias=acc[1, :32],
                 d_skip=jnp.sum(wide[1].reshape(32, 64), axis=1), norm_w=wide[0])
    return dx, grads


_WEIGHTS = ['gdn_w_in', 'gdn_conv_w', 'gdn_a_log', 'gdn_dt_bias', 'gdn_norm_w', 'gdn_w_out', 'sc_w_in', 'sc_conv_w',
            'sc_w_out', 'ssd_w_in', 'ssd_conv_w', 'ssd_conv_b', 'ssd_a_log', 'ssd_dt_bias', 'ssd_d_skip',
            'ssd_norm_w', 'ssd_w_out', 'ln_g', 'ln_b']
_BIG = {'gdn_w_in': 'cols', 'gdn_w_out': 'rows', 'sc_w_in': 'cols', 'sc_w_out': 'rows', 'ssd_w_in': 'cols',
        'ssd_w_out': 'rows'}
_SMALL_SHARDED = ['gdn_conv_w', 'sc_conv_w', 'ssd_conv_w', 'ssd_conv_b', 'ssd_norm_w']
_SMALL = [n for n in _WEIGHTS if n not in _BIG]


def kernel(x, gdn_w_in, gdn_conv_w, gdn_a_log, gdn_dt_bias, gdn_norm_w, gdn_w_out, sc_w_in, sc_conv_w, sc_w_out, ssd_w_in, ssd_conv_w, ssd_conv_b, ssd_a_log, ssd_dt_bias, ssd_d_skip, ssd_norm_w, ssd_w_out, ln_g, ln_b, loss_target, m_gdn_w_in, m_gdn_conv_w, m_gdn_a_log, m_gdn_dt_bias, m_gdn_norm_w, m_gdn_w_out, m_sc_w_in, m_sc_conv_w, m_sc_w_out, m_ssd_w_in, m_ssd_conv_w, m_ssd_conv_b, m_ssd_a_log, m_ssd_dt_bias, m_ssd_d_skip, m_ssd_norm_w, m_ssd_w_out, m_ln_g, m_ln_b, v_gdn_w_in, v_gdn_conv_w, v_gdn_a_log, v_gdn_dt_bias, v_gdn_norm_w, v_gdn_w_out, v_sc_w_in, v_sc_conv_w, v_sc_w_out, v_ssd_w_in, v_ssd_conv_w, v_ssd_conv_b, v_ssd_a_log, v_ssd_dt_bias, v_ssd_d_skip, v_ssd_norm_w, v_ssd_w_out, v_ln_g, v_ln_b):
    args = locals()
    wts = {n: args[n] for n in _WEIGHTS}
    mom = {n: args["m_" + n] for n in _WEIGHTS}
    vel = {n: args["v_" + n] for n in _WEIGHTS}
    me = 4 * lax.axis_index("x") + 2 * lax.axis_index("y") + lax.axis_index("c")
    x0, target = x[0], loss_target[0]

    car = _Carrier()
    shard = lambda n, j: wts[n][j:j + 1].astype(MM_DTYPE)
    gathered_w = lambda n, j: (_cols_gathered if _BIG[n] == "cols" else _rows_gathered)(car.got[n, j])[0]

    for n in ('gdn_w_in', 'gdn_w_out'):
        car.got[n, 0] = _exchange(shard(n, 0), "gather_%s0" % n, slabs=False)
    riders = {0: [("l0_gdn_in_qkv", 'sc_w_in', 0), ("l0_gdn_in_z", 'sc_w_out', 0), ("l0_gdn_out", 'ssd_w_out', 0)],
              1: [("l1_sc_in_h", 'ssd_w_in', 0)],
              2: [("l2_ssd_in_xbc", 'gdn_w_in', 1), ("l2_ssd_in_z", 'gdn_w_out', 1)]}
    full = {}
    small_shapes = [wts[n].shape for n in _SMALL_SHARDED]
    gathered = _exchange(_pack([wts[n] for n in _SMALL_SHARDED]), "gather_small", slabs=False)
    for n, g in zip(_SMALL_SHARDED, _unpack(gathered, small_shapes, lead=1)):
        full[n] = jnp.moveaxis(g, 0, -2).reshape(g.shape[1:-1] + (N_DEV * g.shape[-1],))
    for n in _SMALL:
        full.setdefault(n, wts[n])

    def gdn_params(j):
        w = gathered_w('gdn_w_in', j)
        return dict(w_qkv=w[:, :GDN_CONV_DIM], w_z=w[:, GDN_CONV_DIM:GDN_CONV_DIM + D_INNER],
                    w_ba=jnp.concatenate([_pad_cols(w[:, 6144:6160], 0, HEAD), _pad_cols(w[:, 6160:6176], 0, HEAD)], 1),
                    conv8=_taps8(full['gdn_conv_w'][j]), a_log=_pad_lanes(full['gdn_a_log'][j]),
                    dt_bias=_pad_lanes(full['gdn_dt_bias'][j]), norm_w=full['gdn_norm_w'][j].reshape(1, HEAD),
                    w_out=gathered_w('gdn_w_out', j))

    def sc_params():
        w = gathered_w('sc_w_in', 0)
        return dict(w_h=w[:, :2048], w_b=w[:, 2048:4096], w_c=w[:, 4096:6144], w_z=w[:, 6144:],
                    conv8=_taps8(full['sc_conv_w'][0]), w_out=gathered_w('sc_w_out', 0))

    def ssd_params():
        w = gathered_w('ssd_w_in', 0)
        return dict(w_z=w[:, :D_INNER], w_xbc=w[:, D_INNER:D_INNER + SSD_CONV_DIM],
                    w_dt=_pad_cols(w[:, D_INNER + SSD_CONV_DIM:], 0, HEAD),
                    conv8=_taps8(full['ssd_conv_w'][0], full['ssd_conv_b'][0]), a_log=_pad_lanes(full['ssd_a_log'][0]),
                    dt_bias=_pad_lanes(full['ssd_dt_bias'][0]),
                    dskip=jnp.repeat(full['ssd_d_skip'][0], 64).reshape(1, D_INNER),
                    norm_w=full['ssd_norm_w'][0].reshape(1, D_INNER), w_out=gathered_w('ssd_w_out', 0))

    layers = [("gdn", _gdn_forward, _gdn_backward, lambda: gdn_params(0)), ("sc", _sc_forward, _sc_backward, sc_params),
              ("ssd", _ssd_forward, _ssd_backward, ssd_params), ("gdn", _gdn_forward, _gdn_backward, lambda: gdn_params(1))]

    acts, acts_mm, ys, saved, params = [x0], [x0.astype(MM_DTYPE)], [], [], []
    for i, (kind, fwd, _, make_params) in enumerate(layers):
        params.append(make_params())
        for matmul_name, n, j in riders.get(i, ()):
            car.put(matmul_name, (n, j), shard(n, j), False)
        y, s = fwd(acts_mm[-1], params[i], "l%d_%s" % (i, kind), car.matmul)
        ys.append(y)
        saved.append(s)
        gain, bias = full['ln_g'][i].reshape(1, -1), full['ln_b'][i].reshape(1, -1)
        if i < DEPTH - 1:
            out, out_mm = _ln_fwd(acts[-1], y, gain, bias, "l%d_ln" % i)
            acts.append(out)
            acts_mm.append(out_mm)
        else:
            dact, loss_acc = _ln_loss(acts[-1], y, gain, bias, target, "l%d_ln_loss" % i)
    loss = lax.psum(0.5 / D_MODEL * jnp.sum(loss_acc[0]), ("x", "y", "c"))

    grad_riders = {3: dict(w_out="l3_gdn_bwd_w_qkv", w_in="l3_gdn_bwd_dx_qkv"),
                   2: dict(w_out="l2_ssd_bwd_w_z", w_in="l2_ssd_bwd_dx_xbc"),
                   1: dict(w_out="l1_sc_bwd_w_h", w_in="l0_gdn_bwd_w_qkv"),
                   0: dict(w_out="l0_gdn_bwd_w_z", w_in="l0_gdn_bwd_dx_qkv")}

    def shipper(i):
        def ship(key, g):
            slabs = _cols_to_slabs(g[None]) if key == 'w_in' else _rows_to_slabs(g[None])
            car.put(grad_riders[i][key], ('grad', i, key), slabs.astype(MM_DTYPE), True)
        return ship

    lg = [None] * DEPTH
    d_ln_g, d_ln_b = [None] * DEPTH, [None] * DEPTH
    for i in reversed(range(DEPTH)):
        kind, _, bwd, _ = layers[i]
        du, du_mm, acc = _ln_bwd(dact, acts[i], ys[i], full['ln_g'][i].reshape(1, -1), "l%d_ln_bwd" % i)
        d_ln_g[i], d_ln_b[i] = acc[0], acc[1]
        dact, lg[i] = bwd(acts_mm[i], du, du_mm, params[i], saved[i], "l%d_%s" % (i, kind), car.matmul, shipper(i))
    assert not car.jobs, car.jobs
    grad_x = dact[None]

    stack = lambda k: jnp.stack([lg[0][k], lg[3][k]])
    local = {
        'gdn_conv_w': stack('conv_w'), 'gdn_a_log': stack('a_log'),
        'gdn_dt_bias': stack('dt_bias'), 'gdn_norm_w': stack('norm_w'),
        'sc_conv_w': lg[1]['conv_w'][None],
        'ssd_conv_w': lg[2]['conv_w'][None], 'ssd_conv_b': lg[2]['conv_b'][None],
        'ssd_a_log': lg[2]['a_log'][None], 'ssd_dt_bias': lg[2]['dt_bias'][None], 'ssd_d_skip': lg[2]['d_skip'][None],
        'ssd_norm_w': lg[2]['norm_w'][None],
        'ln_g': jnp.stack(d_ln_g), 'ln_b': jnp.stack(d_ln_b)}

    out = {}
    layers_of = {'gdn': (0, 3), 'sc': (1,), 'ssd': (2,)}
    for n in _BIG:
        kind, key = n.split('_', 1)
        recv = jnp.concatenate([car.got['grad', i, key] for i in layers_of[kind]], axis=1)
        shp = wts[n].shape
        r, c = shp[0] * shp[1], shp[2]
        res = _adamw(wts[n].reshape(r, c), recv.reshape(N_DEV, r, c), mom[n].reshape(r, c), vel[n].reshape(r, c),
                     "adamw_" + n)
        out[n] = [a.reshape(shp) for a in res]
    full_shapes = [local[n].shape for n in _SMALL]
    gathered = _exchange(_pack([local[n] for n in _SMALL]), "gather_small_grads", slabs=False)
    gs = []
    for n, g in zip(_SMALL, _unpack(gathered, full_shapes, lead=1)):
        if n in _SMALL_SHARDED:
            width = wts[n].shape[-1]
            g = lax.dynamic_slice_in_dim(g, me * width, width, axis=g.ndim - 1)
        gs.append(g)
    shapes = [wts[n].shape for n in _SMALL]
    res = _adamw(_pack([wts[n] for n in _SMALL]), _pack(gs, lead=1), _pack([mom[n] for n in _SMALL]),
                 _pack([vel[n] for n in _SMALL]), "adamw_small")
    for k, n in enumerate(_SMALL):
        out[n] = [_unpack(a, shapes)[k] for a in res]

    return (loss, grad_x, *[out[n][0] for n in _WEIGHTS], *[out[n][1] for n in _WEIGHTS],
            *[out[n][2] for n in _WEIGHTS], *[out[n][3] for n in _WEIGHTS])
```
